```python
import jax, jax.numpy as jnp
from jax import lax
import numpy as np

D_MODEL = 1024
BATCH = 8
SEQ = 4096
DEPTH = 2

LRU_WIDTH = 512
LRU_BLOCKS = 8
LRU_BLOCK = LRU_WIDTH // LRU_BLOCKS
LRU_C = 8.0
CONV_WIDTH = 4
RWKV_HEADS = 8
RWKV_HEAD = 64
RWKV_WIDTH = RWKV_HEADS * RWKV_HEAD
W_LORA = 64
A_LORA = 64
G_LORA = 128
LN_X_EPS = 1e-5 * RWKV_HEAD
MLSTM_HEADS = 8
MLSTM_HEAD = 128
MLSTM_WIDTH = MLSTM_HEADS * MLSTM_HEAD
MLSTM_CHUNK = 128
D_FF = 2816
N_EXPERTS = 8
TOP_K = 2
EXPERT_BLOCK = 256
RMS_EPS = 1e-6
EVEN_IN = 2 * LRU_WIDTH + 3 * RWKV_WIDTH + W_LORA + A_LORA + G_LORA
ODD_IN = 2 * MLSTM_WIDTH + 2 * MLSTM_HEADS

kernel_name = 'hybrid_rglru_rwkv7_mlstm_moe'


def rms_norm(x, w):
    x32 = x.astype(jnp.float32)
    y = x32 * lax.rsqrt(jnp.mean(x32 * x32, axis=-1, keepdims=True) + RMS_EPS)
    return y.astype(x.dtype) * w


def causal_dwconv(x, w, b):
    y = lax.conv_general_dilated(
        x, w[:, None, :].astype(x.dtype), window_strides=(1,),
        padding=[(w.shape[0] - 1, 0)],
        dimension_numbers=('NWC', 'WIO', 'NWC'),
        feature_group_count=x.shape[-1])
    return y + b


def token_shift(x):
    return jnp.pad(x, ((0, 0), (1, 0), (0, 0)))[:, :-1]


def swiglu(x, w1, w3, w2):
    return (jax.nn.silu(x @ w1) * (x @ w3)) @ w2


def linear_recurrence(a, b):
    def combine(l, r):
        return l[0] * r[0], r[0] * l[1] + r[1]
    return lax.associative_scan(combine, (a, b), axis=1)[1]


def rg_lru(xc, gate_a_w, gate_a_b, gate_x_w, gate_x_b, lam):
    B, S, _ = xc.shape
    xb = xc.reshape(B, S, LRU_BLOCKS, LRU_BLOCK)
    r = jax.nn.sigmoid(jnp.einsum('bsgi,gij->bsgj', xb, gate_a_w).reshape(B, S, LRU_WIDTH) + gate_a_b)
    i = jax.nn.sigmoid(jnp.einsum('bsgi,gij->bsgj', xb, gate_x_w).reshape(B, S, LRU_WIDTH) + gate_x_b)
    log_a = -LRU_C * r * jax.nn.softplus(-lam)
    is_first = (jnp.arange(S) == 0)[None, :, None]
    mult = jnp.where(is_first, 1.0, jnp.sqrt(-jnp.expm1(2.0 * log_a)))
    return linear_recurrence(jnp.exp(log_a), mult * i * xc)


def rwkv7_scan(r, w, k, v, kk, a):
    B, S, H, N = r.shape

    def step(state, inp):
        r_t, w_t, k_t, v_t, kk_t, a_t = inp
        sa = jnp.einsum('bhvk,bhk->bhv', state, -kk_t)
        state = (state * w_t[:, :, None, :]
                 + sa[..., None] * (kk_t * a_t)[:, :, None, :]
                 + v_t[..., None] * k_t[:, :, None, :])
        return state, jnp.einsum('bhvk,bhk->bhv', state, r_t)

    xs = tuple(jnp.moveaxis(t, 1, 0) for t in (r, w, k, v, kk, a))
    _, y = lax.scan(step, jnp.zeros((B, H, N, N), jnp.float32), xs)
    return jnp.moveaxis(y, 0, 1)


def rwkv7_mix(p_rw, shift_mu, w0, w_up, a0, a_up, g_up, k_k, k_a, r_k, ln_x_w, ln_x_b):
    B, S, _ = p_rw.shape
    p = p_rw.astype(jnp.float32)
    p = p + shift_mu * (token_shift(p) - p)
    cuts = [RWKV_WIDTH, 2 * RWKV_WIDTH, 3 * RWKV_WIDTH, 3 * RWKV_WIDTH + W_LORA,
            3 * RWKV_WIDTH + W_LORA + A_LORA]
    r, k, v, xw, xa, xg = jnp.split(p, cuts, axis=-1)
    w_log = -jax.nn.softplus(-(w0 + jnp.tanh(xw) @ w_up)) - 0.5
    decay = jnp.exp(-jnp.exp(w_log))
    a = jax.nn.sigmoid(a0 + xa @ a_up)
    g = jax.nn.sigmoid(xg) @ g_up

    def heads(t):
        return t.reshape(B, S, RWKV_HEADS, RWKV_HEAD)

    kk = heads(k * k_k)
    kk = kk / jnp.maximum(jnp.sqrt(jnp.sum(kk * kk, axis=-1, keepdims=True)), 1e-12)
    r, k, v, decay, a = heads(r), heads(k), heads(v), heads(decay), heads(a)
    k = k * (1.0 + (a - 1.0) * k_a.reshape(RWKV_HEADS, RWKV_HEAD))
    y = rwkv7_scan(r, decay, k, v, kk, a)
    mu = jnp.mean(y, axis=-1, keepdims=True)
    var = jnp.mean(jnp.square(y - mu), axis=-1, keepdims=True)
    y = ((y - mu) * lax.rsqrt(var + LN_X_EPS)).reshape(B, S, RWKV_WIDTH) * ln_x_w + ln_x_b
    bonus = jnp.sum(r * k * r_k, axis=-1, keepdims=True) * v
    y = y + bonus.reshape(B, S, RWKV_WIDTH)
    return y * g


def mlstm_chunkwise(q, k, v, ig, logf):
    B, H, S, dk = q.shape
    dv = v.shape[-1]
    L = MLSTM_CHUNK
    nc = S // L
    q = q * (dk ** -0.5)

    def chunks(t):
        return jnp.moveaxis(t.reshape(t.shape[:2] + (nc, L) + t.shape[3:]), 2, 0)

    causal = jnp.tril(jnp.ones((L, L), dtype=bool))

    def step(carry, inp):
        C, n, m = carry
        qc, kc, vc, igc, lfc = inp
        b = jnp.cumsum(lfc, axis=-1)
        dlog = jnp.where(causal, b[..., :, None] - b[..., None, :] + igc[..., None, :], -jnp.inf)
        inter = b + m[..., None]
        m_t = jnp.maximum(inter, jnp.max(dlog, axis=-1))
        dmat = jnp.exp(dlog - m_t[..., None])
        s = jnp.einsum('bhtd,bhsd->bhts', qc, kc) * dmat
        sc = jnp.exp(inter - m_t)
        num = sc[..., None] * jnp.einsum('bhtd,bhde->bhte', qc, C) + jnp.einsum('bhts,bhse->bhte', s, vc)
        den = sc * jnp.einsum('bhtd,bhd->bht', qc, n) + jnp.sum(s, axis=-1)
        h = num / jnp.maximum(jnp.abs(den), jnp.exp(-m_t))[..., None]
        b_last = b[..., -1]
        wlog = b_last[..., None] - b + igc
        m_new = jnp.maximum(b_last + m, jnp.max(wlog, axis=-1))
        wts = jnp.exp(wlog - m_new[..., None])
        dec = jnp.exp(b_last + m - m_new)
        C = dec[..., None, None] * C + jnp.einsum('bhs,bhsd,bhse->bhde', wts, kc, vc)
        n = dec[..., None] * n + jnp.einsum('bhs,bhsd->bhd', wts, kc)
        return (C, n, m_new), h

    init = (jnp.zeros((B, H, dk, dv), jnp.float32), jnp.zeros((B, H, dk), jnp.float32),
            jnp.zeros((B, H), jnp.float32))
    _, h = lax.scan(step, init, (chunks(q), chunks(k), chunks(v), chunks(ig), chunks(logf)))
    return jnp.moveaxis(h, 0, 2).reshape(B, H, S, dv)


def moe_swiglu(xn, w_router, w1, w3, w2):
    B, S, D = xn.shape
    T = B * S
    M = T * TOP_K
    xt = xn.reshape(T, D)
    logits = (xt @ w_router).astype(jnp.float32)
    top_logit, top_idx = lax.top_k(logits, TOP_K)
    gates = jax.nn.softmax(top_logit, axis=-1)
    flat_e = top_idx.reshape(-1)
    flat_tok = jnp.repeat(jnp.arange(T, dtype=jnp.int32), TOP_K)
    order = jnp.argsort(flat_e)
    sorted_e = flat_e[order]
    sorted_tok = flat_tok[order]
    sorted_g = gates.reshape(-1)[order]
    counts = jnp.bincount(flat_e, length=N_EXPERTS)
    padded = ((counts + EXPERT_BLOCK - 1) // EXPERT_BLOCK) * EXPERT_BLOCK
    start = jnp.cumsum(counts) - counts
    pend = jnp.cumsum(padded)
    pstart = pend - padded
    dest = pstart[sorted_e] + (jnp.arange(M) - start[sorted_e])
    P = ((M + EXPERT_BLOCK - 1) // EXPERT_BLOCK + N_EXPERTS) * EXPERT_BLOCK
    nblk = P // EXPERT_BLOCK
    buf = jnp.zeros((P, D), xn.dtype).at[dest].set(xt[sorted_tok])
    blk_e = jnp.clip(jnp.searchsorted(pend, jnp.arange(nblk) * EXPERT_BLOCK, side='right'),
                     0, N_EXPERTS - 1)

    def expert_block(args):
        xb, e = args
        return swiglu(xb, w1[e], w3[e], w2[e])

    yb = lax.map(expert_block, (buf.reshape(nblk, EXPERT_BLOCK, D), blk_e))
    y_slot = yb.reshape(P, D)[dest] * sorted_g[:, None].astype(xn.dtype)
    out = jnp.zeros((T, D), xn.dtype).at[sorted_tok].add(y_slot)
    return out.reshape(B, S, D)


def even_layer(x, norm_mix, w_in, conv_w, conv_b, gate_a_w, gate_a_b, gate_x_w, gate_x_b,
               lam, shift_mu, w0, w_up, a0, a_up, g_up, k_k, k_a, r_k, ln_x_w, ln_x_b,
               w_out, norm_ffn, ffn_w1, ffn_w3, ffn_w2):
    xn = rms_norm(x, norm_mix)
    p = xn @ w_in
    lru_x = p[..., :LRU_WIDTH]
    lru_gate = p[..., LRU_WIDTH:2 * LRU_WIDTH]
    p_rw = p[..., 2 * LRU_WIDTH:]
    xc = causal_dwconv(lru_x, conv_w, conv_b).astype(jnp.float32)
    h = rg_lru(xc, gate_a_w, gate_a_b, gate_x_w, gate_x_b, lam)
    y_lru = h * jax.nn.gelu(lru_gate.astype(jnp.float32), approximate=True)
    y_rw = rwkv7_mix(p_rw, shift_mu, w0, w_up, a0, a_up, g_up, k_k, k_a, r_k, ln_x_w, ln_x_b)
    y = jnp.concatenate([y_lru, y_rw], axis=-1).astype(x.dtype)
    x = x + y @ w_out
    return x + swiglu(rms_norm(x, norm_ffn), ffn_w1, ffn_w3, ffn_w2)


def odd_layer(x, norm_mix, w_in, conv_w, conv_b, wq, wk, wv, ig_b, fg_b, mh_norm_w, w_out,
              norm_ffn, router, moe_w1, moe_w3, moe_w2):
    B, S, _ = x.shape
    xn = rms_norm(x, norm_mix)
    p = xn @ w_in
    x_m = p[..., :MLSTM_WIDTH]
    z = p[..., MLSTM_WIDTH:2 * MLSTM_WIDTH]
    ig_pre = p[..., 2 * MLSTM_WIDTH:2 * MLSTM_WIDTH + MLSTM_HEADS]
    fg_pre = p[..., 2 * MLSTM_WIDTH + MLSTM_HEADS:]
    xc = jax.nn.silu(causal_dwconv(x_m, conv_w, conv_b))
    xch = xc.reshape(B, S, MLSTM_HEADS, MLSTM_HEAD)
    q = jnp.einsum('bshi,hij->bhsj', xch, wq).astype(jnp.float32)
    k = jnp.einsum('bshi,hij->bhsj', xch, wk).astype(jnp.float32)
    v = jnp.einsum('bshi,hij->bhsj', x_m.reshape(B, S, MLSTM_HEADS, MLSTM_HEAD), wv).astype(jnp.float32)
    ig = jnp.moveaxis((ig_pre + ig_b).astype(jnp.float32), 2, 1)
    logf = jnp.moveaxis(jax.nn.log_sigmoid((fg_pre + fg_b).astype(jnp.float32)), 2, 1)
    h = jnp.moveaxis(mlstm_chunkwise(q, k, v, ig, logf), 1, 2)
    h = h * lax.rsqrt(jnp.mean(h * h, axis=-1, keepdims=True) + RMS_EPS)
    h = h.reshape(B, S, MLSTM_WIDTH) * mh_norm_w
    h = jax.nn.sigmoid(z.astype(jnp.float32)) * h
    x = x + h.astype(x.dtype) @ w_out
    return x + moe_swiglu(rms_norm(x, norm_ffn), router, moe_w1, moe_w3, moe_w2)


def setup_inputs(seed: int = 0) -> dict:
    key = jax.random.key(seed)
    ks = iter(jax.random.split(key, 64))
    D = D_MODEL

    def nrm(shape, scale):
        return scale * jax.random.normal(next(ks), shape, jnp.float32)

    def gain(n):
        return 1.0 + nrm((n,), 0.05)

    u = jax.random.uniform(next(ks), (LRU_WIDTH,), jnp.float32, 0.9, 0.999)
    base = u ** (1.0 / LRU_C)
    lam = jnp.log(base) - jnp.log1p(-base)
    return {
        'x': nrm((BATCH, SEQ, D), 1.0),
        'l0_norm_mix': gain(D),
        'l0_w_in': nrm((D, EVEN_IN), D ** -0.5),
        'l0_conv_w': nrm((CONV_WIDTH, LRU_WIDTH), CONV_WIDTH ** -0.5),
        'l0_conv_b': nrm((LRU_WIDTH,), 0.02),
        'l0_gate_a_w': nrm((LRU_BLOCKS, LRU_BLOCK, LRU_BLOCK), LRU_BLOCK ** -0.5),
        'l0_gate_a_b': nrm((LRU_WIDTH,), 0.1),
        'l0_gate_x_w': nrm((LRU_BLOCKS, LRU_BLOCK, LRU_BLOCK), LRU_BLOCK ** -0.5),
        'l0_gate_x_b': nrm((LRU_WIDTH,), 0.1),
        'l0_lru_lambda': lam,
        'l0_shift_mu': jax.random.uniform(next(ks), (EVEN_IN - 2 * LRU_WIDTH,), jnp.float32),
        'l0_w0': jnp.linspace(-6.0, 1.0, RWKV_WIDTH, dtype=jnp.float32) + nrm((RWKV_WIDTH,), 0.1),
        'l0_w_up': nrm((W_LORA, RWKV_WIDTH), 0.1),
        'l0_a0': nrm((RWKV_WIDTH,), 0.1),
        'l0_a_up': nrm((A_LORA, RWKV_WIDTH), 0.1),
        'l0_g_up': nrm((G_LORA, RWKV_WIDTH), G_LORA ** -0.5),
        'l0_k_k': 0.85 + nrm((RWKV_WIDTH,), 0.05),
        'l0_k_a': gain(RWKV_WIDTH),
        'l0_r_k': nrm((RWKV_HEADS, RWKV_HEAD), 0.1),
        'l0_ln_x_w': gain(RWKV_WIDTH),
        'l0_ln_x_b': nrm((RWKV_WIDTH,), 0.02),
        'l0_w_out': nrm((LRU_WIDTH + RWKV_WIDTH, D), (LRU_WIDTH + RWKV_WIDTH) ** -0.5),
        'l0_norm_ffn': gain(D),
        'l0_ffn_w1': nrm((D, D_FF), D ** -0.5),
        'l0_ffn_w3': nrm((D, D_FF), D ** -0.5),
        'l0_ffn_w2': nrm((D_FF, D), D_FF ** -0.5),
        'l1_norm_mix': gain(D),
        'l1_w_in': nrm((D, ODD_IN), D ** -0.5),
        'l1_conv_w': nrm((CONV_WIDTH, MLSTM_WIDTH), CONV_WIDTH ** -0.5),
        'l1_conv_b': nrm((MLSTM_WIDTH,), 0.02),
        'l1_wq': nrm((MLSTM_HEADS, MLSTM_HEAD, MLSTM_HEAD), MLSTM_HEAD ** -0.5),
        'l1_wk': nrm((MLSTM_HEADS, MLSTM_HEAD, MLSTM_HEAD), MLSTM_HEAD ** -0.5),
        'l1_wv': nrm((MLSTM_HEADS, MLSTM_HEAD, MLSTM_HEAD), MLSTM_HEAD ** -0.5),
        'l1_ig_b': nrm((MLSTM_HEADS,), 0.1),
        'l1_fg_b': jnp.linspace(3.0, 6.0, MLSTM_HEADS, dtype=jnp.float32) + nrm((MLSTM_HEADS,), 0.1),
        'l1_mh_norm_w': gain(MLSTM_WIDTH),
        'l1_w_out': nrm((MLSTM_WIDTH, D), MLSTM_WIDTH ** -0.5),
        'l1_norm_ffn': gain(D),
        'l1_router': nrm((D, N_EXPERTS), D ** -0.5),
        'l1_moe_w1': nrm((N_EXPERTS, D, D_FF), D ** -0.5),
        'l1_moe_w3': nrm((N_EXPERTS, D, D_FF), D ** -0.5),
        'l1_moe_w2': nrm((N_EXPERTS, D_FF, D), D_FF ** -0.5),
        'final_norm': gain(D),
    }


def reference(x, l0_norm_mix, l0_w_in, l0_conv_w, l0_conv_b, l0_gate_a_w, l0_gate_a_b,
              l0_gate_x_w, l0_gate_x_b, l0_lru_lambda, l0_shift_mu, l0_w0, l0_w_up, l0_a0,
              l0_a_up, l0_g_up, l0_k_k, l0_k_a, l0_r_k, l0_ln_x_w, l0_ln_x_b, l0_w_out,
              l0_norm_ffn, l0_ffn_w1, l0_ffn_w3, l0_ffn_w2,
              l1_norm_mix, l1_w_in, l1_conv_w, l1_conv_b, l1_wq, l1_wk, l1_wv, l1_ig_b,
              l1_fg_b, l1_mh_norm_w, l1_w_out, l1_norm_ffn, l1_router, l1_moe_w1,
              l1_moe_w3, l1_moe_w2, final_norm):
    layer_params = [
        (l0_norm_mix, l0_w_in, l0_conv_w, l0_conv_b, l0_gate_a_w, l0_gate_a_b, l0_gate_x_w,
         l0_gate_x_b, l0_lru_lambda, l0_shift_mu, l0_w0, l0_w_up, l0_a0, l0_a_up, l0_g_up,
         l0_k_k, l0_k_a, l0_r_k, l0_ln_x_w, l0_ln_x_b, l0_w_out, l0_norm_ffn, l0_ffn_w1,
         l0_ffn_w3, l0_ffn_w2),
        (l1_norm_mix, l1_w_in, l1_conv_w, l1_conv_b, l1_wq, l1_wk, l1_wv, l1_ig_b, l1_fg_b,
         l1_mh_norm_w, l1_w_out, l1_norm_ffn, l1_router, l1_moe_w1, l1_moe_w3, l1_moe_w2),
    ]
    for layer in range(DEPTH):
        if layer % 2 == 0:
            x = even_layer(x, *layer_params[layer])
        else:
            x = odd_layer(x, *layer_params[layer])
    return rms_norm(x, final_norm)
```

```python
import functools

import jax
import jax.numpy as jnp
from jax import lax
from jax.experimental import pallas as pl
from jax.experimental.pallas import tpu as pltpu

F32 = jnp.float32
BF16 = jnp.bfloat16

D_MODEL = 1024
LRU_WIDTH = 512
LRU_BLOCKS = 8
LRU_C = 8.0
CONV_WIDTH = 4
RWKV_HEADS = 8
RWKV_HEAD = 64
RWKV_WIDTH = 512
LN_X_EPS = 1e-5 * RWKV_HEAD
MLSTM_HEADS = 8
MLSTM_HEAD = 128
MLSTM_WIDTH = 1024
MLSTM_CHUNK = 128
D_FF = 2816
N_EXPERTS = 8
RMS_EPS = 1e-6
RW_IN = 3 * RWKV_WIDTH + 64 + 64 + 128
RW_CHUNK = 64
LANES = 128
CARRY_ROWS = 8
VMEM_LIMIT = 56 * 1024 * 1024


def _cparams(sem):
    return pltpu.CompilerParams(dimension_semantics=sem, vmem_limit_bytes=VMEM_LIMIT)


def _rms(x, w):
    return x * lax.rsqrt(jnp.mean(x * x, axis=-1, keepdims=True) + RMS_EPS) * w


def _sigmoid(x):
    return 1.0 / (1.0 + jnp.exp(-x))


def _softplus(x):
    return jnp.maximum(x, 0.0) + jnp.log(1.0 + jnp.exp(-jnp.abs(x)))


def _dot(a, b):
    return jnp.dot(a, b, preferred_element_type=F32)


def _dot_nt(a, b):
    return lax.dot_general(a, b, (((1,), (1,)), ((), ())), preferred_element_type=F32)


def _dot_tn(a, b):
    return lax.dot_general(a, b, (((0,), (0,)), ((), ())), preferred_element_type=F32)


def _split2(x):
    hi = x.astype(BF16)
    lo = (x - hi.astype(F32)).astype(BF16)
    return hi, lo


def _split3(x):
    hi = x.astype(BF16)
    r = x - hi.astype(F32)
    mid = r.astype(BF16)
    lo = (r - mid.astype(F32)).astype(BF16)
    return hi, mid, lo


def _mm3(fn, a, b):
    ah, al = _split2(a)
    bh, bl = _split2(b)
    return fn(ah, bh) + fn(al, bh) + fn(ah, bl)


def _mm_exact_lhs(fn, a_bf16, b):
    h, m, l = _split3(b)
    return fn(a_bf16, h) + fn(a_bf16, m) + fn(a_bf16, l)


def _mm_exact_rhs(fn, a, b_bf16):
    h, m, l = _split3(a)
    return fn(h, b_bf16) + fn(m, b_bf16) + fn(l, b_bf16)


def _shift_hist(buf_ref, x, first):
    n = x.shape[0]

    @pl.when(first)
    def _():
        buf_ref[0:CARRY_ROWS, :] = jnp.zeros((CARRY_ROWS, x.shape[1]), F32)

    @pl.when(jnp.logical_not(first))
    def _():
        buf_ref[0:CARRY_ROWS, :] = buf_ref[n:n + CARRY_ROWS, :]

    buf_ref[CARRY_ROWS:CARRY_ROWS + n, :] = x


def _causal_conv(buf_ref, n, w, b):
    acc = b
    for j in range(CONV_WIDTH):
        off = CARRY_ROWS - (CONV_WIDTH - 1) + j
        acc = acc + w[j:j + 1, :] * buf_ref[off:off + n, :]
    return acc


def _norm_mm_kernel(x_ref, g_ref, w_ref, o_ref):
    xn = _rms(x_ref[...], g_ref[...]).astype(BF16)
    o_ref[...] = _dot(xn, w_ref[...]).astype(o_ref.dtype)


def _norm_mm(x, g, w, tm, name):
    t, d = x.shape
    n = w.shape[1]
    return pl.pallas_call(
        _norm_mm_kernel,
        grid=(t // tm,),
        in_specs=[pl.BlockSpec((tm, d), lambda i: (i, 0)),
                  pl.BlockSpec((1, d), lambda i: (0, 0)),
                  pl.BlockSpec((d, n), lambda i: (0, 0))],
        out_specs=pl.BlockSpec((tm, n), lambda i: (i, 0)),
        out_shape=jax.ShapeDtypeStruct((t, n), F32),
        compiler_params=_cparams(("parallel",)),
        name=name,
    )(x, g.reshape(1, d), w)


def _mm_res_kernel(*refs):
    x_ref, o_ref = refs[0], refs[-1]
    acc = x_ref[...]
    for i in range(1, len(refs) - 1, 2):
        acc = acc + _dot(refs[i][...], refs[i + 1][...])
    o_ref[...] = acc


def _mm_res(x, pairs, tm, name):
    t, d = x.shape
    in_specs = [pl.BlockSpec((tm, d), lambda i: (i, 0))]
    args = [x]
    for y, w in pairs:
        in_specs.append(pl.BlockSpec((tm, y.shape[1]), lambda i: (i, 0)))
        in_specs.append(pl.BlockSpec(w.shape, lambda i: (0, 0)))
        args += [y, w]
    return pl.pallas_call(
        _mm_res_kernel,
        grid=(t // tm,),
        in_specs=in_specs,
        out_specs=pl.BlockSpec((tm, d), lambda i: (i, 0)),
        out_shape=jax.ShapeDtypeStruct((t, d), F32),
        compiler_params=_cparams(("parallel",)),
        name=name,
    )(*args)


def _lru_kernel(p_ref, cw_ref, cb_ref, wa_ref, ba_ref, wx_ref, bx_ref, lam_ref, o_ref,
                xbuf, abuf, bbuf, hbuf, hcar):
    i = pl.program_id(1)
    n = p_ref.shape[0]
    first = i == 0
    _shift_hist(xbuf, p_ref[:, 0:LRU_WIDTH], first)
    xc = _causal_conv(xbuf, n, cw_ref[...], cb_ref[...])
    xcb = xc.astype(BF16)
    r = _sigmoid(_dot(xcb, wa_ref[...]) + ba_ref[...])
    ig = _sigmoid(_dot(xcb, wx_ref[...]) + bx_ref[...])
    log_a = (-LRU_C) * r * _softplus(-lam_ref[...])
    a = jnp.exp(log_a)
    mult = jnp.sqrt(1.0 - jnp.exp(2.0 * log_a))
    row = lax.broadcasted_iota(jnp.int32, (n, 1), 0)
    mult = jnp.where(jnp.logical_and(first, row == 0), 1.0, mult)
    abuf[...] = a
    bbuf[...] = mult * ig * xc

    @pl.when(first)
    def _():
        hcar[...] = jnp.zeros(hcar.shape, F32)

    def step(t, h):
        h = abuf[pl.ds(t, 1), :] * h + bbuf[pl.ds(t, 1), :]
        hbuf[pl.ds(t, 1), :] = h
        return h

    h = lax.fori_loop(0, n, step, hcar[0:1, :], unroll=8)
    hcar[0:1, :] = h
    gate = p_ref[:, LRU_WIDTH:2 * LRU_WIDTH]
    gelu = 0.5 * gate * (1.0 + jnp.tanh(0.7978845608028654 * (gate + 0.044715 * gate * gate * gate)))
    o_ref[...] = (hbuf[...] * gelu).astype(o_ref.dtype)


def _lru(p_lru, batch, seq, conv_w, conv_b, wa, ba, wx, bx, lam, tb):
    nb = seq // tb
    c = LRU_WIDTH
    vec = lambda: pl.BlockSpec((1, c), lambda b, i: (0, 0))
    return pl.pallas_call(
        _lru_kernel,
        grid=(batch, nb),
        in_specs=[pl.BlockSpec((tb, 2 * c), lambda b, i: (b * nb + i, 0)),
                  pl.BlockSpec((CONV_WIDTH, c), lambda b, i: (0, 0)), vec(),
                  pl.BlockSpec((c, c), lambda b, i: (0, 0)), vec(),
                  pl.BlockSpec((c, c), lambda b, i: (0, 0)), vec(), vec()],
        out_specs=pl.BlockSpec((tb, c), lambda b, i: (b * nb + i, 0)),
        out_shape=jax.ShapeDtypeStruct((batch * seq, c), BF16),
        scratch_shapes=[pltpu.VMEM((tb + CARRY_ROWS, c), F32), pltpu.VMEM((tb, c), F32),
                        pltpu.VMEM((tb, c), F32), pltpu.VMEM((tb, c), F32),
                        pltpu.VMEM((CARRY_ROWS, c), F32)],
        compiler_params=_cparams(("parallel", "arbitrary")),
        name="rg_lru",
    )(p_lru, conv_w, conv_b.reshape(1, c), wa, ba.reshape(1, c), wx, bx.reshape(1, c),
      lam.reshape(1, c))


def _blk(x, masks):
    return jnp.concatenate([x * mk for mk in masks], axis=0)


def _rwkv_kernel(p_ref, mu_ref, w0_ref, wup_ref, a0_ref, aup_ref, gup_ref, kk_ref, ka_ref,
                 rk_ref, lnw_ref, lnb_ref, seg_ref, tri_ref, o_ref, pbuf, state):
    c = pl.program_id(1)
    L = RW_CHUNK
    W = RWKV_WIDTH
    first = c == 0
    p = p_ref[...]
    _shift_hist(pbuf, p, first)
    prev = pbuf[CARRY_ROWS - 1:CARRY_ROWS - 1 + L, :]
    ps = p + mu_ref[...] * (prev - p)
    r = ps[:, 0:W]
    k = ps[:, W:2 * W]
    v = ps[:, 2 * W:3 * W]
    x2 = ps[:, 3 * W:3 * W + LANES]
    xg = ps[:, 3 * W + LANES:3 * W + 2 * LANES]
    wl = w0_ref[...] + _dot(jnp.tanh(x2).astype(BF16), wup_ref[...])
    lw = -jnp.exp(-_softplus(-wl) - 0.5)
    a = _sigmoid(a0_ref[...] + _dot(x2.astype(BF16), aup_ref[...]))
    g = _dot(_sigmoid(xg).astype(BF16), gup_ref[...])
    seg = seg_ref[...]
    kk = k * kk_ref[...]
    kk = kk / jnp.maximum(jnp.sqrt(_mm_exact_rhs(_dot, kk * kk, seg)), 1e-12)
    k2 = k * (1.0 + (a - 1.0) * ka_ref[...])

    cw = _mm_exact_lhs(_dot, tri_ref[...], lw)
    cw_end = cw[L - 1:L, :]
    w_in = jnp.exp(cw)
    w_inv = jnp.exp(-cw)
    w_end = jnp.exp(cw_end - cw)
    kka = kk * a
    a_t = -kk * jnp.exp(cw - lw)
    b_t = kka * w_inv
    k_t = k2 * w_inv
    r_t = r * w_in
    b_bar = kka * w_end
    k_bar = k2 * w_end
    w_tot = jnp.exp(cw_end)

    @pl.when(first)
    def _():
        state[...] = jnp.zeros(state.shape, F32)

    lane = lax.broadcasted_iota(jnp.int32, (1, LANES), 1)
    m1 = [(lane < RWKV_HEAD).astype(F32), (lane >= RWKV_HEAD).astype(F32)]
    m2 = [jnp.concatenate([mk, mk], axis=1) for mk in m1]
    ti = lax.broadcasted_iota(jnp.int32, (L, LANES), 0)
    si = lax.broadcasted_iota(jnp.int32, (L, LANES), 1) % RWKV_HEAD
    strict = si < ti
    incl = si <= ti
    bi = lax.broadcasted_iota(jnp.int32, (LANES, LANES), 0) // RWKV_HEAD
    bj = lax.broadcasted_iota(jnp.int32, (LANES, LANES), 1) // RWKV_HEAD
    bd = bi == bj
    mm = functools.partial(_mm3, _dot)
    mm_nt = functools.partial(_mm3, _dot_nt)
    mm_tn = functools.partial(_mm3, _dot_tn)

    ys = []
    for pr in range(RWKV_HEADS // 2):
        sl = slice(pr * LANES, (pr + 1) * LANES)
        ar = jnp.concatenate([a_t[:, sl], r_t[:, sl]], axis=0)
        sb = mm_nt(ar, _blk(b_t[:, sl], m1))
        sk = mm_nt(ar, _blk(k_t[:, sl], m1))
        nmat = jnp.where(strict, sb[0:L], 0.0)
        vv = v[:, sl]
        rhs = jnp.concatenate([a_t[:, sl], mm(jnp.where(strict, sk[0:L], 0.0), _blk(vv, m1))], axis=1)
        x = rhs + mm(nmat, _blk(rhs, m2))
        pw = nmat
        for _ in range(5):
            pw = mm(pw, _blk(pw, m1))
            x = x + mm(pw, _blk(x, m2))
        corr = mm(jnp.where(incl, sb[L:2 * L], 0.0), _blk(x, m2))
        r_hat = r_t[:, sl] + corr[:, 0:LANES]
        y0 = corr[:, LANES:2 * LANES] + mm(jnp.where(incl, sk[L:2 * L], 0.0), _blk(vv, m1))
        a_hat = x[:, 0:LANES]
        u0 = x[:, LANES:2 * LANES]
        gmat = jnp.where(bd, mm_tn(a_hat, b_bar[:, sl]), 0.0)
        hmat = jnp.where(bd, mm_tn(u0, b_bar[:, sl]) + mm_tn(vv, k_bar[:, sl]), 0.0)
        s0 = state[pr]
        ys.append(mm_nt(r_hat, s0) + y0)
        state[pr] = s0 * w_tot[:, sl] + mm(s0, gmat) + hmat
    y = jnp.concatenate(ys, axis=1)

    inv = 1.0 / RWKV_HEAD
    mean = _mm_exact_rhs(_dot, y, seg) * inv
    yc = y - mean
    var = _mm_exact_rhs(_dot, yc * yc, seg) * inv
    yn = yc * lax.rsqrt(var + LN_X_EPS) * lnw_ref[...] + lnb_ref[...]
    bonus = _mm_exact_rhs(_dot, r * k2 * rk_ref[...], seg) * v
    o_ref[...] = ((yn + bonus) * g).astype(o_ref.dtype)


def _rwkv(p_rw, batch, seq, mu, w0, w_up, a0, a_up, g_up, k_k, k_a, r_k, ln_w, ln_b):
    L = RW_CHUNK
    nc = seq // L
    W = RWKV_WIDTH
    wup = jnp.zeros((LANES, W), F32).at[0:64].set(w_up).astype(BF16)
    aup = jnp.zeros((LANES, W), F32).at[64:128].set(a_up).astype(BF16)
    hid = jnp.arange(W) // RWKV_HEAD
    seg = (hid[:, None] == hid[None, :]).astype(BF16)
    tri = (jnp.arange(L)[:, None] >= jnp.arange(L)[None, :]).astype(BF16)
    vec = lambda: pl.BlockSpec((1, W), lambda b, i: (0, 0))
    mat = lambda s: pl.BlockSpec(s, lambda b, i: (0, 0))
    return pl.pallas_call(
        _rwkv_kernel,
        grid=(batch, nc),
        in_specs=[pl.BlockSpec((L, RW_IN), lambda b, i: (b * nc + i, 0)),
                  mat((1, RW_IN)), vec(), mat((LANES, W)), vec(), mat((LANES, W)),
                  mat((LANES, W)), vec(), vec(), vec(), vec(), vec(), mat((W, W)), mat((L, L))],
        out_specs=pl.BlockSpec((L, W), lambda b, i: (b * nc + i, 0)),
        out_shape=jax.ShapeDtypeStruct((batch * seq, W), BF16),
        scratch_shapes=[pltpu.VMEM((L + CARRY_ROWS, RW_IN), F32),
                        pltpu.VMEM((RWKV_HEADS // 2, LANES, LANES), F32)],
        compiler_params=_cparams(("parallel", "arbitrary")),
        name="rwkv7",
    )(p_rw, mu.reshape(1, RW_IN), w0.reshape(1, W), wup, a0.reshape(1, W), aup,
      g_up.astype(BF16), k_k.reshape(1, W), k_a.reshape(1, W), r_k.reshape(1, W),
      ln_w.reshape(1, W), ln_b.reshape(1, W), seg, tri)


def _ffn_kernel(x_ref, g_ref, w1_ref, w3_ref, w2_ref, o_ref, xn_s, acc_s):
    f = pl.program_id(1)

    @pl.when(f == 0)
    def _():
        xn_s[...] = _rms(x_ref[...], g_ref[...]).astype(BF16)
        acc_s[...] = x_ref[...]

    xn = xn_s[...]
    h1 = _dot(xn, w1_ref[...])
    h3 = _dot(xn, w3_ref[...])
    act = (h1 * _sigmoid(h1) * h3).astype(BF16)
    acc_s[...] += _dot(act, w2_ref[...])

    @pl.when(f == pl.num_programs(1) - 1)
    def _():
        o_ref[...] = acc_s[...]


def _ffn(x, g, w1, w3, w2, tm, tf):
    t, d = x.shape
    nf = w1.shape[1] // tf
    return pl.pallas_call(
        _ffn_kernel,
        grid=(t // tm, nf),
        in_specs=[pl.BlockSpec((tm, d), lambda i, f: (i, 0)),
                  pl.BlockSpec((1, d), lambda i, f: (0, 0)),
                  pl.BlockSpec((d, tf), lambda i, f: (0, f)),
                  pl.BlockSpec((d, tf), lambda i, f: (0, f)),
                  pl.BlockSpec((tf, d), lambda i, f: (f, 0))],
        out_specs=pl.BlockSpec((tm, d), lambda i, f: (i, 0)),
        out_shape=jax.ShapeDtypeStruct((t, d), F32),
        scratch_shapes=[pltpu.VMEM((tm, d), BF16), pltpu.VMEM((tm, d), F32)],
        compiler_params=_cparams(("parallel", "arbitrary")),
        name="ffn_swiglu",
    )(x, g.reshape(1, d), w1, w3, w2)


def _mlstm_kernel(xm_ref, z_ref, gt_ref, cw_ref, cb_ref, wq_ref, wk_ref, wv_ref, gb_ref,
                  nw_ref, tri_ref, o_ref, xbuf, c_s, n_s, m_s):
    ci = pl.program_id(1)
    L = MLSTM_CHUNK
    dh = MLSTM_HEAD
    first = ci == 0
    xm = xm_ref[...]
    _shift_hist(xbuf, xm, first)
    xc = _causal_conv(xbuf, L, cw_ref[...], cb_ref[...])
    xc = xc * _sigmoid(xc)
    gl = gt_ref[...] + gb_ref[...]
    lf = jnp.minimum(gl, 0.0) - jnp.log(1.0 + jnp.exp(-jnp.abs(gl)))
    bcum = _mm_exact_lhs(_dot, tri_ref[...], lf)
    gl_t = gl.T
    bcum_t = bcum.T

    @pl.when(first)
    def _():
        c_s[...] = jnp.zeros(c_s.shape, F32)
        n_s[...] = jnp.zeros(n_s.shape, F32)
        m_s[...] = jnp.zeros(m_s.shape, F32)

    ti = lax.broadcasted_iota(jnp.int32, (L, L), 0)
    si = lax.broadcasted_iota(jnp.int32, (L, L), 1)
    causal = si <= ti
    scale = dh ** -0.5
    for h in range(MLSTM_HEADS):
        sl = slice(h * dh, (h + 1) * dh)
        xs = xc[:, sl].astype(BF16)
        q = _dot(xs, wq_ref[h]) * scale
        k = _dot(xs, wk_ref[h])
        v = _dot(xm[:, sl].astype(BF16), wv_ref[h])
        fh = MLSTM_HEADS + h
        bcol = bcum[:, fh:fh + 1]
        brow = bcum_t[fh:fh + 1, :]
        igrow = gl_t[h:h + 1, :]
        igcol = gl[:, h:h + 1]
        m_prev = m_s[h:h + 1, 0:1]
        dlog = jnp.where(causal, bcol - brow + igrow, -jnp.inf)
        inter = bcol + m_prev
        m_t = jnp.maximum(inter, jnp.max(dlog, axis=-1, keepdims=True))
        dmat = jnp.exp(dlog - m_t)
        qb = q.astype(BF16)
        kb = k.astype(BF16)
        vb = v.astype(BF16)
        s = _dot_nt(qb, kb) * dmat
        sc = jnp.exp(inter - m_t)
        num = sc * _dot(qb, c_s[h].astype(BF16)) + _dot(s.astype(BF16), vb)
        den = sc * jnp.sum(q * n_s[h:h + 1, :], axis=-1, keepdims=True) + jnp.sum(s, axis=-1, keepdims=True)
        hh = num / jnp.maximum(jnp.abs(den), jnp.exp(-m_t))
        b_last = bcol[L - 1:L, :]
        wlog = b_last - bcol + igcol
        m_new = jnp.maximum(b_last + m_prev, jnp.max(wlog, axis=0, keepdims=True))
        kw = k * jnp.exp(wlog - m_new)
        dec = jnp.exp(b_last + m_prev - m_new)
        c_s[h] = dec * c_s[h] + _dot_tn(kw.astype(BF16), vb)
        n_s[h:h + 1, :] = dec * n_s[h:h + 1, :] + jnp.sum(kw, axis=0, keepdims=True)
        m_s[h:h + 1, :] = jnp.broadcast_to(m_new, (1, LANES))
        hn = hh * lax.rsqrt(jnp.mean(hh * hh, axis=-1, keepdims=True) + RMS_EPS)
        o_ref[:, sl] = (_sigmoid(z_ref[:, sl]) * (hn * nw_ref[:, sl])).astype(o_ref.dtype)


def _mlstm(p1, batch, seq, conv_w, conv_b, wq, wk, wv, ig_b, fg_b, mh_w):
    L = MLSTM_CHUNK
    nc = seq // L
    W = MLSTM_WIDTH
    gb = jnp.zeros((1, LANES), F32).at[0, 0:8].set(ig_b).at[0, 8:16].set(fg_b)
    tri = (jnp.arange(L)[:, None] >= jnp.arange(L)[None, :]).astype(BF16)
    hw = lambda: pl.BlockSpec((MLSTM_HEADS, MLSTM_HEAD, MLSTM_HEAD), lambda b, i: (0, 0, 0))
    return pl.pallas_call(
        _mlstm_kernel,
        grid=(batch, nc),
        in_specs=[pl.BlockSpec((L, W), lambda b, i: (b * nc + i, 0)),
                  pl.BlockSpec((L, W), lambda b, i: (b * nc + i, 1)),
                  pl.BlockSpec((L, LANES), lambda b, i: (b * nc + i, 2 * W // LANES)),
                  pl.BlockSpec((CONV_WIDTH, W), lambda b, i: (0, 0)),
                  pl.BlockSpec((1, W), lambda b, i: (0, 0)),
                  hw(), hw(), hw(),
                  pl.BlockSpec((1, LANES), lambda b, i: (0, 0)),
                  pl.BlockSpec((1, W), lambda b, i: (0, 0)),
                  pl.BlockSpec((L, L), lambda b, i: (0, 0))],
        out_specs=pl.BlockSpec((L, W), lambda b, i: (b * nc + i, 0)),
        out_shape=jax.ShapeDtypeStruct((batch * seq, W), BF16),
        scratch_shapes=[pltpu.VMEM((L + CARRY_ROWS, W), F32),
                        pltpu.VMEM((MLSTM_HEADS, MLSTM_HEAD, MLSTM_HEAD), F32),
                        pltpu.VMEM((MLSTM_HEADS, LANES), F32),
                        pltpu.VMEM((MLSTM_HEADS, LANES), F32)],
        compiler_params=_cparams(("parallel", "arbitrary")),
        name="mlstm",
    )(p1, p1, p1, conv_w, conv_b.reshape(1, W), wq.astype(BF16), wk.astype(BF16),
      wv.astype(BF16), gb, mh_w.reshape(1, W), tri)


def _moe_kernel(x_ref, g_ref, rt_ref, w1_ref, w3_ref, w2_ref, fn_ref, o_ref, xn_s, gd_s, acc_s):
    e = pl.program_id(1)
    f = pl.program_id(2)

    @pl.when(jnp.logical_and(e == 0, f == 0))
    def _():
        xn = _rms(x_ref[...], g_ref[...])
        xn_s[...] = xn.astype(BF16)
        logits = _mm3(_dot, xn, rt_ref[...])
        lane = lax.broadcasted_iota(jnp.int32, logits.shape, 1)
        valid = lane < N_EXPERTS
        lg = jnp.where(valid, logits, -jnp.inf)
        v1 = jnp.max(lg, axis=-1, keepdims=True)
        i1 = jnp.min(jnp.where(lg == v1, lane, LANES), axis=-1, keepdims=True)
        lg2 = jnp.where(lane == i1, -jnp.inf, lg)
        v2 = jnp.max(lg2, axis=-1, keepdims=True)
        i2 = jnp.min(jnp.where(lg2 == v2, lane, LANES), axis=-1, keepdims=True)
        ex = jnp.exp(v2 - v1)
        g1 = 1.0 / (1.0 + ex)
        g2 = ex / (1.0 + ex)
        gd_s[...] = jnp.where(lane == i1, g1, 0.0) + jnp.where(lane == i2, g2, 0.0)
        acc_s[...] = x_ref[...]

    xn = xn_s[...]
    lane = lax.broadcasted_iota(jnp.int32, gd_s.shape, 1)
    ge = jnp.sum(jnp.where(lane == e, gd_s[...], 0.0), axis=-1, keepdims=True)
    h1 = _dot(xn, w1_ref[0])
    h3 = _dot(xn, w3_ref[0])
    act = (h1 * _sigmoid(h1) * h3).astype(BF16)
    acc_s[...] += ge * _dot(act, w2_ref[0])

    @pl.when(jnp.logical_and(e == pl.num_programs(1) - 1, f == pl.num_programs(2) - 1))
    def _():
        o_ref[...] = _rms(acc_s[...], fn_ref[...])


def _moe(x, g, router, w1, w3, w2, fnorm, tm, tf):
    t, d = x.shape
    ne = w1.shape[0]
    nf = w1.shape[2] // tf
    return pl.pallas_call(
        _moe_kernel,
        grid=(t // tm, ne, nf),
        in_specs=[pl.BlockSpec((tm, d), lambda i, e, f: (i, 0)),
                  pl.BlockSpec((1, d), lambda i, e, f: (0, 0)),
                  pl.BlockSpec((d, LANES), lambda i, e, f: (0, 0)),
                  pl.BlockSpec((1, d, tf), lambda i, e, f: (e, 0, f)),
                  pl.BlockSpec((1, d, tf), lambda i, e, f: (e, 0, f)),
                  pl.BlockSpec((1, tf, d), lambda i, e, f: (e, f, 0)),
                  pl.BlockSpec((1, d), lambda i, e, f: (0, 0))],
        out_specs=pl.BlockSpec((tm, d), lambda i, e, f: (i, 0)),
        out_shape=jax.ShapeDtypeStruct((t, d), F32),
        scratch_shapes=[pltpu.VMEM((tm, d), BF16), pltpu.VMEM((tm, LANES), F32),
                        pltpu.VMEM((tm, d), F32)],
        compiler_params=_cparams(("parallel", "arbitrary", "arbitrary")),
        name="moe_final",
    )(x, g.reshape(1, d), router, w1, w3, w2, fnorm.reshape(1, d))


def _block_diag(w):
    g, a, b = w.shape
    eye = jnp.eye(g, dtype=w.dtype)
    return (eye[:, None, :, None] * w[:, :, None, :]).reshape(g * a, g * b)


def kernel(x, l0_norm_mix, l0_w_in, l0_conv_w, l0_conv_b, l0_gate_a_w, l0_gate_a_b, l0_gate_x_w, l0_gate_x_b, l0_lru_lambda, l0_shift_mu, l0_w0, l0_w_up, l0_a0, l0_a_up, l0_g_up, l0_k_k, l0_k_a, l0_r_k, l0_ln_x_w, l0_ln_x_b, l0_w_out, l0_norm_ffn, l0_ffn_w1, l0_ffn_w3, l0_ffn_w2, l1_norm_mix, l1_w_in, l1_conv_w, l1_conv_b, l1_wq, l1_wk, l1_wv, l1_ig_b, l1_fg_b, l1_mh_norm_w, l1_w_out, l1_norm_ffn, l1_router, l1_moe_w1, l1_moe_w3, l1_moe_w2, final_norm):
    batch, seq, d = x.shape
    t = batch * seq
    xt = x.reshape(t, d)
    tm = min(512, t)

    w_in0 = l0_w_in.astype(BF16)
    p_lru = _norm_mm(xt, l0_norm_mix, w_in0[:, :2 * LRU_WIDTH], tm, "l0_in_lru")
    p_rw = _norm_mm(xt, l0_norm_mix, w_in0[:, 2 * LRU_WIDTH:], tm, "l0_in_rwkv")
    y_lru = _lru(p_lru, batch, seq, l0_conv_w, l0_conv_b,
                 _block_diag(l0_gate_a_w).astype(BF16), l0_gate_a_b,
                 _block_diag(l0_gate_x_w).astype(BF16), l0_gate_x_b, l0_lru_lambda,
                 min(512, seq))
    y_rw = _rwkv(p_rw, batch, seq, l0_shift_mu, l0_w0, l0_w_up, l0_a0, l0_a_up, l0_g_up,
                 l0_k_k, l0_k_a, l0_r_k.reshape(-1), l0_ln_x_w, l0_ln_x_b)
    w_out0 = l0_w_out.astype(BF16)
    x1 = _mm_res(xt, [(y_lru, w_out0[:LRU_WIDTH]), (y_rw, w_out0[LRU_WIDTH:])], tm, "l0_out")
    x2 = _ffn(x1, l0_norm_ffn, l0_ffn_w1.astype(BF16), l0_ffn_w3.astype(BF16),
              l0_ffn_w2.astype(BF16), tm, D_FF // 2)

    n_in1 = 2 * MLSTM_WIDTH + LANES
    w_in1 = jnp.zeros((d, n_in1), F32).at[:, :l1_w_in.shape[1]].set(l1_w_in).astype(BF16)
    p1 = _norm_mm(x2, l1_norm_mix, w_in1, tm, "l1_in")
    h1 = _mlstm(p1, batch, seq, l1_conv_w, l1_conv_b, l1_wq, l1_wk, l1_wv, l1_ig_b, l1_fg_b,
                l1_mh_norm_w)
    x3 = _mm_res(x2, [(h1, l1_w_out.astype(BF16))], tm, "l1_out")
    router = jnp.zeros((d, LANES), F32).at[:, :N_EXPERTS].set(l1_router)
    out = _moe(x3, l1_norm_ffn, router, l1_moe_w1.astype(BF16), l1_moe_w3.astype(BF16),
               l1_moe_w2.astype(BF16), final_norm, tm, D_FF // 2)
    return out.reshape(batch, seq, d)
```

```python
import functools

import jax
import jax.numpy as jnp
from jax import lax
from jax.experimental import pallas as pl
from jax.experimental.pallas import tpu as pltpu

F32 = jnp.float32
BF16 = jnp.bfloat16

D_MODEL = 1024
LRU_WIDTH = 512
LRU_BLOCKS = 8
LRU_C = 8.0
CONV_WIDTH = 4
RWKV_HEADS = 8
RWKV_HEAD = 64
RWKV_WIDTH = 512
LN_X_EPS = 1e-5 * RWKV_HEAD
MLSTM_HEADS = 8
MLSTM_HEAD = 128
MLSTM_WIDTH = 1024
MLSTM_CHUNK = 128
D_FF = 2816
N_EXPERTS = 8
TOP_K = 2
MOE_BLOCK = 512
MOE_F_SPLIT = 2
RMS_EPS = 1e-6
RW_IN = 3 * RWKV_WIDTH + 64 + 64 + 128
RW_CHUNK = 64
LANES = 128
CARRY_ROWS = 8
VMEM_LIMIT = 56 * 1024 * 1024


def _cparams(sem):
    return pltpu.CompilerParams(dimension_semantics=sem, vmem_limit_bytes=VMEM_LIMIT)


def _rms(x, w):
    return x * lax.rsqrt(jnp.mean(x * x, axis=-1, keepdims=True) + RMS_EPS) * w


def _sigmoid(x):
    return 1.0 / (1.0 + jnp.exp(-x))


def _softplus(x):
    return jnp.maximum(x, 0.0) + jnp.log(1.0 + jnp.exp(-jnp.abs(x)))


def _dot(a, b):
    return jnp.dot(a, b, preferred_element_type=F32)


def _dot_nt(a, b):
    return lax.dot_general(a, b, (((1,), (1,)), ((), ())), preferred_element_type=F32)


def _dot_tn(a, b):
    return lax.dot_general(a, b, (((0,), (0,)), ((), ())), preferred_element_type=F32)


def _split2(x):
    hi = x.astype(BF16)
    lo = (x - hi.astype(F32)).astype(BF16)
    return hi, lo


def _split3(x):
    hi = x.astype(BF16)
    r = x - hi.astype(F32)
    mid = r.astype(BF16)
    lo = (r - mid.astype(F32)).astype(BF16)
    return hi, mid, lo


def _mm3(fn, a, b):
    ah, al = _split2(a)
    bh, bl = _split2(b)
    return fn(ah, bh) + fn(al, bh) + fn(ah, bl)


def _mm_exact_lhs(fn, a_bf16, b):
    h, m, l = _split3(b)
    return fn(a_bf16, h) + fn(a_bf16, m) + fn(a_bf16, l)


def _mm_exact_rhs(fn, a, b_bf16):
    h, m, l = _split3(a)
    return fn(h, b_bf16) + fn(m, b_bf16) + fn(l, b_bf16)


def _shift_hist(buf_ref, x, first):
    n = x.shape[0]

    @pl.when(first)
    def _():
        buf_ref[0:CARRY_ROWS, :] = jnp.zeros((CARRY_ROWS, x.shape[1]), F32)

    @pl.when(jnp.logical_not(first))
    def _():
        buf_ref[0:CARRY_ROWS, :] = buf_ref[n:n + CARRY_ROWS, :]

    buf_ref[CARRY_ROWS:CARRY_ROWS + n, :] = x


def _causal_conv(buf_ref, n, w, b):
    acc = b
    for j in range(CONV_WIDTH):
        off = CARRY_ROWS - (CONV_WIDTH - 1) + j
        acc = acc + w[j:j + 1, :] * buf_ref[off:off + n, :]
    return acc


def _norm_mm_kernel(x_ref, g_ref, w_ref, o_ref):
    xn = _rms(x_ref[...], g_ref[...]).astype(BF16)
    o_ref[...] = _dot(xn, w_ref[...]).astype(o_ref.dtype)


def _norm_mm(x, g, w, tm, name):
    t, d = x.shape
    n = w.shape[1]
    return pl.pallas_call(
        _norm_mm_kernel,
        grid=(t // tm,),
        in_specs=[pl.BlockSpec((tm, d), lambda i: (i, 0)),
                  pl.BlockSpec((1, d), lambda i: (0, 0)),
                  pl.BlockSpec((d, n), lambda i: (0, 0))],
        out_specs=pl.BlockSpec((tm, n), lambda i: (i, 0)),
        out_shape=jax.ShapeDtypeStruct((t, n), F32),
        compiler_params=_cparams(("parallel",)),
        name=name,
    )(x, g.reshape(1, d), w)


def _mm_res_kernel(*refs):
    x_ref, o_ref = refs[0], refs[-1]
    acc = x_ref[...]
    for i in range(1, len(refs) - 1, 2):
        acc = acc + _dot(refs[i][...], refs[i + 1][...])
    o_ref[...] = acc


def _mm_res(x, pairs, tm, name):
    t, d = x.shape
    in_specs = [pl.BlockSpec((tm, d), lambda i: (i, 0))]
    args = [x]
    for y, w in pairs:
        in_specs.append(pl.BlockSpec((tm, y.shape[1]), lambda i: (i, 0)))
        in_specs.append(pl.BlockSpec(w.shape, lambda i: (0, 0)))
        args += [y, w]
    return pl.pallas_call(
        _mm_res_kernel,
        grid=(t // tm,),
        in_specs=in_specs,
        out_specs=pl.BlockSpec((tm, d), lambda i: (i, 0)),
        out_shape=jax.ShapeDtypeStruct((t, d), F32),
        compiler_params=_cparams(("parallel",)),
        name=name,
    )(*args)


def _lru_kernel(p_ref, cw_ref, cb_ref, wa_ref, ba_ref, wx_ref, bx_ref, lam_ref, o_ref,
                xbuf, abuf, bbuf, hbuf, hcar):
    i = pl.program_id(1)
    n = p_ref.shape[0]
    first = i == 0
    _shift_hist(xbuf, p_ref[:, 0:LRU_WIDTH], first)
    xc = _causal_conv(xbuf, n, cw_ref[...], cb_ref[...])
    xcb = xc.astype(BF16)
    r = _sigmoid(_dot(xcb, wa_ref[...]) + ba_ref[...])
    ig = _sigmoid(_dot(xcb, wx_ref[...]) + bx_ref[...])
    log_a = (-LRU_C) * r * _softplus(-lam_ref[...])
    a = jnp.exp(log_a)
    mult = jnp.sqrt(1.0 - jnp.exp(2.0 * log_a))
    row = lax.broadcasted_iota(jnp.int32, (n, 1), 0)
    mult = jnp.where(jnp.logical_and(first, row == 0), 1.0, mult)
    abuf[...] = a
    bbuf[...] = mult * ig * xc

    @pl.when(first)
    def _():
        hcar[...] = jnp.zeros(hcar.shape, F32)

    def step(t, h):
        h = abuf[pl.ds(t, 1), :] * h + bbuf[pl.ds(t, 1), :]
        hbuf[pl.ds(t, 1), :] = h
        return h

    h = lax.fori_loop(0, n, step, hcar[0:1, :], unroll=8)
    hcar[0:1, :] = h
    gate = p_ref[:, LRU_WIDTH:2 * LRU_WIDTH]
    gelu = 0.5 * gate * (1.0 + jnp.tanh(0.7978845608028654 * (gate + 0.044715 * gate * gate * gate)))
    o_ref[...] = (hbuf[...] * gelu).astype(o_ref.dtype)


def _lru(p_lru, batch, seq, conv_w, conv_b, wa, ba, wx, bx, lam, tb):
    nb = seq // tb
    c = LRU_WIDTH
    vec = lambda: pl.BlockSpec((1, c), lambda b, i: (0, 0))
    return pl.pallas_call(
        _lru_kernel,
        grid=(batch, nb),
        in_specs=[pl.BlockSpec((tb, 2 * c), lambda b, i: (b * nb + i, 0)),
                  pl.BlockSpec((CONV_WIDTH, c), lambda b, i: (0, 0)), vec(),
                  pl.BlockSpec((c, c), lambda b, i: (0, 0)), vec(),
                  pl.BlockSpec((c, c), lambda b, i: (0, 0)), vec(), vec()],
        out_specs=pl.BlockSpec((tb, c), lambda b, i: (b * nb + i, 0)),
        out_shape=jax.ShapeDtypeStruct((batch * seq, c), BF16),
        scratch_shapes=[pltpu.VMEM((tb + CARRY_ROWS, c), F32), pltpu.VMEM((tb, c), F32),
                        pltpu.VMEM((tb, c), F32), pltpu.VMEM((tb, c), F32),
                        pltpu.VMEM((CARRY_ROWS, c), F32)],
        compiler_params=_cparams(("parallel", "arbitrary")),
        name="rg_lru",
    )(p_lru, conv_w, conv_b.reshape(1, c), wa, ba.reshape(1, c), wx, bx.reshape(1, c),
      lam.reshape(1, c))


def _blk(x, masks):
    return jnp.concatenate([x * mk for mk in masks], axis=0)


def _rwkv_kernel(p_ref, mu_ref, w0_ref, wup_ref, a0_ref, aup_ref, gup_ref, kk_ref, ka_ref,
                 rk_ref, lnw_ref, lnb_ref, seg_ref, tri_ref, o_ref, pbuf, state):
    c = pl.program_id(1)
    L = RW_CHUNK
    W = RWKV_WIDTH
    first = c == 0
    p = p_ref[...]
    _shift_hist(pbuf, p, first)
    prev = pbuf[CARRY_ROWS - 1:CARRY_ROWS - 1 + L, :]
    ps = p + mu_ref[...] * (prev - p)
    r = ps[:, 0:W]
    k = ps[:, W:2 * W]
    v = ps[:, 2 * W:3 * W]
    x2 = ps[:, 3 * W:3 * W + LANES]
    xg = ps[:, 3 * W + LANES:3 * W + 2 * LANES]
    wl = w0_ref[...] + _dot(jnp.tanh(x2).astype(BF16), wup_ref[...])
    lw = -jnp.exp(-_softplus(-wl) - 0.5)
    a = _sigmoid(a0_ref[...] + _dot(x2.astype(BF16), aup_ref[...]))
    g = _dot(_sigmoid(xg).astype(BF16), gup_ref[...])
    seg = seg_ref[...]
    kk = k * kk_ref[...]
    kk = kk / jnp.maximum(jnp.sqrt(_mm_exact_rhs(_dot, kk * kk, seg)), 1e-12)
    k2 = k * (1.0 + (a - 1.0) * ka_ref[...])

    cw = _mm_exact_lhs(_dot, tri_ref[...], lw)
    cw_end = cw[L - 1:L, :]
    w_in = jnp.exp(cw)
    w_inv = jnp.exp(-cw)
    w_end = jnp.exp(cw_end - cw)
    kka = kk * a
    a_t = -kk * jnp.exp(cw - lw)
    b_t = kka * w_inv
    k_t = k2 * w_inv
    r_t = r * w_in
    b_bar = kka * w_end
    k_bar = k2 * w_end
    w_tot = jnp.exp(cw_end)

    @pl.when(first)
    def _():
        state[...] = jnp.zeros(state.shape, F32)

    lane = lax.broadcasted_iota(jnp.int32, (1, LANES), 1)
    m1 = [(lane < RWKV_HEAD).astype(F32), (lane >= RWKV_HEAD).astype(F32)]
    m2 = [jnp.concatenate([mk, mk], axis=1) for mk in m1]
    ti = lax.broadcasted_iota(jnp.int32, (L, LANES), 0)
    si = lax.broadcasted_iota(jnp.int32, (L, LANES), 1) % RWKV_HEAD
    strict = si < ti
    incl = si <= ti
    bi = lax.broadcasted_iota(jnp.int32, (LANES, LANES), 0) // RWKV_HEAD
    bj = lax.broadcasted_iota(jnp.int32, (LANES, LANES), 1) // RWKV_HEAD
    bd = bi == bj
    mm = functools.partial(_mm3, _dot)
    mm_nt = functools.partial(_mm3, _dot_nt)
    mm_tn = functools.partial(_mm3, _dot_tn)

    ys = []
    for pr in range(RWKV_HEADS // 2):
        sl = slice(pr * LANES, (pr + 1) * LANES)
        ar = jnp.concatenate([a_t[:, sl], r_t[:, sl]], axis=0)
        sb = mm_nt(ar, _blk(b_t[:, sl], m1))
        sk = mm_nt(ar, _blk(k_t[:, sl], m1))
        nmat = jnp.where(strict, sb[0:L], 0.0)
        vv = v[:, sl]
        rhs = jnp.concatenate([a_t[:, sl], mm(jnp.where(strict, sk[0:L], 0.0), _blk(vv, m1))], axis=1)
        x = rhs + mm(nmat, _blk(rhs, m2))
        pw = nmat
        for _ in range(5):
            pw = mm(pw, _blk(pw, m1))
            x = x + mm(pw, _blk(x, m2))
        corr = mm(jnp.where(incl, sb[L:2 * L], 0.0), _blk(x, m2))
        r_hat = r_t[:, sl] + corr[:, 0:LANES]
        y0 = corr[:, LANES:2 * LANES] + mm(jnp.where(incl, sk[L:2 * L], 0.0), _blk(vv, m1))
        a_hat = x[:, 0:LANES]
        u0 = x[:, LANES:2 * LANES]
        gmat = jnp.where(bd, mm_tn(a_hat, b_bar[:, sl]), 0.0)
        hmat = jnp.where(bd, mm_tn(u0, b_bar[:, sl]) + mm_tn(vv, k_bar[:, sl]), 0.0)
        s0 = state[pr]
        ys.append(mm_nt(r_hat, s0) + y0)
        state[pr] = s0 * w_tot[:, sl] + mm(s0, gmat) + hmat
    y = jnp.concatenate(ys, axis=1)

    inv = 1.0 / RWKV_HEAD
    mean = _mm_exact_rhs(_dot, y, seg) * inv
    yc = y - mean
    var = _mm_exact_rhs(_dot, yc * yc, seg) * inv
    yn = yc * lax.rsqrt(var + LN_X_EPS) * lnw_ref[...] + lnb_ref[...]
    bonus = _mm_exact_rhs(_dot, r * k2 * rk_ref[...], seg) * v
    o_ref[...] = ((yn + bonus) * g).astype(o_ref.dtype)


def _rwkv(p_rw, batch, seq, mu, w0, w_up, a0, a_up, g_up, k_k, k_a, r_k, ln_w, ln_b):
    L = RW_CHUNK
    nc = seq // L
    W = RWKV_WIDTH
    wup = jnp.zeros((LANES, W), F32).at[0:64].set(w_up).astype(BF16)
    aup = jnp.zeros((LANES, W), F32).at[64:128].set(a_up).astype(BF16)
    hid = jnp.arange(W) // RWKV_HEAD
    seg = (hid[:, None] == hid[None, :]).astype(BF16)
    tri = (jnp.arange(L)[:, None] >= jnp.arange(L)[None, :]).astype(BF16)
    vec = lambda: pl.BlockSpec((1, W), lambda b, i: (0, 0))
    mat = lambda s: pl.BlockSpec(s, lambda b, i: (0, 0))
    return pl.pallas_call(
        _rwkv_kernel,
        grid=(batch, nc),
        in_specs=[pl.BlockSpec((L, RW_IN), lambda b, i: (b * nc + i, 0)),
                  mat((1, RW_IN)), vec(), mat((LANES, W)), vec(), mat((LANES, W)),
                  mat((LANES, W)), vec(), vec(), vec(), vec(), vec(), mat((W, W)), mat((L, L))],
        out_specs=pl.BlockSpec((L, W), lambda b, i: (b * nc + i, 0)),
        out_shape=jax.ShapeDtypeStruct((batch * seq, W), BF16),
        scratch_shapes=[pltpu.VMEM((L + CARRY_ROWS, RW_IN), F32),
                        pltpu.VMEM((RWKV_HEADS // 2, LANES, LANES), F32)],
        compiler_params=_cparams(("parallel", "arbitrary")),
        name="rwkv7",
    )(p_rw, mu.reshape(1, RW_IN), w0.reshape(1, W), wup, a0.reshape(1, W), aup,
      g_up.astype(BF16), k_k.reshape(1, W), k_a.reshape(1, W), r_k.reshape(1, W),
      ln_w.reshape(1, W), ln_b.reshape(1, W), seg, tri)


def _ffn_kernel(x_ref, g_ref, w1_ref, w3_ref, w2_ref, o_ref, xn_s, acc_s):
    f = pl.program_id(1)

    @pl.when(f == 0)
    def _():
        xn_s[...] = _rms(x_ref[...], g_ref[...]).astype(BF16)
        acc_s[...] = x_ref[...]

    xn = xn_s[...]
    h1 = _dot(xn, w1_ref[...])
    h3 = _dot(xn, w3_ref[...])
    act = (h1 * _sigmoid(h1) * h3).astype(BF16)
    acc_s[...] += _dot(act, w2_ref[...])

    @pl.when(f == pl.num_programs(1) - 1)
    def _():
        o_ref[...] = acc_s[...]


def _ffn(x, g, w1, w3, w2, tm, tf):
    t, d = x.shape
    nf = w1.shape[1] // tf
    return pl.pallas_call(
        _ffn_kernel,
        grid=(t // tm, nf),
        in_specs=[pl.BlockSpec((tm, d), lambda i, f: (i, 0)),
                  pl.BlockSpec((1, d), lambda i, f: (0, 0)),
                  pl.BlockSpec((d, tf), lambda i, f: (0, f)),
                  pl.BlockSpec((d, tf), lambda i, f: (0, f)),
                  pl.BlockSpec((tf, d), lambda i, f: (f, 0))],
        out_specs=pl.BlockSpec((tm, d), lambda i, f: (i, 0)),
        out_shape=jax.ShapeDtypeStruct((t, d), F32),
        scratch_shapes=[pltpu.VMEM((tm, d), BF16), pltpu.VMEM((tm, d), F32)],
        compiler_params=_cparams(("parallel", "arbitrary")),
        name="ffn_swiglu",
    )(x, g.reshape(1, d), w1, w3, w2)


def _mlstm_kernel(xm_ref, z_ref, gt_ref, cw_ref, cb_ref, wq_ref, wk_ref, wv_ref, gb_ref,
                  nw_ref, tri_ref, o_ref, xbuf, c_s, n_s, m_s):
    ci = pl.program_id(1)
    L = MLSTM_CHUNK
    dh = MLSTM_HEAD
    first = ci == 0
    xm = xm_ref[...]
    _shift_hist(xbuf, xm, first)
    xc = _causal_conv(xbuf, L, cw_ref[...], cb_ref[...])
    xc = xc * _sigmoid(xc)
    gl = gt_ref[...] + gb_ref[...]
    lf = jnp.minimum(gl, 0.0) - jnp.log(1.0 + jnp.exp(-jnp.abs(gl)))
    bcum = _mm_exact_lhs(_dot, tri_ref[...], lf)
    gl_t = gl.T
    bcum_t = bcum.T

    @pl.when(first)
    def _():
        c_s[...] = jnp.zeros(c_s.shape, F32)
        n_s[...] = jnp.zeros(n_s.shape, F32)
        m_s[...] = jnp.zeros(m_s.shape, F32)

    ti = lax.broadcasted_iota(jnp.int32, (L, L), 0)
    si = lax.broadcasted_iota(jnp.int32, (L, L), 1)
    causal = si <= ti
    scale = dh ** -0.5
    for h in range(MLSTM_HEADS):
        sl = slice(h * dh, (h + 1) * dh)
        xs = xc[:, sl].astype(BF16)
        q = _dot(xs, wq_ref[h]) * scale
        k = _dot(xs, wk_ref[h])
        v = _dot(xm[:, sl].astype(BF16), wv_ref[h])
        fh = MLSTM_HEADS + h
        bcol = bcum[:, fh:fh + 1]
        brow = bcum_t[fh:fh + 1, :]
        igrow = gl_t[h:h + 1, :]
        igcol = gl[:, h:h + 1]
        m_prev = m_s[h:h + 1, 0:1]
        dlog = jnp.where(causal, bcol - brow + igrow, -jnp.inf)
        inter = bcol + m_prev
        m_t = jnp.maximum(inter, jnp.max(dlog, axis=-1, keepdims=True))
        dmat = jnp.exp(dlog - m_t)
        qb = q.astype(BF16)
        kb = k.astype(BF16)
        vb = v.astype(BF16)
        s = _dot_nt(qb, kb) * dmat
        sc = jnp.exp(inter - m_t)
        num = sc * _dot(qb, c_s[h].astype(BF16)) + _dot(s.astype(BF16), vb)
        den = sc * jnp.sum(q * n_s[h:h + 1, :], axis=-1, keepdims=True) + jnp.sum(s, axis=-1, keepdims=True)
        hh = num / jnp.maximum(jnp.abs(den), jnp.exp(-m_t))
        b_last = bcol[L - 1:L, :]
        wlog = b_last - bcol + igcol
        m_new = jnp.maximum(b_last + m_prev, jnp.max(wlog, axis=0, keepdims=True))
        kw = k * jnp.exp(wlog - m_new)
        dec = jnp.exp(b_last + m_prev - m_new)
        c_s[h] = dec * c_s[h] + _dot_tn(kw.astype(BF16), vb)
        n_s[h:h + 1, :] = dec * n_s[h:h + 1, :] + jnp.sum(kw, axis=0, keepdims=True)
        m_s[h:h + 1, :] = jnp.broadcast_to(m_new, (1, LANES))
        hn = hh * lax.rsqrt(jnp.mean(hh * hh, axis=-1, keepdims=True) + RMS_EPS)
        o_ref[:, sl] = (_sigmoid(z_ref[:, sl]) * (hn * nw_ref[:, sl])).astype(o_ref.dtype)


def _mlstm(p1, batch, seq, conv_w, conv_b, wq, wk, wv, ig_b, fg_b, mh_w):
    L = MLSTM_CHUNK
    nc = seq // L
    W = MLSTM_WIDTH
    gb = jnp.zeros((1, LANES), F32).at[0, 0:8].set(ig_b).at[0, 8:16].set(fg_b)
    tri = (jnp.arange(L)[:, None] >= jnp.arange(L)[None, :]).astype(BF16)
    hw = lambda: pl.BlockSpec((MLSTM_HEADS, MLSTM_HEAD, MLSTM_HEAD), lambda b, i: (0, 0, 0))
    return pl.pallas_call(
        _mlstm_kernel,
        grid=(batch, nc),
        in_specs=[pl.BlockSpec((L, W), lambda b, i: (b * nc + i, 0)),
                  pl.BlockSpec((L, W), lambda b, i: (b * nc + i, 1)),
                  pl.BlockSpec((L, LANES), lambda b, i: (b * nc + i, 2 * W // LANES)),
                  pl.BlockSpec((CONV_WIDTH, W), lambda b, i: (0, 0)),
                  pl.BlockSpec((1, W), lambda b, i: (0, 0)),
                  hw(), hw(), hw(),
                  pl.BlockSpec((1, LANES), lambda b, i: (0, 0)),
                  pl.BlockSpec((1, W), lambda b, i: (0, 0)),
                  pl.BlockSpec((L, L), lambda b, i: (0, 0))],
        out_specs=pl.BlockSpec((L, W), lambda b, i: (b * nc + i, 0)),
        out_shape=jax.ShapeDtypeStruct((batch * seq, W), BF16),
        scratch_shapes=[pltpu.VMEM((L + CARRY_ROWS, W), F32),
                        pltpu.VMEM((MLSTM_HEADS, MLSTM_HEAD, MLSTM_HEAD), F32),
                        pltpu.VMEM((MLSTM_HEADS, LANES), F32),
                        pltpu.VMEM((MLSTM_HEADS, LANES), F32)],
        compiler_params=_cparams(("parallel", "arbitrary")),
        name="mlstm",
    )(p1, p1, p1, conv_w, conv_b.reshape(1, W), wq.astype(BF16), wk.astype(BF16),
      wv.astype(BF16), gb, mh_w.reshape(1, W), tri)


def _router_kernel(x_ref, g_ref, rt_ref, o_ref):
    xn = _rms(x_ref[...], g_ref[...])
    logits = _mm3(_dot, xn, rt_ref[...])
    lane = lax.broadcasted_iota(jnp.int32, logits.shape, 1)
    lg = jnp.where(lane < N_EXPERTS, logits, -jnp.inf)
    v1 = jnp.max(lg, axis=-1, keepdims=True)
    i1 = jnp.min(jnp.where(lg == v1, lane, LANES), axis=-1, keepdims=True)
    lg2 = jnp.where(lane == i1, -jnp.inf, lg)
    v2 = jnp.max(lg2, axis=-1, keepdims=True)
    i2 = jnp.min(jnp.where(lg2 == v2, lane, LANES), axis=-1, keepdims=True)
    ex = jnp.exp(v2 - v1)
    g1 = 1.0 / (1.0 + ex)
    g2 = ex / (1.0 + ex)
    out = jnp.where(lane == 0, i1.astype(F32), 0.0)
    out = jnp.where(lane == 1, i2.astype(F32), out)
    out = jnp.where(lane == 2, g1, out)
    o_ref[...] = jnp.where(lane == 3, g2, out)


def _router(x, g, router, tm):
    t, d = x.shape
    return pl.pallas_call(
        _router_kernel,
        grid=(t // tm,),
        in_specs=[pl.BlockSpec((tm, d), lambda i: (i, 0)),
                  pl.BlockSpec((1, d), lambda i: (0, 0)),
                  pl.BlockSpec((d, LANES), lambda i: (0, 0))],
        out_specs=pl.BlockSpec((tm, LANES), lambda i: (i, 0)),
        out_shape=jax.ShapeDtypeStruct((t, LANES), F32),
        compiler_params=_cparams(("parallel",)),
        name="moe_router",
    )(x, g.reshape(1, d), router)


def _route_plan(route, blk):
    e = route[:, 0:TOP_K].astype(jnp.int32).reshape(-1)
    m = e.shape[0]
    onehot = (e[:, None] == jnp.arange(N_EXPERTS, dtype=jnp.int32)[None, :]).astype(jnp.int32)
    csum = jnp.cumsum(onehot, axis=0)
    rank = jnp.sum((csum - onehot) * onehot, axis=1)
    nb = (csum[-1] + blk - 1) // blk
    bend = jnp.cumsum(nb)
    dest = jnp.sum(onehot * (bend - nb)[None, :], axis=1) * blk + rank
    nblk = m // blk + N_EXPERTS
    n_used = bend[-1]
    bidx = jnp.minimum(jnp.arange(nblk, dtype=jnp.int32), n_used - 1)
    blk_e = jnp.minimum(jnp.searchsorted(bend, bidx, side="right"), N_EXPERTS - 1).astype(jnp.int32)
    return dest.astype(jnp.int32), blk_e, n_used.reshape(1).astype(jnp.int32), nblk


def _row_copy(src, src_row, dst, dst_row, sem):
    return pltpu.make_async_copy(src.at[pl.ds(src_row, 1), :], dst.at[pl.ds(dst_row, 1), :], sem)


def _dispatch_kernel(dest_ref, x_ref, g_ref, buf_in, buf_out, xn_s, sem):
    del buf_in
    tm = x_ref.shape[0]
    xn_s[...] = _rms(x_ref[...], g_ref[...])

    def copy(j):
        return _row_copy(xn_s, j // TOP_K, buf_out, dest_ref[0, 0, j], sem.at[0])

    def start(j, c):
        copy(j).start()
        return c

    def wait(j, c):
        copy(j).wait()
        return c

    lax.fori_loop(0, TOP_K * tm, start, 0, unroll=8)
    lax.fori_loop(0, TOP_K * tm, wait, 0, unroll=8)


def _dispatch(x, g, dest, rows, tm):
    t, d = x.shape
    nt = t // tm
    return pl.pallas_call(
        _dispatch_kernel,
        grid=(nt,),
        in_specs=[pl.BlockSpec((1, 1, TOP_K * tm), lambda i: (i, 0, 0), memory_space=pltpu.SMEM),
                  pl.BlockSpec((tm, d), lambda i: (i, 0)),
                  pl.BlockSpec((1, d), lambda i: (0, 0)),
                  pl.BlockSpec(memory_space=pl.ANY)],
        out_specs=pl.BlockSpec(memory_space=pl.ANY),
        out_shape=jax.ShapeDtypeStruct((rows, d), F32),
        scratch_shapes=[pltpu.VMEM((tm, d), F32), pltpu.SemaphoreType.DMA((1,))],
        input_output_aliases={3: 0},
        compiler_params=_cparams(("arbitrary",)),
        name="moe_dispatch",
    )(dest.reshape(nt, 1, TOP_K * tm), x, g.reshape(1, d), jnp.zeros((rows, d), F32))


def _expert_kernel(be_ref, nu_ref, x_ref, w1_ref, w3_ref, w2_ref, y_ref):
    del be_ref
    used = pl.program_id(0) < nu_ref[0]

    @pl.when(jnp.logical_not(used))
    def _():
        y_ref[...] = jnp.zeros(y_ref.shape, F32)

    @pl.when(used)
    def _():
        x = x_ref[...].astype(BF16)
        acc = None
        for f in range(MOE_F_SPLIT):
            sl = slice(f * (D_FF // MOE_F_SPLIT), (f + 1) * (D_FF // MOE_F_SPLIT))
            h1 = _dot(x, w1_ref[0, :, sl])
            h3 = _dot(x, w3_ref[0, :, sl])
            act = (h1 * _sigmoid(h1) * h3).astype(BF16)
            part = _dot(act, w2_ref[0, sl, :])
            acc = part if acc is None else acc + part
        y_ref[...] = acc


def _experts(buf, blk_e, n_used, w1, w3, w2, blk):
    rows, d = buf.shape
    f = w1.shape[2]
    once = pl.Buffered(1)
    grid_spec = pltpu.PrefetchScalarGridSpec(
        num_scalar_prefetch=2,
        grid=(rows // blk,),
        in_specs=[pl.BlockSpec((blk, d), lambda b, be, nu: (b, 0)),
                  pl.BlockSpec((1, d, f), lambda b, be, nu: (be[b], 0, 0), pipeline_mode=once),
                  pl.BlockSpec((1, d, f), lambda b, be, nu: (be[b], 0, 0), pipeline_mode=once),
                  pl.BlockSpec((1, f, d), lambda b, be, nu: (be[b], 0, 0), pipeline_mode=once)],
        out_specs=pl.BlockSpec((blk, d), lambda b, be, nu: (b, 0)))
    return pl.pallas_call(
        _expert_kernel,
        grid_spec=grid_spec,
        out_shape=jax.ShapeDtypeStruct((rows, d), F32),
        compiler_params=_cparams(("arbitrary",)),
        name="moe_experts",
    )(blk_e, n_used, buf, w1, w3, w2)


def _combine_kernel(dest_ref, x_ref, rt_ref, fn_ref, y_hbm, o_ref, ya, yb, sem):
    tm = x_ref.shape[0]

    def copies(r):
        return (_row_copy(y_hbm, dest_ref[0, 0, TOP_K * r], ya, r, sem.at[0]),
                _row_copy(y_hbm, dest_ref[0, 0, TOP_K * r + 1], yb, r, sem.at[1]))

    def start(r, c):
        for cp in copies(r):
            cp.start()
        return c

    def wait(r, c):
        for cp in copies(r):
            cp.wait()
        return c

    lax.fori_loop(0, tm, start, 0, unroll=8)
    lax.fori_loop(0, tm, wait, 0, unroll=8)
    rt = rt_ref[...]
    lane = lax.broadcasted_iota(jnp.int32, rt.shape, 1)
    g1 = jnp.sum(jnp.where(lane == 2, rt, 0.0), axis=-1, keepdims=True)
    g2 = jnp.sum(jnp.where(lane == 3, rt, 0.0), axis=-1, keepdims=True)
    o_ref[...] = _rms(x_ref[...] + (g1 * ya[...] + g2 * yb[...]), fn_ref[...])


def _combine(x, route, fnorm, y, dest, tm):
    t, d = x.shape
    nt = t // tm
    return pl.pallas_call(
        _combine_kernel,
        grid=(nt,),
        in_specs=[pl.BlockSpec((1, 1, TOP_K * tm), lambda i: (i, 0, 0), memory_space=pltpu.SMEM),
                  pl.BlockSpec((tm, d), lambda i: (i, 0)),
                  pl.BlockSpec((tm, LANES), lambda i: (i, 0)),
                  pl.BlockSpec((1, d), lambda i: (0, 0)),
                  pl.BlockSpec(memory_space=pl.ANY)],
        out_specs=pl.BlockSpec((tm, d), lambda i: (i, 0)),
        out_shape=jax.ShapeDtypeStruct((t, d), F32),
        scratch_shapes=[pltpu.VMEM((tm, d), F32), pltpu.VMEM((tm, d), F32),
                        pltpu.SemaphoreType.DMA((2,))],
        compiler_params=_cparams(("arbitrary",)),
        name="moe_combine",
    )(dest.reshape(nt, 1, TOP_K * tm), x, route, fnorm.reshape(1, d), y)


def _moe(x, g, router, w1, w3, w2, fnorm, tm):
    route = _router(x, g, router, tm)
    blk = min(MOE_BLOCK, x.shape[0])
    dest, blk_e, n_used, nblk = _route_plan(route, blk)
    buf = _dispatch(x, g, dest, nblk * blk, tm)
    y = _experts(buf, blk_e, n_used, w1, w3, w2, blk)
    return _combine(x, route, fnorm, y, dest, tm)


def _block_diag(w):
    g, a, b = w.shape
    eye = jnp.eye(g, dtype=w.dtype)
    return (eye[:, None, :, None] * w[:, :, None, :]).reshape(g * a, g * b)


def kernel(x, l0_norm_mix, l0_w_in, l0_conv_w, l0_conv_b, l0_gate_a_w, l0_gate_a_b, l0_gate_x_w, l0_gate_x_b, l0_lru_lambda, l0_shift_mu, l0_w0, l0_w_up, l0_a0, l0_a_up, l0_g_up, l0_k_k, l0_k_a, l0_r_k, l0_ln_x_w, l0_ln_x_b, l0_w_out, l0_norm_ffn, l0_ffn_w1, l0_ffn_w3, l0_ffn_w2, l1_norm_mix, l1_w_in, l1_conv_w, l1_conv_b, l1_wq, l1_wk, l1_wv, l1_ig_b, l1_fg_b, l1_mh_norm_w, l1_w_out, l1_norm_ffn, l1_router, l1_moe_w1, l1_moe_w3, l1_moe_w2, final_norm):
    batch, seq, d = x.shape
    t = batch * seq
    xt = x.reshape(t, d)
    tm = min(512, t)

    w_in0 = l0_w_in.astype(BF16)
    p_lru = _norm_mm(xt, l0_norm_mix, w_in0[:, :2 * LRU_WIDTH], tm, "l0_in_lru")
    p_rw = _norm_mm(xt, l0_norm_mix, w_in0[:, 2 * LRU_WIDTH:], tm, "l0_in_rwkv")
    y_lru = _lru(p_lru, batch, seq, l0_conv_w, l0_conv_b,
                 _block_diag(l0_gate_a_w).astype(BF16), l0_gate_a_b,
                 _block_diag(l0_gate_x_w).astype(BF16), l0_gate_x_b, l0_lru_lambda,
                 min(512, seq))
    y_rw = _rwkv(p_rw, batch, seq, l0_shift_mu, l0_w0, l0_w_up, l0_a0, l0_a_up, l0_g_up,
                 l0_k_k, l0_k_a, l0_r_k.reshape(-1), l0_ln_x_w, l0_ln_x_b)
    w_out0 = l0_w_out.astype(BF16)
    x1 = _mm_res(xt, [(y_lru, w_out0[:LRU_WIDTH]), (y_rw, w_out0[LRU_WIDTH:])], tm, "l0_out")
    x2 = _ffn(x1, l0_norm_ffn, l0_ffn_w1.astype(BF16), l0_ffn_w3.astype(BF16),
              l0_ffn_w2.astype(BF16), tm, D_FF // 2)

    n_in1 = 2 * MLSTM_WIDTH + LANES
    w_in1 = jnp.zeros((d, n_in1), F32).at[:, :l1_w_in.shape[1]].set(l1_w_in).astype(BF16)
    p1 = _norm_mm(x2, l1_norm_mix, w_in1, tm, "l1_in")
    h1 = _mlstm(p1, batch, seq, l1_conv_w, l1_conv_b, l1_wq, l1_wk, l1_wv, l1_ig_b, l1_fg_b,
                l1_mh_norm_w)
    x3 = _mm_res(x2, [(h1, l1_w_out.astype(BF16))], tm, "l1_out")
    router = jnp.zeros((d, LANES), F32).at[:, :N_EXPERTS].set(l1_router)
    out = _moe(x3, l1_norm_ffn, router, l1_moe_w1.astype(BF16), l1_moe_w3.astype(BF16),
               l1_moe_w2.astype(BF16), final_norm, tm)
    return out.reshape(batch, seq, d)
```

```python
import functools

import jax
import jax.numpy as jnp
from jax import lax
from jax.experimental import pallas as pl
from jax.experimental.pallas import tpu as pltpu

F32 = jnp.float32
BF16 = jnp.bfloat16

D_MODEL = 1024
LRU_WIDTH = 512
LRU_BLOCKS = 8
LRU_C = 8.0
CONV_WIDTH = 4
RWKV_HEADS = 8
RWKV_HEAD = 64
RWKV_WIDTH = 512
LN_X_EPS = 1e-5 * RWKV_HEAD
MLSTM_HEADS = 8
MLSTM_HEAD = 128
MLSTM_WIDTH = 1024
MLSTM_CHUNK = 128
D_FF = 2816
N_EXPERTS = 8
TOP_K = 2
MOE_BLOCK = 512
MOE_F_SPLIT = 2
RMS_EPS = 1e-6
RW_IN = 3 * RWKV_WIDTH + 64 + 64 + 128
RW_CHUNK = 64
RW_STEP_ROWS = 256
LANES = 128
CARRY_ROWS = 8
VMEM_LIMIT = 56 * 1024 * 1024


def _cparams(sem):
    return pltpu.CompilerParams(dimension_semantics=sem, vmem_limit_bytes=VMEM_LIMIT)


def _rms(x, w):
    return x * lax.rsqrt(jnp.mean(x * x, axis=-1, keepdims=True) + RMS_EPS) * w


def _sigmoid(x):
    return 1.0 / (1.0 + jnp.exp(-x))


def _softplus(x):
    return jnp.maximum(x, 0.0) + jnp.log(1.0 + jnp.exp(-jnp.abs(x)))


def _dot(a, b):
    return jnp.dot(a, b, preferred_element_type=F32)


def _dot_nt(a, b):
    return lax.dot_general(a, b, (((1,), (1,)), ((), ())), preferred_element_type=F32)


def _dot_tn(a, b):
    return lax.dot_general(a, b, (((0,), (0,)), ((), ())), preferred_element_type=F32)


def _split2(x):
    hi = x.astype(BF16)
    lo = (x - hi.astype(F32)).astype(BF16)
    return hi, lo


def _split3(x):
    hi = x.astype(BF16)
    r = x - hi.astype(F32)
    mid = r.astype(BF16)
    lo = (r - mid.astype(F32)).astype(BF16)
    return hi, mid, lo


def _mm3(fn, a, b):
    ah, al = _split2(a)
    bh, bl = _split2(b)
    return fn(ah, bh) + fn(al, bh) + fn(ah, bl)


def _mm_exact_lhs(fn, a_bf16, b):
    h, m, l = _split3(b)
    return fn(a_bf16, h) + fn(a_bf16, m) + fn(a_bf16, l)


def _mm_exact_rhs(fn, a, b_bf16):
    h, m, l = _split3(a)
    return fn(h, b_bf16) + fn(m, b_bf16) + fn(l, b_bf16)


def _shift_hist(buf_ref, x, first):
    n = x.shape[0]

    @pl.when(first)
    def _():
        buf_ref[0:CARRY_ROWS, :] = jnp.zeros((CARRY_ROWS, x.shape[1]), F32)

    @pl.when(jnp.logical_not(first))
    def _():
        buf_ref[0:CARRY_ROWS, :] = buf_ref[n:n + CARRY_ROWS, :]

    buf_ref[CARRY_ROWS:CARRY_ROWS + n, :] = x


def _causal_conv(buf_ref, n, w, b):
    acc = b
    for j in range(CONV_WIDTH):
        off = CARRY_ROWS - (CONV_WIDTH - 1) + j
        acc = acc + w[j:j + 1, :] * buf_ref[off:off + n, :]
    return acc


def _norm_mm_kernel(x_ref, g_ref, w_ref, o_ref):
    xn = _rms(x_ref[...], g_ref[...]).astype(BF16)
    o_ref[...] = _dot(xn, w_ref[...]).astype(o_ref.dtype)


def _norm_mm(x, g, w, tm, name):
    t, d = x.shape
    n = w.shape[1]
    return pl.pallas_call(
        _norm_mm_kernel,
        grid=(t // tm,),
        in_specs=[pl.BlockSpec((tm, d), lambda i: (i, 0)),
                  pl.BlockSpec((1, d), lambda i: (0, 0)),
                  pl.BlockSpec((d, n), lambda i: (0, 0))],
        out_specs=pl.BlockSpec((tm, n), lambda i: (i, 0)),
        out_shape=jax.ShapeDtypeStruct((t, n), F32),
        compiler_params=_cparams(("parallel",)),
        name=name,
    )(x, g.reshape(1, d), w)


def _mm_res_kernel(*refs):
    x_ref, o_ref = refs[0], refs[-1]
    acc = x_ref[...]
    for i in range(1, len(refs) - 1, 2):
        acc = acc + _dot(refs[i][...], refs[i + 1][...])
    o_ref[...] = acc


def _mm_res(x, pairs, tm, name):
    t, d = x.shape
    in_specs = [pl.BlockSpec((tm, d), lambda i: (i, 0))]
    args = [x]
    for y, w in pairs:
        in_specs.append(pl.BlockSpec((tm, y.shape[1]), lambda i: (i, 0)))
        in_specs.append(pl.BlockSpec(w.shape, lambda i: (0, 0)))
        args += [y, w]
    return pl.pallas_call(
        _mm_res_kernel,
        grid=(t // tm,),
        in_specs=in_specs,
        out_specs=pl.BlockSpec((tm, d), lambda i: (i, 0)),
        out_shape=jax.ShapeDtypeStruct((t, d), F32),
        compiler_params=_cparams(("parallel",)),
        name=name,
    )(*args)


def _lru_kernel(p_ref, cw_ref, cb_ref, wa_ref, ba_ref, wx_ref, bx_ref, lam_ref, o_ref,
                xbuf, abuf, bbuf, hbuf, hcar):
    i = pl.program_id(1)
    n = p_ref.shape[0]
    first = i == 0
    _shift_hist(xbuf, p_ref[:, 0:LRU_WIDTH], first)
    xc = _causal_conv(xbuf, n, cw_ref[...], cb_ref[...])
    xcb = xc.astype(BF16)
    r = _sigmoid(_dot(xcb, wa_ref[...]) + ba_ref[...])
    ig = _sigmoid(_dot(xcb, wx_ref[...]) + bx_ref[...])
    log_a = (-LRU_C) * r * _softplus(-lam_ref[...])
    a = jnp.exp(log_a)
    mult = jnp.sqrt(1.0 - jnp.exp(2.0 * log_a))
    row = lax.broadcasted_iota(jnp.int32, (n, 1), 0)
    mult = jnp.where(jnp.logical_and(first, row == 0), 1.0, mult)
    abuf[...] = a
    bbuf[...] = mult * ig * xc

    @pl.when(first)
    def _():
        hcar[...] = jnp.zeros(hcar.shape, F32)

    def step(t, h):
        h = abuf[pl.ds(t, 1), :] * h + bbuf[pl.ds(t, 1), :]
        hbuf[pl.ds(t, 1), :] = h
        return h

    h = lax.fori_loop(0, n, step, hcar[0:1, :], unroll=8)
    hcar[0:1, :] = h
    gate = p_ref[:, LRU_WIDTH:2 * LRU_WIDTH]
    gelu = 0.5 * gate * (1.0 + jnp.tanh(0.7978845608028654 * (gate + 0.044715 * gate * gate * gate)))
    o_ref[...] = (hbuf[...] * gelu).astype(o_ref.dtype)


def _lru(p_lru, batch, seq, conv_w, conv_b, wa, ba, wx, bx, lam, tb):
    nb = seq // tb
    c = LRU_WIDTH
    vec = lambda: pl.BlockSpec((1, c), lambda b, i: (0, 0))
    return pl.pallas_call(
        _lru_kernel,
        grid=(batch, nb),
        in_specs=[pl.BlockSpec((tb, 2 * c), lambda b, i: (b * nb + i, 0)),
                  pl.BlockSpec((CONV_WIDTH, c), lambda b, i: (0, 0)), vec(),
                  pl.BlockSpec((c, c), lambda b, i: (0, 0)), vec(),
                  pl.BlockSpec((c, c), lambda b, i: (0, 0)), vec(), vec()],
        out_specs=pl.BlockSpec((tb, c), lambda b, i: (b * nb + i, 0)),
        out_shape=jax.ShapeDtypeStruct((batch * seq, c), BF16),
        scratch_shapes=[pltpu.VMEM((tb + CARRY_ROWS, c), F32), pltpu.VMEM((tb, c), F32),
                        pltpu.VMEM((tb, c), F32), pltpu.VMEM((tb, c), F32),
                        pltpu.VMEM((CARRY_ROWS, c), F32)],
        compiler_params=_cparams(("parallel", "arbitrary")),
        name="rg_lru",
    )(p_lru, conv_w, conv_b.reshape(1, c), wa, ba.reshape(1, c), wx, bx.reshape(1, c),
      lam.reshape(1, c))


def _blk(x, masks):
    xb = x.astype(BF16)
    return jnp.concatenate([jnp.where(mk, xb, jnp.zeros_like(xb)) for mk in masks], axis=0)


def _rwkv_kernel(p_ref, mu_ref, w0_ref, wup_ref, a0_ref, aup_ref, gup_ref, kk_ref, ka_ref,
                 rk_ref, lnw_ref, lnb_ref, seg_ref, tri_ref, o_ref, pbuf, state):
    L = RW_CHUNK
    W = RWKV_WIDTH
    rows = p_ref.shape[0]
    first = pl.program_id(1) == 0
    p = p_ref[...]
    _shift_hist(pbuf, p, first)
    prev = pbuf[CARRY_ROWS - 1:CARRY_ROWS - 1 + rows, :]
    ps = p + mu_ref[...] * (prev - p)
    r = ps[:, 0:W]
    k = ps[:, W:2 * W]
    v = ps[:, 2 * W:3 * W]
    x2 = ps[:, 3 * W:3 * W + LANES]
    xg = ps[:, 3 * W + LANES:3 * W + 2 * LANES]
    wl = w0_ref[...] + _dot(jnp.tanh(x2).astype(BF16), wup_ref[...])
    lw = -jnp.exp(-_softplus(-wl) - 0.5)
    a = _sigmoid(a0_ref[...] + _dot(x2.astype(BF16), aup_ref[...]))
    g = _dot(_sigmoid(xg).astype(BF16), gup_ref[...])
    seg = seg_ref[...]
    kk = k * kk_ref[...]
    kk = kk / jnp.maximum(jnp.sqrt(_mm_exact_rhs(_dot, kk * kk, seg)), 1e-12)
    k2 = k * (1.0 + (a - 1.0) * ka_ref[...])

    cw = _mm_exact_lhs(_dot, tri_ref[...], lw)
    cw_end = jnp.concatenate(
        [jnp.broadcast_to(cw[c * L + L - 1:c * L + L, :], (L, W)) for c in range(rows // L)], axis=0)
    w_in = jnp.exp(cw)
    w_inv = jnp.exp(-cw)
    w_end = jnp.exp(cw_end - cw)
    kka = kk * a
    a_t = -kk * jnp.exp(cw - lw)
    b_t = kka * w_inv
    k_t = k2 * w_inv
    r_t = r * w_in
    b_bar = kka * w_end
    k_bar = k2 * w_end
    w_tot = jnp.exp(cw_end)

    @pl.when(first)
    def _():
        state[...] = jnp.zeros(state.shape, F32)

    lane = lax.broadcasted_iota(jnp.int32, (1, LANES), 1)
    m1 = [lane < RWKV_HEAD, lane >= RWKV_HEAD]
    m2 = [jnp.concatenate([mk, mk], axis=1) for mk in m1]
    ti = lax.broadcasted_iota(jnp.int32, (L, LANES), 0)
    si = lax.broadcasted_iota(jnp.int32, (L, LANES), 1) % RWKV_HEAD
    strict = si < ti
    incl = si <= ti
    bi = lax.broadcasted_iota(jnp.int32, (LANES, LANES), 0) // RWKV_HEAD
    bj = lax.broadcasted_iota(jnp.int32, (LANES, LANES), 1) // RWKV_HEAD
    bd = bi == bj

    def tril(mask, s):
        return jnp.where(mask, s, 0.0).astype(BF16)

    n_pairs = RWKV_HEADS // 2
    n_ch = rows // L
    chains = [(slice(c * L, (c + 1) * L), slice(pr * LANES, (pr + 1) * LANES))
              for c in range(n_ch) for pr in range(n_pairs)]
    idx = range(len(chains))
    a_c = [a_t[rs, sl] for rs, sl in chains]
    r_c = [r_t[rs, sl] for rs, sl in chains]
    ar = [jnp.concatenate([a_c[i], r_c[i]], axis=0).astype(BF16) for i in idx]
    sb = [_dot_nt(ar[i], _blk(b_t[rs, sl], m1)) for i, (rs, sl) in enumerate(chains)]
    sk = [_dot_nt(ar[i], _blk(k_t[rs, sl], m1)) for i, (rs, sl) in enumerate(chains)]
    vblk = [_blk(v[rs, sl], m1) for rs, sl in chains]
    rhs = [jnp.concatenate([a_c[i], _dot(tril(strict, sk[i][0:L]), vblk[i])], axis=1) for i in idx]
    pw = [tril(strict, sb[i][0:L]) for i in idx]
    x = [rhs[i] + _dot(pw[i], _blk(rhs[i], m2)) for i in idx]
    for _ in range(5):
        pw = [_dot(pw[i], _blk(pw[i], m1)).astype(BF16) for i in idx]
        x = [x[i] + _dot(pw[i], _blk(x[i], m2)) for i in idx]
    corr = [_dot(tril(incl, sb[i][L:2 * L]), _blk(x[i], m2)) for i in idx]
    r_hat = [(r_c[i] + corr[i][:, 0:LANES]).astype(BF16) for i in idx]
    y0 = [corr[i][:, LANES:2 * LANES] + _dot(tril(incl, sk[i][L:2 * L]), vblk[i]) for i in idx]
    xb = [x[i].astype(BF16) for i in idx]
    bb = [b_bar[rs, sl].astype(BF16) for rs, sl in chains]
    gmat = [jnp.where(bd, _dot_tn(xb[i][:, 0:LANES], bb[i]), 0.0).astype(BF16) for i in idx]
    hmat = [jnp.where(bd, _dot_tn(xb[i][:, LANES:2 * LANES], bb[i])
                      + _dot_tn(v[rs, sl].astype(BF16), k_bar[rs, sl].astype(BF16)), 0.0)
            for i, (rs, sl) in enumerate(chains)]

    s_cur = [state[pr] for pr in range(n_pairs)]
    y_rows = []
    for c in range(n_ch):
        ys = []
        for pr in range(n_pairs):
            i = c * n_pairs + pr
            s0 = s_cur[pr]
            s0b = s0.astype(BF16)
            ys.append(_dot_nt(r_hat[i], s0b) + y0[i])
            s_cur[pr] = s0 * w_tot[c * L:c * L + 1, chains[i][1]] + _dot(s0b, gmat[i]) + hmat[i]
        y_rows.append(jnp.concatenate(ys, axis=1))
    for pr in range(n_pairs):
        state[pr] = s_cur[pr]
    y = jnp.concatenate(y_rows, axis=0)

    inv = 1.0 / RWKV_HEAD
    mean = _mm_exact_rhs(_dot, y, seg) * inv
    yc = y - mean
    var = _mm_exact_rhs(_dot, yc * yc, seg) * inv
    yn = yc * lax.rsqrt(var + LN_X_EPS) * lnw_ref[...] + lnb_ref[...]
    bonus = _mm_exact_rhs(_dot, r * k2 * rk_ref[...], seg) * v
    o_ref[...] = ((yn + bonus) * g).astype(o_ref.dtype)


def _rwkv(p_rw, batch, seq, mu, w0, w_up, a0, a_up, g_up, k_k, k_a, r_k, ln_w, ln_b):
    L = min(RW_STEP_ROWS, seq)
    nc = seq // L
    W = RWKV_WIDTH
    wup = jnp.zeros((LANES, W), F32).at[0:64].set(w_up).astype(BF16)
    aup = jnp.zeros((LANES, W), F32).at[64:128].set(a_up).astype(BF16)
    hid = jnp.arange(W) // RWKV_HEAD
    seg = (hid[:, None] == hid[None, :]).astype(BF16)
    ri = jnp.arange(L)
    tri = jnp.logical_and(ri[:, None] >= ri[None, :],
                          ri[:, None] // RW_CHUNK == ri[None, :] // RW_CHUNK).astype(BF16)
    vec = lambda: pl.BlockSpec((1, W), lambda b, i: (0, 0))
    mat = lambda s: pl.BlockSpec(s, lambda b, i: (0, 0))
    return pl.pallas_call(
        _rwkv_kernel,
        grid=(batch, nc),
        in_specs=[pl.BlockSpec((L, RW_IN), lambda b, i: (b * nc + i, 0)),
                  mat((1, RW_IN)), vec(), mat((LANES, W)), vec(), mat((LANES, W)),
                  mat((LANES, W)), vec(), vec(), vec(), vec(), vec(), mat((W, W)), mat((L, L))],
        out_specs=pl.BlockSpec((L, W), lambda b, i: (b * nc + i, 0)),
        out_shape=jax.ShapeDtypeStruct((batch * seq, W), BF16),
        scratch_shapes=[pltpu.VMEM((L + CARRY_ROWS, RW_IN), F32),
                        pltpu.VMEM((RWKV_HEADS // 2, LANES, LANES), F32)],
        compiler_params=_cparams(("parallel", "arbitrary")),
        name="rwkv7",
    )(p_rw, mu.reshape(1, RW_IN), w0.reshape(1, W), wup, a0.reshape(1, W), aup,
      g_up.astype(BF16), k_k.reshape(1, W), k_a.reshape(1, W), r_k.reshape(1, W),
      ln_w.reshape(1, W), ln_b.reshape(1, W), seg, tri)


def _ffn_kernel(x_ref, g_ref, w1_ref, w3_ref, w2_ref, o_ref, xn_s, acc_s):
    f = pl.program_id(1)

    @pl.when(f == 0)
    def _():
        xn_s[...] = _rms(x_ref[...], g_ref[...]).astype(BF16)
        acc_s[...] = x_ref[...]

    xn = xn_s[...]
    h1 = _dot(xn, w1_ref[...])
    h3 = _dot(xn, w3_ref[...])
    act = (h1 * _sigmoid(h1) * h3).astype(BF16)
    acc_s[...] += _dot(act, w2_ref[...])

    @pl.when(f == pl.num_programs(1) - 1)
    def _():
        o_ref[...] = acc_s[...]


def _ffn(x, g, w1, w3, w2, tm, tf):
    t, d = x.shape
    nf = w1.shape[1] // tf
    return pl.pallas_call(
        _ffn_kernel,
        grid=(t // tm, nf),
        in_specs=[pl.BlockSpec((tm, d), lambda i, f: (i, 0)),
                  pl.BlockSpec((1, d), lambda i, f: (0, 0)),
                  pl.BlockSpec((d, tf), lambda i, f: (0, f)),
                  pl.BlockSpec((d, tf), lambda i, f: (0, f)),
                  pl.BlockSpec((tf, d), lambda i, f: (f, 0))],
        out_specs=pl.BlockSpec((tm, d), lambda i, f: (i, 0)),
        out_shape=jax.ShapeDtypeStruct((t, d), F32),
        scratch_shapes=[pltpu.VMEM((tm, d), BF16), pltpu.VMEM((tm, d), F32)],
        compiler_params=_cparams(("parallel", "arbitrary")),
        name="ffn_swiglu",
    )(x, g.reshape(1, d), w1, w3, w2)


def _mlstm_kernel(xm_ref, z_ref, gt_ref, cw_ref, cb_ref, wq_ref, wk_ref, wv_ref, gb_ref,
                  nw_ref, tri_ref, o_ref, xbuf, c_s, n_s, m_s):
    ci = pl.program_id(1)
    L = MLSTM_CHUNK
    dh = MLSTM_HEAD
    first = ci == 0
    xm = xm_ref[...]
    _shift_hist(xbuf, xm, first)
    xc = _causal_conv(xbuf, L, cw_ref[...], cb_ref[...])
    xc = xc * _sigmoid(xc)
    gl = gt_ref[...] + gb_ref[...]
    lf = jnp.minimum(gl, 0.0) - jnp.log(1.0 + jnp.exp(-jnp.abs(gl)))
    bcum = _mm_exact_lhs(_dot, tri_ref[...], lf)
    gl_t = gl.T
    bcum_t = bcum.T

    @pl.when(first)
    def _():
        c_s[...] = jnp.zeros(c_s.shape, F32)
        n_s[...] = jnp.zeros(n_s.shape, F32)
        m_s[...] = jnp.zeros(m_s.shape, F32)

    ti = lax.broadcasted_iota(jnp.int32, (L, L), 0)
    si = lax.broadcasted_iota(jnp.int32, (L, L), 1)
    causal = si <= ti
    scale = dh ** -0.5
    hs = range(MLSTM_HEADS)
    sls = [slice(h * dh, (h + 1) * dh) for h in hs]
    xs = [xc[:, sl].astype(BF16) for sl in sls]
    q = [_dot(xs[h], wq_ref[h]) * scale for h in hs]
    k = [_dot(xs[h], wk_ref[h]) for h in hs]
    vb = [_dot(xm[:, sls[h]].astype(BF16), wv_ref[h]).astype(BF16) for h in hs]
    qb = [q[h].astype(BF16) for h in hs]
    kb = [k[h].astype(BF16) for h in hs]
    qk = [_dot_nt(qb[h], kb[h]) for h in hs]
    qc = [_dot(qb[h], c_s[h].astype(BF16)) for h in hs]
    bcol = [bcum[:, MLSTM_HEADS + h:MLSTM_HEADS + h + 1] for h in hs]
    m_prev = [m_s[h:h + 1, 0:1] for h in hs]
    dlog = [jnp.where(causal, bcol[h] - bcum_t[MLSTM_HEADS + h:MLSTM_HEADS + h + 1, :] + gl_t[h:h + 1, :],
                      -jnp.inf) for h in hs]
    inter = [bcol[h] + m_prev[h] for h in hs]
    m_t = [jnp.maximum(inter[h], jnp.max(dlog[h], axis=-1, keepdims=True)) for h in hs]
    s = [qk[h] * jnp.exp(dlog[h] - m_t[h]) for h in hs]
    sv = [_dot(s[h].astype(BF16), vb[h]) for h in hs]
    b_last = [bcol[h][L - 1:L, :] for h in hs]
    wlog = [b_last[h] - bcol[h] + gl[:, h:h + 1] for h in hs]
    m_new = [jnp.maximum(b_last[h] + m_prev[h], jnp.max(wlog[h], axis=0, keepdims=True)) for h in hs]
    kw = [k[h] * jnp.exp(wlog[h] - m_new[h]) for h in hs]
    kv = [_dot_tn(kw[h].astype(BF16), vb[h]) for h in hs]
    for h in hs:
        sc = jnp.exp(inter[h] - m_t[h])
        num = sc * qc[h] + sv[h]
        den = (sc * jnp.sum(q[h] * n_s[h:h + 1, :], axis=-1, keepdims=True)
               + jnp.sum(s[h], axis=-1, keepdims=True))
        hh = num / jnp.maximum(jnp.abs(den), jnp.exp(-m_t[h]))
        dec = jnp.exp(b_last[h] + m_prev[h] - m_new[h])
        c_s[h] = dec * c_s[h] + kv[h]
        n_s[h:h + 1, :] = dec * n_s[h:h + 1, :] + jnp.sum(kw[h], axis=0, keepdims=True)
        m_s[h:h + 1, :] = jnp.broadcast_to(m_new[h], (1, LANES))
        hn = hh * lax.rsqrt(jnp.mean(hh * hh, axis=-1, keepdims=True) + RMS_EPS)
        o_ref[:, sls[h]] = (_sigmoid(z_ref[:, sls[h]]) * (hn * nw_ref[:, sls[h]])).astype(o_ref.dtype)


def _mlstm(p1, batch, seq, conv_w, conv_b, wq, wk, wv, ig_b, fg_b, mh_w):
    L = MLSTM_CHUNK
    nc = seq // L
    W = MLSTM_WIDTH
    gb = jnp.zeros((1, LANES), F32).at[0, 0:8].set(ig_b).at[0, 8:16].set(fg_b)
    tri = (jnp.arange(L)[:, None] >= jnp.arange(L)[None, :]).astype(BF16)
    hw = lambda: pl.BlockSpec((MLSTM_HEADS, MLSTM_HEAD, MLSTM_HEAD), lambda b, i: (0, 0, 0))
    return pl.pallas_call(
        _mlstm_kernel,
        grid=(batch, nc),
        in_specs=[pl.BlockSpec((L, W), lambda b, i: (b * nc + i, 0)),
                  pl.BlockSpec((L, W), lambda b, i: (b * nc + i, 1)),
                  pl.BlockSpec((L, LANES), lambda b, i: (b * nc + i, 2 * W // LANES)),
                  pl.BlockSpec((CONV_WIDTH, W), lambda b, i: (0, 0)),
                  pl.BlockSpec((1, W), lambda b, i: (0, 0)),
                  hw(), hw(), hw(),
                  pl.BlockSpec((1, LANES), lambda b, i: (0, 0)),
                  pl.BlockSpec((1, W), lambda b, i: (0, 0)),
                  pl.BlockSpec((L, L), lambda b, i: (0, 0))],
        out_specs=pl.BlockSpec((L, W), lambda b, i: (b * nc + i, 0)),
        out_shape=jax.ShapeDtypeStruct((batch * seq, W), BF16),
        scratch_shapes=[pltpu.VMEM((L + CARRY_ROWS, W), F32),
                        pltpu.VMEM((MLSTM_HEADS, MLSTM_HEAD, MLSTM_HEAD), F32),
                        pltpu.VMEM((MLSTM_HEADS, LANES), F32),
                        pltpu.VMEM((MLSTM_HEADS, LANES), F32)],
        compiler_params=_cparams(("parallel", "arbitrary")),
        name="mlstm",
    )(p1, p1, p1, conv_w, conv_b.reshape(1, W), wq.astype(BF16), wk.astype(BF16),
      wv.astype(BF16), gb, mh_w.reshape(1, W), tri)


def _router_kernel(x_ref, g_ref, rt_ref, o_ref):
    xn = _rms(x_ref[...], g_ref[...])
    logits = _mm3(_dot, xn, rt_ref[...])
    lane = lax.broadcasted_iota(jnp.int32, logits.shape, 1)
    lg = jnp.where(lane < N_EXPERTS, logits, -jnp.inf)
    v1 = jnp.max(lg, axis=-1, keepdims=True)
    i1 = jnp.min(jnp.where(lg == v1, lane, LANES), axis=-1, keepdims=True)
    lg2 = jnp.where(lane == i1, -jnp.inf, lg)
    v2 = jnp.max(lg2, axis=-1, keepdims=True)
    i2 = jnp.min(jnp.where(lg2 == v2, lane, LANES), axis=-1, keepdims=True)
    ex = jnp.exp(v2 - v1)
    g1 = 1.0 / (1.0 + ex)
    g2 = ex / (1.0 + ex)
    out = jnp.where(lane == 0, i1.astype(F32), 0.0)
    out = jnp.where(lane == 1, i2.astype(F32), out)
    out = jnp.where(lane == 2, g1, out)
    o_ref[...] = jnp.where(lane == 3, g2, out)


def _router(x, g, router, tm):
    t, d = x.shape
    return pl.pallas_call(
        _router_kernel,
        grid=(t // tm,),
        in_specs=[pl.BlockSpec((tm, d), lambda i: (i, 0)),
                  pl.BlockSpec((1, d), lambda i: (0, 0)),
                  pl.BlockSpec((d, LANES), lambda i: (0, 0))],
        out_specs=pl.BlockSpec((tm, LANES), lambda i: (i, 0)),
        out_shape=jax.ShapeDtypeStruct((t, LANES), F32),
        compiler_params=_cparams(("parallel",)),
        name="moe_router",
    )(x, g.reshape(1, d), router)


def _route_plan(route, blk):
    e = route[:, 0:TOP_K].astype(jnp.int32).reshape(-1)
    m = e.shape[0]
    onehot = (e[:, None] == jnp.arange(N_EXPERTS, dtype=jnp.int32)[None, :]).astype(jnp.int32)
    csum = jnp.cumsum(onehot, axis=0)
    rank = jnp.sum((csum - onehot) * onehot, axis=1)
    nb = (csum[-1] + blk - 1) // blk
    bend = jnp.cumsum(nb)
    dest = jnp.sum(onehot * (bend - nb)[None, :], axis=1) * blk + rank
    nblk = m // blk + N_EXPERTS
    n_used = bend[-1]
    bidx = jnp.minimum(jnp.arange(nblk, dtype=jnp.int32), n_used - 1)
    blk_e = jnp.minimum(jnp.searchsorted(bend, bidx, side="right"), N_EXPERTS - 1).astype(jnp.int32)
    return dest.astype(jnp.int32), blk_e, n_used.reshape(1).astype(jnp.int32), nblk


def _row_copy(src, src_row, dst, dst_row, sem):
    return pltpu.make_async_copy(src.at[pl.ds(src_row, 1), :], dst.at[pl.ds(dst_row, 1), :], sem)


def _dispatch_kernel(dest_ref, x_ref, g_ref, buf_in, buf_out, xn_s, sem):
    del buf_in
    tm = x_ref.shape[0]
    xn_s[...] = _rms(x_ref[...], g_ref[...])

    def copy(j):
        return _row_copy(xn_s, j // TOP_K, buf_out, dest_ref[0, 0, j], sem.at[0])

    def start(j, c):
        copy(j).start()
        return c

    def wait(j, c):
        copy(j).wait()
        return c

    lax.fori_loop(0, TOP_K * tm, start, 0, unroll=8)
    lax.fori_loop(0, TOP_K * tm, wait, 0, unroll=8)


def _dispatch(x, g, dest, rows, tm):
    t, d = x.shape
    nt = t // tm
    return pl.pallas_call(
        _dispatch_kernel,
        grid=(nt,),
        in_specs=[pl.BlockSpec((1, 1, TOP_K * tm), lambda i: (i, 0, 0), memory_space=pltpu.SMEM),
                  pl.BlockSpec((tm, d), lambda i: (i, 0)),
                  pl.BlockSpec((1, d), lambda i: (0, 0)),
                  pl.BlockSpec(memory_space=pl.ANY)],
        out_specs=pl.BlockSpec(memory_space=pl.ANY),
        out_shape=jax.ShapeDtypeStruct((rows, d), F32),
        scratch_shapes=[pltpu.VMEM((tm, d), F32), pltpu.SemaphoreType.DMA((1,))],
        input_output_aliases={3: 0},
        compiler_params=_cparams(("arbitrary",)),
        name="moe_dispatch",
    )(dest.reshape(nt, 1, TOP_K * tm), x, g.reshape(1, d), jnp.zeros((rows, d), F32))


def _expert_kernel(be_ref, nu_ref, x_ref, w1_ref, w3_ref, w2_ref, y_ref):
    del be_ref
    used = pl.program_id(0) < nu_ref[0]

    @pl.when(jnp.logical_not(used))
    def _():
        y_ref[...] = jnp.zeros(y_ref.shape, F32)

    @pl.when(used)
    def _():
        x = x_ref[...].astype(BF16)
        acc = None
        for f in range(MOE_F_SPLIT):
            sl = slice(f * (D_FF // MOE_F_SPLIT), (f + 1) * (D_FF // MOE_F_SPLIT))
            h1 = _dot(x, w1_ref[0, :, sl])
            h3 = _dot(x, w3_ref[0, :, sl])
            act = (h1 * _sigmoid(h1) * h3).astype(BF16)
            part = _dot(act, w2_ref[0, sl, :])
            acc = part if acc is None else acc + part
        y_ref[...] = acc


def _experts(buf, blk_e, n_used, w1, w3, w2, blk):
    rows, d = buf.shape
    f = w1.shape[2]
    once = pl.Buffered(1)
    grid_spec = pltpu.PrefetchScalarGridSpec(
        num_scalar_prefetch=2,
        grid=(rows // blk,),
        in_specs=[pl.BlockSpec((blk, d), lambda b, be, nu: (b, 0)),
                  pl.BlockSpec((1, d, f), lambda b, be, nu: (be[b], 0, 0), pipeline_mode=once),
                  pl.BlockSpec((1, d, f), lambda b, be, nu: (be[b], 0, 0), pipeline_mode=once),
                  pl.BlockSpec((1, f, d), lambda b, be, nu: (be[b], 0, 0), pipeline_mode=once)],
        out_specs=pl.BlockSpec((blk, d), lambda b, be, nu: (b, 0)))
    return pl.pallas_call(
        _expert_kernel,
        grid_spec=grid_spec,
        out_shape=jax.ShapeDtypeStruct((rows, d), F32),
        compiler_params=_cparams(("arbitrary",)),
        name="moe_experts",
    )(blk_e, n_used, buf, w1, w3, w2)


def _combine_kernel(dest_ref, x_ref, rt_ref, fn_ref, y_hbm, o_ref, ya, yb, sem):
    tm = x_ref.shape[0]

    def copies(r):
        return (_row_copy(y_hbm, dest_ref[0, 0, TOP_K * r], ya, r, sem.at[0]),
                _row_copy(y_hbm, dest_ref[0, 0, TOP_K * r + 1], yb, r, sem.at[1]))

    def start(r, c):
        for cp in copies(r):
            cp.start()
        return c

    def wait(r, c):
        for cp in copies(r):
            cp.wait()
        return c

    lax.fori_loop(0, tm, start, 0, unroll=8)
    lax.fori_loop(0, tm, wait, 0, unroll=8)
    rt = rt_ref[...]
    lane = lax.broadcasted_iota(jnp.int32, rt.shape, 1)
    g1 = jnp.sum(jnp.where(lane == 2, rt, 0.0), axis=-1, keepdims=True)
    g2 = jnp.sum(jnp.where(lane == 3, rt, 0.0), axis=-1, keepdims=True)
    o_ref[...] = _rms(x_ref[...] + (g1 * ya[...] + g2 * yb[...]), fn_ref[...])


def _combine(x, route, fnorm, y, dest, tm):
    t, d = x.shape
    nt = t // tm
    return pl.pallas_call(
        _combine_kernel,
        grid=(nt,),
        in_specs=[pl.BlockSpec((1, 1, TOP_K * tm), lambda i: (i, 0, 0), memory_space=pltpu.SMEM),
                  pl.BlockSpec((tm, d), lambda i: (i, 0)),
                  pl.BlockSpec((tm, LANES), lambda i: (i, 0)),
                  pl.BlockSpec((1, d), lambda i: (0, 0)),
                  pl.BlockSpec(memory_space=pl.ANY)],
        out_specs=pl.BlockSpec((tm, d), lambda i: (i, 0)),
        out_shape=jax.ShapeDtypeStruct((t, d), F32),
        scratch_shapes=[pltpu.VMEM((tm, d), F32), pltpu.VMEM((tm, d), F32),
                        pltpu.SemaphoreType.DMA((2,))],
        compiler_params=_cparams(("arbitrary",)),
        name="moe_combine",
    )(dest.reshape(nt, 1, TOP_K * tm), x, route, fnorm.reshape(1, d), y)


def _moe(x, g, router, w1, w3, w2, fnorm, tm):
    route = _router(x, g, router, tm)
    blk = min(MOE_BLOCK, x.shape[0])
    dest, blk_e, n_used, nblk = _route_plan(route, blk)
    buf = _dispatch(x, g, dest, nblk * blk, tm)
    y = _experts(buf, blk_e, n_used, w1, w3, w2, blk)
    return _combine(x, route, fnorm, y, dest, tm)


def _block_diag(w):
    g, a, b = w.shape
    eye = jnp.eye(g, dtype=w.dtype)
    return (eye[:, None, :, None] * w[:, :, None, :]).reshape(g * a, g * b)


def kernel(x, l0_norm_mix, l0_w_in, l0_conv_w, l0_conv_b, l0_gate_a_w, l0_gate_a_b, l0_gate_x_w, l0_gate_x_b, l0_lru_lambda, l0_shift_mu, l0_w0, l0_w_up, l0_a0, l0_a_up, l0_g_up, l0_k_k, l0_k_a, l0_r_k, l0_ln_x_w, l0_ln_x_b, l0_w_out, l0_norm_ffn, l0_ffn_w1, l0_ffn_w3, l0_ffn_w2, l1_norm_mix, l1_w_in, l1_conv_w, l1_conv_b, l1_wq, l1_wk, l1_wv, l1_ig_b, l1_fg_b, l1_mh_norm_w, l1_w_out, l1_norm_ffn, l1_router, l1_moe_w1, l1_moe_w3, l1_moe_w2, final_norm):
    batch, seq, d = x.shape
    t = batch * seq
    xt = x.reshape(t, d)
    tm = min(512, t)

    w_in0 = l0_w_in.astype(BF16)
    p_lru = _norm_mm(xt, l0_norm_mix, w_in0[:, :2 * LRU_WIDTH], tm, "l0_in_lru")
    p_rw = _norm_mm(xt, l0_norm_mix, w_in0[:, 2 * LRU_WIDTH:], tm, "l0_in_rwkv")
    y_lru = _lru(p_lru, batch, seq, l0_conv_w, l0_conv_b,
                 _block_diag(l0_gate_a_w).astype(BF16), l0_gate_a_b,
                 _block_diag(l0_gate_x_w).astype(BF16), l0_gate_x_b, l0_lru_lambda,
                 min(512, seq))
    y_rw = _rwkv(p_rw, batch, seq, l0_shift_mu, l0_w0, l0_w_up, l0_a0, l0_a_up, l0_g_up,
                 l0_k_k, l0_k_a, l0_r_k.reshape(-1), l0_ln_x_w, l0_ln_x_b)
    w_out0 = l0_w_out.astype(BF16)
    x1 = _mm_res(xt, [(y_lru, w_out0[:LRU_WIDTH]), (y_rw, w_out0[LRU_WIDTH:])], tm, "l0_out")
    x2 = _ffn(x1, l0_norm_ffn, l0_ffn_w1.astype(BF16), l0_ffn_w3.astype(BF16),
              l0_ffn_w2.astype(BF16), tm, D_FF // 2)

    n_in1 = 2 * MLSTM_WIDTH + LANES
    w_in1 = jnp.zeros((d, n_in1), F32).at[:, :l1_w_in.shape[1]].set(l1_w_in).astype(BF16)
    p1 = _norm_mm(x2, l1_norm_mix, w_in1, tm, "l1_in")
    h1 = _mlstm(p1, batch, seq, l1_conv_w, l1_conv_b, l1_wq, l1_wk, l1_wv, l1_ig_b, l1_fg_b,
                l1_mh_norm_w)
    x3 = _mm_res(x2, [(h1, l1_w_out.astype(BF16))], tm, "l1_out")
    router = jnp.zeros((d, LANES), F32).at[:, :N_EXPERTS].set(l1_router)
    out = _moe(x3, l1_norm_ffn, router, l1_moe_w1.astype(BF16), l1_moe_w3.astype(BF16),
               l1_moe_w2.astype(BF16), final_norm, tm)
    return out.reshape(batch, seq, d)
```

```python
import functools

import jax
import jax.numpy as jnp
from jax import lax
from jax.experimental import pallas as pl
from jax.experimental.pallas import tpu as pltpu

F32 = jnp.float32
BF16 = jnp.bfloat16

D_MODEL = 1024
LRU_WIDTH = 512
LRU_BLOCKS = 8
LRU_C = 8.0
CONV_WIDTH = 4
RWKV_HEADS = 8
RWKV_HEAD = 64
RWKV_WIDTH = 512
LN_X_EPS = 1e-5 * RWKV_HEAD
MLSTM_HEADS = 8
MLSTM_HEAD = 128
MLSTM_WIDTH = 1024
MLSTM_CHUNK = 128
D_FF = 2816
N_EXPERTS = 8
TOP_K = 2
MOE_BLOCK = 512
MOE_F_SPLIT = 2
RMS_EPS = 1e-6
RW_IN = 3 * RWKV_WIDTH + 64 + 64 + 128
RW_CHUNK = 64
RW_STEP_ROWS = 256
LANES = 128
CARRY_ROWS = 8
ROW_TILE = 8
VMEM_LIMIT = 56 * 1024 * 1024


def _cparams(sem):
    return pltpu.CompilerParams(dimension_semantics=sem, vmem_limit_bytes=VMEM_LIMIT)


def _rms(x, w):
    return x * lax.rsqrt(jnp.mean(x * x, axis=-1, keepdims=True) + RMS_EPS) * w


def _sigmoid(x):
    return 1.0 / (1.0 + jnp.exp(-x))


def _softplus(x):
    return jnp.maximum(x, 0.0) + jnp.log(1.0 + jnp.exp(-jnp.abs(x)))


def _dot(a, b):
    return jnp.dot(a, b, preferred_element_type=F32)


def _dot_nt(a, b):
    return lax.dot_general(a, b, (((1,), (1,)), ((), ())), preferred_element_type=F32)


def _dot_tn(a, b):
    return lax.dot_general(a, b, (((0,), (0,)), ((), ())), preferred_element_type=F32)


def _split2(x):
    hi = x.astype(BF16)
    lo = (x - hi.astype(F32)).astype(BF16)
    return hi, lo


def _split3(x):
    hi = x.astype(BF16)
    r = x - hi.astype(F32)
    mid = r.astype(BF16)
    lo = (r - mid.astype(F32)).astype(BF16)
    return hi, mid, lo


def _mm3(fn, a, b):
    ah, al = _split2(a)
    bh, bl = _split2(b)
    return fn(ah, bh) + fn(al, bh) + fn(ah, bl)


def _mm_exact_lhs(fn, a_bf16, b):
    h, m, l = _split3(b)
    return fn(a_bf16, h) + fn(a_bf16, m) + fn(a_bf16, l)


def _mm_exact_rhs(fn, a, b_bf16):
    h, m, l = _split3(a)
    return fn(h, b_bf16) + fn(m, b_bf16) + fn(l, b_bf16)


def _shift_hist(buf_ref, x, first):
    n = x.shape[0]

    @pl.when(first)
    def _():
        buf_ref[0:CARRY_ROWS, :] = jnp.zeros((CARRY_ROWS, x.shape[1]), F32)

    @pl.when(jnp.logical_not(first))
    def _():
        buf_ref[0:CARRY_ROWS, :] = buf_ref[n:n + CARRY_ROWS, :]

    buf_ref[CARRY_ROWS:CARRY_ROWS + n, :] = x


def _causal_conv(buf_ref, n, w, b):
    acc = b
    for j in range(CONV_WIDTH):
        off = CARRY_ROWS - (CONV_WIDTH - 1) + j
        acc = acc + w[j:j + 1, :] * buf_ref[off:off + n, :]
    return acc


def _norm_mm_kernel(x_ref, g_ref, w_ref, o_ref):
    xn = _rms(x_ref[...], g_ref[...]).astype(BF16)
    o_ref[...] = _dot(xn, w_ref[...]).astype(o_ref.dtype)


def _norm_mm(x, g, w, tm, name):
    t, d = x.shape
    n = w.shape[1]
    return pl.pallas_call(
        _norm_mm_kernel,
        grid=(t // tm,),
        in_specs=[pl.BlockSpec((tm, d), lambda i: (i, 0)),
                  pl.BlockSpec((1, d), lambda i: (0, 0)),
                  pl.BlockSpec((d, n), lambda i: (0, 0))],
        out_specs=pl.BlockSpec((tm, n), lambda i: (i, 0)),
        out_shape=jax.ShapeDtypeStruct((t, n), F32),
        compiler_params=_cparams(("parallel",)),
        name=name,
    )(x, g.reshape(1, d), w)


def _mm_res_kernel(*refs):
    x_ref, o_ref = refs[0], refs[-1]
    acc = x_ref[...]
    for i in range(1, len(refs) - 1, 2):
        acc = acc + _dot(refs[i][...], refs[i + 1][...])
    o_ref[...] = acc


def _mm_res(x, pairs, tm, name):
    t, d = x.shape
    in_specs = [pl.BlockSpec((tm, d), lambda i: (i, 0))]
    args = [x]
    for y, w in pairs:
        in_specs.append(pl.BlockSpec((tm, y.shape[1]), lambda i: (i, 0)))
        in_specs.append(pl.BlockSpec(w.shape, lambda i: (0, 0)))
        args += [y, w]
    return pl.pallas_call(
        _mm_res_kernel,
        grid=(t // tm,),
        in_specs=in_specs,
        out_specs=pl.BlockSpec((tm, d), lambda i: (i, 0)),
        out_shape=jax.ShapeDtypeStruct((t, d), F32),
        compiler_params=_cparams(("parallel",)),
        name=name,
    )(*args)


def _lru_kernel(p_ref, cw_ref, cb_ref, wa_ref, ba_ref, wx_ref, bx_ref, lam_ref, o_ref,
                xbuf, abuf, bbuf, hbuf, hcar):
    i = pl.program_id(1)
    n = p_ref.shape[0]
    first = i == 0
    _shift_hist(xbuf, p_ref[:, 0:LRU_WIDTH], first)
    xc = _causal_conv(xbuf, n, cw_ref[...], cb_ref[...])
    xcb = xc.astype(BF16)
    r = _sigmoid(_dot(xcb, wa_ref[...]) + ba_ref[...])
    ig = _sigmoid(_dot(xcb, wx_ref[...]) + bx_ref[...])
    log_a = (-LRU_C) * r * _softplus(-lam_ref[...])
    a = jnp.exp(log_a)
    mult = jnp.sqrt(1.0 - jnp.exp(2.0 * log_a))
    row = lax.broadcasted_iota(jnp.int32, (n, 1), 0)
    mult = jnp.where(jnp.logical_and(first, row == 0), 1.0, mult)
    abuf[...] = a
    bbuf[...] = mult * ig * xc

    @pl.when(first)
    def _():
        hcar[...] = jnp.zeros(hcar.shape, F32)

    def step(t, h):
        h = abuf[pl.ds(t, 1), :] * h + bbuf[pl.ds(t, 1), :]
        hbuf[pl.ds(t, 1), :] = h
        return h

    h = lax.fori_loop(0, n, step, hcar[0:1, :], unroll=8)
    hcar[0:1, :] = h
    gate = p_ref[:, LRU_WIDTH:2 * LRU_WIDTH]
    gelu = 0.5 * gate * (1.0 + jnp.tanh(0.7978845608028654 * (gate + 0.044715 * gate * gate * gate)))
    o_ref[...] = (hbuf[...] * gelu).astype(o_ref.dtype)


def _lru(p_lru, batch, seq, conv_w, conv_b, wa, ba, wx, bx, lam, tb):
    nb = seq // tb
    c = LRU_WIDTH
    vec = lambda: pl.BlockSpec((1, c), lambda b, i: (0, 0))
    return pl.pallas_call(
        _lru_kernel,
        grid=(batch, nb),
        in_specs=[pl.BlockSpec((tb, 2 * c), lambda b, i: (b * nb + i, 0)),
                  pl.BlockSpec((CONV_WIDTH, c), lambda b, i: (0, 0)), vec(),
                  pl.BlockSpec((c, c), lambda b, i: (0, 0)), vec(),
                  pl.BlockSpec((c, c), lambda b, i: (0, 0)), vec(), vec()],
        out_specs=pl.BlockSpec((tb, c), lambda b, i: (b * nb + i, 0)),
        out_shape=jax.ShapeDtypeStruct((batch * seq, c), BF16),
        scratch_shapes=[pltpu.VMEM((tb + CARRY_ROWS, c), F32), pltpu.VMEM((tb, c), F32),
                        pltpu.VMEM((tb, c), F32), pltpu.VMEM((tb, c), F32),
                        pltpu.VMEM((CARRY_ROWS, c), F32)],
        compiler_params=_cparams(("parallel", "arbitrary")),
        name="rg_lru",
    )(p_lru, conv_w, conv_b.reshape(1, c), wa, ba.reshape(1, c), wx, bx.reshape(1, c),
      lam.reshape(1, c))


def _blk(x, masks):
    xb = x.astype(BF16)
    return jnp.concatenate([jnp.where(mk, xb, jnp.zeros_like(xb)) for mk in masks], axis=0)


def _rwkv_kernel(p_ref, mu_ref, w0_ref, wup_ref, a0_ref, aup_ref, gup_ref, kk_ref, ka_ref,
                 rk_ref, lnw_ref, lnb_ref, seg_ref, tri_ref, o_ref, pbuf, state):
    L = RW_CHUNK
    W = RWKV_WIDTH
    rows = p_ref.shape[0]
    first = pl.program_id(1) == 0
    p = p_ref[...]
    _shift_hist(pbuf, p, first)
    prev = pbuf[CARRY_ROWS - 1:CARRY_ROWS - 1 + rows, :]
    ps = p + mu_ref[...] * (prev - p)
    r = ps[:, 0:W]
    k = ps[:, W:2 * W]
    v = ps[:, 2 * W:3 * W]
    x2 = ps[:, 3 * W:3 * W + LANES]
    xg = ps[:, 3 * W + LANES:3 * W + 2 * LANES]
    wl = w0_ref[...] + _dot(jnp.tanh(x2).astype(BF16), wup_ref[...])
    lw = -jnp.exp(-_softplus(-wl) - 0.5)
    a = _sigmoid(a0_ref[...] + _dot(x2.astype(BF16), aup_ref[...]))
    g = _dot(_sigmoid(xg).astype(BF16), gup_ref[...])
    seg = seg_ref[...]
    kk = k * kk_ref[...]
    kk = kk / jnp.maximum(jnp.sqrt(_mm_exact_rhs(_dot, kk * kk, seg)), 1e-12)
    k2 = k * (1.0 + (a - 1.0) * ka_ref[...])

    cw = _mm_exact_lhs(_dot, tri_ref[...], lw)
    cw_end = jnp.concatenate(
        [jnp.broadcast_to(cw[c * L + L - 1:c * L + L, :], (L, W)) for c in range(rows // L)], axis=0)
    w_in = jnp.exp(cw)
    w_inv = jnp.exp(-cw)
    w_end = jnp.exp(cw_end - cw)
    kka = kk * a
    a_t = -kk * jnp.exp(cw - lw)
    b_t = kka * w_inv
    k_t = k2 * w_inv
    r_t = r * w_in
    b_bar = kka * w_end
    k_bar = k2 * w_end
    w_tot = jnp.exp(cw_end)

    @pl.when(first)
    def _():
        state[...] = jnp.zeros(state.shape, F32)

    lane = lax.broadcasted_iota(jnp.int32, (1, LANES), 1)
    m1 = [lane < RWKV_HEAD, lane >= RWKV_HEAD]
    m2 = [jnp.concatenate([mk, mk], axis=1) for mk in m1]
    ti = lax.broadcasted_iota(jnp.int32, (L, LANES), 0)
    si = lax.broadcasted_iota(jnp.int32, (L, LANES), 1) % RWKV_HEAD
    strict = si < ti
    incl = si <= ti
    bi = lax.broadcasted_iota(jnp.int32, (LANES, LANES), 0) // RWKV_HEAD
    bj = lax.broadcasted_iota(jnp.int32, (LANES, LANES), 1) // RWKV_HEAD
    bd = bi == bj

    def tril(mask, s):
        return jnp.where(mask, s, 0.0).astype(BF16)

    n_pairs = RWKV_HEADS // 2
    n_ch = rows // L
    chains = [(slice(c * L, (c + 1) * L), slice(pr * LANES, (pr + 1) * LANES))
              for c in range(n_ch) for pr in range(n_pairs)]
    idx = range(len(chains))
    a_c = [a_t[rs, sl] for rs, sl in chains]
    r_c = [r_t[rs, sl] for rs, sl in chains]
    ar = [jnp.concatenate([a_c[i], r_c[i]], axis=0).astype(BF16) for i in idx]
    sb = [_dot_nt(ar[i], _blk(b_t[rs, sl], m1)) for i, (rs, sl) in enumerate(chains)]
    sk = [_dot_nt(ar[i], _blk(k_t[rs, sl], m1)) for i, (rs, sl) in enumerate(chains)]
    vblk = [_blk(v[rs, sl], m1) for rs, sl in chains]
    rhs = [jnp.concatenate([a_c[i], _dot(tril(strict, sk[i][0:L]), vblk[i])], axis=1) for i in idx]
    pw = [tril(strict, sb[i][0:L]) for i in idx]
    tinv = [jnp.where(si == ti, 1.0, 0.0) + pw[i].astype(F32) for i in idx]
    for _ in range(5):
        pw = [_dot(pw[i], _blk(pw[i], m1)).astype(BF16) for i in idx]
        tinv = [tinv[i] + _dot(pw[i], _blk(tinv[i], m1)) for i in idx]
    x = [_dot(tinv[i].astype(BF16), _blk(rhs[i], m2)) for i in idx]
    corr = [_dot(tril(incl, sb[i][L:2 * L]), _blk(x[i], m2)) for i in idx]
    r_hat = [(r_c[i] + corr[i][:, 0:LANES]).astype(BF16) for i in idx]
    y0 = [corr[i][:, LANES:2 * LANES] + _dot(tril(incl, sk[i][L:2 * L]), vblk[i]) for i in idx]
    xb = [x[i].astype(BF16) for i in idx]
    bb = [b_bar[rs, sl].astype(BF16) for rs, sl in chains]
    gmat = [jnp.where(bd, _dot_tn(xb[i][:, 0:LANES], bb[i]), 0.0).astype(BF16) for i in idx]
    hmat = [jnp.where(bd, _dot_tn(xb[i][:, LANES:2 * LANES], bb[i])
                      + _dot_tn(v[rs, sl].astype(BF16), k_bar[rs, sl].astype(BF16)), 0.0)
            for i, (rs, sl) in enumerate(chains)]

    s_cur = [state[pr] for pr in range(n_pairs)]
    y_rows = []
    for c in range(n_ch):
        ys = []
        for pr in range(n_pairs):
            i = c * n_pairs + pr
            s0 = s_cur[pr]
            s0b = s0.astype(BF16)
            ys.append(_dot_nt(r_hat[i], s0b) + y0[i])
            s_cur[pr] = s0 * w_tot[c * L:c * L + 1, chains[i][1]] + _dot(s0b, gmat[i]) + hmat[i]
        y_rows.append(jnp.concatenate(ys, axis=1))
    for pr in range(n_pairs):
        state[pr] = s_cur[pr]
    y = jnp.concatenate(y_rows, axis=0)

    inv = 1.0 / RWKV_HEAD
    mean = _mm_exact_rhs(_dot, y, seg) * inv
    yc = y - mean
    var = _mm_exact_rhs(_dot, yc * yc, seg) * inv
    yn = yc * lax.rsqrt(var + LN_X_EPS) * lnw_ref[...] + lnb_ref[...]
    bonus = _mm_exact_rhs(_dot, r * k2 * rk_ref[...], seg) * v
    o_ref[...] = ((yn + bonus) * g).astype(o_ref.dtype)


def _rwkv(p_rw, batch, seq, mu, w0, w_up, a0, a_up, g_up, k_k, k_a, r_k, ln_w, ln_b):
    L = min(RW_STEP_ROWS, seq)
    nc = seq // L
    W = RWKV_WIDTH
    wup = jnp.zeros((LANES, W), F32).at[0:64].set(w_up).astype(BF16)
    aup = jnp.zeros((LANES, W), F32).at[64:128].set(a_up).astype(BF16)
    hid = jnp.arange(W) // RWKV_HEAD
    seg = (hid[:, None] == hid[None, :]).astype(BF16)
    ri = jnp.arange(L)
    tri = jnp.logical_and(ri[:, None] >= ri[None, :],
                          ri[:, None] // RW_CHUNK == ri[None, :] // RW_CHUNK).astype(BF16)
    vec = lambda: pl.BlockSpec((1, W), lambda b, i: (0, 0))
    mat = lambda s: pl.BlockSpec(s, lambda b, i: (0, 0))
    return pl.pallas_call(
        _rwkv_kernel,
        grid=(batch, nc),
        in_specs=[pl.BlockSpec((L, RW_IN), lambda b, i: (b * nc + i, 0)),
                  mat((1, RW_IN)), vec(), mat((LANES, W)), vec(), mat((LANES, W)),
                  mat((LANES, W)), vec(), vec(), vec(), vec(), vec(), mat((W, W)), mat((L, L))],
        out_specs=pl.BlockSpec((L, W), lambda b, i: (b * nc + i, 0)),
        out_shape=jax.ShapeDtypeStruct((batch * seq, W), BF16),
        scratch_shapes=[pltpu.VMEM((L + CARRY_ROWS, RW_IN), F32),
                        pltpu.VMEM((RWKV_HEADS // 2, LANES, LANES), F32)],
        compiler_params=_cparams(("parallel", "arbitrary")),
        name="rwkv7",
    )(p_rw, mu.reshape(1, RW_IN), w0.reshape(1, W), wup, a0.reshape(1, W), aup,
      g_up.astype(BF16), k_k.reshape(1, W), k_a.reshape(1, W), r_k.reshape(1, W),
      ln_w.reshape(1, W), ln_b.reshape(1, W), seg, tri)


def _ffn_kernel(x_ref, ya_ref, yb_ref, wa_ref, wb_ref, g_ref, w1_ref, w3_ref, w2_ref, o_ref, xn_s, acc_s):
    f = pl.program_id(1)

    @pl.when(f == 0)
    def _():
        x1 = x_ref[...] + _dot(ya_ref[...], wa_ref[...]) + _dot(yb_ref[...], wb_ref[...])
        xn_s[...] = _rms(x1, g_ref[...]).astype(BF16)
        acc_s[...] = x1

    xn = xn_s[...]
    h1 = _dot(xn, w1_ref[...])
    h3 = _dot(xn, w3_ref[...])
    act = (h1 * _sigmoid(h1) * h3).astype(BF16)
    acc_s[...] += _dot(act, w2_ref[...])

    @pl.when(f == pl.num_programs(1) - 1)
    def _():
        o_ref[...] = acc_s[...]


def _ffn(x, ya, yb, wa, wb, g, w1, w3, w2, tm, tf):
    t, d = x.shape
    nf = w1.shape[1] // tf
    return pl.pallas_call(
        _ffn_kernel,
        grid=(t // tm, nf),
        in_specs=[pl.BlockSpec((tm, d), lambda i, f: (i, 0)),
                  pl.BlockSpec((tm, ya.shape[1]), lambda i, f: (i, 0)),
                  pl.BlockSpec((tm, yb.shape[1]), lambda i, f: (i, 0)),
                  pl.BlockSpec(wa.shape, lambda i, f: (0, 0)),
                  pl.BlockSpec(wb.shape, lambda i, f: (0, 0)),
                  pl.BlockSpec((1, d), lambda i, f: (0, 0)),
                  pl.BlockSpec((d, tf), lambda i, f: (0, f)),
                  pl.BlockSpec((d, tf), lambda i, f: (0, f)),
                  pl.BlockSpec((tf, d), lambda i, f: (f, 0))],
        out_specs=pl.BlockSpec((tm, d), lambda i, f: (i, 0)),
        out_shape=jax.ShapeDtypeStruct((t, d), F32),
        scratch_shapes=[pltpu.VMEM((tm, d), BF16), pltpu.VMEM((tm, d), F32)],
        compiler_params=_cparams(("parallel", "arbitrary")),
        name="ffn_swiglu",
    )(x, ya, yb, wa, wb, g.reshape(1, d), w1, w3, w2)


def _mlstm_kernel(xm_ref, z_ref, gt_ref, cw_ref, cb_ref, wq_ref, wk_ref, wv_ref, gb_ref,
                  nw_ref, tri_ref, o_ref, xbuf, c_s, n_s, m_s):
    ci = pl.program_id(1)
    L = MLSTM_CHUNK
    dh = MLSTM_HEAD
    first = ci == 0
    xm = xm_ref[...]
    _shift_hist(xbuf, xm, first)
    xc = _causal_conv(xbuf, L, cw_ref[...], cb_ref[...])
    xc = xc * _sigmoid(xc)
    gl = gt_ref[...] + gb_ref[...]
    lf = jnp.minimum(gl, 0.0) - jnp.log(1.0 + jnp.exp(-jnp.abs(gl)))
    bcum = _mm_exact_lhs(_dot, tri_ref[...], lf)
    gl_t = gl.T
    bcum_t = bcum.T

    @pl.when(first)
    def _():
        c_s[...] = jnp.zeros(c_s.shape, F32)
        n_s[...] = jnp.zeros(n_s.shape, F32)
        m_s[...] = jnp.zeros(m_s.shape, F32)

    ti = lax.broadcasted_iota(jnp.int32, (L, L), 0)
    si = lax.broadcasted_iota(jnp.int32, (L, L), 1)
    causal = si <= ti
    scale = dh ** -0.5
    hs = range(MLSTM_HEADS)
    sls = [slice(h * dh, (h + 1) * dh) for h in hs]
    xs = [xc[:, sl].astype(BF16) for sl in sls]
    q = [_dot(xs[h], wq_ref[h]) * scale for h in hs]
    k = [_dot(xs[h], wk_ref[h]) for h in hs]
    vb = [_dot(xm[:, sls[h]].astype(BF16), wv_ref[h]).astype(BF16) for h in hs]
    qb = [q[h].astype(BF16) for h in hs]
    kb = [k[h].astype(BF16) for h in hs]
    qk = [_dot_nt(qb[h], kb[h]) for h in hs]
    qc = [_dot(qb[h], c_s[h].astype(BF16)) for h in hs]
    bcol = [bcum[:, MLSTM_HEADS + h:MLSTM_HEADS + h + 1] for h in hs]
    m_prev = [m_s[h:h + 1, 0:1] for h in hs]
    dlog = [jnp.where(causal, bcol[h] - bcum_t[MLSTM_HEADS + h:MLSTM_HEADS + h + 1, :] + gl_t[h:h + 1, :],
                      -jnp.inf) for h in hs]
    inter = [bcol[h] + m_prev[h] for h in hs]
    m_t = [jnp.maximum(inter[h], jnp.max(dlog[h], axis=-1, keepdims=True)) for h in hs]
    s = [qk[h] * jnp.exp(dlog[h] - m_t[h]) for h in hs]
    sv = [_dot(s[h].astype(BF16), vb[h]) for h in hs]
    b_last = [bcol[h][L - 1:L, :] for h in hs]
    wlog = [b_last[h] - bcol[h] + gl[:, h:h + 1] for h in hs]
    m_new = [jnp.maximum(b_last[h] + m_prev[h], jnp.max(wlog[h], axis=0, keepdims=True)) for h in hs]
    kw = [k[h] * jnp.exp(wlog[h] - m_new[h]) for h in hs]
    kv = [_dot_tn(kw[h].astype(BF16), vb[h]) for h in hs]
    sc = [jnp.exp(inter[h] - m_t[h]) for h in hs]
    qn = [jnp.sum(q[h] * n_s[h:h + 1, :], axis=-1, keepdims=True) for h in hs]
    ssum = [jnp.sum(s[h], axis=-1, keepdims=True) for h in hs]
    den = [sc[h] * qn[h] + ssum[h] for h in hs]
    hh = [(sc[h] * qc[h] + sv[h]) / jnp.maximum(jnp.abs(den[h]), jnp.exp(-m_t[h])) for h in hs]
    ms = [jnp.mean(hh[h] * hh[h], axis=-1, keepdims=True) for h in hs]
    dec = [jnp.exp(b_last[h] + m_prev[h] - m_new[h]) for h in hs]
    for h in hs:
        c_s[h] = dec[h] * c_s[h] + kv[h]
        n_s[h:h + 1, :] = dec[h] * n_s[h:h + 1, :] + jnp.sum(kw[h], axis=0, keepdims=True)
        m_s[h:h + 1, :] = jnp.broadcast_to(m_new[h], (1, LANES))
        hn = hh[h] * lax.rsqrt(ms[h] + RMS_EPS)
        o_ref[:, sls[h]] = (_sigmoid(z_ref[:, sls[h]]) * (hn * nw_ref[:, sls[h]])).astype(o_ref.dtype)


def _mlstm(p1, batch, seq, conv_w, conv_b, wq, wk, wv, ig_b, fg_b, mh_w):
    L = MLSTM_CHUNK
    nc = seq // L
    W = MLSTM_WIDTH
    gb = jnp.zeros((1, LANES), F32).at[0, 0:8].set(ig_b).at[0, 8:16].set(fg_b)
    tri = (jnp.arange(L)[:, None] >= jnp.arange(L)[None, :]).astype(BF16)
    hw = lambda: pl.BlockSpec((MLSTM_HEADS, MLSTM_HEAD, MLSTM_HEAD), lambda b, i: (0, 0, 0))
    return pl.pallas_call(
        _mlstm_kernel,
        grid=(batch, nc),
        in_specs=[pl.BlockSpec((L, W), lambda b, i: (b * nc + i, 0)),
                  pl.BlockSpec((L, W), lambda b, i: (b * nc + i, 1)),
                  pl.BlockSpec((L, LANES), lambda b, i: (b * nc + i, 2 * W // LANES)),
                  pl.BlockSpec((CONV_WIDTH, W), lambda b, i: (0, 0)),
                  pl.BlockSpec((1, W), lambda b, i: (0, 0)),
                  hw(), hw(), hw(),
                  pl.BlockSpec((1, LANES), lambda b, i: (0, 0)),
                  pl.BlockSpec((1, W), lambda b, i: (0, 0)),
                  pl.BlockSpec((L, L), lambda b, i: (0, 0))],
        out_specs=pl.BlockSpec((L, W), lambda b, i: (b * nc + i, 0)),
        out_shape=jax.ShapeDtypeStruct((batch * seq, W), BF16),
        scratch_shapes=[pltpu.VMEM((L + CARRY_ROWS, W), F32),
                        pltpu.VMEM((MLSTM_HEADS, MLSTM_HEAD, MLSTM_HEAD), F32),
                        pltpu.VMEM((MLSTM_HEADS, LANES), F32),
                        pltpu.VMEM((MLSTM_HEADS, LANES), F32)],
        compiler_params=_cparams(("parallel", "arbitrary")),
        name="mlstm",
    )(p1, p1, p1, conv_w, conv_b.reshape(1, W), wq.astype(BF16), wk.astype(BF16),
      wv.astype(BF16), gb, mh_w.reshape(1, W), tri)


def _router_kernel(x_ref, g_ref, rt_ref, o_ref):
    xn = _rms(x_ref[...], g_ref[...])
    logits = _mm3(_dot, xn, rt_ref[...])
    lane = lax.broadcasted_iota(jnp.int32, logits.shape, 1)
    lg = jnp.where(lane < N_EXPERTS, logits, -jnp.inf)
    v1 = jnp.max(lg, axis=-1, keepdims=True)
    i1 = jnp.min(jnp.where(lg == v1, lane, LANES), axis=-1, keepdims=True)
    lg2 = jnp.where(lane == i1, -jnp.inf, lg)
    v2 = jnp.max(lg2, axis=-1, keepdims=True)
    i2 = jnp.min(jnp.where(lg2 == v2, lane, LANES), axis=-1, keepdims=True)
    ex = jnp.exp(v2 - v1)
    g1 = 1.0 / (1.0 + ex)
    g2 = ex / (1.0 + ex)
    out = jnp.where(lane == 0, i1.astype(F32), 0.0)
    out = jnp.where(lane == 1, i2.astype(F32), out)
    out = jnp.where(lane == 2, g1, out)
    o_ref[...] = jnp.where(lane == 3, g2, out)


def _router(x, g, router, tm):
    t, d = x.shape
    return pl.pallas_call(
        _router_kernel,
        grid=(t // tm,),
        in_specs=[pl.BlockSpec((tm, d), lambda i: (i, 0)),
                  pl.BlockSpec((1, d), lambda i: (0, 0)),
                  pl.BlockSpec((d, LANES), lambda i: (0, 0))],
        out_specs=pl.BlockSpec((tm, LANES), lambda i: (i, 0)),
        out_shape=jax.ShapeDtypeStruct((t, LANES), F32),
        compiler_params=_cparams(("parallel",)),
        name="moe_router",
    )(x, g.reshape(1, d), router)


def _route_plan(route, blk):
    e = route[:, 0:TOP_K].astype(jnp.int32).reshape(-1)
    m = e.shape[0]
    onehot = (e[:, None] == jnp.arange(N_EXPERTS, dtype=jnp.int32)[None, :]).astype(jnp.int32)
    csum = jnp.cumsum(onehot, axis=0)
    rank = jnp.sum((csum - onehot) * onehot, axis=1)
    nb = (csum[-1] + blk - 1) // blk
    bend = jnp.cumsum(nb)
    dest = jnp.sum(onehot * (bend - nb)[None, :], axis=1) * blk + rank
    nblk = m // blk + N_EXPERTS
    n_used = bend[-1]
    bidx = jnp.minimum(jnp.arange(nblk, dtype=jnp.int32), n_used - 1)
    blk_e = jnp.minimum(jnp.searchsorted(bend, bidx, side="right"), N_EXPERTS - 1).astype(jnp.int32)
    return dest.astype(jnp.int32), blk_e, n_used.reshape(1).astype(jnp.int32), nblk


def _to_tiles(ref, x):
    n = x.shape[0]
    for s in range(ROW_TILE):
        ref[pl.ds(s, n, stride=ROW_TILE), :] = x[:, s * LANES:(s + 1) * LANES]


def _from_tiles(ref, n):
    return jnp.concatenate([ref[pl.ds(s, n, stride=ROW_TILE), :] for s in range(ROW_TILE)], axis=1)


def _row_copy(src, src_row, dst, dst_row, sem):
    s0 = pl.multiple_of(src_row * ROW_TILE, ROW_TILE)
    d0 = pl.multiple_of(dst_row * ROW_TILE, ROW_TILE)
    return pltpu.make_async_copy(src.at[pl.ds(s0, ROW_TILE), :], dst.at[pl.ds(d0, ROW_TILE), :], sem)


def _dispatch_kernel(dest_ref, x_ref, g_ref, buf_in, buf_out, xn_s, sem):
    del buf_in
    tm = x_ref.shape[0]
    _to_tiles(xn_s, _rms(x_ref[...], g_ref[...]))

    def copy(j):
        return _row_copy(xn_s, j // TOP_K, buf_out, dest_ref[0, 0, j], sem.at[0])

    def start(j, c):
        copy(j).start()
        return c

    def wait(j, c):
        copy(j).wait()
        return c

    lax.fori_loop(0, TOP_K * tm, start, 0, unroll=8)
    lax.fori_loop(0, TOP_K * tm, wait, 0, unroll=8)


def _dispatch(x, g, dest, rows, tm):
    t, d = x.shape
    nt = t // tm
    return pl.pallas_call(
        _dispatch_kernel,
        grid=(nt,),
        in_specs=[pl.BlockSpec((1, 1, TOP_K * tm), lambda i: (i, 0, 0), memory_space=pltpu.SMEM),
                  pl.BlockSpec((tm, d), lambda i: (i, 0)),
                  pl.BlockSpec((1, d), lambda i: (0, 0)),
                  pl.BlockSpec(memory_space=pl.ANY)],
        out_specs=pl.BlockSpec(memory_space=pl.ANY),
        out_shape=jax.ShapeDtypeStruct((rows * ROW_TILE, LANES), F32),
        scratch_shapes=[pltpu.VMEM((tm * ROW_TILE, LANES), F32), pltpu.SemaphoreType.DMA((1,))],
        input_output_aliases={3: 0},
        compiler_params=_cparams(("arbitrary",)),
        name="moe_dispatch",
    )(dest.reshape(nt, 1, TOP_K * tm), x, g.reshape(1, d), jnp.zeros((rows * ROW_TILE, LANES), F32))


def _expert_kernel(be_ref, nu_ref, x_ref, w1_ref, w3_ref, w2_ref, y_ref):
    del be_ref
    used = pl.program_id(0) < nu_ref[0]

    @pl.when(jnp.logical_not(used))
    def _():
        y_ref[...] = jnp.zeros(y_ref.shape, F32)

    @pl.when(used)
    def _():
        x = _from_tiles(x_ref, x_ref.shape[0] // ROW_TILE).astype(BF16)
        acc = None
        for f in range(MOE_F_SPLIT):
            sl = slice(f * (D_FF // MOE_F_SPLIT), (f + 1) * (D_FF // MOE_F_SPLIT))
            h1 = _dot(x, w1_ref[0, :, sl])
            h3 = _dot(x, w3_ref[0, :, sl])
            act = (h1 * _sigmoid(h1) * h3).astype(BF16)
            part = _dot(act, w2_ref[0, sl, :])
            acc = part if acc is None else acc + part
        _to_tiles(y_ref, acc)


def _experts(buf, blk_e, n_used, w1, w3, w2, blk):
    d, f = w1.shape[1], w1.shape[2]
    once = pl.Buffered(1)
    grid_spec = pltpu.PrefetchScalarGridSpec(
        num_scalar_prefetch=2,
        grid=(buf.shape[0] // (blk * ROW_TILE),),
        in_specs=[pl.BlockSpec((blk * ROW_TILE, LANES), lambda b, be, nu: (b, 0)),
                  pl.BlockSpec((1, d, f), lambda b, be, nu: (be[b], 0, 0), pipeline_mode=once),
                  pl.BlockSpec((1, d, f), lambda b, be, nu: (be[b], 0, 0), pipeline_mode=once),
                  pl.BlockSpec((1, f, d), lambda b, be, nu: (be[b], 0, 0), pipeline_mode=once)],
        out_specs=pl.BlockSpec((blk * ROW_TILE, LANES), lambda b, be, nu: (b, 0)))
    return pl.pallas_call(
        _expert_kernel,
        grid_spec=grid_spec,
        out_shape=jax.ShapeDtypeStruct(buf.shape, F32),
        compiler_params=_cparams(("arbitrary",)),
        name="moe_experts",
    )(blk_e, n_used, buf, w1, w3, w2)


def _combine_kernel(dest_ref, x_ref, rt_ref, fn_ref, y_hbm, o_ref, ya, yb, sem):
    tm = x_ref.shape[0]

    def copies(r):
        return (_row_copy(y_hbm, dest_ref[0, 0, TOP_K * r], ya, r, sem.at[0]),
                _row_copy(y_hbm, dest_ref[0, 0, TOP_K * r + 1], yb, r, sem.at[1]))

    def start(r, c):
        for cp in copies(r):
            cp.start()
        return c

    def wait(r, c):
        for cp in copies(r):
            cp.wait()
        return c

    lax.fori_loop(0, tm, start, 0, unroll=8)
    lax.fori_loop(0, tm, wait, 0, unroll=8)
    rt = rt_ref[...]
    lane = lax.broadcasted_iota(jnp.int32, rt.shape, 1)
    g1 = jnp.sum(jnp.where(lane == 2, rt, 0.0), axis=-1, keepdims=True)
    g2 = jnp.sum(jnp.where(lane == 3, rt, 0.0), axis=-1, keepdims=True)
    moe = g1 * _from_tiles(ya, tm) + g2 * _from_tiles(yb, tm)
    o_ref[...] = _rms(x_ref[...] + moe, fn_ref[...])


def _combine(x, route, fnorm, y, dest, tm):
    t, d = x.shape
    nt = t // tm
    return pl.pallas_call(
        _combine_kernel,
        grid=(nt,),
        in_specs=[pl.BlockSpec((1, 1, TOP_K * tm), lambda i: (i, 0, 0), memory_space=pltpu.SMEM),
                  pl.BlockSpec((tm, d), lambda i: (i, 0)),
                  pl.BlockSpec((tm, LANES), lambda i: (i, 0)),
                  pl.BlockSpec((1, d), lambda i: (0, 0)),
                  pl.BlockSpec(memory_space=pl.ANY)],
        out_specs=pl.BlockSpec((tm, d), lambda i: (i, 0)),
        out_shape=jax.ShapeDtypeStruct((t, d), F32),
        scratch_shapes=[pltpu.VMEM((tm * ROW_TILE, LANES), F32), pltpu.VMEM((tm * ROW_TILE, LANES), F32),
                        pltpu.SemaphoreType.DMA((2,))],
        compiler_params=_cparams(("arbitrary",)),
        name="moe_combine",
    )(dest.reshape(nt, 1, TOP_K * tm), x, route, fnorm.reshape(1, d), y)


def _moe(x, g, router, w1, w3, w2, fnorm, tm):
    route = _router(x, g, router, tm)
    blk = min(MOE_BLOCK, x.shape[0])
    dest, blk_e, n_used, nblk = _route_plan(route, blk)
    buf = _dispatch(x, g, dest, nblk * blk, tm)
    y = _experts(buf, blk_e, n_used, w1, w3, w2, blk)
    return _combine(x, route, fnorm, y, dest, tm)


def _block_diag(w):
    g, a, b = w.shape
    eye = jnp.eye(g, dtype=w.dtype)
    return (eye[:, None, :, None] * w[:, :, None, :]).reshape(g * a, g * b)


def kernel(x, l0_norm_mix, l0_w_in, l0_conv_w, l0_conv_b, l0_gate_a_w, l0_gate_a_b, l0_gate_x_w, l0_gate_x_b, l0_lru_lambda, l0_shift_mu, l0_w0, l0_w_up, l0_a0, l0_a_up, l0_g_up, l0_k_k, l0_k_a, l0_r_k, l0_ln_x_w, l0_ln_x_b, l0_w_out, l0_norm_ffn, l0_ffn_w1, l0_ffn_w3, l0_ffn_w2, l1_norm_mix, l1_w_in, l1_conv_w, l1_conv_b, l1_wq, l1_wk, l1_wv, l1_ig_b, l1_fg_b, l1_mh_norm_w, l1_w_out, l1_norm_ffn, l1_router, l1_moe_w1, l1_moe_w3, l1_moe_w2, final_norm):
    batch, seq, d = x.shape
    t = batch * seq
    xt = x.reshape(t, d)
    tm = min(512, t)

    w_in0 = l0_w_in.astype(BF16)
    p_lru = _norm_mm(xt, l0_norm_mix, w_in0[:, :2 * LRU_WIDTH], tm, "l0_in_lru")
    p_rw = _norm_mm(xt, l0_norm_mix, w_in0[:, 2 * LRU_WIDTH:], tm, "l0_in_rwkv")
    y_lru = _lru(p_lru, batch, seq, l0_conv_w, l0_conv_b,
                 _block_diag(l0_gate_a_w).astype(BF16), l0_gate_a_b,
                 _block_diag(l0_gate_x_w).astype(BF16), l0_gate_x_b, l0_lru_lambda,
                 min(512, seq))
    y_rw = _rwkv(p_rw, batch, seq, l0_shift_mu, l0_w0, l0_w_up, l0_a0, l0_a_up, l0_g_up,
                 l0_k_k, l0_k_a, l0_r_k.reshape(-1), l0_ln_x_w, l0_ln_x_b)
    w_out0 = l0_w_out.astype(BF16)
    x2 = _ffn(xt, y_lru, y_rw, w_out0[:LRU_WIDTH], w_out0[LRU_WIDTH:], l0_norm_ffn,
              l0_ffn_w1.astype(BF16), l0_ffn_w3.astype(BF16), l0_ffn_w2.astype(BF16), tm, D_FF // 2)

    n_in1 = 2 * MLSTM_WIDTH + LANES
    w_in1 = jnp.zeros((d, n_in1), F32).at[:, :l1_w_in.shape[1]].set(l1_w_in).astype(BF16)
    p1 = _norm_mm(x2, l1_norm_mix, w_in1, tm, "l1_in")
    h1 = _mlstm(p1, batch, seq, l1_conv_w, l1_conv_b, l1_wq, l1_wk, l1_wv, l1_ig_b, l1_fg_b,
                l1_mh_norm_w)
    x3 = _mm_res(x2, [(h1, l1_w_out.astype(BF16))], tm, "l1_out")
    router = jnp.zeros((d, LANES), F32).at[:, :N_EXPERTS].set(l1_router)
    out = _moe(x3, l1_norm_ffn, router, l1_moe_w1.astype(BF16), l1_moe_w3.astype(BF16),
               l1_moe_w2.astype(BF16), final_norm, tm)
    return out.reshape(batch, seq, d)
```

```python
import functools

import jax
import jax.numpy as jnp
from jax import lax
from jax.experimental import pallas as pl
from jax.experimental.pallas import tpu as pltpu

F32 = jnp.float32
BF16 = jnp.bfloat16

D_MODEL = 1024
LRU_WIDTH = 512
LRU_BLOCKS = 8
LRU_C = 8.0
LRU_SEGS = 8
CONV_WIDTH = 4
RWKV_HEADS = 8
RWKV_HEAD = 64
RWKV_WIDTH = 512
LN_X_EPS = 1e-5 * RWKV_HEAD
MLSTM_HEADS = 8
MLSTM_HEAD = 128
MLSTM_WIDTH = 1024
MLSTM_CHUNK = 128
D_FF = 2816
N_EXPERTS = 8
TOP_K = 2
ROUTER_SPLIT = 4
MOE_BLOCK = 512
MOE_F_SPLIT = 2
RMS_EPS = 1e-6
RW_IN = 3 * RWKV_WIDTH + 64 + 64 + 128
RW_CHUNK = 64
RW_STEP_ROWS = 256
LANES = 128
CARRY_ROWS = 8
ROW_TILE = 8
VMEM_LIMIT = 56 * 1024 * 1024


def _cparams(sem):
    return pltpu.CompilerParams(dimension_semantics=sem, vmem_limit_bytes=VMEM_LIMIT)


def _rms(x, w):
    return x * lax.rsqrt(jnp.mean(x * x, axis=-1, keepdims=True) + RMS_EPS) * w


def _sigmoid(x):
    return 1.0 / (1.0 + jnp.exp(-x))


def _softplus(x):
    return jnp.maximum(x, 0.0) + jnp.log(1.0 + jnp.exp(-jnp.abs(x)))


def _dot(a, b):
    return jnp.dot(a, b, preferred_element_type=F32)


def _dot_nt(a, b):
    return lax.dot_general(a, b, (((1,), (1,)), ((), ())), preferred_element_type=F32)


def _dot_tn(a, b):
    return lax.dot_general(a, b, (((0,), (0,)), ((), ())), preferred_element_type=F32)


def _split2(x):
    hi = x.astype(BF16)
    lo = (x - hi.astype(F32)).astype(BF16)
    return hi, lo


def _split3(x):
    hi = x.astype(BF16)
    r = x - hi.astype(F32)
    mid = r.astype(BF16)
    lo = (r - mid.astype(F32)).astype(BF16)
    return hi, mid, lo


def _mm3(fn, a, b):
    ah, al = _split2(a)
    bh, bl = _split2(b)
    return fn(ah, bh) + fn(al, bh) + fn(ah, bl)


def _mm_exact_lhs(fn, a_bf16, b):
    h, m, l = _split3(b)
    return fn(a_bf16, h) + fn(a_bf16, m) + fn(a_bf16, l)


def _mm_exact_rhs(fn, a, b_bf16):
    h, m, l = _split3(a)
    return fn(h, b_bf16) + fn(m, b_bf16) + fn(l, b_bf16)


def _shift_hist(buf_ref, x, first):
    n = x.shape[0]

    @pl.when(first)
    def _():
        buf_ref[0:CARRY_ROWS, :] = jnp.zeros((CARRY_ROWS, x.shape[1]), F32)

    @pl.when(jnp.logical_not(first))
    def _():
        buf_ref[0:CARRY_ROWS, :] = buf_ref[n:n + CARRY_ROWS, :]

    buf_ref[CARRY_ROWS:CARRY_ROWS + n, :] = x


def _causal_conv(buf_ref, n, w, b):
    acc = b
    for j in range(CONV_WIDTH):
        off = CARRY_ROWS - (CONV_WIDTH - 1) + j
        acc = acc + w[j:j + 1, :] * buf_ref[off:off + n, :]
    return acc


def _norm_mm_kernel(x_ref, g_ref, w_ref, *o_refs):
    xn = _rms(x_ref[...], g_ref[...]).astype(BF16)
    c0 = 0
    for o_ref in o_refs:
        c1 = c0 + o_ref.shape[1]
        o_ref[...] = _dot(xn, w_ref[:, c0:c1]).astype(o_ref.dtype)
        c0 = c1


def _norm_mm(x, g, w, widths, tm, name):
    t, d = x.shape
    n = w.shape[1]
    assert sum(widths) == n
    return pl.pallas_call(
        _norm_mm_kernel,
        grid=(t // tm,),
        in_specs=[pl.BlockSpec((tm, d), lambda i: (i, 0)),
                  pl.BlockSpec((1, d), lambda i: (0, 0)),
                  pl.BlockSpec((d, n), lambda i: (0, 0))],
        out_specs=[pl.BlockSpec((tm, c), lambda i: (i, 0)) for c in widths],
        out_shape=[jax.ShapeDtypeStruct((t, c), BF16) for c in widths],
        compiler_params=_cparams(("parallel",)),
        name=name,
    )(x, g.reshape(1, d), w)


def _lru_kernel(p_ref, cw_ref, cb_ref, wa_ref, ba_ref, wx_ref, bx_ref, lam_ref, o_ref,
                xbuf, abuf, bbuf, hbuf, hcar):
    i = pl.program_id(1)
    n = p_ref.shape[0]
    first = i == 0
    _shift_hist(xbuf, p_ref[:, 0:LRU_WIDTH].astype(F32), first)
    xc = _causal_conv(xbuf, n, cw_ref[...], cb_ref[...])
    xcb = xc.astype(BF16)
    r = _sigmoid(_dot(xcb, wa_ref[...]) + ba_ref[...])
    ig = _sigmoid(_dot(xcb, wx_ref[...]) + bx_ref[...])
    log_a = (-LRU_C) * r * _softplus(-lam_ref[...])
    a = jnp.exp(log_a)
    mult = jnp.sqrt(1.0 - jnp.exp(2.0 * log_a))
    row = lax.broadcasted_iota(jnp.int32, (n, 1), 0)
    mult = jnp.where(jnp.logical_and(first, row == 0), 1.0, mult)
    bvals = mult * ig * xc
    n_slab = LRU_WIDTH // LANES
    slabs = [slice(c * LANES, (c + 1) * LANES) for c in range(n_slab)]
    for c, cs in enumerate(slabs):
        abuf[c] = a[:, cs]
        bbuf[c] = bvals[:, cs]

    @pl.when(first)
    def _():
        hcar[...] = jnp.zeros(hcar.shape, F32)

    seg = n // LRU_SEGS

    def step(j, carry):
        rows = pl.ds(j, LRU_SEGS, stride=seg)
        out = []
        for c in range(n_slab):
            h, acc = carry[c]
            a_j = abuf[c, rows, :]
            h = a_j * h + bbuf[c, rows, :]
            acc = acc * a_j
            hbuf[c, rows, :] = h
            abuf[c, rows, :] = acc
            out.append((h, acc))
        return tuple(out)

    init = tuple((jnp.zeros((LRU_SEGS, LANES), F32), jnp.ones((LRU_SEGS, LANES), F32))
                 for _ in range(n_slab))
    ends = lax.fori_loop(0, seg, step, init, unroll=8)
    cols = []
    for c, cs in enumerate(slabs):
        h_end, a_end = ends[c]
        h0 = hcar[0:1, cs]
        parts = []
        for s in range(LRU_SEGS):
            parts.append(hbuf[c, s * seg:(s + 1) * seg, :] + abuf[c, s * seg:(s + 1) * seg, :] * h0)
            h0 = h_end[s:s + 1, :] + a_end[s:s + 1, :] * h0
        hcar[0:1, cs] = h0
        cols.append(jnp.concatenate(parts, axis=0))
    gate = p_ref[:, LRU_WIDTH:2 * LRU_WIDTH].astype(F32)
    gelu = 0.5 * gate * (1.0 + jnp.tanh(0.7978845608028654 * (gate + 0.044715 * gate * gate * gate)))
    o_ref[...] = (jnp.concatenate(cols, axis=1) * gelu).astype(o_ref.dtype)


def _lru(p_lru, batch, seq, conv_w, conv_b, wa, ba, wx, bx, lam, tb):
    nb = seq // tb
    c = LRU_WIDTH
    vec = lambda: pl.BlockSpec((1, c), lambda b, i: (0, 0))
    return pl.pallas_call(
        _lru_kernel,
        grid=(batch, nb),
        in_specs=[pl.BlockSpec((tb, 2 * c), lambda b, i: (b * nb + i, 0)),
                  pl.BlockSpec((CONV_WIDTH, c), lambda b, i: (0, 0)), vec(),
                  pl.BlockSpec((c, c), lambda b, i: (0, 0)), vec(),
                  pl.BlockSpec((c, c), lambda b, i: (0, 0)), vec(), vec()],
        out_specs=pl.BlockSpec((tb, c), lambda b, i: (b * nb + i, 0)),
        out_shape=jax.ShapeDtypeStruct((batch * seq, c), BF16),
        scratch_shapes=[pltpu.VMEM((tb + CARRY_ROWS, c), F32)]
        + [pltpu.VMEM((c // LANES, tb, LANES), F32)] * 3
        + [pltpu.VMEM((CARRY_ROWS, c), F32)],
        compiler_params=_cparams(("parallel", "arbitrary")),
        name="rg_lru",
    )(p_lru, conv_w, conv_b.reshape(1, c), wa, ba.reshape(1, c), wx, bx.reshape(1, c),
      lam.reshape(1, c))


def _blk(x, masks):
    xb = x.astype(BF16)
    return jnp.concatenate([jnp.where(mk, xb, jnp.zeros_like(xb)) for mk in masks], axis=0)


def _rwkv_kernel(p_ref, mu_ref, w0_ref, wup_ref, a0_ref, aup_ref, gup_ref, kk_ref, ka_ref,
                 rk_ref, lnw_ref, lnb_ref, seg_ref, tri_ref, o_ref, pbuf, state):
    L = RW_CHUNK
    W = RWKV_WIDTH
    rows = p_ref.shape[0]
    first = pl.program_id(1) == 0
    p = p_ref[...].astype(F32)
    _shift_hist(pbuf, p, first)
    prev = pbuf[CARRY_ROWS - 1:CARRY_ROWS - 1 + rows, :]
    ps = p + mu_ref[...] * (prev - p)
    r = ps[:, 0:W]
    k = ps[:, W:2 * W]
    v = ps[:, 2 * W:3 * W]
    x2 = ps[:, 3 * W:3 * W + LANES]
    xg = ps[:, 3 * W + LANES:3 * W + 2 * LANES]
    wl = w0_ref[...] + _dot(jnp.tanh(x2).astype(BF16), wup_ref[...])
    lw = -jnp.exp(-_softplus(-wl) - 0.5)
    a = _sigmoid(a0_ref[...] + _dot(x2.astype(BF16), aup_ref[...]))
    g = _dot(_sigmoid(xg).astype(BF16), gup_ref[...])
    seg = seg_ref[...]
    kk = k * kk_ref[...]
    kk = kk / jnp.maximum(jnp.sqrt(_mm_exact_rhs(_dot, kk * kk, seg)), 1e-12)
    k2 = k * (1.0 + (a - 1.0) * ka_ref[...])

    cw = _mm_exact_lhs(_dot, tri_ref[...], lw)
    cw_end = jnp.concatenate(
        [jnp.broadcast_to(cw[c * L + L - 1:c * L + L, :], (L, W)) for c in range(rows // L)], axis=0)
    w_in = jnp.exp(cw)
    w_inv = jnp.exp(-cw)
    w_end = jnp.exp(cw_end - cw)
    kka = kk * a
    a_t = -kk * jnp.exp(cw - lw)
    b_t = kka * w_inv
    k_t = k2 * w_inv
    r_t = r * w_in
    b_bar = kka * w_end
    k_bar = k2 * w_end
    w_tot = jnp.exp(cw_end)

    @pl.when(first)
    def _():
        state[...] = jnp.zeros(state.shape, F32)

    lane = lax.broadcasted_iota(jnp.int32, (1, LANES), 1)
    m1 = [lane < RWKV_HEAD, lane >= RWKV_HEAD]
    m2 = [jnp.concatenate([mk, mk], axis=1) for mk in m1]
    ti = lax.broadcasted_iota(jnp.int32, (L, LANES), 0)
    si = lax.broadcasted_iota(jnp.int32, (L, LANES), 1) % RWKV_HEAD
    strict = si < ti
    incl = si <= ti
    bi = lax.broadcasted_iota(jnp.int32, (LANES, LANES), 0) // RWKV_HEAD
    bj = lax.broadcasted_iota(jnp.int32, (LANES, LANES), 1) // RWKV_HEAD
    bd = bi == bj

    def tril(mask, s):
        return jnp.where(mask, s, 0.0).astype(BF16)

    n_pairs = RWKV_HEADS // 2
    n_ch = rows // L
    chains = [(slice(c * L, (c + 1) * L), slice(pr * LANES, (pr + 1) * LANES))
              for c in range(n_ch) for pr in range(n_pairs)]
    idx = range(len(chains))
    a_c = [a_t[rs, sl] for rs, sl in chains]
    r_c = [r_t[rs, sl] for rs, sl in chains]
    ar = [jnp.concatenate([a_c[i], r_c[i]], axis=0).astype(BF16) for i in idx]
    sb = [_dot_nt(ar[i], _blk(b_t[rs, sl], m1)) for i, (rs, sl) in enumerate(chains)]
    sk = [_dot_nt(ar[i], _blk(k_t[rs, sl], m1)) for i, (rs, sl) in enumerate(chains)]
    vblk = [_blk(v[rs, sl], m1) for rs, sl in chains]
    rhs = [jnp.concatenate([a_c[i], _dot(tril(strict, sk[i][0:L]), vblk[i])], axis=1) for i in idx]
    pw = [tril(strict, sb[i][0:L]) for i in idx]
    tinv = [jnp.where(si == ti, 1.0, 0.0) + pw[i].astype(F32) for i in idx]
    for _ in range(5):
        pw = [_dot(pw[i], _blk(pw[i], m1)).astype(BF16) for i in idx]
        tinv = [tinv[i] + _dot(pw[i], _blk(tinv[i], m1)) for i in idx]
    x = [_dot(tinv[i].astype(BF16), _blk(rhs[i], m2)) for i in idx]
    corr = [_dot(tril(incl, sb[i][L:2 * L]), _blk(x[i], m2)) for i in idx]
    r_hat = [(r_c[i] + corr[i][:, 0:LANES]).astype(BF16) for i in idx]
    y0 = [corr[i][:, LANES:2 * LANES] + _dot(tril(incl, sk[i][L:2 * L]), vblk[i]) for i in idx]
    xb = [x[i].astype(BF16) for i in idx]
    bb = [b_bar[rs, sl].astype(BF16) for rs, sl in chains]
    gmat = [jnp.where(bd, _dot_tn(xb[i][:, 0:LANES], bb[i]), 0.0).astype(BF16) for i in idx]
    hmat = [jnp.where(bd, _dot_tn(xb[i][:, LANES:2 * LANES], bb[i])
                      + _dot_tn(v[rs, sl].astype(BF16), k_bar[rs, sl].astype(BF16)), 0.0)
            for i, (rs, sl) in enumerate(chains)]

    s_cur = [state[pr] for pr in range(n_pairs)]
    y_rows = []
    for c in range(n_ch):
        ys = []
        for pr in range(n_pairs):
            i = c * n_pairs + pr
            s0 = s_cur[pr]
            s0b = s0.astype(BF16)
            ys.append(_dot_nt(r_hat[i], s0b) + y0[i])
            s_cur[pr] = s0 * w_tot[c * L:c * L + 1, chains[i][1]] + _dot(s0b, gmat[i]) + hmat[i]
        y_rows.append(jnp.concatenate(ys, axis=1))
    for pr in range(n_pairs):
        state[pr] = s_cur[pr]
    y = jnp.concatenate(y_rows, axis=0)

    inv = 1.0 / RWKV_HEAD
    mean = _mm_exact_rhs(_dot, y, seg) * inv
    yc = y - mean
    var = _mm_exact_rhs(_dot, yc * yc, seg) * inv
    yn = yc * lax.rsqrt(var + LN_X_EPS) * lnw_ref[...] + lnb_ref[...]
    bonus = _mm_exact_rhs(_dot, r * k2 * rk_ref[...], seg) * v
    o_ref[...] = ((yn + bonus) * g).astype(o_ref.dtype)


def _rwkv(p_rw, batch, seq, mu, w0, w_up, a0, a_up, g_up, k_k, k_a, r_k, ln_w, ln_b):
    L = min(RW_STEP_ROWS, seq)
    nc = seq // L
    W = RWKV_WIDTH
    wup = jnp.zeros((LANES, W), F32).at[0:64].set(w_up).astype(BF16)
    aup = jnp.zeros((LANES, W), F32).at[64:128].set(a_up).astype(BF16)
    hid = jnp.arange(W) // RWKV_HEAD
    seg = (hid[:, None] == hid[None, :]).astype(BF16)
    ri = jnp.arange(L)
    tri = jnp.logical_and(ri[:, None] >= ri[None, :],
                          ri[:, None] // RW_CHUNK == ri[None, :] // RW_CHUNK).astype(BF16)
    vec = lambda: pl.BlockSpec((1, W), lambda b, i: (0, 0))
    mat = lambda s: pl.BlockSpec(s, lambda b, i: (0, 0))
    return pl.pallas_call(
        _rwkv_kernel,
        grid=(batch, nc),
        in_specs=[pl.BlockSpec((L, RW_IN), lambda b, i: (b * nc + i, 0)),
                  mat((1, RW_IN)), vec(), mat((LANES, W)), vec(), mat((LANES, W)),
                  mat((LANES, W)), vec(), vec(), vec(), vec(), vec(), mat((W, W)), mat((L, L))],
        out_specs=pl.BlockSpec((L, W), lambda b, i: (b * nc + i, 0)),
        out_shape=jax.ShapeDtypeStruct((batch * seq, W), BF16),
        scratch_shapes=[pltpu.VMEM((L + CARRY_ROWS, RW_IN), F32),
                        pltpu.VMEM((RWKV_HEADS // 2, LANES, LANES), F32)],
        compiler_params=_cparams(("parallel", "arbitrary")),
        name="rwkv7",
    )(p_rw, mu.reshape(1, RW_IN), w0.reshape(1, W), wup, a0.reshape(1, W), aup,
      g_up.astype(BF16), k_k.reshape(1, W), k_a.reshape(1, W), r_k.reshape(1, W),
      ln_w.reshape(1, W), ln_b.reshape(1, W), seg, tri)


def _ffn_kernel(x_ref, ya_ref, yb_ref, wa_ref, wb_ref, g_ref, w1_ref, w3_ref, w2_ref, o_ref, xn_s, acc_s):
    f = pl.program_id(1)

    @pl.when(f == 0)
    def _():
        x1 = x_ref[...] + _dot(ya_ref[...], wa_ref[...]) + _dot(yb_ref[...], wb_ref[...])
        xn_s[...] = _rms(x1, g_ref[...]).astype(BF16)
        acc_s[...] = x1

    xn = xn_s[...]
    h1 = _dot(xn, w1_ref[...])
    h3 = _dot(xn, w3_ref[...])
    act = (h1 * _sigmoid(h1) * h3).astype(BF16)
    acc_s[...] += _dot(act, w2_ref[...])

    @pl.when(f == pl.num_programs(1) - 1)
    def _():
        o_ref[...] = acc_s[...]


def _ffn(x, ya, yb, wa, wb, g, w1, w3, w2, tm, tf):
    t, d = x.shape
    nf = w1.shape[1] // tf
    return pl.pallas_call(
        _ffn_kernel,
        grid=(t // tm, nf),
        in_specs=[pl.BlockSpec((tm, d), lambda i, f: (i, 0)),
                  pl.BlockSpec((tm, ya.shape[1]), lambda i, f: (i, 0)),
                  pl.BlockSpec((tm, yb.shape[1]), lambda i, f: (i, 0)),
                  pl.BlockSpec(wa.shape, lambda i, f: (0, 0)),
                  pl.BlockSpec(wb.shape, lambda i, f: (0, 0)),
                  pl.BlockSpec((1, d), lambda i, f: (0, 0)),
                  pl.BlockSpec((d, tf), lambda i, f: (0, f)),
                  pl.BlockSpec((d, tf), lambda i, f: (0, f)),
                  pl.BlockSpec((tf, d), lambda i, f: (f, 0))],
        out_specs=pl.BlockSpec((tm, d), lambda i, f: (i, 0)),
        out_shape=jax.ShapeDtypeStruct((t, d), F32),
        scratch_shapes=[pltpu.VMEM((tm, d), BF16), pltpu.VMEM((tm, d), F32)],
        compiler_params=_cparams(("parallel", "arbitrary")),
        name="ffn_swiglu",
    )(x, ya, yb, wa, wb, g.reshape(1, d), w1, w3, w2)


def _mlstm_kernel(xm_ref, z_ref, gt_ref, cw_ref, cb_ref, wq_ref, wk_ref, wv_ref, gb_ref,
                  nw_ref, tri_ref, o_ref, xbuf, c_s, n_s, m_s):
    ci = pl.program_id(1)
    L = MLSTM_CHUNK
    dh = MLSTM_HEAD
    first = ci == 0
    xm = xm_ref[...].astype(F32)
    _shift_hist(xbuf, xm, first)
    xc = _causal_conv(xbuf, L, cw_ref[...], cb_ref[...])
    xc = xc * _sigmoid(xc)
    gl = gt_ref[...].astype(F32) + gb_ref[...]
    lf = jnp.minimum(gl, 0.0) - jnp.log(1.0 + jnp.exp(-jnp.abs(gl)))
    bcum = _mm_exact_lhs(_dot, tri_ref[...], lf)
    gl_t = gl.T
    bcum_t = bcum.T

    @pl.when(first)
    def _():
        c_s[...] = jnp.zeros(c_s.shape, F32)
        n_s[...] = jnp.zeros(n_s.shape, F32)
        m_s[...] = jnp.zeros(m_s.shape, F32)

    ti = lax.broadcasted_iota(jnp.int32, (L, L), 0)
    si = lax.broadcasted_iota(jnp.int32, (L, L), 1)
    causal = si <= ti
    scale = dh ** -0.5
    hs = range(MLSTM_HEADS)
    sls = [slice(h * dh, (h + 1) * dh) for h in hs]
    xs = [xc[:, sl].astype(BF16) for sl in sls]
    q = [_dot(xs[h], wq_ref[h]) * scale for h in hs]
    k = [_dot(xs[h], wk_ref[h]) for h in hs]
    vb = [_dot(xm[:, sls[h]].astype(BF16), wv_ref[h]).astype(BF16) for h in hs]
    qb = [q[h].astype(BF16) for h in hs]
    kb = [k[h].astype(BF16) for h in hs]
    qk = [_dot_nt(qb[h], kb[h]) for h in hs]
    qc = [_dot(qb[h], c_s[h].astype(BF16)) for h in hs]
    bcol = [bcum[:, MLSTM_HEADS + h:MLSTM_HEADS + h + 1] for h in hs]
    m_prev = [m_s[h:h + 1, 0:1] for h in hs]
    dlog = [jnp.where(causal, bcol[h] - bcum_t[MLSTM_HEADS + h:MLSTM_HEADS + h + 1, :] + gl_t[h:h + 1, :],
                      -jnp.inf) for h in hs]
    inter = [bcol[h] + m_prev[h] for h in hs]
    m_t = [jnp.maximum(inter[h], jnp.max(dlog[h], axis=-1, keepdims=True)) for h in hs]
    s = [qk[h] * jnp.exp(dlog[h] - m_t[h]) for h in hs]
    sv = [_dot(s[h].astype(BF16), vb[h]) for h in hs]
    b_last = [bcol[h][L - 1:L, :] for h in hs]
    wlog = [b_last[h] - bcol[h] + gl[:, h:h + 1] for h in hs]
    m_new = [jnp.maximum(b_last[h] + m_prev[h], jnp.max(wlog[h], axis=0, keepdims=True)) for h in hs]
    kw = [k[h] * jnp.exp(wlog[h] - m_new[h]) for h in hs]
    kv = [_dot_tn(kw[h].astype(BF16), vb[h]) for h in hs]
    sc = [jnp.exp(inter[h] - m_t[h]) for h in hs]
    qn = [jnp.sum(q[h] * n_s[h:h + 1, :], axis=-1, keepdims=True) for h in hs]
    ssum = [jnp.sum(s[h], axis=-1, keepdims=True) for h in hs]
    den = [sc[h] * qn[h] + ssum[h] for h in hs]
    hh = [(sc[h] * qc[h] + sv[h]) / jnp.maximum(jnp.abs(den[h]), jnp.exp(-m_t[h])) for h in hs]
    ms = [jnp.mean(hh[h] * hh[h], axis=-1, keepdims=True) for h in hs]
    dec = [jnp.exp(b_last[h] + m_prev[h] - m_new[h]) for h in hs]
    for h in hs:
        c_s[h] = dec[h] * c_s[h] + kv[h]
        n_s[h:h + 1, :] = dec[h] * n_s[h:h + 1, :] + jnp.sum(kw[h], axis=0, keepdims=True)
        m_s[h:h + 1, :] = jnp.broadcast_to(m_new[h], (1, LANES))
        hn = hh[h] * lax.rsqrt(ms[h] + RMS_EPS)
        gate = _sigmoid(z_ref[:, sls[h]].astype(F32))
        o_ref[:, sls[h]] = (gate * (hn * nw_ref[:, sls[h]])).astype(o_ref.dtype)


def _mlstm(p1, batch, seq, conv_w, conv_b, wq, wk, wv, ig_b, fg_b, mh_w):
    L = MLSTM_CHUNK
    nc = seq // L
    W = MLSTM_WIDTH
    gb = jnp.zeros((1, LANES), F32).at[0, 0:8].set(ig_b).at[0, 8:16].set(fg_b)
    tri = (jnp.arange(L)[:, None] >= jnp.arange(L)[None, :]).astype(BF16)
    hw = lambda: pl.BlockSpec((MLSTM_HEADS, MLSTM_HEAD, MLSTM_HEAD), lambda b, i: (0, 0, 0))
    return pl.pallas_call(
        _mlstm_kernel,
        grid=(batch, nc),
        in_specs=[pl.BlockSpec((L, W), lambda b, i: (b * nc + i, 0)),
                  pl.BlockSpec((L, W), lambda b, i: (b * nc + i, 1)),
                  pl.BlockSpec((L, LANES), lambda b, i: (b * nc + i, 2 * W // LANES)),
                  pl.BlockSpec((CONV_WIDTH, W), lambda b, i: (0, 0)),
                  pl.BlockSpec((1, W), lambda b, i: (0, 0)),
                  hw(), hw(), hw(),
                  pl.BlockSpec((1, LANES), lambda b, i: (0, 0)),
                  pl.BlockSpec((1, W), lambda b, i: (0, 0)),
                  pl.BlockSpec((L, L), lambda b, i: (0, 0))],
        out_specs=pl.BlockSpec((L, W), lambda b, i: (b * nc + i, 0)),
        out_shape=jax.ShapeDtypeStruct((batch * seq, W), BF16),
        scratch_shapes=[pltpu.VMEM((L + CARRY_ROWS, W), F32),
                        pltpu.VMEM((MLSTM_HEADS, MLSTM_HEAD, MLSTM_HEAD), F32),
                        pltpu.VMEM((MLSTM_HEADS, LANES), F32),
                        pltpu.VMEM((MLSTM_HEADS, LANES), F32)],
        compiler_params=_cparams(("parallel", "arbitrary")),
        name="mlstm",
    )(p1, p1, p1, conv_w, conv_b.reshape(1, W), wq.astype(BF16), wk.astype(BF16),
      wv.astype(BF16), gb, mh_w.reshape(1, W), tri)


def _out_router_kernel(x_ref, y_ref, w_ref, g_ref, rt_ref, x3_ref, o_ref):
    tm = x_ref.shape[0]
    sub = tm // ROUTER_SPLIT
    rs = [slice(q * sub, (q + 1) * sub) for q in range(ROUTER_SPLIT)]
    qs = range(ROUTER_SPLIT)
    w = w_ref[...]
    x3 = [x_ref[r, :] + _dot(y_ref[r, :], w) for r in rs]
    for q in qs:
        x3_ref[rs[q], :] = x3[q]
    gw = g_ref[...]
    rt = rt_ref[...]
    logits = [_mm3(_dot, _rms(x3[q], gw), rt) for q in qs]
    lane = lax.broadcasted_iota(jnp.int32, (sub, LANES), 1)
    lg = [jnp.where(lane < N_EXPERTS, logits[q], -jnp.inf) for q in qs]
    v1 = [jnp.max(lg[q], axis=-1, keepdims=True) for q in qs]
    i1 = [jnp.min(jnp.where(lg[q] == v1[q], lane, LANES), axis=-1, keepdims=True) for q in qs]
    lg2 = [jnp.where(lane == i1[q], -jnp.inf, lg[q]) for q in qs]
    v2 = [jnp.max(lg2[q], axis=-1, keepdims=True) for q in qs]
    i2 = [jnp.min(jnp.where(lg2[q] == v2[q], lane, LANES), axis=-1, keepdims=True) for q in qs]
    for q in qs:
        ex = jnp.exp(v2[q] - v1[q])
        g1 = 1.0 / (1.0 + ex)
        g2 = ex / (1.0 + ex)
        out = jnp.where(lane == 0, i1[q].astype(F32), 0.0)
        out = jnp.where(lane == 1, i2[q].astype(F32), out)
        out = jnp.where(lane == 2, g1, out)
        o_ref[rs[q], :] = jnp.where(lane == 3, g2, out)


def _out_router(x, y, w, g, router, tm):
    t, d = x.shape
    return pl.pallas_call(
        _out_router_kernel,
        grid=(t // tm,),
        in_specs=[pl.BlockSpec((tm, d), lambda i: (i, 0)),
                  pl.BlockSpec((tm, y.shape[1]), lambda i: (i, 0)),
                  pl.BlockSpec(w.shape, lambda i: (0, 0)),
                  pl.BlockSpec((1, d), lambda i: (0, 0)),
                  pl.BlockSpec((d, LANES), lambda i: (0, 0))],
        out_specs=[pl.BlockSpec((tm, d), lambda i: (i, 0)),
                   pl.BlockSpec((tm, LANES), lambda i: (i, 0))],
        out_shape=[jax.ShapeDtypeStruct((t, d), F32), jax.ShapeDtypeStruct((t, LANES), F32)],
        compiler_params=_cparams(("parallel",)),
        name="l1_out_router",
    )(x, y, w, g.reshape(1, d), router)


def _route_plan(route, blk):
    e = route[:, 0:TOP_K].astype(jnp.int32).reshape(-1)
    m = e.shape[0]
    onehot = (e[:, None] == jnp.arange(N_EXPERTS, dtype=jnp.int32)[None, :]).astype(jnp.int32)
    csum = jnp.cumsum(onehot, axis=0)
    rank = jnp.sum((csum - onehot) * onehot, axis=1)
    nb = (csum[-1] + blk - 1) // blk
    bend = jnp.cumsum(nb)
    dest = jnp.sum(onehot * (bend - nb)[None, :], axis=1) * blk + rank
    nblk = m // blk + N_EXPERTS
    n_used = bend[-1]
    bidx = jnp.minimum(jnp.arange(nblk, dtype=jnp.int32), n_used - 1)
    blk_e = jnp.minimum(jnp.searchsorted(bend, bidx, side="right"), N_EXPERTS - 1).astype(jnp.int32)
    return dest.astype(jnp.int32), blk_e, n_used.reshape(1).astype(jnp.int32), nblk


def _to_tiles(ref, x):
    n = x.shape[0]
    for s in range(ROW_TILE):
        ref[pl.ds(s, n, stride=ROW_TILE), :] = x[:, s * LANES:(s + 1) * LANES]


def _from_tiles(ref, n):
    return jnp.concatenate([ref[pl.ds(s, n, stride=ROW_TILE), :] for s in range(ROW_TILE)], axis=1)


def _row_copy(src, src_row, dst, dst_row, sem):
    s0 = pl.multiple_of(src_row * ROW_TILE, ROW_TILE)
    d0 = pl.multiple_of(dst_row * ROW_TILE, ROW_TILE)
    return pltpu.make_async_copy(src.at[pl.ds(s0, ROW_TILE), :], dst.at[pl.ds(d0, ROW_TILE), :], sem)


def _wait_rows(hbm, vmem, sem, to_hbm):
    rows = hbm.at[pl.ds(0, vmem.shape[0]), :]
    (pltpu.make_async_copy(vmem, rows, sem) if to_hbm else pltpu.make_async_copy(rows, vmem, sem)).wait()


def _dispatch_kernel(nt, dest_ref, x_ref, g_ref, buf_in, buf_out, xn_s, sem):
    del buf_in
    i = pl.program_id(0)
    tm = x_ref.shape[0]
    slot = i % 2

    def drain(s):
        for _ in range(TOP_K):
            _wait_rows(buf_out, xn_s.at[s], sem.at[s], True)

    @pl.when(i >= 2)
    def _():
        drain(slot)

    _to_tiles(xn_s.at[slot], _rms(x_ref[...], g_ref[...]))

    def start(j, c):
        _row_copy(xn_s.at[slot], j // TOP_K, buf_out, dest_ref[0, 0, j], sem.at[slot]).start()
        return c

    lax.fori_loop(0, TOP_K * tm, start, 0, unroll=8)

    @pl.when(i == nt - 1)
    def _():
        drain(slot)
        if nt > 1:
            drain(1 - slot)


def _dispatch(x, g, dest, rows, tm):
    t, d = x.shape
    nt = t // tm
    return pl.pallas_call(
        functools.partial(_dispatch_kernel, nt),
        grid=(nt,),
        in_specs=[pl.BlockSpec((1, 1, TOP_K * tm), lambda i: (i, 0, 0), memory_space=pltpu.SMEM),
                  pl.BlockSpec((tm, d), lambda i: (i, 0)),
                  pl.BlockSpec((1, d), lambda i: (0, 0)),
                  pl.BlockSpec(memory_space=pl.ANY)],
        out_specs=pl.BlockSpec(memory_space=pl.ANY),
        out_shape=jax.ShapeDtypeStruct((rows * ROW_TILE, LANES), F32),
        scratch_shapes=[pltpu.VMEM((2, tm * ROW_TILE, LANES), F32), pltpu.SemaphoreType.DMA((2,))],
        input_output_aliases={3: 0},
        compiler_params=_cparams(("arbitrary",)),
        name="moe_dispatch",
    )(dest.reshape(nt, 1, TOP_K * tm), x, g.reshape(1, d), jnp.zeros((rows * ROW_TILE, LANES), F32))


def _expert_kernel(be_ref, nu_ref, x_ref, w1_ref, w3_ref, w2_ref, y_ref):
    del be_ref
    used = pl.program_id(0) < nu_ref[0]

    @pl.when(jnp.logical_not(used))
    def _():
        y_ref[...] = jnp.zeros(y_ref.shape, F32)

    @pl.when(used)
    def _():
        x = _from_tiles(x_ref, x_ref.shape[0] // ROW_TILE).astype(BF16)
        acc = None
        for f in range(MOE_F_SPLIT):
            sl = slice(f * (D_FF // MOE_F_SPLIT), (f + 1) * (D_FF // MOE_F_SPLIT))
            h1 = _dot(x, w1_ref[0, :, sl])
            h3 = _dot(x, w3_ref[0, :, sl])
            act = (h1 * _sigmoid(h1) * h3).astype(BF16)
            part = _dot(act, w2_ref[0, sl, :])
            acc = part if acc is None else acc + part
        _to_tiles(y_ref, acc)


def _experts(buf, blk_e, n_used, w1, w3, w2, blk):
    d, f = w1.shape[1], w1.shape[2]
    once = pl.Buffered(1)
    grid_spec = pltpu.PrefetchScalarGridSpec(
        num_scalar_prefetch=2,
        grid=(buf.shape[0] // (blk * ROW_TILE),),
        in_specs=[pl.BlockSpec((blk * ROW_TILE, LANES), lambda b, be, nu: (b, 0)),
                  pl.BlockSpec((1, d, f), lambda b, be, nu: (be[b], 0, 0), pipeline_mode=once),
                  pl.BlockSpec((1, d, f), lambda b, be, nu: (be[b], 0, 0), pipeline_mode=once),
                  pl.BlockSpec((1, f, d), lambda b, be, nu: (be[b], 0, 0), pipeline_mode=once)],
        out_specs=pl.BlockSpec((blk * ROW_TILE, LANES), lambda b, be, nu: (b, 0)))
    return pl.pallas_call(
        _expert_kernel,
        grid_spec=grid_spec,
        out_shape=jax.ShapeDtypeStruct(buf.shape, F32),
        compiler_params=_cparams(("arbitrary",)),
        name="moe_experts",
    )(blk_e, n_used, buf, w1, w3, w2)


def _combine_kernel(nt, dcur_ref, dnxt_ref, x_ref, rt_ref, fn_ref, y_hbm, o_ref, ya, yb, sem):
    i = pl.program_id(0)
    tm = x_ref.shape[0]
    slot = i % 2

    def gather(dref, s):
        def start(r, c):
            _row_copy(y_hbm, dref[0, 0, TOP_K * r], ya.at[s], r, sem.at[s, 0]).start()
            _row_copy(y_hbm, dref[0, 0, TOP_K * r + 1], yb.at[s], r, sem.at[s, 1]).start()
            return c

        lax.fori_loop(0, tm, start, 0, unroll=8)

    @pl.when(i == 0)
    def _():
        gather(dcur_ref, slot)

    @pl.when(i + 1 < nt)
    def _():
        gather(dnxt_ref, 1 - slot)

    _wait_rows(y_hbm, ya.at[slot], sem.at[slot, 0], False)
    _wait_rows(y_hbm, yb.at[slot], sem.at[slot, 1], False)
    rt = rt_ref[...]
    lane = lax.broadcasted_iota(jnp.int32, rt.shape, 1)
    g1 = jnp.sum(jnp.where(lane == 2, rt, 0.0), axis=-1, keepdims=True)
    g2 = jnp.sum(jnp.where(lane == 3, rt, 0.0), axis=-1, keepdims=True)
    moe = g1 * _from_tiles(ya.at[slot], tm) + g2 * _from_tiles(yb.at[slot], tm)
    o_ref[...] = _rms(x_ref[...] + moe, fn_ref[...])


def _combine(x, route, fnorm, y, dest, tm):
    t, d = x.shape
    nt = t // tm
    dest3 = dest.reshape(nt, 1, TOP_K * tm)
    return pl.pallas_call(
        functools.partial(_combine_kernel, nt),
        grid=(nt,),
        in_specs=[pl.BlockSpec((1, 1, TOP_K * tm), lambda i: (i, 0, 0), memory_space=pltpu.SMEM),
                  pl.BlockSpec((1, 1, TOP_K * tm), lambda i: (jnp.minimum(i + 1, nt - 1), 0, 0),
                               memory_space=pltpu.SMEM),
                  pl.BlockSpec((tm, d), lambda i: (i, 0)),
                  pl.BlockSpec((tm, LANES), lambda i: (i, 0)),
                  pl.BlockSpec((1, d), lambda i: (0, 0)),
                  pl.BlockSpec(memory_space=pl.ANY)],
        out_specs=pl.BlockSpec((tm, d), lambda i: (i, 0)),
        out_shape=jax.ShapeDtypeStruct((t, d), F32),
        scratch_shapes=[pltpu.VMEM((2, tm * ROW_TILE, LANES), F32),
                        pltpu.VMEM((2, tm * ROW_TILE, LANES), F32),
                        pltpu.SemaphoreType.DMA((2, 2))],
        compiler_params=_cparams(("arbitrary",)),
        name="moe_combine",
    )(dest3, dest3, x, route, fnorm.reshape(1, d), y)


def _moe(x, route, g, w1, w3, w2, fnorm, tm):
    blk = min(MOE_BLOCK, x.shape[0])
    dest, blk_e, n_used, nblk = _route_plan(route, blk)
    buf = _dispatch(x, g, dest, nblk * blk, tm)
    y = _experts(buf, blk_e, n_used, w1, w3, w2, blk)
    return _combine(x, route, fnorm, y, dest, tm)


def _block_diag(w):
    g, a, b = w.shape
    eye = jnp.eye(g, dtype=w.dtype)
    return (eye[:, None, :, None] * w[:, :, None, :]).reshape(g * a, g * b)


def kernel(x, l0_norm_mix, l0_w_in, l0_conv_w, l0_conv_b, l0_gate_a_w, l0_gate_a_b, l0_gate_x_w, l0_gate_x_b, l0_lru_lambda, l0_shift_mu, l0_w0, l0_w_up, l0_a0, l0_a_up, l0_g_up, l0_k_k, l0_k_a, l0_r_k, l0_ln_x_w, l0_ln_x_b, l0_w_out, l0_norm_ffn, l0_ffn_w1, l0_ffn_w3, l0_ffn_w2, l1_norm_mix, l1_w_in, l1_conv_w, l1_conv_b, l1_wq, l1_wk, l1_wv, l1_ig_b, l1_fg_b, l1_mh_norm_w, l1_w_out, l1_norm_ffn, l1_router, l1_moe_w1, l1_moe_w3, l1_moe_w2, final_norm):
    batch, seq, d = x.shape
    t = batch * seq
    xt = x.reshape(t, d)
    tm = min(512, t)

    w_in0 = l0_w_in.astype(BF16)
    p_lru, p_rw = _norm_mm(xt, l0_norm_mix, w_in0, (2 * LRU_WIDTH, RW_IN), tm, "l0_in")
    y_lru = _lru(p_lru, batch, seq, l0_conv_w, l0_conv_b,
                 _block_diag(l0_gate_a_w).astype(BF16), l0_gate_a_b,
                 _block_diag(l0_gate_x_w).astype(BF16), l0_gate_x_b, l0_lru_lambda,
                 min(512, seq))
    y_rw = _rwkv(p_rw, batch, seq, l0_shift_mu, l0_w0, l0_w_up, l0_a0, l0_a_up, l0_g_up,
                 l0_k_k, l0_k_a, l0_r_k.reshape(-1), l0_ln_x_w, l0_ln_x_b)
    w_out0 = l0_w_out.astype(BF16)
    x2 = _ffn(xt, y_lru, y_rw, w_out0[:LRU_WIDTH], w_out0[LRU_WIDTH:], l0_norm_ffn,
              l0_ffn_w1.astype(BF16), l0_ffn_w3.astype(BF16), l0_ffn_w2.astype(BF16), tm, D_FF // 2)

    n_in1 = 2 * MLSTM_WIDTH + LANES
    w_in1 = jnp.zeros((d, n_in1), F32).at[:, :l1_w_in.shape[1]].set(l1_w_in).astype(BF16)
    (p1,) = _norm_mm(x2, l1_norm_mix, w_in1, (n_in1,), tm, "l1_in")
    h1 = _mlstm(p1, batch, seq, l1_conv_w, l1_conv_b, l1_wq, l1_wk, l1_wv, l1_ig_b, l1_fg_b,
                l1_mh_norm_w)
    router = jnp.zeros((d, LANES), F32).at[:, :N_EXPERTS].set(l1_router)
    x3, route = _out_router(x2, h1, l1_w_out.astype(BF16), l1_norm_ffn, router, tm)
    out = _moe(x3, route, l1_norm_ffn, l1_moe_w1.astype(BF16), l1_moe_w3.astype(BF16),
               l1_moe_w2.astype(BF16), final_norm, tm)
    return out.reshape(batch, seq, d)
```

```python
import functools

import jax
import jax.numpy as jnp
from jax import lax
from jax.experimental import pallas as pl
from jax.experimental.pallas import tpu as pltpu

F32 = jnp.float32
BF16 = jnp.bfloat16

D_MODEL = 1024
LRU_WIDTH = 512
LRU_BLOCKS = 8
LRU_C = 8.0
LRU_SEGS = 8
CONV_WIDTH = 4
RWKV_HEADS = 8
RWKV_HEAD = 64
RWKV_WIDTH = 512
LN_X_EPS = 1e-5 * RWKV_HEAD
MLSTM_HEADS = 8
MLSTM_HEAD = 128
MLSTM_WIDTH = 1024
MLSTM_CHUNK = 128
D_FF = 2816
N_EXPERTS = 8
TOP_K = 2
DMA_QUEUES = 2
ROUTER_SPLIT = 4
MOE_BLOCK = 512
MOE_F_SPLIT = 2
RMS_EPS = 1e-6
RW_IN = 3 * RWKV_WIDTH + 64 + 64 + 128
RW_CHUNK = 64
RW_STEP_ROWS = 256
LANES = 128
CARRY_ROWS = 8
ROW_TILE = 8
VMEM_LIMIT = 56 * 1024 * 1024


def _cparams(sem):
    return pltpu.CompilerParams(dimension_semantics=sem, vmem_limit_bytes=VMEM_LIMIT)


def _rms(x, w):
    return x * lax.rsqrt(jnp.mean(x * x, axis=-1, keepdims=True) + RMS_EPS) * w


def _sigmoid(x):
    return 1.0 / (1.0 + jnp.exp(-x))


def _softplus(x):
    return jnp.maximum(x, 0.0) + jnp.log(1.0 + jnp.exp(-jnp.abs(x)))


def _dot(a, b):
    return jnp.dot(a, b, preferred_element_type=F32)


def _dot_nt(a, b):
    return lax.dot_general(a, b, (((1,), (1,)), ((), ())), preferred_element_type=F32)


def _dot_tn(a, b):
    return lax.dot_general(a, b, (((0,), (0,)), ((), ())), preferred_element_type=F32)


def _split2(x):
    hi = x.astype(BF16)
    lo = (x - hi.astype(F32)).astype(BF16)
    return hi, lo


def _split3(x):
    hi = x.astype(BF16)
    r = x - hi.astype(F32)
    mid = r.astype(BF16)
    lo = (r - mid.astype(F32)).astype(BF16)
    return hi, mid, lo


def _mm3(fn, a, b):
    ah, al = _split2(a)
    bh, bl = _split2(b)
    return fn(ah, bh) + fn(al, bh) + fn(ah, bl)


def _mm_exact_lhs(fn, a_bf16, b):
    h, m, l = _split3(b)
    return fn(a_bf16, h) + fn(a_bf16, m) + fn(a_bf16, l)


def _mm_exact_rhs(fn, a, b_bf16):
    h, m, l = _split3(a)
    return fn(h, b_bf16) + fn(m, b_bf16) + fn(l, b_bf16)


def _shift_hist(buf_ref, x, first):
    n = x.shape[0]

    @pl.when(first)
    def _():
        buf_ref[0:CARRY_ROWS, :] = jnp.zeros((CARRY_ROWS, x.shape[1]), F32)

    @pl.when(jnp.logical_not(first))
    def _():
        buf_ref[0:CARRY_ROWS, :] = buf_ref[n:n + CARRY_ROWS, :]

    buf_ref[CARRY_ROWS:CARRY_ROWS + n, :] = x


def _causal_conv(buf_ref, n, w, b):
    acc = b
    for j in range(CONV_WIDTH):
        off = CARRY_ROWS - (CONV_WIDTH - 1) + j
        acc = acc + w[j:j + 1, :] * buf_ref[off:off + n, :]
    return acc


def _norm_mm_kernel(x_ref, g_ref, w_ref, *o_refs):
    xn = _rms(x_ref[...], g_ref[...]).astype(BF16)
    c0 = 0
    for o_ref in o_refs:
        c1 = c0 + o_ref.shape[1]
        o_ref[...] = _dot(xn, w_ref[:, c0:c1]).astype(o_ref.dtype)
        c0 = c1


def _norm_mm(x, g, w, widths, tm, name):
    t, d = x.shape
    n = w.shape[1]
    assert sum(widths) == n
    return pl.pallas_call(
        _norm_mm_kernel,
        grid=(t // tm,),
        in_specs=[pl.BlockSpec((tm, d), lambda i: (i, 0)),
                  pl.BlockSpec((1, d), lambda i: (0, 0)),
                  pl.BlockSpec((d, n), lambda i: (0, 0))],
        out_specs=[pl.BlockSpec((tm, c), lambda i: (i, 0)) for c in widths],
        out_shape=[jax.ShapeDtypeStruct((t, c), BF16) for c in widths],
        compiler_params=_cparams(("parallel",)),
        name=name,
    )(x, g.reshape(1, d), w)


def _lru_kernel(p_ref, cw_ref, cb_ref, wa_ref, ba_ref, wx_ref, bx_ref, lam_ref, o_ref,
                xbuf, abuf, bbuf, hbuf, hcar):
    i = pl.program_id(1)
    n = p_ref.shape[0]
    first = i == 0
    _shift_hist(xbuf, p_ref[:, 0:LRU_WIDTH].astype(F32), first)
    xc = _causal_conv(xbuf, n, cw_ref[...], cb_ref[...])
    xcb = xc.astype(BF16)
    r = _sigmoid(_dot(xcb, wa_ref[...]) + ba_ref[...])
    ig = _sigmoid(_dot(xcb, wx_ref[...]) + bx_ref[...])
    log_a = (-LRU_C) * r * _softplus(-lam_ref[...])
    a = jnp.exp(log_a)
    mult = jnp.sqrt(1.0 - jnp.exp(2.0 * log_a))
    row = lax.broadcasted_iota(jnp.int32, (n, 1), 0)
    mult = jnp.where(jnp.logical_and(first, row == 0), 1.0, mult)
    bvals = mult * ig * xc
    n_slab = LRU_WIDTH // LANES
    slabs = [slice(c * LANES, (c + 1) * LANES) for c in range(n_slab)]
    seg = n // LRU_SEGS
    pitch = _lru_pitch(n)
    for c, cs in enumerate(slabs):
        for s in range(LRU_SEGS):
            abuf[c, s * pitch:s * pitch + seg, :] = a[s * seg:(s + 1) * seg, cs]
            bbuf[c, s * pitch:s * pitch + seg, :] = bvals[s * seg:(s + 1) * seg, cs]

    @pl.when(first)
    def _():
        hcar[...] = jnp.zeros(hcar.shape, F32)

    def step(j, carry):
        rows = pl.ds(j, LRU_SEGS, stride=pitch)
        out = []
        for c in range(n_slab):
            h, acc = carry[c]
            a_j = abuf[c, rows, :]
            h = a_j * h + bbuf[c, rows, :]
            acc = acc * a_j
            hbuf[c, rows, :] = h
            abuf[c, rows, :] = acc
            out.append((h, acc))
        return tuple(out)

    init = tuple((jnp.zeros((LRU_SEGS, LANES), F32), jnp.ones((LRU_SEGS, LANES), F32))
                 for _ in range(n_slab))
    ends = lax.fori_loop(0, seg, step, init, unroll=8)
    cols = []
    for c, cs in enumerate(slabs):
        h_end, a_end = ends[c]
        h0 = hcar[0:1, cs]
        parts = []
        for s in range(LRU_SEGS):
            rows = slice(s * pitch, s * pitch + seg)
            parts.append(hbuf[c, rows, :] + abuf[c, rows, :] * h0)
            h0 = h_end[s:s + 1, :] + a_end[s:s + 1, :] * h0
        hcar[0:1, cs] = h0
        cols.append(jnp.concatenate(parts, axis=0))
    gate = p_ref[:, LRU_WIDTH:2 * LRU_WIDTH].astype(F32)
    gelu = 0.5 * gate * (1.0 + jnp.tanh(0.7978845608028654 * (gate + 0.044715 * gate * gate * gate)))
    o_ref[...] = (jnp.concatenate(cols, axis=1) * gelu).astype(o_ref.dtype)


def _lru_pitch(n):
    tiles = n // LRU_SEGS // CARRY_ROWS
    return (tiles + 1 - tiles % 2) * CARRY_ROWS


def _lru(p_lru, batch, seq, conv_w, conv_b, wa, ba, wx, bx, lam, tb):
    nb = seq // tb
    c = LRU_WIDTH
    vec = lambda: pl.BlockSpec((1, c), lambda b, i: (0, 0))
    return pl.pallas_call(
        _lru_kernel,
        grid=(batch, nb),
        in_specs=[pl.BlockSpec((tb, 2 * c), lambda b, i: (b * nb + i, 0)),
                  pl.BlockSpec((CONV_WIDTH, c), lambda b, i: (0, 0)), vec(),
                  pl.BlockSpec((c, c), lambda b, i: (0, 0)), vec(),
                  pl.BlockSpec((c, c), lambda b, i: (0, 0)), vec(), vec()],
        out_specs=pl.BlockSpec((tb, c), lambda b, i: (b * nb + i, 0)),
        out_shape=jax.ShapeDtypeStruct((batch * seq, c), BF16),
        scratch_shapes=[pltpu.VMEM((tb + CARRY_ROWS, c), F32)]
        + [pltpu.VMEM((c // LANES, LRU_SEGS * _lru_pitch(tb), LANES), F32)] * 3
        + [pltpu.VMEM((CARRY_ROWS, c), F32)],
        compiler_params=_cparams(("parallel", "arbitrary")),
        name="rg_lru",
    )(p_lru, conv_w, conv_b.reshape(1, c), wa, ba.reshape(1, c), wx, bx.reshape(1, c),
      lam.reshape(1, c))


def _blk(x, masks):
    xb = x.astype(BF16)
    return jnp.concatenate([jnp.where(mk, xb, jnp.zeros_like(xb)) for mk in masks], axis=0)


def _rwkv_kernel(p_ref, mu_ref, w0_ref, wup_ref, a0_ref, aup_ref, gup_ref, kk_ref, ka_ref,
                 rk_ref, lnw_ref, lnb_ref, seg_ref, tri_ref, o_ref, pbuf, state):
    L = RW_CHUNK
    W = RWKV_WIDTH
    rows = p_ref.shape[0]
    first = pl.program_id(1) == 0
    p = p_ref[...].astype(F32)
    _shift_hist(pbuf, p, first)
    prev = pbuf[CARRY_ROWS - 1:CARRY_ROWS - 1 + rows, :]
    ps = p + mu_ref[...] * (prev - p)
    r = ps[:, 0:W]
    k = ps[:, W:2 * W]
    v = ps[:, 2 * W:3 * W]
    x2 = ps[:, 3 * W:3 * W + LANES]
    xg = ps[:, 3 * W + LANES:3 * W + 2 * LANES]
    wl = w0_ref[...] + _dot(jnp.tanh(x2).astype(BF16), wup_ref[...])
    lw = -jnp.exp(-_softplus(-wl) - 0.5)
    a = _sigmoid(a0_ref[...] + _dot(x2.astype(BF16), aup_ref[...]))
    g = _dot(_sigmoid(xg).astype(BF16), gup_ref[...])
    seg = seg_ref[...]
    kk = k * kk_ref[...]
    kk = kk / jnp.maximum(jnp.sqrt(_mm_exact_rhs(_dot, kk * kk, seg)), 1e-12)
    k2 = k * (1.0 + (a - 1.0) * ka_ref[...])

    cw = _mm_exact_lhs(_dot, tri_ref[...], lw)
    cw_end = jnp.concatenate(
        [jnp.broadcast_to(cw[c * L + L - 1:c * L + L, :], (L, W)) for c in range(rows // L)], axis=0)
    w_in = jnp.exp(cw)
    w_inv = jnp.exp(-cw)
    w_end = jnp.exp(cw_end - cw)
    kka = kk * a
    a_t = -kk * jnp.exp(cw - lw)
    b_t = kka * w_inv
    k_t = k2 * w_inv
    r_t = r * w_in
    b_bar = kka * w_end
    k_bar = k2 * w_end
    w_tot = jnp.exp(cw_end)

    @pl.when(first)
    def _():
        state[...] = jnp.zeros(state.shape, F32)

    lane = lax.broadcasted_iota(jnp.int32, (1, LANES), 1)
    m1 = [lane < RWKV_HEAD, lane >= RWKV_HEAD]
    m2 = [jnp.concatenate([mk, mk], axis=1) for mk in m1]
    ti = lax.broadcasted_iota(jnp.int32, (L, LANES), 0)
    si = lax.broadcasted_iota(jnp.int32, (L, LANES), 1) % RWKV_HEAD
    strict = si < ti
    incl = si <= ti
    bi = lax.broadcasted_iota(jnp.int32, (LANES, LANES), 0) // RWKV_HEAD
    bj = lax.broadcasted_iota(jnp.int32, (LANES, LANES), 1) // RWKV_HEAD
    bd = bi == bj

    def tril(mask, s):
        return jnp.where(mask, s, 0.0).astype(BF16)

    n_pairs = RWKV_HEADS // 2
    n_ch = rows // L
    chains = [(slice(c * L, (c + 1) * L), slice(pr * LANES, (pr + 1) * LANES))
              for c in range(n_ch) for pr in range(n_pairs)]
    idx = range(len(chains))
    a_c = [a_t[rs, sl] for rs, sl in chains]
    r_c = [r_t[rs, sl] for rs, sl in chains]
    ar = [jnp.concatenate([a_c[i], r_c[i]], axis=0).astype(BF16) for i in idx]
    sb = [_dot_nt(ar[i], _blk(b_t[rs, sl], m1)) for i, (rs, sl) in enumerate(chains)]
    sk = [_dot_nt(ar[i], _blk(k_t[rs, sl], m1)) for i, (rs, sl) in enumerate(chains)]
    vblk = [_blk(v[rs, sl], m1) for rs, sl in chains]
    rhs = [jnp.concatenate([a_c[i], _dot(tril(strict, sk[i][0:L]), vblk[i])], axis=1) for i in idx]
    pw = [tril(strict, sb[i][0:L]) for i in idx]
    tinv = [jnp.where(si == ti, 1.0, 0.0) + pw[i].astype(F32) for i in idx]
    for _ in range(5):
        pw = [_dot(pw[i], _blk(pw[i], m1)).astype(BF16) for i in idx]
        tinv = [tinv[i] + _dot(pw[i], _blk(tinv[i], m1)) for i in idx]
    x = [_dot(tinv[i].astype(BF16), _blk(rhs[i], m2)) for i in idx]
    corr = [_dot(tril(incl, sb[i][L:2 * L]), _blk(x[i], m2)) for i in idx]
    r_hat = [(r_c[i] + corr[i][:, 0:LANES]).astype(BF16) for i in idx]
    y0 = [corr[i][:, LANES:2 * LANES] + _dot(tril(incl, sk[i][L:2 * L]), vblk[i]) for i in idx]
    xb = [x[i].astype(BF16) for i in idx]
    bb = [b_bar[rs, sl].astype(BF16) for rs, sl in chains]
    gmat = [jnp.where(bd, _dot_tn(xb[i][:, 0:LANES], bb[i]), 0.0).astype(BF16) for i in idx]
    hmat = [jnp.where(bd, _dot_tn(xb[i][:, LANES:2 * LANES], bb[i])
                      + _dot_tn(v[rs, sl].astype(BF16), k_bar[rs, sl].astype(BF16)), 0.0)
            for i, (rs, sl) in enumerate(chains)]

    s_cur = [state[pr] for pr in range(n_pairs)]
    y_rows = []
    for c in range(n_ch):
        ys = []
        for pr in range(n_pairs):
            i = c * n_pairs + pr
            s0 = s_cur[pr]
            s0b = s0.astype(BF16)
            ys.append(_dot_nt(r_hat[i], s0b) + y0[i])
            s_cur[pr] = s0 * w_tot[c * L:c * L + 1, chains[i][1]] + _dot(s0b, gmat[i]) + hmat[i]
        y_rows.append(jnp.concatenate(ys, axis=1))
    for pr in range(n_pairs):
        state[pr] = s_cur[pr]
    y = jnp.concatenate(y_rows, axis=0)

    inv = 1.0 / RWKV_HEAD
    mean = _mm_exact_rhs(_dot, y, seg) * inv
    yc = y - mean
    var = _mm_exact_rhs(_dot, yc * yc, seg) * inv
    yn = yc * lax.rsqrt(var + LN_X_EPS) * lnw_ref[...] + lnb_ref[...]
    bonus = _mm_exact_rhs(_dot, r * k2 * rk_ref[...], seg) * v
    o_ref[...] = ((yn + bonus) * g).astype(o_ref.dtype)


def _rwkv(p_rw, batch, seq, mu, w0, w_up, a0, a_up, g_up, k_k, k_a, r_k, ln_w, ln_b):
    L = min(RW_STEP_ROWS, seq)
    nc = seq // L
    W = RWKV_WIDTH
    wup = jnp.zeros((LANES, W), F32).at[0:64].set(w_up).astype(BF16)
    aup = jnp.zeros((LANES, W), F32).at[64:128].set(a_up).astype(BF16)
    hid = jnp.arange(W) // RWKV_HEAD
    seg = (hid[:, None] == hid[None, :]).astype(BF16)
    ri = jnp.arange(L)
    tri = jnp.logical_and(ri[:, None] >= ri[None, :],
                          ri[:, None] // RW_CHUNK == ri[None, :] // RW_CHUNK).astype(BF16)
    vec = lambda: pl.BlockSpec((1, W), lambda b, i: (0, 0))
    mat = lambda s: pl.BlockSpec(s, lambda b, i: (0, 0))
    return pl.pallas_call(
        _rwkv_kernel,
        grid=(batch, nc),
        in_specs=[pl.BlockSpec((L, RW_IN), lambda b, i: (b * nc + i, 0)),
                  mat((1, RW_IN)), vec(), mat((LANES, W)), vec(), mat((LANES, W)),
                  mat((LANES, W)), vec(), vec(), vec(), vec(), vec(), mat((W, W)), mat((L, L))],
        out_specs=pl.BlockSpec((L, W), lambda b, i: (b * nc + i, 0)),
        out_shape=jax.ShapeDtypeStruct((batch * seq, W), BF16),
        scratch_shapes=[pltpu.VMEM((L + CARRY_ROWS, RW_IN), F32),
                        pltpu.VMEM((RWKV_HEADS // 2, LANES, LANES), F32)],
        compiler_params=_cparams(("parallel", "arbitrary")),
        name="rwkv7",
    )(p_rw, mu.reshape(1, RW_IN), w0.reshape(1, W), wup, a0.reshape(1, W), aup,
      g_up.astype(BF16), k_k.reshape(1, W), k_a.reshape(1, W), r_k.reshape(1, W),
      ln_w.reshape(1, W), ln_b.reshape(1, W), seg, tri)


def _ffn_kernel(x_ref, ya_ref, yb_ref, wa_ref, wb_ref, g_ref, w1_ref, w3_ref, w2_ref, o_ref, xn_s, acc_s):
    f = pl.program_id(1)

    @pl.when(f == 0)
    def _():
        x1 = x_ref[...] + _dot(ya_ref[...], wa_ref[...]) + _dot(yb_ref[...], wb_ref[...])
        xn_s[...] = _rms(x1, g_ref[...]).astype(BF16)
        acc_s[...] = x1

    xn = xn_s[...]
    h1 = _dot(xn, w1_ref[...])
    h3 = _dot(xn, w3_ref[...])
    act = (h1 * _sigmoid(h1) * h3).astype(BF16)
    acc_s[...] += _dot(act, w2_ref[...])

    @pl.when(f == pl.num_programs(1) - 1)
    def _():
        o_ref[...] = acc_s[...]


def _ffn(x, ya, yb, wa, wb, g, w1, w3, w2, tm, tf):
    t, d = x.shape
    nf = w1.shape[1] // tf
    return pl.pallas_call(
        _ffn_kernel,
        grid=(t // tm, nf),
        in_specs=[pl.BlockSpec((tm, d), lambda i, f: (i, 0)),
                  pl.BlockSpec((tm, ya.shape[1]), lambda i, f: (i, 0)),
                  pl.BlockSpec((tm, yb.shape[1]), lambda i, f: (i, 0)),
                  pl.BlockSpec(wa.shape, lambda i, f: (0, 0)),
                  pl.BlockSpec(wb.shape, lambda i, f: (0, 0)),
                  pl.BlockSpec((1, d), lambda i, f: (0, 0)),
                  pl.BlockSpec((d, tf), lambda i, f: (0, f)),
                  pl.BlockSpec((d, tf), lambda i, f: (0, f)),
                  pl.BlockSpec((tf, d), lambda i, f: (f, 0))],
        out_specs=pl.BlockSpec((tm, d), lambda i, f: (i, 0)),
        out_shape=jax.ShapeDtypeStruct((t, d), F32),
        scratch_shapes=[pltpu.VMEM((tm, d), BF16), pltpu.VMEM((tm, d), F32)],
        compiler_params=_cparams(("parallel", "arbitrary")),
        name="ffn_swiglu",
    )(x, ya, yb, wa, wb, g.reshape(1, d), w1, w3, w2)


def _mlstm_kernel(xm_ref, z_ref, gt_ref, cw_ref, cb_ref, wq_ref, wk_ref, wv_ref, gb_ref,
                  nw_ref, tri_ref, o_ref, xbuf, c_s, n_s, m_s):
    ci = pl.program_id(1)
    L = MLSTM_CHUNK
    dh = MLSTM_HEAD
    first = ci == 0
    xm = xm_ref[...].astype(F32)
    _shift_hist(xbuf, xm, first)
    xc = _causal_conv(xbuf, L, cw_ref[...], cb_ref[...])
    xc = xc * _sigmoid(xc)
    gl = gt_ref[...].astype(F32) + gb_ref[...]
    lf = jnp.minimum(gl, 0.0) - jnp.log(1.0 + jnp.exp(-jnp.abs(gl)))
    bcum = _mm_exact_lhs(_dot, tri_ref[...], lf)
    gl_t = gl.T
    bcum_t = bcum.T

    @pl.when(first)
    def _():
        c_s[...] = jnp.zeros(c_s.shape, F32)
        n_s[...] = jnp.zeros(n_s.shape, F32)
        m_s[...] = jnp.zeros(m_s.shape, F32)

    ti = lax.broadcasted_iota(jnp.int32, (L, L), 0)
    si = lax.broadcasted_iota(jnp.int32, (L, L), 1)
    causal = si <= ti
    scale = dh ** -0.5
    hs = range(MLSTM_HEADS)
    sls = [slice(h * dh, (h + 1) * dh) for h in hs]
    xs = [xc[:, sl].astype(BF16) for sl in sls]
    q = [_dot(xs[h], wq_ref[h]) * scale for h in hs]
    k = [_dot(xs[h], wk_ref[h]) for h in hs]
    vb = [_dot(xm[:, sls[h]].astype(BF16), wv_ref[h]).astype(BF16) for h in hs]
    qb = [q[h].astype(BF16) for h in hs]
    kb = [k[h].astype(BF16) for h in hs]
    qk = [_dot_nt(qb[h], kb[h]) for h in hs]
    qc = [_dot(qb[h], c_s[h].astype(BF16)) for h in hs]
    bcol = [bcum[:, MLSTM_HEADS + h:MLSTM_HEADS + h + 1] for h in hs]
    m_prev = [m_s[h:h + 1, 0:1] for h in hs]
    dlog = [jnp.where(causal, bcol[h] - bcum_t[MLSTM_HEADS + h:MLSTM_HEADS + h + 1, :] + gl_t[h:h + 1, :],
                      -jnp.inf) for h in hs]
    inter = [bcol[h] + m_prev[h] for h in hs]
    m_t = [jnp.maximum(inter[h], jnp.max(dlog[h], axis=-1, keepdims=True)) for h in hs]
    s = [qk[h] * jnp.exp(dlog[h] - m_t[h]) for h in hs]
    sv = [_dot(s[h].astype(BF16), vb[h]) for h in hs]
    b_last = [bcol[h][L - 1:L, :] for h in hs]
    wlog = [b_last[h] - bcol[h] + gl[:, h:h + 1] for h in hs]
    m_new = [jnp.maximum(b_last[h] + m_prev[h], jnp.max(wlog[h], axis=0, keepdims=True)) for h in hs]
    kw = [k[h] * jnp.exp(wlog[h] - m_new[h]) for h in hs]
    kv = [_dot_tn(kw[h].astype(BF16), vb[h]) for h in hs]
    sc = [jnp.exp(inter[h] - m_t[h]) for h in hs]
    qn = [jnp.sum(q[h] * n_s[h:h + 1, :], axis=-1, keepdims=True) for h in hs]
    ssum = [jnp.sum(s[h], axis=-1, keepdims=True) for h in hs]
    den = [sc[h] * qn[h] + ssum[h] for h in hs]
    hh = [(sc[h] * qc[h] + sv[h]) / jnp.maximum(jnp.abs(den[h]), jnp.exp(-m_t[h])) for h in hs]
    ms = [jnp.mean(hh[h] * hh[h], axis=-1, keepdims=True) for h in hs]
    dec = [jnp.exp(b_last[h] + m_prev[h] - m_new[h]) for h in hs]
    for h in hs:
        c_s[h] = dec[h] * c_s[h] + kv[h]
        n_s[h:h + 1, :] = dec[h] * n_s[h:h + 1, :] + jnp.sum(kw[h], axis=0, keepdims=True)
        m_s[h:h + 1, :] = jnp.broadcast_to(m_new[h], (1, LANES))
        hn = hh[h] * lax.rsqrt(ms[h] + RMS_EPS)
        gate = _sigmoid(z_ref[:, sls[h]].astype(F32))
        o_ref[:, sls[h]] = (gate * (hn * nw_ref[:, sls[h]])).astype(o_ref.dtype)


def _mlstm(p1, batch, seq, conv_w, conv_b, wq, wk, wv, ig_b, fg_b, mh_w):
    L = MLSTM_CHUNK
    nc = seq // L
    W = MLSTM_WIDTH
    gb = jnp.zeros((1, LANES), F32).at[0, 0:8].set(ig_b).at[0, 8:16].set(fg_b)
    tri = (jnp.arange(L)[:, None] >= jnp.arange(L)[None, :]).astype(BF16)
    hw = lambda: pl.BlockSpec((MLSTM_HEADS, MLSTM_HEAD, MLSTM_HEAD), lambda b, i: (0, 0, 0))
    return pl.pallas_call(
        _mlstm_kernel,
        grid=(batch, nc),
        in_specs=[pl.BlockSpec((L, W), lambda b, i: (b * nc + i, 0)),
                  pl.BlockSpec((L, W), lambda b, i: (b * nc + i, 1)),
                  pl.BlockSpec((L, LANES), lambda b, i: (b * nc + i, 2 * W // LANES)),
                  pl.BlockSpec((CONV_WIDTH, W), lambda b, i: (0, 0)),
                  pl.BlockSpec((1, W), lambda b, i: (0, 0)),
                  hw(), hw(), hw(),
                  pl.BlockSpec((1, LANES), lambda b, i: (0, 0)),
                  pl.BlockSpec((1, W), lambda b, i: (0, 0)),
                  pl.BlockSpec((L, L), lambda b, i: (0, 0))],
        out_specs=pl.BlockSpec((L, W), lambda b, i: (b * nc + i, 0)),
        out_shape=jax.ShapeDtypeStruct((batch * seq, W), BF16),
        scratch_shapes=[pltpu.VMEM((L + CARRY_ROWS, W), F32),
                        pltpu.VMEM((MLSTM_HEADS, MLSTM_HEAD, MLSTM_HEAD), F32),
                        pltpu.VMEM((MLSTM_HEADS, LANES), F32),
                        pltpu.VMEM((MLSTM_HEADS, LANES), F32)],
        compiler_params=_cparams(("parallel", "arbitrary")),
        name="mlstm",
    )(p1, p1, p1, conv_w, conv_b.reshape(1, W), wq.astype(BF16), wk.astype(BF16),
      wv.astype(BF16), gb, mh_w.reshape(1, W), tri)


def _out_router_kernel(x_ref, y_ref, w_ref, g_ref, rt_ref, x3_ref, o_ref):
    tm = x_ref.shape[0]
    sub = tm // ROUTER_SPLIT
    rs = [slice(q * sub, (q + 1) * sub) for q in range(ROUTER_SPLIT)]
    qs = range(ROUTER_SPLIT)
    w = w_ref[...]
    x3 = [x_ref[r, :] + _dot(y_ref[r, :], w) for r in rs]
    for q in qs:
        x3_ref[rs[q], :] = x3[q]
    gw = g_ref[...]
    rt = rt_ref[...]
    logits = [_mm3(_dot, _rms(x3[q], gw), rt) for q in qs]
    lane = lax.broadcasted_iota(jnp.int32, (sub, LANES), 1)
    lg = [jnp.where(lane < N_EXPERTS, logits[q], -jnp.inf) for q in qs]
    v1 = [jnp.max(lg[q], axis=-1, keepdims=True) for q in qs]
    i1 = [jnp.min(jnp.where(lg[q] == v1[q], lane, LANES), axis=-1, keepdims=True) for q in qs]
    lg2 = [jnp.where(lane == i1[q], -jnp.inf, lg[q]) for q in qs]
    v2 = [jnp.max(lg2[q], axis=-1, keepdims=True) for q in qs]
    i2 = [jnp.min(jnp.where(lg2[q] == v2[q], lane, LANES), axis=-1, keepdims=True) for q in qs]
    for q in qs:
        ex = jnp.exp(v2[q] - v1[q])
        g1 = 1.0 / (1.0 + ex)
        g2 = ex / (1.0 + ex)
        out = jnp.where(lane == 0, i1[q].astype(F32), 0.0)
        out = jnp.where(lane == 1, i2[q].astype(F32), out)
        out = jnp.where(lane == 2, g1, out)
        o_ref[rs[q], :] = jnp.where(lane == 3, g2, out)


def _out_router(x, y, w, g, router, tm):
    t, d = x.shape
    return pl.pallas_call(
        _out_router_kernel,
        grid=(t // tm,),
        in_specs=[pl.BlockSpec((tm, d), lambda i: (i, 0)),
                  pl.BlockSpec((tm, y.shape[1]), lambda i: (i, 0)),
                  pl.BlockSpec(w.shape, lambda i: (0, 0)),
                  pl.BlockSpec((1, d), lambda i: (0, 0)),
                  pl.BlockSpec((d, LANES), lambda i: (0, 0))],
        out_specs=[pl.BlockSpec((tm, d), lambda i: (i, 0)),
                   pl.BlockSpec((tm, LANES), lambda i: (i, 0))],
        out_shape=[jax.ShapeDtypeStruct((t, d), F32), jax.ShapeDtypeStruct((t, LANES), F32)],
        compiler_params=_cparams(("parallel",)),
        name="l1_out_router",
    )(x, y, w, g.reshape(1, d), router)


def _route_plan(route, blk):
    e = route[:, 0:TOP_K].astype(jnp.int32).reshape(-1)
    m = e.shape[0]
    onehot = (e[:, None] == jnp.arange(N_EXPERTS, dtype=jnp.int32)[None, :]).astype(jnp.int32)
    csum = jnp.cumsum(onehot, axis=0)
    rank = jnp.sum((csum - onehot) * onehot, axis=1)
    nb = (csum[-1] + blk - 1) // blk
    bend = jnp.cumsum(nb)
    dest = jnp.sum(onehot * (bend - nb)[None, :], axis=1) * blk + rank
    nblk = m // blk + N_EXPERTS
    n_used = bend[-1]
    bidx = jnp.minimum(jnp.arange(nblk, dtype=jnp.int32), n_used - 1)
    blk_e = jnp.minimum(jnp.searchsorted(bend, bidx, side="right"), N_EXPERTS - 1).astype(jnp.int32)
    gap_start = jnp.concatenate([(bend - nb) * blk + csum[-1], (n_used * blk)[None]])
    gap_len = jnp.concatenate([nb * blk - csum[-1], ((nblk - n_used) * blk)[None]])
    gap_end = jnp.cumsum(gap_len)
    q = jnp.arange(nblk * blk - m, dtype=jnp.int32)
    gi = jnp.searchsorted(gap_end, q, side="right")
    pad = gap_start[gi] + q - (gap_end - gap_len)[gi]
    return dest.astype(jnp.int32), pad.astype(jnp.int32), blk_e, n_used.reshape(1).astype(jnp.int32), nblk


def _to_tiles(ref, x):
    n = x.shape[0]
    for s in range(ROW_TILE):
        ref[pl.ds(s, n, stride=ROW_TILE), :] = x[:, s * LANES:(s + 1) * LANES]


def _from_tiles(ref, n):
    return jnp.concatenate([ref[pl.ds(s, n, stride=ROW_TILE), :] for s in range(ROW_TILE)], axis=1)


def _row_copy(src, src_row, dst, dst_row, sem):
    s0 = pl.multiple_of(src_row * ROW_TILE, ROW_TILE)
    d0 = pl.multiple_of(dst_row * ROW_TILE, ROW_TILE)
    return pltpu.make_async_copy(src.at[pl.ds(s0, ROW_TILE), :], dst.at[pl.ds(d0, ROW_TILE), :], sem)


def _wait_rows(hbm, vmem, sem, to_hbm):
    rows = hbm.at[pl.ds(0, vmem.shape[0]), :]
    (pltpu.make_async_copy(vmem, rows, sem) if to_hbm else pltpu.make_async_copy(rows, vmem, sem)).wait()


def _dispatch_kernel(nt, dest_ref, pad_ref, x_ref, g_ref, buf_out, xn_s, zero_s, sem, zsem):
    i = pl.program_id(0)
    tm = x_ref.shape[0]
    slot = i % 2
    n_pad = pad_ref.shape[2]

    def drain(s):
        for _ in range(TOP_K):
            _wait_rows(buf_out, xn_s.at[s], sem.at[s], True)

    @pl.when(i >= 2)
    def _():
        drain(slot)

    zero_s[...] = jnp.zeros(zero_s.shape, F32)

    def zero(q, c):
        _row_copy(zero_s, q, buf_out, pad_ref[0, 0, q], zsem.at[0]).start()
        return c

    lax.fori_loop(0, n_pad, zero, 0, unroll=8)
    _to_tiles(xn_s.at[slot], _rms(x_ref[...], g_ref[...]))

    def start(r, c):
        for k in range(TOP_K):
            _row_copy(xn_s.at[slot], r, buf_out, dest_ref[0, 0, TOP_K * r + k],
                      sem.at[slot]).start(priority=k % DMA_QUEUES)
        return c

    lax.fori_loop(0, tm, start, 0, unroll=8)
    _wait_rows(buf_out, zero_s, zsem.at[0], True)

    @pl.when(i == nt - 1)
    def _():
        drain(slot)
        if nt > 1:
            drain(1 - slot)


def _dispatch(x, g, dest, pad, rows, tm):
    t, d = x.shape
    nt = t // tm
    n_pad = pad.shape[0] // nt
    assert n_pad * nt == pad.shape[0]
    return pl.pallas_call(
        functools.partial(_dispatch_kernel, nt),
        grid=(nt,),
        in_specs=[pl.BlockSpec((1, 1, TOP_K * tm), lambda i: (i, 0, 0), memory_space=pltpu.SMEM),
                  pl.BlockSpec((1, 1, n_pad), lambda i: (i, 0, 0), memory_space=pltpu.SMEM),
                  pl.BlockSpec((tm, d), lambda i: (i, 0)),
                  pl.BlockSpec((1, d), lambda i: (0, 0))],
        out_specs=pl.BlockSpec(memory_space=pl.ANY),
        out_shape=jax.ShapeDtypeStruct((rows * ROW_TILE, LANES), F32),
        scratch_shapes=[pltpu.VMEM((2, tm * ROW_TILE, LANES), F32),
                        pltpu.VMEM((n_pad * ROW_TILE, LANES), F32),
                        pltpu.SemaphoreType.DMA((2,)), pltpu.SemaphoreType.DMA((1,))],
        compiler_params=_cparams(("arbitrary",)),
        name="moe_dispatch",
    )(dest.reshape(nt, 1, TOP_K * tm), pad.reshape(nt, 1, n_pad), x, g.reshape(1, d))


def _expert_kernel(be_ref, nu_ref, x_ref, w1_ref, w3_ref, w2_ref, y_ref):
    del be_ref
    used = pl.program_id(0) < nu_ref[0]

    @pl.when(jnp.logical_not(used))
    def _():
        y_ref[...] = jnp.zeros(y_ref.shape, F32)

    @pl.when(used)
    def _():
        x = _from_tiles(x_ref, x_ref.shape[0] // ROW_TILE).astype(BF16)
        acc = None
        for f in range(MOE_F_SPLIT):
            sl = slice(f * (D_FF // MOE_F_SPLIT), (f + 1) * (D_FF // MOE_F_SPLIT))
            h1 = _dot(x, w1_ref[0, :, sl])
            h3 = _dot(x, w3_ref[0, :, sl])
            act = (h1 * _sigmoid(h1) * h3).astype(BF16)
            part = _dot(act, w2_ref[0, sl, :])
            acc = part if acc is None else acc + part
        _to_tiles(y_ref, acc)


def _experts(buf, blk_e, n_used, w1, w3, w2, blk):
    d, f = w1.shape[1], w1.shape[2]
    once = pl.Buffered(1)
    grid_spec = pltpu.PrefetchScalarGridSpec(
        num_scalar_prefetch=2,
        grid=(buf.shape[0] // (blk * ROW_TILE),),
        in_specs=[pl.BlockSpec((blk * ROW_TILE, LANES), lambda b, be, nu: (b, 0)),
                  pl.BlockSpec((1, d, f), lambda b, be, nu: (be[b], 0, 0), pipeline_mode=once),
                  pl.BlockSpec((1, d, f), lambda b, be, nu: (be[b], 0, 0), pipeline_mode=once),
                  pl.BlockSpec((1, f, d), lambda b, be, nu: (be[b], 0, 0), pipeline_mode=once)],
        out_specs=pl.BlockSpec((blk * ROW_TILE, LANES), lambda b, be, nu: (b, 0)))
    return pl.pallas_call(
        _expert_kernel,
        grid_spec=grid_spec,
        out_shape=jax.ShapeDtypeStruct(buf.shape, F32),
        compiler_params=_cparams(("arbitrary",)),
        name="moe_experts",
    )(blk_e, n_used, buf, w1, w3, w2)


def _combine_kernel(nt, dcur_ref, dnxt_ref, x_ref, rt_ref, fn_ref, y_hbm, o_ref, ya, yb, sem):
    i = pl.program_id(0)
    tm = x_ref.shape[0]
    slot = i % 2

    def gather(dref, s):
        def start(r, c):
            _row_copy(y_hbm, dref[0, 0, TOP_K * r], ya.at[s], r, sem.at[s, 0]).start(priority=0)
            _row_copy(y_hbm, dref[0, 0, TOP_K * r + 1], yb.at[s], r, sem.at[s, 1]).start(
                priority=1 % DMA_QUEUES)
            return c

        lax.fori_loop(0, tm, start, 0, unroll=8)

    @pl.when(i == 0)
    def _():
        gather(dcur_ref, slot)

    @pl.when(i + 1 < nt)
    def _():
        gather(dnxt_ref, 1 - slot)

    _wait_rows(y_hbm, ya.at[slot], sem.at[slot, 0], False)
    _wait_rows(y_hbm, yb.at[slot], sem.at[slot, 1], False)
    rt = rt_ref[...]
    lane = lax.broadcasted_iota(jnp.int32, rt.shape, 1)
    g1 = jnp.sum(jnp.where(lane == 2, rt, 0.0), axis=-1, keepdims=True)
    g2 = jnp.sum(jnp.where(lane == 3, rt, 0.0), axis=-1, keepdims=True)
    moe = g1 * _from_tiles(ya.at[slot], tm) + g2 * _from_tiles(yb.at[slot], tm)
    o_ref[...] = _rms(x_ref[...] + moe, fn_ref[...])


def _combine(x, route, fnorm, y, dest, tm):
    t, d = x.shape
    nt = t // tm
    dest3 = dest.reshape(nt, 1, TOP_K * tm)
    return pl.pallas_call(
        functools.partial(_combine_kernel, nt),
        grid=(nt,),
        in_specs=[pl.BlockSpec((1, 1, TOP_K * tm), lambda i: (i, 0, 0), memory_space=pltpu.SMEM),
                  pl.BlockSpec((1, 1, TOP_K * tm), lambda i: (jnp.minimum(i + 1, nt - 1), 0, 0),
                               memory_space=pltpu.SMEM),
                  pl.BlockSpec((tm, d), lambda i: (i, 0)),
                  pl.BlockSpec((tm, LANES), lambda i: (i, 0)),
                  pl.BlockSpec((1, d), lambda i: (0, 0)),
                  pl.BlockSpec(memory_space=pl.ANY)],
        out_specs=pl.BlockSpec((tm, d), lambda i: (i, 0)),
        out_shape=jax.ShapeDtypeStruct((t, d), F32),
        scratch_shapes=[pltpu.VMEM((2, tm * ROW_TILE, LANES), F32),
                        pltpu.VMEM((2, tm * ROW_TILE, LANES), F32),
                        pltpu.SemaphoreType.DMA((2, 2))],
        compiler_params=_cparams(("arbitrary",)),
        name="moe_combine",
    )(dest3, dest3, x, route, fnorm.reshape(1, d), y)


def _moe(x, route, g, w1, w3, w2, fnorm, tm):
    blk = min(MOE_BLOCK, x.shape[0])
    dest, pad, blk_e, n_used, nblk = _route_plan(route, blk)
    buf = _dispatch(x, g, dest, pad, nblk * blk, tm)
    y = _experts(buf, blk_e, n_used, w1, w3, w2, blk)
    return _combine(x, route, fnorm, y, dest, tm)


def _block_diag(w):
    g, a, b = w.shape
    eye = jnp.eye(g, dtype=w.dtype)
    return (eye[:, None, :, None] * w[:, :, None, :]).reshape(g * a, g * b)


def kernel(x, l0_norm_mix, l0_w_in, l0_conv_w, l0_conv_b, l0_gate_a_w, l0_gate_a_b, l0_gate_x_w, l0_gate_x_b, l0_lru_lambda, l0_shift_mu, l0_w0, l0_w_up, l0_a0, l0_a_up, l0_g_up, l0_k_k, l0_k_a, l0_r_k, l0_ln_x_w, l0_ln_x_b, l0_w_out, l0_norm_ffn, l0_ffn_w1, l0_ffn_w3, l0_ffn_w2, l1_norm_mix, l1_w_in, l1_conv_w, l1_conv_b, l1_wq, l1_wk, l1_wv, l1_ig_b, l1_fg_b, l1_mh_norm_w, l1_w_out, l1_norm_ffn, l1_router, l1_moe_w1, l1_moe_w3, l1_moe_w2, final_norm):
    batch, seq, d = x.shape
    t = batch * seq
    xt = x.reshape(t, d)
    tm = min(512, t)

    w_in0 = l0_w_in.astype(BF16)
    p_lru, p_rw = _norm_mm(xt, l0_norm_mix, w_in0, (2 * LRU_WIDTH, RW_IN), tm, "l0_in")
    y_lru = _lru(p_lru, batch, seq, l0_conv_w, l0_conv_b,
                 _block_diag(l0_gate_a_w).astype(BF16), l0_gate_a_b,
                 _block_diag(l0_gate_x_w).astype(BF16), l0_gate_x_b, l0_lru_lambda,
                 min(512, seq))
    y_rw = _rwkv(p_rw, batch, seq, l0_shift_mu, l0_w0, l0_w_up, l0_a0, l0_a_up, l0_g_up,
                 l0_k_k, l0_k_a, l0_r_k.reshape(-1), l0_ln_x_w, l0_ln_x_b)
    w_out0 = l0_w_out.astype(BF16)
    x2 = _ffn(xt, y_lru, y_rw, w_out0[:LRU_WIDTH], w_out0[LRU_WIDTH:], l0_norm_ffn,
              l0_ffn_w1.astype(BF16), l0_ffn_w3.astype(BF16), l0_ffn_w2.astype(BF16), tm, D_FF // 2)

    n_in1 = 2 * MLSTM_WIDTH + LANES
    w_in1 = jnp.zeros((d, n_in1), F32).at[:, :l1_w_in.shape[1]].set(l1_w_in).astype(BF16)
    (p1,) = _norm_mm(x2, l1_norm_mix, w_in1, (n_in1,), tm, "l1_in")
    h1 = _mlstm(p1, batch, seq, l1_conv_w, l1_conv_b, l1_wq, l1_wk, l1_wv, l1_ig_b, l1_fg_b,
                l1_mh_norm_w)
    router = jnp.zeros((d, LANES), F32).at[:, :N_EXPERTS].set(l1_router)
    x3, route = _out_router(x2, h1, l1_w_out.astype(BF16), l1_norm_ffn, router, tm)
    out = _moe(x3, route, l1_norm_ffn, l1_moe_w1.astype(BF16), l1_moe_w3.astype(BF16),
               l1_moe_w2.astype(BF16), final_norm, tm)
    return out.reshape(batch, seq, d)
```

```python
import functools

import jax
import jax.numpy as jnp
from jax import lax
from jax.experimental import pallas as pl
from jax.experimental.pallas import tpu as pltpu

F32 = jnp.float32
BF16 = jnp.bfloat16

D_MODEL = 1024
LRU_WIDTH = 512
LRU_BLOCKS = 8
LRU_C = 8.0
LRU_SEGS = 8
CONV_WIDTH = 4
RWKV_HEADS = 8
RWKV_HEAD = 64
RWKV_WIDTH = 512
LN_X_EPS = 1e-5 * RWKV_HEAD
MLSTM_HEADS = 8
MLSTM_HEAD = 128
MLSTM_WIDTH = 1024
MLSTM_CHUNK = 128
D_FF = 2816
N_EXPERTS = 8
TOP_K = 2
DMA_QUEUES = 2
ROUTER_SPLIT = 4
MOE_BLOCK = 512
FF_CHUNKS = (768, 768, 768, 512)
RMS_EPS = 1e-6
RW_IN = 3 * RWKV_WIDTH + 64 + 64 + 128
RW_CHUNK = 64
RW_STEP_ROWS = 512
RW_WAVE_ROWS = 256
LANES = 128
CARRY_ROWS = 8
ROW_TILE = 8
VMEM_LIMIT = 56 * 1024 * 1024


def _cparams(sem):
    return pltpu.CompilerParams(dimension_semantics=sem, vmem_limit_bytes=VMEM_LIMIT)


def _rms(x, w):
    return x * lax.rsqrt(jnp.mean(x * x, axis=-1, keepdims=True) + RMS_EPS) * w


def _sigmoid(x):
    return 1.0 / (1.0 + jnp.exp(-x))


def _softplus(x):
    return jnp.maximum(x, 0.0) + jnp.log(1.0 + jnp.exp(-jnp.abs(x)))


def _dot(a, b):
    return jnp.dot(a, b, preferred_element_type=F32)


def _dot_nt(a, b):
    return lax.dot_general(a, b, (((1,), (1,)), ((), ())), preferred_element_type=F32)


def _dot_tn(a, b):
    return lax.dot_general(a, b, (((0,), (0,)), ((), ())), preferred_element_type=F32)


def _split2(x):
    hi = x.astype(BF16)
    lo = (x - hi.astype(F32)).astype(BF16)
    return hi, lo


def _split3(x):
    hi = x.astype(BF16)
    r = x - hi.astype(F32)
    mid = r.astype(BF16)
    lo = (r - mid.astype(F32)).astype(BF16)
    return hi, mid, lo


def _mm3(fn, a, b):
    ah, al = _split2(a)
    bh, bl = _split2(b)
    return fn(ah, bh) + fn(al, bh) + fn(ah, bl)


def _mm_exact_lhs(fn, a_bf16, b):
    h, m, l = _split3(b)
    return fn(a_bf16, h) + fn(a_bf16, m) + fn(a_bf16, l)


def _mm_exact_rhs(fn, a, b_bf16):
    h, m, l = _split3(a)
    return fn(h, b_bf16) + fn(m, b_bf16) + fn(l, b_bf16)


def _shift_hist(buf_ref, x, first):
    n = x.shape[0]

    @pl.when(first)
    def _():
        buf_ref[0:CARRY_ROWS, :] = jnp.zeros((CARRY_ROWS, x.shape[1]), F32)

    @pl.when(jnp.logical_not(first))
    def _():
        buf_ref[0:CARRY_ROWS, :] = buf_ref[n:n + CARRY_ROWS, :]

    buf_ref[CARRY_ROWS:CARRY_ROWS + n, :] = x


def _causal_conv(buf_ref, n, w, b):
    acc = b
    for j in range(CONV_WIDTH):
        off = CARRY_ROWS - (CONV_WIDTH - 1) + j
        acc = acc + w[j:j + 1, :] * buf_ref[off:off + n, :]
    return acc


def _norm_mm_kernel(x_ref, g_ref, w_ref, *o_refs):
    xn = _rms(x_ref[...], g_ref[...]).astype(BF16)
    c0 = 0
    for o_ref in o_refs:
        c1 = c0 + o_ref.shape[1]
        o_ref[...] = _dot(xn, w_ref[:, c0:c1]).astype(o_ref.dtype)
        c0 = c1


def _norm_mm(x, g, w, widths, tm, name):
    t, d = x.shape
    n = w.shape[1]
    assert sum(widths) == n
    return pl.pallas_call(
        _norm_mm_kernel,
        grid=(t // tm,),
        in_specs=[pl.BlockSpec((tm, d), lambda i: (i, 0)),
                  pl.BlockSpec((1, d), lambda i: (0, 0)),
                  pl.BlockSpec((d, n), lambda i: (0, 0))],
        out_specs=[pl.BlockSpec((tm, c), lambda i: (i, 0)) for c in widths],
        out_shape=[jax.ShapeDtypeStruct((t, c), BF16) for c in widths],
        compiler_params=_cparams(("parallel",)),
        name=name,
    )(x, g.reshape(1, d), w)


def _lru_kernel(p_ref, cw_ref, cb_ref, wa_ref, ba_ref, wx_ref, bx_ref, lam_ref, o_ref,
                xbuf, abuf, bbuf, hbuf, hcar):
    i = pl.program_id(1)
    n = p_ref.shape[0]
    first = i == 0
    _shift_hist(xbuf, p_ref[:, 0:LRU_WIDTH].astype(F32), first)
    xc = _causal_conv(xbuf, n, cw_ref[...], cb_ref[...])
    xcb = xc.astype(BF16)
    r = _sigmoid(_dot(xcb, wa_ref[...]) + ba_ref[...])
    ig = _sigmoid(_dot(xcb, wx_ref[...]) + bx_ref[...])
    log_a = (-LRU_C) * r * _softplus(-lam_ref[...])
    a = jnp.exp(log_a)
    mult = jnp.sqrt(1.0 - jnp.exp(2.0 * log_a))
    row = lax.broadcasted_iota(jnp.int32, (n, 1), 0)
    mult = jnp.where(jnp.logical_and(first, row == 0), 1.0, mult)
    bvals = mult * ig * xc
    n_slab = LRU_WIDTH // LANES
    slabs = [slice(c * LANES, (c + 1) * LANES) for c in range(n_slab)]
    seg = n // LRU_SEGS
    pitch = _lru_pitch(n)
    for c, cs in enumerate(slabs):
        for s in range(LRU_SEGS):
            abuf[c, s * pitch:s * pitch + seg, :] = a[s * seg:(s + 1) * seg, cs]
            bbuf[c, s * pitch:s * pitch + seg, :] = bvals[s * seg:(s + 1) * seg, cs]

    @pl.when(first)
    def _():
        hcar[...] = jnp.zeros(hcar.shape, F32)

    def step(j, carry):
        rows = pl.ds(j, LRU_SEGS, stride=pitch)
        out = []
        for c in range(n_slab):
            h, acc = carry[c]
            a_j = abuf[c, rows, :]
            h = a_j * h + bbuf[c, rows, :]
            acc = acc * a_j
            hbuf[c, rows, :] = h
            abuf[c, rows, :] = acc
            out.append((h, acc))
        return tuple(out)

    init = tuple((jnp.zeros((LRU_SEGS, LANES), F32), jnp.ones((LRU_SEGS, LANES), F32))
                 for _ in range(n_slab))
    ends = lax.fori_loop(0, seg, step, init, unroll=8)
    cols = []
    for c, cs in enumerate(slabs):
        h_end, a_end = ends[c]
        h0 = hcar[0:1, cs]
        parts = []
        for s in range(LRU_SEGS):
            rows = slice(s * pitch, s * pitch + seg)
            parts.append(hbuf[c, rows, :] + abuf[c, rows, :] * h0)
            h0 = h_end[s:s + 1, :] + a_end[s:s + 1, :] * h0
        hcar[0:1, cs] = h0
        cols.append(jnp.concatenate(parts, axis=0))
    gate = p_ref[:, LRU_WIDTH:2 * LRU_WIDTH].astype(F32)
    gelu = 0.5 * gate * (1.0 + jnp.tanh(0.7978845608028654 * (gate + 0.044715 * gate * gate * gate)))
    o_ref[...] = (jnp.concatenate(cols, axis=1) * gelu).astype(o_ref.dtype)


def _lru_pitch(n):
    tiles = n // LRU_SEGS // CARRY_ROWS
    return (tiles + 1 - tiles % 2) * CARRY_ROWS


def _lru(p_lru, batch, seq, conv_w, conv_b, wa, ba, wx, bx, lam, tb):
    nb = seq // tb
    c = LRU_WIDTH
    vec = lambda: pl.BlockSpec((1, c), lambda b, i: (0, 0))
    return pl.pallas_call(
        _lru_kernel,
        grid=(batch, nb),
        in_specs=[pl.BlockSpec((tb, 2 * c), lambda b, i: (b * nb + i, 0)),
                  pl.BlockSpec((CONV_WIDTH, c), lambda b, i: (0, 0)), vec(),
                  pl.BlockSpec((c, c), lambda b, i: (0, 0)), vec(),
                  pl.BlockSpec((c, c), lambda b, i: (0, 0)), vec(), vec()],
        out_specs=pl.BlockSpec((tb, c), lambda b, i: (b * nb + i, 0)),
        out_shape=jax.ShapeDtypeStruct((batch * seq, c), BF16),
        scratch_shapes=[pltpu.VMEM((tb + CARRY_ROWS, c), F32)]
        + [pltpu.VMEM((c // LANES, LRU_SEGS * _lru_pitch(tb), LANES), F32)] * 3
        + [pltpu.VMEM((CARRY_ROWS, c), F32)],
        compiler_params=_cparams(("parallel", "arbitrary")),
        name="rg_lru",
    )(p_lru, conv_w, conv_b.reshape(1, c), wa, ba.reshape(1, c), wx, bx.reshape(1, c),
      lam.reshape(1, c))


def _blk(x, masks):
    xb = x.astype(BF16)
    return jnp.concatenate([jnp.where(mk, xb, jnp.zeros_like(xb)) for mk in masks], axis=0)


def _rwkv_kernel(p_ref, mu_ref, w0_ref, wup_ref, a0_ref, aup_ref, gup_ref, kk_ref, ka_ref,
                 rk_ref, lnw_ref, lnb_ref, seg_ref, tri_ref, o_ref, pbuf, state):
    L = RW_CHUNK
    W = RWKV_WIDTH
    rows = p_ref.shape[0]
    wave = tri_ref.shape[0]
    first = pl.program_id(1) == 0
    p = p_ref[...].astype(F32)
    _shift_hist(pbuf, p, first)
    prev = pbuf[CARRY_ROWS - 1:CARRY_ROWS - 1 + rows, :]
    ps = p + mu_ref[...] * (prev - p)
    seg = seg_ref[...]
    tri = tri_ref[...]

    @pl.when(first)
    def _():
        state[...] = jnp.zeros(state.shape, F32)

    lane = lax.broadcasted_iota(jnp.int32, (1, LANES), 1)
    m1 = [lane < RWKV_HEAD, lane >= RWKV_HEAD]
    m2 = [jnp.concatenate([mk, mk], axis=1) for mk in m1]
    ti = lax.broadcasted_iota(jnp.int32, (L, LANES), 0)
    si = lax.broadcasted_iota(jnp.int32, (L, LANES), 1) % RWKV_HEAD
    strict = si < ti
    incl = si <= ti
    bi = lax.broadcasted_iota(jnp.int32, (LANES, LANES), 0) // RWKV_HEAD
    bj = lax.broadcasted_iota(jnp.int32, (LANES, LANES), 1) // RWKV_HEAD
    bd = bi == bj

    def tril(mask, s):
        return jnp.where(mask, s, 0.0).astype(BF16)

    wv = [slice(w * wave, (w + 1) * wave) for w in range(rows // wave)]
    ws = range(len(wv))
    r = [ps[rw, 0:W] for rw in wv]
    k = [ps[rw, W:2 * W] for rw in wv]
    v = [ps[rw, 2 * W:3 * W] for rw in wv]
    x2 = [ps[rw, 3 * W:3 * W + LANES] for rw in wv]
    xg = [ps[rw, 3 * W + LANES:3 * W + 2 * LANES] for rw in wv]
    wl = [w0_ref[...] + _dot(jnp.tanh(x2[w]).astype(BF16), wup_ref[...]) for w in ws]
    a = [_sigmoid(a0_ref[...] + _dot(x2[w].astype(BF16), aup_ref[...])) for w in ws]
    g = [_dot(_sigmoid(xg[w]).astype(BF16), gup_ref[...]) for w in ws]
    kk = [k[w] * kk_ref[...] for w in ws]
    kk = [kk[w] / jnp.maximum(jnp.sqrt(_mm_exact_rhs(_dot, kk[w] * kk[w], seg)), 1e-12) for w in ws]
    lw = [-jnp.exp(-_softplus(-wl[w]) - 0.5) for w in ws]
    cw = [_mm_exact_lhs(_dot, tri, lw[w]) for w in ws]

    n_pairs = RWKV_HEADS // 2
    n_ch = wave // L
    s_cur = [state[pr] for pr in range(n_pairs)]
    for w in ws:
        cw_end = jnp.concatenate(
            [jnp.broadcast_to(cw[w][c * L + L - 1:c * L + L, :], (L, W)) for c in range(n_ch)], axis=0)
        w_in = jnp.exp(cw[w])
        w_inv = jnp.exp(-cw[w])
        w_end = jnp.exp(cw_end - cw[w])
        k2 = k[w] * (1.0 + (a[w] - 1.0) * ka_ref[...])
        kka = kk[w] * a[w]
        a_t = -kk[w] * jnp.exp(cw[w] - lw[w])
        b_t = kka * w_inv
        k_t = k2 * w_inv
        r_t = r[w] * w_in
        b_bar = kka * w_end
        k_bar = k2 * w_end
        w_tot = jnp.exp(cw_end)
        vw = v[w]

        chains = [(slice(c * L, (c + 1) * L), slice(pr * LANES, (pr + 1) * LANES))
                  for c in range(n_ch) for pr in range(n_pairs)]
        idx = range(len(chains))
        a_c = [a_t[rs, sl] for rs, sl in chains]
        r_c = [r_t[rs, sl] for rs, sl in chains]
        ar = [jnp.concatenate([a_c[i], r_c[i]], axis=0).astype(BF16) for i in idx]
        sb = [_dot_nt(ar[i], _blk(b_t[rs, sl], m1)) for i, (rs, sl) in enumerate(chains)]
        sk = [_dot_nt(ar[i], _blk(k_t[rs, sl], m1)) for i, (rs, sl) in enumerate(chains)]
        vblk = [_blk(vw[rs, sl], m1) for rs, sl in chains]
        rhs = [jnp.concatenate([a_c[i], _dot(tril(strict, sk[i][0:L]), vblk[i])], axis=1) for i in idx]
        pw = [tril(strict, sb[i][0:L]) for i in idx]
        tinv = [jnp.where(si == ti, 1.0, 0.0) + pw[i].astype(F32) for i in idx]
        for _ in range(5):
            pw = [_dot(pw[i], _blk(pw[i], m1)).astype(BF16) for i in idx]
            tinv = [tinv[i] + _dot(pw[i], _blk(tinv[i], m1)) for i in idx]
        x = [_dot(tinv[i].astype(BF16), _blk(rhs[i], m2)) for i in idx]
        corr = [_dot(tril(incl, sb[i][L:2 * L]), _blk(x[i], m2)) for i in idx]
        r_hat = [(r_c[i] + corr[i][:, 0:LANES]).astype(BF16) for i in idx]
        y0 = [corr[i][:, LANES:2 * LANES] + _dot(tril(incl, sk[i][L:2 * L]), vblk[i]) for i in idx]
        xb = [x[i].astype(BF16) for i in idx]
        bb = [b_bar[rs, sl].astype(BF16) for rs, sl in chains]
        gmat = [jnp.where(bd, _dot_tn(xb[i][:, 0:LANES], bb[i]), 0.0).astype(BF16) for i in idx]
        hmat = [jnp.where(bd, _dot_tn(xb[i][:, LANES:2 * LANES], bb[i])
                          + _dot_tn(vw[rs, sl].astype(BF16), k_bar[rs, sl].astype(BF16)), 0.0)
                for i, (rs, sl) in enumerate(chains)]

        y_rows = []
        for c in range(n_ch):
            ys = []
            for pr in range(n_pairs):
                i = c * n_pairs + pr
                s0 = s_cur[pr]
                s0b = s0.astype(BF16)
                ys.append(_dot_nt(r_hat[i], s0b) + y0[i])
                s_cur[pr] = s0 * w_tot[c * L:c * L + 1, chains[i][1]] + _dot(s0b, gmat[i]) + hmat[i]
            y_rows.append(jnp.concatenate(ys, axis=1))
        y = jnp.concatenate(y_rows, axis=0)

        inv = 1.0 / RWKV_HEAD
        mean = _mm_exact_rhs(_dot, y, seg) * inv
        yc = y - mean
        var = _mm_exact_rhs(_dot, yc * yc, seg) * inv
        yn = yc * lax.rsqrt(var + LN_X_EPS) * lnw_ref[...] + lnb_ref[...]
        bonus = _mm_exact_rhs(_dot, r[w] * k2 * rk_ref[...], seg) * vw
        o_ref[wv[w], :] = ((yn + bonus) * g[w]).astype(o_ref.dtype)
    for pr in range(n_pairs):
        state[pr] = s_cur[pr]


def _rwkv(p_rw, batch, seq, mu, w0, w_up, a0, a_up, g_up, k_k, k_a, r_k, ln_w, ln_b):
    L = min(RW_STEP_ROWS, seq)
    nc = seq // L
    W = RWKV_WIDTH
    wup = jnp.zeros((LANES, W), F32).at[0:64].set(w_up).astype(BF16)
    aup = jnp.zeros((LANES, W), F32).at[64:128].set(a_up).astype(BF16)
    hid = jnp.arange(W) // RWKV_HEAD
    seg = (hid[:, None] == hid[None, :]).astype(BF16)
    wave = min(RW_WAVE_ROWS, L)
    ri = jnp.arange(wave)
    tri = jnp.logical_and(ri[:, None] >= ri[None, :],
                          ri[:, None] // RW_CHUNK == ri[None, :] // RW_CHUNK).astype(BF16)
    vec = lambda: pl.BlockSpec((1, W), lambda b, i: (0, 0))
    mat = lambda s: pl.BlockSpec(s, lambda b, i: (0, 0))
    return pl.pallas_call(
        _rwkv_kernel,
        grid=(batch, nc),
        in_specs=[pl.BlockSpec((L, RW_IN), lambda b, i: (b * nc + i, 0)),
                  mat((1, RW_IN)), vec(), mat((LANES, W)), vec(), mat((LANES, W)),
                  mat((LANES, W)), vec(), vec(), vec(), vec(), vec(), mat((W, W)), mat((wave, wave))],
        out_specs=pl.BlockSpec((L, W), lambda b, i: (b * nc + i, 0)),
        out_shape=jax.ShapeDtypeStruct((batch * seq, W), BF16),
        scratch_shapes=[pltpu.VMEM((L + CARRY_ROWS, RW_IN), F32),
                        pltpu.VMEM((RWKV_HEADS // 2, LANES, LANES), F32)],
        compiler_params=_cparams(("parallel", "arbitrary")),
        name="rwkv7",
    )(p_rw, mu.reshape(1, RW_IN), w0.reshape(1, W), wup, a0.reshape(1, W), aup,
      g_up.astype(BF16), k_k.reshape(1, W), k_a.reshape(1, W), r_k.reshape(1, W),
      ln_w.reshape(1, W), ln_b.reshape(1, W), seg, tri)


def _swiglu(xb, w1, w3, w2):
    edges = [sum(FF_CHUNKS[:i]) for i in range(len(FF_CHUNKS) + 1)]
    cols = [slice(edges[i], edges[i + 1]) for i in range(len(FF_CHUNKS))]
    hid = (_dot(xb, w1(cols[0])), _dot(xb, w3(cols[0])))
    acc = None
    for i, c in enumerate(cols):
        nxt = (_dot(xb, w1(cols[i + 1])), _dot(xb, w3(cols[i + 1]))) if i + 1 < len(cols) else None
        act = (hid[0] * _sigmoid(hid[0]) * hid[1]).astype(BF16)
        part = _dot(act, w2(c))
        acc = part if acc is None else acc + part
        hid = nxt
    return acc


def _ffn_kernel(x_ref, ya_ref, yb_ref, wa_ref, wb_ref, g_ref, w1_ref, w3_ref, w2_ref, o_ref):
    x1 = x_ref[...] + _dot(ya_ref[...], wa_ref[...]) + _dot(yb_ref[...], wb_ref[...])
    xn = _rms(x1, g_ref[...]).astype(BF16)
    o_ref[...] = x1 + _swiglu(xn, lambda c: w1_ref[:, c], lambda c: w3_ref[:, c], lambda c: w2_ref[c, :])


def _ffn(x, ya, yb, wa, wb, g, w1, w3, w2, tm):
    t, d = x.shape
    once = pl.Buffered(1)
    fixed = lambda a: pl.BlockSpec(a.shape, lambda i: (0, 0), pipeline_mode=once)
    return pl.pallas_call(
        _ffn_kernel,
        grid=(t // tm,),
        in_specs=[pl.BlockSpec((tm, d), lambda i: (i, 0)),
                  pl.BlockSpec((tm, ya.shape[1]), lambda i: (i, 0)),
                  pl.BlockSpec((tm, yb.shape[1]), lambda i: (i, 0)),
                  fixed(wa), fixed(wb),
                  pl.BlockSpec((1, d), lambda i: (0, 0)),
                  fixed(w1), fixed(w3), fixed(w2)],
        out_specs=pl.BlockSpec((tm, d), lambda i: (i, 0)),
        out_shape=jax.ShapeDtypeStruct((t, d), F32),
        compiler_params=_cparams(("parallel",)),
        name="ffn_swiglu",
    )(x, ya, yb, wa, wb, g.reshape(1, d), w1, w3, w2)


def _mlstm_kernel(xm_ref, z_ref, gt_ref, cw_ref, cb_ref, wq_ref, wk_ref, wv_ref, gb_ref,
                  nw_ref, tri_ref, o_ref, xbuf, c_s, n_s, m_s):
    ci = pl.program_id(1)
    L = MLSTM_CHUNK
    dh = MLSTM_HEAD
    first = ci == 0
    xm = xm_ref[...].astype(F32)
    _shift_hist(xbuf, xm, first)
    xc = _causal_conv(xbuf, L, cw_ref[...], cb_ref[...])
    xc = xc * _sigmoid(xc)
    gl = gt_ref[...].astype(F32) + gb_ref[...]
    lf = jnp.minimum(gl, 0.0) - jnp.log(1.0 + jnp.exp(-jnp.abs(gl)))
    bcum = _mm_exact_lhs(_dot, tri_ref[...], lf)
    gl_t = gl.T
    bcum_t = bcum.T

    @pl.when(first)
    def _():
        c_s[...] = jnp.zeros(c_s.shape, F32)
        n_s[...] = jnp.zeros(n_s.shape, F32)
        m_s[...] = jnp.zeros(m_s.shape, F32)

    ti = lax.broadcasted_iota(jnp.int32, (L, L), 0)
    si = lax.broadcasted_iota(jnp.int32, (L, L), 1)
    causal = si <= ti
    scale = dh ** -0.5
    hs = range(MLSTM_HEADS)
    sls = [slice(h * dh, (h + 1) * dh) for h in hs]
    xs = [xc[:, sl].astype(BF16) for sl in sls]
    q = [_dot(xs[h], wq_ref[h]) * scale for h in hs]
    k = [_dot(xs[h], wk_ref[h]) for h in hs]
    vb = [_dot(xm[:, sls[h]].astype(BF16), wv_ref[h]).astype(BF16) for h in hs]
    qb = [q[h].astype(BF16) for h in hs]
    kb = [k[h].astype(BF16) for h in hs]
    qk = [_dot_nt(qb[h], kb[h]) for h in hs]
    qc = [_dot(qb[h], c_s[h].astype(BF16)) for h in hs]
    bcol = [bcum[:, MLSTM_HEADS + h:MLSTM_HEADS + h + 1] for h in hs]
    m_prev = [m_s[h:h + 1, 0:1] for h in hs]
    dlog = [jnp.where(causal, bcol[h] - bcum_t[MLSTM_HEADS + h:MLSTM_HEADS + h + 1, :] + gl_t[h:h + 1, :],
                      -jnp.inf) for h in hs]
    inter = [bcol[h] + m_prev[h] for h in hs]
    m_t = [jnp.maximum(inter[h], jnp.max(dlog[h], axis=-1, keepdims=True)) for h in hs]
    s = [qk[h] * jnp.exp(dlog[h] - m_t[h]) for h in hs]
    sv = [_dot(s[h].astype(BF16), vb[h]) for h in hs]
    b_last = [bcol[h][L - 1:L, :] for h in hs]
    wlog = [b_last[h] - bcol[h] + gl[:, h:h + 1] for h in hs]
    m_new = [jnp.maximum(b_last[h] + m_prev[h], jnp.max(wlog[h], axis=0, keepdims=True)) for h in hs]
    kw = [k[h] * jnp.exp(wlog[h] - m_new[h]) for h in hs]
    kv = [_dot_tn(kw[h].astype(BF16), vb[h]) for h in hs]
    sc = [jnp.exp(inter[h] - m_t[h]) for h in hs]
    qn = [jnp.sum(q[h] * n_s[h:h + 1, :], axis=-1, keepdims=True) for h in hs]
    ssum = [jnp.sum(s[h], axis=-1, keepdims=True) for h in hs]
    den = [sc[h] * qn[h] + ssum[h] for h in hs]
    hh = [(sc[h] * qc[h] + sv[h]) / jnp.maximum(jnp.abs(den[h]), jnp.exp(-m_t[h])) for h in hs]
    ms = [jnp.mean(hh[h] * hh[h], axis=-1, keepdims=True) for h in hs]
    dec = [jnp.exp(b_last[h] + m_prev[h] - m_new[h]) for h in hs]
    for h in hs:
        c_s[h] = dec[h] * c_s[h] + kv[h]
        n_s[h:h + 1, :] = dec[h] * n_s[h:h + 1, :] + jnp.sum(kw[h], axis=0, keepdims=True)
        m_s[h:h + 1, :] = jnp.broadcast_to(m_new[h], (1, LANES))
        hn = hh[h] * lax.rsqrt(ms[h] + RMS_EPS)
        gate = _sigmoid(z_ref[:, sls[h]].astype(F32))
        o_ref[:, sls[h]] = (gate * (hn * nw_ref[:, sls[h]])).astype(o_ref.dtype)


def _mlstm(p1, batch, seq, conv_w, conv_b, wq, wk, wv, ig_b, fg_b, mh_w):
    L = MLSTM_CHUNK
    nc = seq // L
    W = MLSTM_WIDTH
    gb = jnp.zeros((1, LANES), F32).at[0, 0:8].set(ig_b).at[0, 8:16].set(fg_b)
    tri = (jnp.arange(L)[:, None] >= jnp.arange(L)[None, :]).astype(BF16)
    hw = lambda: pl.BlockSpec((MLSTM_HEADS, MLSTM_HEAD, MLSTM_HEAD), lambda b, i: (0, 0, 0))
    return pl.pallas_call(
        _mlstm_kernel,
        grid=(batch, nc),
        in_specs=[pl.BlockSpec((L, W), lambda b, i: (b * nc + i, 0)),
                  pl.BlockSpec((L, W), lambda b, i: (b * nc + i, 1)),
                  pl.BlockSpec((L, LANES), lambda b, i: (b * nc + i, 2 * W // LANES)),
                  pl.BlockSpec((CONV_WIDTH, W), lambda b, i: (0, 0)),
                  pl.BlockSpec((1, W), lambda b, i: (0, 0)),
                  hw(), hw(), hw(),
                  pl.BlockSpec((1, LANES), lambda b, i: (0, 0)),
                  pl.BlockSpec((1, W), lambda b, i: (0, 0)),
                  pl.BlockSpec((L, L), lambda b, i: (0, 0))],
        out_specs=pl.BlockSpec((L, W), lambda b, i: (b * nc + i, 0)),
        out_shape=jax.ShapeDtypeStruct((batch * seq, W), BF16),
        scratch_shapes=[pltpu.VMEM((L + CARRY_ROWS, W), F32),
                        pltpu.VMEM((MLSTM_HEADS, MLSTM_HEAD, MLSTM_HEAD), F32),
                        pltpu.VMEM((MLSTM_HEADS, LANES), F32),
                        pltpu.VMEM((MLSTM_HEADS, LANES), F32)],
        compiler_params=_cparams(("parallel", "arbitrary")),
        name="mlstm",
    )(p1, p1, p1, conv_w, conv_b.reshape(1, W), wq.astype(BF16), wk.astype(BF16),
      wv.astype(BF16), gb, mh_w.reshape(1, W), tri)


def _out_router_kernel(x_ref, y_ref, w_ref, g_ref, rt_ref, x3_ref, o_ref):
    tm = x_ref.shape[0]
    sub = tm // ROUTER_SPLIT
    rs = [slice(q * sub, (q + 1) * sub) for q in range(ROUTER_SPLIT)]
    qs = range(ROUTER_SPLIT)
    w = w_ref[...]
    x3 = [x_ref[r, :] + _dot(y_ref[r, :], w) for r in rs]
    for q in qs:
        x3_ref[rs[q], :] = x3[q]
    gw = g_ref[...]
    rt = rt_ref[...]
    logits = [_mm3(_dot, _rms(x3[q], gw), rt) for q in qs]
    lane = lax.broadcasted_iota(jnp.int32, (sub, LANES), 1)
    lg = [jnp.where(lane < N_EXPERTS, logits[q], -jnp.inf) for q in qs]
    v1 = [jnp.max(lg[q], axis=-1, keepdims=True) for q in qs]
    i1 = [jnp.min(jnp.where(lg[q] == v1[q], lane, LANES), axis=-1, keepdims=True) for q in qs]
    lg2 = [jnp.where(lane == i1[q], -jnp.inf, lg[q]) for q in qs]
    v2 = [jnp.max(lg2[q], axis=-1, keepdims=True) for q in qs]
    i2 = [jnp.min(jnp.where(lg2[q] == v2[q], lane, LANES), axis=-1, keepdims=True) for q in qs]
    for q in qs:
        ex = jnp.exp(v2[q] - v1[q])
        g1 = 1.0 / (1.0 + ex)
        g2 = ex / (1.0 + ex)
        out = jnp.where(lane == 0, i1[q].astype(F32), 0.0)
        out = jnp.where(lane == 1, i2[q].astype(F32), out)
        out = jnp.where(lane == 2, g1, out)
        o_ref[rs[q], :] = jnp.where(lane == 3, g2, out)


def _out_router(x, y, w, g, router, tm):
    t, d = x.shape
    return pl.pallas_call(
        _out_router_kernel,
        grid=(t // tm,),
        in_specs=[pl.BlockSpec((tm, d), lambda i: (i, 0)),
                  pl.BlockSpec((tm, y.shape[1]), lambda i: (i, 0)),
                  pl.BlockSpec(w.shape, lambda i: (0, 0)),
                  pl.BlockSpec((1, d), lambda i: (0, 0)),
                  pl.BlockSpec((d, LANES), lambda i: (0, 0))],
        out_specs=[pl.BlockSpec((tm, d), lambda i: (i, 0)),
                   pl.BlockSpec((tm, LANES), lambda i: (i, 0))],
        out_shape=[jax.ShapeDtypeStruct((t, d), F32), jax.ShapeDtypeStruct((t, LANES), F32)],
        compiler_params=_cparams(("parallel",)),
        name="l1_out_router",
    )(x, y, w, g.reshape(1, d), router)


def _route_plan(route, blk):
    e = route[:, 0:TOP_K].astype(jnp.int32).reshape(-1)
    m = e.shape[0]
    onehot = (e[:, None] == jnp.arange(N_EXPERTS, dtype=jnp.int32)[None, :]).astype(jnp.int32)
    csum = jnp.cumsum(onehot, axis=0)
    rank = jnp.sum((csum - onehot) * onehot, axis=1)
    nb = (csum[-1] + blk - 1) // blk
    bend = jnp.cumsum(nb)
    dest = jnp.sum(onehot * (bend - nb)[None, :], axis=1) * blk + rank
    nblk = m // blk + N_EXPERTS
    n_used = bend[-1]
    bidx = jnp.minimum(jnp.arange(nblk, dtype=jnp.int32), n_used - 1)
    blk_e = jnp.minimum(jnp.searchsorted(bend, bidx, side="right"), N_EXPERTS - 1).astype(jnp.int32)
    gap_start = jnp.concatenate([(bend - nb) * blk + csum[-1], (n_used * blk)[None]])
    gap_len = jnp.concatenate([nb * blk - csum[-1], ((nblk - n_used) * blk)[None]])
    gap_end = jnp.cumsum(gap_len)
    q = jnp.arange(nblk * blk - m, dtype=jnp.int32)
    gi = jnp.searchsorted(gap_end, q, side="right")
    pad = gap_start[gi] + q - (gap_end - gap_len)[gi]
    return dest.astype(jnp.int32), pad.astype(jnp.int32), blk_e, n_used.reshape(1).astype(jnp.int32), nblk


def _to_tiles(ref, x):
    n = x.shape[0]
    for s in range(ROW_TILE):
        ref[pl.ds(s, n, stride=ROW_TILE), :] = x[:, s * LANES:(s + 1) * LANES]


def _from_tiles(ref, n):
    return jnp.concatenate([ref[pl.ds(s, n, stride=ROW_TILE), :] for s in range(ROW_TILE)], axis=1)


def _row_copy(src, src_row, dst, dst_row, sem):
    s0 = pl.multiple_of(src_row * ROW_TILE, ROW_TILE)
    d0 = pl.multiple_of(dst_row * ROW_TILE, ROW_TILE)
    return pltpu.make_async_copy(src.at[pl.ds(s0, ROW_TILE), :], dst.at[pl.ds(d0, ROW_TILE), :], sem)


def _wait_rows(hbm, vmem, sem, to_hbm):
    rows = hbm.at[pl.ds(0, vmem.shape[0]), :]
    (pltpu.make_async_copy(vmem, rows, sem) if to_hbm else pltpu.make_async_copy(rows, vmem, sem)).wait()


def _dispatch_kernel(nt, dest_ref, pad_ref, x_ref, g_ref, buf_out, xn_s, zero_s, sem, zsem):
    i = pl.program_id(0)
    tm = x_ref.shape[0]
    slot = i % 2
    n_pad = pad_ref.shape[2]

    def drain(s):
        for _ in range(TOP_K):
            _wait_rows(buf_out, xn_s.at[s], sem.at[s], True)

    @pl.when(i >= 2)
    def _():
        drain(slot)

    zero_s[...] = jnp.zeros(zero_s.shape, F32)

    def zero(q, c):
        _row_copy(zero_s, q, buf_out, pad_ref[0, 0, q], zsem.at[0]).start()
        return c

    lax.fori_loop(0, n_pad, zero, 0, unroll=8)
    _to_tiles(xn_s.at[slot], _rms(x_ref[...], g_ref[...]))

    def start(r, c):
        for k in range(TOP_K):
            _row_copy(xn_s.at[slot], r, buf_out, dest_ref[0, 0, TOP_K * r + k],
                      sem.at[slot]).start(priority=k % DMA_QUEUES)
        return c

    lax.fori_loop(0, tm, start, 0, unroll=8)
    _wait_rows(buf_out, zero_s, zsem.at[0], True)

    @pl.when(i == nt - 1)
    def _():
        drain(slot)
        if nt > 1:
            drain(1 - slot)


def _dispatch(x, g, dest, pad, rows, tm):
    t, d = x.shape
    nt = t // tm
    n_pad = pad.shape[0] // nt
    assert n_pad * nt == pad.shape[0]
    return pl.pallas_call(
        functools.partial(_dispatch_kernel, nt),
        grid=(nt,),
        in_specs=[pl.BlockSpec((1, 1, TOP_K * tm), lambda i: (i, 0, 0), memory_space=pltpu.SMEM),
                  pl.BlockSpec((1, 1, n_pad), lambda i: (i, 0, 0), memory_space=pltpu.SMEM),
                  pl.BlockSpec((tm, d), lambda i: (i, 0)),
                  pl.BlockSpec((1, d), lambda i: (0, 0))],
        out_specs=pl.BlockSpec(memory_space=pl.ANY),
        out_shape=jax.ShapeDtypeStruct((rows * ROW_TILE, LANES), F32),
        scratch_shapes=[pltpu.VMEM((2, tm * ROW_TILE, LANES), F32),
                        pltpu.VMEM((n_pad * ROW_TILE, LANES), F32),
                        pltpu.SemaphoreType.DMA((2,)), pltpu.SemaphoreType.DMA((1,))],
        compiler_params=_cparams(("arbitrary",)),
        name="moe_dispatch",
    )(dest.reshape(nt, 1, TOP_K * tm), pad.reshape(nt, 1, n_pad), x, g.reshape(1, d))


def _expert_kernel(be_ref, nu_ref, x_ref, w1_ref, w3_ref, w2_ref, y_ref):
    del be_ref
    used = pl.program_id(0) < nu_ref[0]

    @pl.when(jnp.logical_not(used))
    def _():
        y_ref[...] = jnp.zeros(y_ref.shape, F32)

    @pl.when(used)
    def _():
        x = _from_tiles(x_ref, x_ref.shape[0] // ROW_TILE).astype(BF16)
        y = _swiglu(x, lambda c: w1_ref[0, :, c], lambda c: w3_ref[0, :, c], lambda c: w2_ref[0, c, :])
        _to_tiles(y_ref, y)


def _experts(buf, blk_e, n_used, w1, w3, w2, blk):
    d, f = w1.shape[1], w1.shape[2]
    once = pl.Buffered(1)
    grid_spec = pltpu.PrefetchScalarGridSpec(
        num_scalar_prefetch=2,
        grid=(buf.shape[0] // (blk * ROW_TILE),),
        in_specs=[pl.BlockSpec((blk * ROW_TILE, LANES), lambda b, be, nu: (b, 0)),
                  pl.BlockSpec((1, d, f), lambda b, be, nu: (be[b], 0, 0), pipeline_mode=once),
                  pl.BlockSpec((1, d, f), lambda b, be, nu: (be[b], 0, 0), pipeline_mode=once),
                  pl.BlockSpec((1, f, d), lambda b, be, nu: (be[b], 0, 0), pipeline_mode=once)],
        out_specs=pl.BlockSpec((blk * ROW_TILE, LANES), lambda b, be, nu: (b, 0)))
    return pl.pallas_call(
        _expert_kernel,
        grid_spec=grid_spec,
        out_shape=jax.ShapeDtypeStruct(buf.shape, F32),
        compiler_params=_cparams(("arbitrary",)),
        name="moe_experts",
    )(blk_e, n_used, buf, w1, w3, w2)


def _combine_kernel(nt, dcur_ref, dnxt_ref, x_ref, rt_ref, fn_ref, y_hbm, o_ref, ya, yb, sem):
    i = pl.program_id(0)
    tm = x_ref.shape[0]
    slot = i % 2

    def gather(dref, s):
        def start(r, c):
            _row_copy(y_hbm, dref[0, 0, TOP_K * r], ya.at[s], r, sem.at[s, 0]).start(priority=0)
            _row_copy(y_hbm, dref[0, 0, TOP_K * r + 1], yb.at[s], r, sem.at[s, 1]).start(
                priority=1 % DMA_QUEUES)
            return c

        lax.fori_loop(0, tm, start, 0, unroll=8)

    @pl.when(i == 0)
    def _():
        gather(dcur_ref, slot)

    @pl.when(i + 1 < nt)
    def _():
        gather(dnxt_ref, 1 - slot)

    _wait_rows(y_hbm, ya.at[slot], sem.at[slot, 0], False)
    _wait_rows(y_hbm, yb.at[slot], sem.at[slot, 1], False)
    rt = rt_ref[...]
    lane = lax.broadcasted_iota(jnp.int32, rt.shape, 1)
    g1 = jnp.sum(jnp.where(lane == 2, rt, 0.0), axis=-1, keepdims=True)
    g2 = jnp.sum(jnp.where(lane == 3, rt, 0.0), axis=-1, keepdims=True)
    moe = g1 * _from_tiles(ya.at[slot], tm) + g2 * _from_tiles(yb.at[slot], tm)
    o_ref[...] = _rms(x_ref[...] + moe, fn_ref[...])


def _combine(x, route, fnorm, y, dest, tm):
    t, d = x.shape
    nt = t // tm
    dest3 = dest.reshape(nt, 1, TOP_K * tm)
    return pl.pallas_call(
        functools.partial(_combine_kernel, nt),
        grid=(nt,),
        in_specs=[pl.BlockSpec((1, 1, TOP_K * tm), lambda i: (i, 0, 0), memory_space=pltpu.SMEM),
                  pl.BlockSpec((1, 1, TOP_K * tm), lambda i: (jnp.minimum(i + 1, nt - 1), 0, 0),
                               memory_space=pltpu.SMEM),
                  pl.BlockSpec((tm, d), lambda i: (i, 0)),
                  pl.BlockSpec((tm, LANES), lambda i: (i, 0)),
                  pl.BlockSpec((1, d), lambda i: (0, 0)),
                  pl.BlockSpec(memory_space=pl.ANY)],
        out_specs=pl.BlockSpec((tm, d), lambda i: (i, 0)),
        out_shape=jax.ShapeDtypeStruct((t, d), F32),
        scratch_shapes=[pltpu.VMEM((2, tm * ROW_TILE, LANES), F32),
                        pltpu.VMEM((2, tm * ROW_TILE, LANES), F32),
                        pltpu.SemaphoreType.DMA((2, 2))],
        compiler_params=_cparams(("arbitrary",)),
        name="moe_combine",
    )(dest3, dest3, x, route, fnorm.reshape(1, d), y)


def _moe(x, route, g, w1, w3, w2, fnorm, tm):
    blk = min(MOE_BLOCK, x.shape[0])
    dest, pad, blk_e, n_used, nblk = _route_plan(route, blk)
    buf = _dispatch(x, g, dest, pad, nblk * blk, tm)
    y = _experts(buf, blk_e, n_used, w1, w3, w2, blk)
    return _combine(x, route, fnorm, y, dest, tm)


def _block_diag(w):
    g, a, b = w.shape
    eye = jnp.eye(g, dtype=w.dtype)
    return (eye[:, None, :, None] * w[:, :, None, :]).reshape(g * a, g * b)


def kernel(x, l0_norm_mix, l0_w_in, l0_conv_w, l0_conv_b, l0_gate_a_w, l0_gate_a_b, l0_gate_x_w, l0_gate_x_b, l0_lru_lambda, l0_shift_mu, l0_w0, l0_w_up, l0_a0, l0_a_up, l0_g_up, l0_k_k, l0_k_a, l0_r_k, l0_ln_x_w, l0_ln_x_b, l0_w_out, l0_norm_ffn, l0_ffn_w1, l0_ffn_w3, l0_ffn_w2, l1_norm_mix, l1_w_in, l1_conv_w, l1_conv_b, l1_wq, l1_wk, l1_wv, l1_ig_b, l1_fg_b, l1_mh_norm_w, l1_w_out, l1_norm_ffn, l1_router, l1_moe_w1, l1_moe_w3, l1_moe_w2, final_norm):
    batch, seq, d = x.shape
    t = batch * seq
    xt = x.reshape(t, d)
    tm = min(512, t)

    w_in0 = l0_w_in.astype(BF16)
    p_lru, p_rw = _norm_mm(xt, l0_norm_mix, w_in0, (2 * LRU_WIDTH, RW_IN), tm, "l0_in")
    y_lru = _lru(p_lru, batch, seq, l0_conv_w, l0_conv_b,
                 _block_diag(l0_gate_a_w).astype(BF16), l0_gate_a_b,
                 _block_diag(l0_gate_x_w).astype(BF16), l0_gate_x_b, l0_lru_lambda,
                 min(512, seq))
    y_rw = _rwkv(p_rw, batch, seq, l0_shift_mu, l0_w0, l0_w_up, l0_a0, l0_a_up, l0_g_up,
                 l0_k_k, l0_k_a, l0_r_k.reshape(-1), l0_ln_x_w, l0_ln_x_b)
    w_out0 = l0_w_out.astype(BF16)
    x2 = _ffn(xt, y_lru, y_rw, w_out0[:LRU_WIDTH], w_out0[LRU_WIDTH:], l0_norm_ffn,
              l0_ffn_w1.astype(BF16), l0_ffn_w3.astype(BF16), l0_ffn_w2.astype(BF16), tm)

    n_in1 = 2 * MLSTM_WIDTH + LANES
    w_in1 = jnp.zeros((d, n_in1), F32).at[:, :l1_w_in.shape[1]].set(l1_w_in).astype(BF16)
    (p1,) = _norm_mm(x2, l1_norm_mix, w_in1, (n_in1,), tm, "l1_in")
    h1 = _mlstm(p1, batch, seq, l1_conv_w, l1_conv_b, l1_wq, l1_wk, l1_wv, l1_ig_b, l1_fg_b,
                l1_mh_norm_w)
    router = jnp.zeros((d, LANES), F32).at[:, :N_EXPERTS].set(l1_router)
    x3, route = _out_router(x2, h1, l1_w_out.astype(BF16), l1_norm_ffn, router, tm)
    out = _moe(x3, route, l1_norm_ffn, l1_moe_w1.astype(BF16), l1_moe_w3.astype(BF16),
               l1_moe_w2.astype(BF16), final_norm, tm)
    return out.reshape(batch, seq, d)
```

```python
import functools

import jax
import jax.numpy as jnp
from jax import lax
from jax.experimental import pallas as pl
from jax.experimental.pallas import tpu as pltpu

F32 = jnp.float32
BF16 = jnp.bfloat16

D_MODEL = 1024
LRU_WIDTH = 512
LRU_BLOCKS = 8
LRU_C = 8.0
LRU_SEGS = 8
CONV_WIDTH = 4
RWKV_HEADS = 8
RWKV_HEAD = 64
RWKV_WIDTH = 512
LN_X_EPS = 1e-5 * RWKV_HEAD
MLSTM_HEADS = 8
MLSTM_HEAD = 128
MLSTM_WIDTH = 1024
MLSTM_CHUNK = 128
D_FF = 2816
N_EXPERTS = 8
TOP_K = 2
DMA_QUEUES = 2
ROUTER_SPLIT = 4
MOE_BLOCK = 512
FF_CHUNKS = (768, 768, 768, 512)
RMS_EPS = 1e-6
RW_IN = 3 * RWKV_WIDTH + 64 + 64 + 128
RW_CHUNK = 64
RW_STEP_ROWS = 512
RW_WAVE_ROWS = 256
LANES = 128
MXU_WIDTH = 256
CARRY_ROWS = 8
ROW_TILE = 8
VMEM_LIMIT = 56 * 1024 * 1024


def _cparams(sem):
    return pltpu.CompilerParams(dimension_semantics=sem, vmem_limit_bytes=VMEM_LIMIT)


def _rms(x, w):
    return x * lax.rsqrt(jnp.mean(x * x, axis=-1, keepdims=True) + RMS_EPS) * w


def _sigmoid(x):
    return 1.0 / (1.0 + jnp.exp(-x))


def _softplus(x):
    return jnp.maximum(x, 0.0) + jnp.log(1.0 + jnp.exp(-jnp.abs(x)))


def _dot(a, b):
    return jnp.dot(a, b, preferred_element_type=F32)


def _dot_nt(a, b):
    return lax.dot_general(a, b, (((1,), (1,)), ((), ())), preferred_element_type=F32)


def _dot_tn(a, b):
    return lax.dot_general(a, b, (((0,), (0,)), ((), ())), preferred_element_type=F32)


def _split2(x):
    hi = x.astype(BF16)
    lo = (x - hi.astype(F32)).astype(BF16)
    return hi, lo


def _split3(x):
    hi = x.astype(BF16)
    r = x - hi.astype(F32)
    mid = r.astype(BF16)
    lo = (r - mid.astype(F32)).astype(BF16)
    return hi, mid, lo


def _mm3(fn, a, b):
    ah, al = _split2(a)
    bh, bl = _split2(b)
    return fn(ah, bh) + fn(al, bh) + fn(ah, bl)


def _mm_exact_lhs(fn, a_bf16, b):
    h, m, l = _split3(b)
    return fn(a_bf16, h) + fn(a_bf16, m) + fn(a_bf16, l)


def _seg_sum(x, seg):
    s = seg.shape[0]
    outs = []
    for c in range(x.shape[1] // s):
        hi, lo = _split2(x[:, c * s:(c + 1) * s])
        outs.append(_dot(hi, seg) + _dot(lo, seg))
    return jnp.concatenate(outs, axis=1)


def _shift_hist(buf_ref, x, first):
    n = x.shape[0]

    @pl.when(first)
    def _():
        buf_ref[0:CARRY_ROWS, :] = jnp.zeros((CARRY_ROWS, x.shape[1]), F32)

    @pl.when(jnp.logical_not(first))
    def _():
        buf_ref[0:CARRY_ROWS, :] = buf_ref[n:n + CARRY_ROWS, :]

    buf_ref[CARRY_ROWS:CARRY_ROWS + n, :] = x


def _causal_conv(buf_ref, n, w, b):
    acc = b
    for j in range(CONV_WIDTH):
        off = CARRY_ROWS - (CONV_WIDTH - 1) + j
        acc = acc + w[j:j + 1, :] * buf_ref[off:off + n, :]
    return acc


def _norm_mm_kernel(x_ref, g_ref, w_ref, *o_refs):
    xn = _rms(x_ref[...], g_ref[...]).astype(BF16)
    c0 = 0
    for o_ref in o_refs:
        c1 = c0 + o_ref.shape[1]
        o_ref[...] = _dot(xn, w_ref[:, c0:c1]).astype(o_ref.dtype)
        c0 = c1


def _norm_mm(x, g, w, widths, tm, name):
    t, d = x.shape
    n = w.shape[1]
    assert sum(widths) == n
    return pl.pallas_call(
        _norm_mm_kernel,
        grid=(t // tm,),
        in_specs=[pl.BlockSpec((tm, d), lambda i: (i, 0)),
                  pl.BlockSpec((1, d), lambda i: (0, 0)),
                  pl.BlockSpec((d, n), lambda i: (0, 0))],
        out_specs=[pl.BlockSpec((tm, c), lambda i: (i, 0)) for c in widths],
        out_shape=[jax.ShapeDtypeStruct((t, c), BF16) for c in widths],
        compiler_params=_cparams(("parallel",)),
        name=name,
    )(x, g.reshape(1, d), w)


def _lru_kernel(p_ref, cw_ref, cb_ref, wa_ref, ba_ref, wx_ref, bx_ref, lam_ref, o_ref,
                xbuf, abuf, bbuf, hbuf, hcar):
    i = pl.program_id(1)
    n = p_ref.shape[0]
    first = i == 0
    _shift_hist(xbuf, p_ref[:, 0:LRU_WIDTH].astype(F32), first)
    xc = _causal_conv(xbuf, n, cw_ref[...], cb_ref[...])
    xcb = xc.astype(BF16)
    r = _sigmoid(_dot(xcb, wa_ref[...]) + ba_ref[...])
    ig = _sigmoid(_dot(xcb, wx_ref[...]) + bx_ref[...])
    log_a = (-LRU_C) * r * _softplus(-lam_ref[...])
    a = jnp.exp(log_a)
    mult = jnp.sqrt(1.0 - a * a)
    row = lax.broadcasted_iota(jnp.int32, (n, 1), 0)
    mult = jnp.where(jnp.logical_and(first, row == 0), 1.0, mult)
    bvals = mult * ig * xc
    n_slab = LRU_WIDTH // LANES
    slabs = [slice(c * LANES, (c + 1) * LANES) for c in range(n_slab)]
    seg = n // LRU_SEGS
    pitch = _lru_pitch(n)
    for c, cs in enumerate(slabs):
        for s in range(LRU_SEGS):
            abuf[c, s * pitch:s * pitch + seg, :] = a[s * seg:(s + 1) * seg, cs]
            bbuf[c, s * pitch:s * pitch + seg, :] = bvals[s * seg:(s + 1) * seg, cs]

    @pl.when(first)
    def _():
        hcar[...] = jnp.zeros(hcar.shape, F32)

    def step(j, carry):
        rows = pl.ds(j, LRU_SEGS, stride=pitch)
        out = []
        for c in range(n_slab):
            h, acc = carry[c]
            a_j = abuf[c, rows, :]
            h = a_j * h + bbuf[c, rows, :]
            acc = acc * a_j
            hbuf[c, rows, :] = h
            abuf[c, rows, :] = acc
            out.append((h, acc))
        return tuple(out)

    init = tuple((jnp.zeros((LRU_SEGS, LANES), F32), jnp.ones((LRU_SEGS, LANES), F32))
                 for _ in range(n_slab))
    ends = lax.fori_loop(0, seg, step, init, unroll=8)
    cols = []
    for c, cs in enumerate(slabs):
        h_end, a_end = ends[c]
        h0 = hcar[0:1, cs]
        parts = []
        for s in range(LRU_SEGS):
            rows = slice(s * pitch, s * pitch + seg)
            parts.append(hbuf[c, rows, :] + abuf[c, rows, :] * h0)
            h0 = h_end[s:s + 1, :] + a_end[s:s + 1, :] * h0
        hcar[0:1, cs] = h0
        cols.append(jnp.concatenate(parts, axis=0))
    gate = p_ref[:, LRU_WIDTH:2 * LRU_WIDTH].astype(F32)
    gelu = 0.5 * gate * (1.0 + jnp.tanh(0.7978845608028654 * (gate + 0.044715 * gate * gate * gate)))
    o_ref[...] = (jnp.concatenate(cols, axis=1) * gelu).astype(o_ref.dtype)


def _lru_pitch(n):
    tiles = n // LRU_SEGS // CARRY_ROWS
    return (tiles + 1 - tiles % 2) * CARRY_ROWS


def _lru(p_lru, batch, seq, conv_w, conv_b, wa, ba, wx, bx, lam, tb):
    nb = seq // tb
    c = LRU_WIDTH
    vec = lambda: pl.BlockSpec((1, c), lambda b, i: (0, 0))
    return pl.pallas_call(
        _lru_kernel,
        grid=(batch, nb),
        in_specs=[pl.BlockSpec((tb, 2 * c), lambda b, i: (b * nb + i, 0)),
                  pl.BlockSpec((CONV_WIDTH, c), lambda b, i: (0, 0)), vec(),
                  pl.BlockSpec((c, c), lambda b, i: (0, 0)), vec(),
                  pl.BlockSpec((c, c), lambda b, i: (0, 0)), vec(), vec()],
        out_specs=pl.BlockSpec((tb, c), lambda b, i: (b * nb + i, 0)),
        out_shape=jax.ShapeDtypeStruct((batch * seq, c), BF16),
        scratch_shapes=[pltpu.VMEM((tb + CARRY_ROWS, c), F32)]
        + [pltpu.VMEM((c // LANES, LRU_SEGS * _lru_pitch(tb), LANES), F32)] * 3
        + [pltpu.VMEM((CARRY_ROWS, c), F32)],
        compiler_params=_cparams(("parallel", "arbitrary")),
        name="rg_lru",
    )(p_lru, conv_w, conv_b.reshape(1, c), wa, ba.reshape(1, c), wx, bx.reshape(1, c),
      lam.reshape(1, c))


def _blk(x, masks):
    xb = x.astype(BF16)
    return jnp.concatenate([jnp.where(mk, xb, jnp.zeros_like(xb)) for mk in masks], axis=0)


def _rwkv_kernel(p_ref, mu_ref, w0_ref, wup_ref, a0_ref, aup_ref, gup_ref, kk_ref, ka_ref,
                 rk_ref, lnw_ref, lnb_ref, seg_ref, tri_ref, o_ref, pbuf, state):
    L = RW_CHUNK
    W = RWKV_WIDTH
    rows = p_ref.shape[0]
    wave = tri_ref.shape[0]
    first = pl.program_id(1) == 0
    p = p_ref[...].astype(F32)
    _shift_hist(pbuf, p, first)
    prev = pbuf[CARRY_ROWS - 1:CARRY_ROWS - 1 + rows, :]
    ps = p + mu_ref[...] * (prev - p)
    seg = seg_ref[...]
    tri = tri_ref[...]

    @pl.when(first)
    def _():
        state[...] = jnp.zeros(state.shape, F32)

    lane = lax.broadcasted_iota(jnp.int32, (1, LANES), 1)
    m1 = [lane < RWKV_HEAD, lane >= RWKV_HEAD]
    m2 = [jnp.concatenate([mk, mk], axis=1) for mk in m1]
    ti = lax.broadcasted_iota(jnp.int32, (L, LANES), 0)
    si = lax.broadcasted_iota(jnp.int32, (L, LANES), 1) % RWKV_HEAD
    strict = si < ti
    incl = si <= ti
    bi = lax.broadcasted_iota(jnp.int32, (LANES, LANES), 0) // RWKV_HEAD
    bj = lax.broadcasted_iota(jnp.int32, (LANES, LANES), 1) // RWKV_HEAD
    bd = bi == bj

    def tril(mask, s):
        return jnp.where(mask, s, 0.0).astype(BF16)

    wv = [slice(w * wave, (w + 1) * wave) for w in range(rows // wave)]
    ws = range(len(wv))
    r = [ps[rw, 0:W] for rw in wv]
    k = [ps[rw, W:2 * W] for rw in wv]
    v = [ps[rw, 2 * W:3 * W] for rw in wv]
    x2 = [ps[rw, 3 * W:3 * W + LANES] for rw in wv]
    xg = [ps[rw, 3 * W + LANES:3 * W + 2 * LANES] for rw in wv]
    wl = [w0_ref[...] + _dot(jnp.tanh(x2[w]).astype(BF16), wup_ref[...]) for w in ws]
    a = [_sigmoid(a0_ref[...] + _dot(x2[w].astype(BF16), aup_ref[...])) for w in ws]
    g = [_dot(_sigmoid(xg[w]).astype(BF16), gup_ref[...]) for w in ws]
    kk = [k[w] * kk_ref[...] for w in ws]
    kk = [kk[w] / jnp.maximum(jnp.sqrt(_seg_sum(kk[w] * kk[w], seg)), 1e-12) for w in ws]
    lw = [-jnp.exp(-_softplus(-wl[w]) - 0.5) for w in ws]
    cw = [_mm_exact_lhs(_dot, tri, lw[w]) for w in ws]

    n_pairs = RWKV_HEADS // 2
    n_ch = wave // L
    s_cur = [state[pr] for pr in range(n_pairs)]
    for w in ws:
        cw_end = jnp.concatenate(
            [jnp.broadcast_to(cw[w][c * L + L - 1:c * L + L, :], (L, W)) for c in range(n_ch)], axis=0)
        w_in = jnp.exp(cw[w])
        w_inv = jnp.exp(-cw[w])
        w_end = jnp.exp(cw_end - cw[w])
        k2 = k[w] * (1.0 + (a[w] - 1.0) * ka_ref[...])
        kka = kk[w] * a[w]
        a_t = -kk[w] * jnp.exp(cw[w] - lw[w])
        b_t = kka * w_inv
        k_t = k2 * w_inv
        r_t = r[w] * w_in
        b_bar = kka * w_end
        k_bar = k2 * w_end
        w_tot = jnp.exp(cw_end)
        vw = v[w]

        chains = [(slice(c * L, (c + 1) * L), slice(pr * LANES, (pr + 1) * LANES))
                  for c in range(n_ch) for pr in range(n_pairs)]
        idx = range(len(chains))
        a_c = [a_t[rs, sl] for rs, sl in chains]
        r_c = [r_t[rs, sl] for rs, sl in chains]
        ar = [jnp.concatenate([a_c[i], r_c[i]], axis=0).astype(BF16) for i in idx]
        sb = [_dot_nt(ar[i], _blk(b_t[rs, sl], m1)) for i, (rs, sl) in enumerate(chains)]
        sk = [_dot_nt(ar[i], _blk(k_t[rs, sl], m1)) for i, (rs, sl) in enumerate(chains)]
        vblk = [_blk(vw[rs, sl], m1) for rs, sl in chains]
        rhs = [jnp.concatenate([a_c[i], _dot(tril(strict, sk[i][0:L]), vblk[i])], axis=1) for i in idx]
        pw = [tril(strict, sb[i][0:L]) for i in idx]
        tinv = [jnp.where(si == ti, 1.0, 0.0) + pw[i].astype(F32) for i in idx]
        for _ in range(5):
            pw = [_dot(pw[i], _blk(pw[i], m1)).astype(BF16) for i in idx]
            tinv = [tinv[i] + _dot(pw[i], _blk(tinv[i], m1)) for i in idx]
        x = [_dot(tinv[i].astype(BF16), _blk(rhs[i], m2)) for i in idx]
        corr = [_dot(tril(incl, sb[i][L:2 * L]), _blk(x[i], m2)) for i in idx]
        r_hat = [(r_c[i] + corr[i][:, 0:LANES]).astype(BF16) for i in idx]
        y0 = [corr[i][:, LANES:2 * LANES] + _dot(tril(incl, sk[i][L:2 * L]), vblk[i]) for i in idx]
        xb = [x[i].astype(BF16) for i in idx]
        bb = [b_bar[rs, sl].astype(BF16) for rs, sl in chains]
        gmat = [jnp.where(bd, _dot_tn(xb[i][:, 0:LANES], bb[i]), 0.0).astype(BF16) for i in idx]
        hmat = [jnp.where(bd, _dot_tn(xb[i][:, LANES:2 * LANES], bb[i])
                          + _dot_tn(vw[rs, sl].astype(BF16), k_bar[rs, sl].astype(BF16)), 0.0)
                for i, (rs, sl) in enumerate(chains)]

        y_rows = []
        for c in range(n_ch):
            ys = []
            for pr in range(n_pairs):
                i = c * n_pairs + pr
                s0 = s_cur[pr]
                s0b = s0.astype(BF16)
                ys.append(_dot_nt(r_hat[i], s0b) + y0[i])
                s_cur[pr] = s0 * w_tot[c * L:c * L + 1, chains[i][1]] + _dot(s0b, gmat[i]) + hmat[i]
            y_rows.append(jnp.concatenate(ys, axis=1))
        y = jnp.concatenate(y_rows, axis=0)

        inv = 1.0 / RWKV_HEAD
        mean = _seg_sum(y, seg) * inv
        yc = y - mean
        var = _seg_sum(yc * yc, seg) * inv
        yn = yc * lax.rsqrt(var + LN_X_EPS) * lnw_ref[...] + lnb_ref[...]
        bonus = _seg_sum(r[w] * k2 * rk_ref[...], seg) * vw
        o_ref[wv[w], :] = ((yn + bonus) * g[w]).astype(o_ref.dtype)
    for pr in range(n_pairs):
        state[pr] = s_cur[pr]


def _rwkv(p_rw, batch, seq, mu, w0, w_up, a0, a_up, g_up, k_k, k_a, r_k, ln_w, ln_b):
    L = min(RW_STEP_ROWS, seq)
    nc = seq // L
    W = RWKV_WIDTH
    wup = jnp.zeros((LANES, W), F32).at[0:64].set(w_up).astype(BF16)
    aup = jnp.zeros((LANES, W), F32).at[64:128].set(a_up).astype(BF16)
    hid = jnp.arange(MXU_WIDTH) // RWKV_HEAD
    seg = (hid[:, None] == hid[None, :]).astype(BF16)
    wave = min(RW_WAVE_ROWS, L)
    ri = jnp.arange(wave)
    tri = jnp.logical_and(ri[:, None] >= ri[None, :],
                          ri[:, None] // RW_CHUNK == ri[None, :] // RW_CHUNK).astype(BF16)
    vec = lambda: pl.BlockSpec((1, W), lambda b, i: (0, 0))
    mat = lambda s: pl.BlockSpec(s, lambda b, i: (0, 0))
    return pl.pallas_call(
        _rwkv_kernel,
        grid=(batch, nc),
        in_specs=[pl.BlockSpec((L, RW_IN), lambda b, i: (b * nc + i, 0)),
                  mat((1, RW_IN)), vec(), mat((LANES, W)), vec(), mat((LANES, W)),
                  mat((LANES, W)), vec(), vec(), vec(), vec(), vec(), mat((MXU_WIDTH, MXU_WIDTH)),
                  mat((wave, wave))],
        out_specs=pl.BlockSpec((L, W), lambda b, i: (b * nc + i, 0)),
        out_shape=jax.ShapeDtypeStruct((batch * seq, W), BF16),
        scratch_shapes=[pltpu.VMEM((L + CARRY_ROWS, RW_IN), F32),
                        pltpu.VMEM((RWKV_HEADS // 2, LANES, LANES), F32)],
        compiler_params=_cparams(("parallel", "arbitrary")),
        name="rwkv7",
    )(p_rw, mu.reshape(1, RW_IN), w0.reshape(1, W), wup, a0.reshape(1, W), aup,
      g_up.astype(BF16), k_k.reshape(1, W), k_a.reshape(1, W), r_k.reshape(1, W),
      ln_w.reshape(1, W), ln_b.reshape(1, W), seg, tri)


def _swiglu(xb, w1, w3, w2):
    edges = [sum(FF_CHUNKS[:i]) for i in range(len(FF_CHUNKS) + 1)]
    cols = [slice(edges[i], edges[i + 1]) for i in range(len(FF_CHUNKS))]
    hid = (_dot(xb, w1(cols[0])), _dot(xb, w3(cols[0])))
    acc = None
    for i, c in enumerate(cols):
        nxt = (_dot(xb, w1(cols[i + 1])), _dot(xb, w3(cols[i + 1]))) if i + 1 < len(cols) else None
        act = (hid[0] * _sigmoid(hid[0]) * hid[1]).astype(BF16)
        part = _dot(act, w2(c))
        acc = part if acc is None else acc + part
        hid = nxt
    return acc


def _ffn_kernel(n_cast, x_ref, ya_ref, yb_ref, wa_ref, wb_ref, g_ref, w1_ref, w3_ref, w2_ref, *rest):
    src, o_ref, dst = rest[:n_cast], rest[n_cast], rest[n_cast + 1:]
    x1 = x_ref[...] + _dot(ya_ref[...], wa_ref[...]) + _dot(yb_ref[...], wb_ref[...])
    xn = _rms(x1, g_ref[...]).astype(BF16)
    o_ref[...] = x1 + _swiglu(xn, lambda c: w1_ref[:, c], lambda c: w3_ref[:, c], lambda c: w2_ref[c, :])
    for s_ref, d_ref in zip(src, dst):
        d_ref[...] = s_ref[...].astype(BF16)


def _ffn(x, ya, yb, wa, wb, g, w1, w3, w2, to_cast, tm):
    t, d = x.shape
    nt = t // tm
    once = pl.Buffered(1)
    fixed = lambda a: pl.BlockSpec(a.shape, lambda i: (0, 0), pipeline_mode=once)
    slab = lambda a: pl.BlockSpec((a.shape[0] // nt, a.shape[1]), lambda i: (i, 0))
    assert all(a.shape[0] % (nt * 2 * ROW_TILE) == 0 for a in to_cast)
    outs = pl.pallas_call(
        functools.partial(_ffn_kernel, len(to_cast)),
        grid=(nt,),
        in_specs=[pl.BlockSpec((tm, d), lambda i: (i, 0)),
                  pl.BlockSpec((tm, ya.shape[1]), lambda i: (i, 0)),
                  pl.BlockSpec((tm, yb.shape[1]), lambda i: (i, 0)),
                  fixed(wa), fixed(wb),
                  pl.BlockSpec((1, d), lambda i: (0, 0)),
                  fixed(w1), fixed(w3), fixed(w2)] + [slab(a) for a in to_cast],
        out_specs=[pl.BlockSpec((tm, d), lambda i: (i, 0))] + [slab(a) for a in to_cast],
        out_shape=[jax.ShapeDtypeStruct((t, d), F32)]
        + [jax.ShapeDtypeStruct(a.shape, BF16) for a in to_cast],
        compiler_params=_cparams(("parallel",)),
        name="ffn_swiglu",
    )(x, ya, yb, wa, wb, g.reshape(1, d), w1, w3, w2, *to_cast)
    return outs[0], outs[1:]


def _mlstm_kernel(xm_ref, z_ref, gt_ref, cw_ref, cb_ref, wq_ref, wk_ref, wv_ref, gb_ref,
                  nw_ref, tri_ref, o_ref, xbuf, c_s, n_s, m_s):
    ci = pl.program_id(1)
    L = MLSTM_CHUNK
    dh = MLSTM_HEAD
    first = ci == 0
    xm = xm_ref[...].astype(F32)
    _shift_hist(xbuf, xm, first)
    xc = _causal_conv(xbuf, L, cw_ref[...], cb_ref[...])
    xc = xc * _sigmoid(xc)
    gl = gt_ref[...].astype(F32) + gb_ref[...]
    lf = jnp.minimum(gl, 0.0) - jnp.log(1.0 + jnp.exp(-jnp.abs(gl)))
    bcum = _mm_exact_lhs(_dot, tri_ref[...], lf)
    gl_t = gl.T
    bcum_t = bcum.T

    @pl.when(first)
    def _():
        c_s[...] = jnp.zeros(c_s.shape, F32)
        n_s[...] = jnp.zeros(n_s.shape, F32)
        m_s[...] = jnp.zeros(m_s.shape, F32)

    ti = lax.broadcasted_iota(jnp.int32, (L, L), 0)
    si = lax.broadcasted_iota(jnp.int32, (L, L), 1)
    causal = si <= ti
    scale = dh ** -0.5
    hs = range(MLSTM_HEADS)
    sls = [slice(h * dh, (h + 1) * dh) for h in hs]
    xs = [xc[:, sl].astype(BF16) for sl in sls]
    q = [_dot(xs[h], wq_ref[h]) * scale for h in hs]
    k = [_dot(xs[h], wk_ref[h]) for h in hs]
    vb = [_dot(xm[:, sls[h]].astype(BF16), wv_ref[h]).astype(BF16) for h in hs]
    qb = [q[h].astype(BF16) for h in hs]
    kb = [k[h].astype(BF16) for h in hs]
    qk = [_dot_nt(qb[h], kb[h]) for h in hs]
    qc = [_dot(qb[h], c_s[h].astype(BF16)) for h in hs]
    bcol = [bcum[:, MLSTM_HEADS + h:MLSTM_HEADS + h + 1] for h in hs]
    m_prev = [m_s[h:h + 1, 0:1] for h in hs]
    dlog = [jnp.where(causal, bcol[h] - bcum_t[MLSTM_HEADS + h:MLSTM_HEADS + h + 1, :] + gl_t[h:h + 1, :],
                      -jnp.inf) for h in hs]
    inter = [bcol[h] + m_prev[h] for h in hs]
    m_t = [jnp.maximum(inter[h], jnp.max(dlog[h], axis=-1, keepdims=True)) for h in hs]
    s = [qk[h] * jnp.exp(dlog[h] - m_t[h]) for h in hs]
    sv = [_dot(s[h].astype(BF16), vb[h]) for h in hs]
    b_last = [bcol[h][L - 1:L, :] for h in hs]
    wlog = [b_last[h] - bcol[h] + gl[:, h:h + 1] for h in hs]
    m_new = [jnp.maximum(b_last[h] + m_prev[h], jnp.max(wlog[h], axis=0, keepdims=True)) for h in hs]
    kw = [k[h] * jnp.exp(wlog[h] - m_new[h]) for h in hs]
    kv = [_dot_tn(kw[h].astype(BF16), vb[h]) for h in hs]
    sc = [jnp.exp(inter[h] - m_t[h]) for h in hs]
    qn = [jnp.sum(q[h] * n_s[h:h + 1, :], axis=-1, keepdims=True) for h in hs]
    ssum = [jnp.sum(s[h], axis=-1, keepdims=True) for h in hs]
    den = [sc[h] * qn[h] + ssum[h] for h in hs]
    hh = [(sc[h] * qc[h] + sv[h]) / jnp.maximum(jnp.abs(den[h]), jnp.exp(-m_t[h])) for h in hs]
    ms = [jnp.mean(hh[h] * hh[h], axis=-1, keepdims=True) for h in hs]
    dec = [jnp.exp(b_last[h] + m_prev[h] - m_new[h]) for h in hs]
    for h in hs:
        c_s[h] = dec[h] * c_s[h] + kv[h]
        n_s[h:h + 1, :] = dec[h] * n_s[h:h + 1, :] + jnp.sum(kw[h], axis=0, keepdims=True)
        m_s[h:h + 1, :] = jnp.broadcast_to(m_new[h], (1, LANES))
        hn = hh[h] * lax.rsqrt(ms[h] + RMS_EPS)
        gate = _sigmoid(z_ref[:, sls[h]].astype(F32))
        o_ref[:, sls[h]] = (gate * (hn * nw_ref[:, sls[h]])).astype(o_ref.dtype)


def _mlstm(p1, batch, seq, conv_w, conv_b, wq, wk, wv, ig_b, fg_b, mh_w):
    L = MLSTM_CHUNK
    nc = seq // L
    W = MLSTM_WIDTH
    gb = jnp.zeros((1, LANES), F32).at[0, 0:8].set(ig_b).at[0, 8:16].set(fg_b)
    tri = (jnp.arange(L)[:, None] >= jnp.arange(L)[None, :]).astype(BF16)
    hw = lambda: pl.BlockSpec((MLSTM_HEADS, MLSTM_HEAD, MLSTM_HEAD), lambda b, i: (0, 0, 0))
    return pl.pallas_call(
        _mlstm_kernel,
        grid=(batch, nc),
        in_specs=[pl.BlockSpec((L, W), lambda b, i: (b * nc + i, 0)),
                  pl.BlockSpec((L, W), lambda b, i: (b * nc + i, 1)),
                  pl.BlockSpec((L, LANES), lambda b, i: (b * nc + i, 2 * W // LANES)),
                  pl.BlockSpec((CONV_WIDTH, W), lambda b, i: (0, 0)),
                  pl.BlockSpec((1, W), lambda b, i: (0, 0)),
                  hw(), hw(), hw(),
                  pl.BlockSpec((1, LANES), lambda b, i: (0, 0)),
                  pl.BlockSpec((1, W), lambda b, i: (0, 0)),
                  pl.BlockSpec((L, L), lambda b, i: (0, 0))],
        out_specs=pl.BlockSpec((L, W), lambda b, i: (b * nc + i, 0)),
        out_shape=jax.ShapeDtypeStruct((batch * seq, W), BF16),
        scratch_shapes=[pltpu.VMEM((L + CARRY_ROWS, W), F32),
                        pltpu.VMEM((MLSTM_HEADS, MLSTM_HEAD, MLSTM_HEAD), F32),
                        pltpu.VMEM((MLSTM_HEADS, LANES), F32),
                        pltpu.VMEM((MLSTM_HEADS, LANES), F32)],
        compiler_params=_cparams(("parallel", "arbitrary")),
        name="mlstm",
    )(p1, p1, p1, conv_w, conv_b.reshape(1, W), wq.astype(BF16), wk.astype(BF16),
      wv.astype(BF16), gb, mh_w.reshape(1, W), tri)


def _out_router_kernel(x_ref, y_ref, w_ref, g_ref, rt_ref, x3_ref, o_ref):
    tm = x_ref.shape[0]
    sub = tm // ROUTER_SPLIT
    rs = [slice(q * sub, (q + 1) * sub) for q in range(ROUTER_SPLIT)]
    qs = range(ROUTER_SPLIT)
    w = w_ref[...]
    x3 = [x_ref[r, :] + _dot(y_ref[r, :], w) for r in rs]
    for q in qs:
        x3_ref[rs[q], :] = x3[q]
    gw = g_ref[...]
    rt = rt_ref[...]
    logits = [_mm3(_dot, _rms(x3[q], gw), rt) for q in qs]
    lane = lax.broadcasted_iota(jnp.int32, (sub, LANES), 1)
    lg = [jnp.where(lane < N_EXPERTS, logits[q], -jnp.inf) for q in qs]
    v1 = [jnp.max(lg[q], axis=-1, keepdims=True) for q in qs]
    i1 = [jnp.min(jnp.where(lg[q] == v1[q], lane, LANES), axis=-1, keepdims=True) for q in qs]
    lg2 = [jnp.where(lane == i1[q], -jnp.inf, lg[q]) for q in qs]
    v2 = [jnp.max(lg2[q], axis=-1, keepdims=True) for q in qs]
    i2 = [jnp.min(jnp.where(lg2[q] == v2[q], lane, LANES), axis=-1, keepdims=True) for q in qs]
    for q in qs:
        ex = jnp.exp(v2[q] - v1[q])
        g1 = 1.0 / (1.0 + ex)
        g2 = ex / (1.0 + ex)
        out = jnp.where(lane == 0, i1[q].astype(F32), 0.0)
        out = jnp.where(lane == 1, i2[q].astype(F32), out)
        out = jnp.where(lane == 2, g1, out)
        o_ref[rs[q], :] = jnp.where(lane == 3, g2, out)


def _out_router(x, y, w, g, router, tm):
    t, d = x.shape
    return pl.pallas_call(
        _out_router_kernel,
        grid=(t // tm,),
        in_specs=[pl.BlockSpec((tm, d), lambda i: (i, 0)),
                  pl.BlockSpec((tm, y.shape[1]), lambda i: (i, 0)),
                  pl.BlockSpec(w.shape, lambda i: (0, 0)),
                  pl.BlockSpec((1, d), lambda i: (0, 0)),
                  pl.BlockSpec((d, LANES), lambda i: (0, 0))],
        out_specs=[pl.BlockSpec((tm, d), lambda i: (i, 0)),
                   pl.BlockSpec((tm, LANES), lambda i: (i, 0))],
        out_shape=[jax.ShapeDtypeStruct((t, d), F32), jax.ShapeDtypeStruct((t, LANES), F32)],
        compiler_params=_cparams(("parallel",)),
        name="l1_out_router",
    )(x, y, w, g.reshape(1, d), router)


def _prefix_sum(x):
    m, c = x.shape
    group = min(LANES, m)
    xg = x.reshape(m // group, group, c).astype(F32)
    gi = jnp.arange(group)
    local = jnp.einsum("ij,bjc->bic", (gi[:, None] >= gi[None, :]).astype(F32), xg,
                       precision=lax.Precision.HIGHEST)
    bi = jnp.arange(m // group)
    before = jnp.dot((bi[:, None] > bi[None, :]).astype(F32), local[:, -1, :],
                     precision=lax.Precision.HIGHEST)
    return jnp.round(local + before[:, None, :]).astype(jnp.int32).reshape(m, c)


def _route_plan(route, blk):
    e = route[:, 0:TOP_K].astype(jnp.int32).reshape(-1)
    m = e.shape[0]
    onehot = (e[:, None] == jnp.arange(N_EXPERTS, dtype=jnp.int32)[None, :]).astype(jnp.int32)
    csum = _prefix_sum(onehot)
    rank = jnp.sum((csum - onehot) * onehot, axis=1)
    nb = (csum[-1] + blk - 1) // blk
    bend = _prefix_sum(nb[:, None])[:, 0]
    dest = jnp.sum(onehot * (bend - nb)[None, :], axis=1) * blk + rank
    nblk = m // blk + N_EXPERTS
    n_used = bend[-1]
    bidx = jnp.minimum(jnp.arange(nblk, dtype=jnp.int32), n_used - 1)
    blk_e = jnp.minimum(jnp.sum(bidx[:, None] >= bend[None, :], axis=1), N_EXPERTS - 1).astype(jnp.int32)
    gap_start = jnp.concatenate([(bend - nb) * blk + csum[-1], (n_used * blk)[None]])
    gap_len = jnp.concatenate([nb * blk - csum[-1], ((nblk - n_used) * blk)[None]])
    gap_end = _prefix_sum(gap_len[:, None])[:, 0]
    q = jnp.arange(nblk * blk - m, dtype=jnp.int32)
    gi = jnp.sum(q[:, None] >= gap_end[None, :], axis=1)
    pad = gap_start[gi] + q - (gap_end - gap_len)[gi]
    return dest.astype(jnp.int32), pad.astype(jnp.int32), blk_e, n_used.reshape(1).astype(jnp.int32), nblk


def _to_tiles(ref, x):
    n = x.shape[0]
    for s in range(ROW_TILE):
        ref[pl.ds(s, n, stride=ROW_TILE), :] = x[:, s * LANES:(s + 1) * LANES]


def _from_tiles(ref, n):
    return jnp.concatenate([ref[pl.ds(s, n, stride=ROW_TILE), :] for s in range(ROW_TILE)], axis=1)


def _row_copy(src, src_row, dst, dst_row, sem):
    s0 = pl.multiple_of(src_row * ROW_TILE, ROW_TILE)
    d0 = pl.multiple_of(dst_row * ROW_TILE, ROW_TILE)
    return pltpu.make_async_copy(src.at[pl.ds(s0, ROW_TILE), :], dst.at[pl.ds(d0, ROW_TILE), :], sem)


def _wait_rows(hbm, vmem, sem, to_hbm):
    rows = hbm.at[pl.ds(0, vmem.shape[0]), :]
    (pltpu.make_async_copy(vmem, rows, sem) if to_hbm else pltpu.make_async_copy(rows, vmem, sem)).wait()


def _dispatch_kernel(nt, dest_ref, pad_ref, x_ref, g_ref, buf_out, xn_s, zero_s, sem, zsem):
    i = pl.program_id(0)
    tm = x_ref.shape[0]
    slot = i % 2
    n_pad = pad_ref.shape[2]

    def drain(s):
        for _ in range(TOP_K):
            _wait_rows(buf_out, xn_s.at[s], sem.at[s], True)

    @pl.when(i >= 2)
    def _():
        drain(slot)

    zero_s[...] = jnp.zeros(zero_s.shape, F32)

    def zero(q, c):
        _row_copy(zero_s, q, buf_out, pad_ref[0, 0, q], zsem.at[0]).start()
        return c

    lax.fori_loop(0, n_pad, zero, 0, unroll=8)
    _to_tiles(xn_s.at[slot], _rms(x_ref[...], g_ref[...]))

    def start(r, c):
        for k in range(TOP_K):
            _row_copy(xn_s.at[slot], r, buf_out, dest_ref[0, 0, TOP_K * r + k],
                      sem.at[slot]).start(priority=k % DMA_QUEUES)
        return c

    lax.fori_loop(0, tm, start, 0, unroll=8)
    _wait_rows(buf_out, zero_s, zsem.at[0], True)

    @pl.when(i == nt - 1)
    def _():
        drain(slot)
        if nt > 1:
            drain(1 - slot)


def _dispatch(x, g, dest, pad, rows, tm):
    t, d = x.shape
    nt = t // tm
    n_pad = pad.shape[0] // nt
    assert n_pad * nt == pad.shape[0]
    return pl.pallas_call(
        functools.partial(_dispatch_kernel, nt),
        grid=(nt,),
        in_specs=[pl.BlockSpec((1, 1, TOP_K * tm), lambda i: (i, 0, 0), memory_space=pltpu.SMEM),
                  pl.BlockSpec((1, 1, n_pad), lambda i: (i, 0, 0), memory_space=pltpu.SMEM),
                  pl.BlockSpec((tm, d), lambda i: (i, 0)),
                  pl.BlockSpec((1, d), lambda i: (0, 0))],
        out_specs=pl.BlockSpec(memory_space=pl.ANY),
        out_shape=jax.ShapeDtypeStruct((rows * ROW_TILE, LANES), F32),
        scratch_shapes=[pltpu.VMEM((2, tm * ROW_TILE, LANES), F32),
                        pltpu.VMEM((n_pad * ROW_TILE, LANES), F32),
                        pltpu.SemaphoreType.DMA((2,)), pltpu.SemaphoreType.DMA((1,))],
        compiler_params=_cparams(("arbitrary",)),
        name="moe_dispatch",
    )(dest.reshape(nt, 1, TOP_K * tm), pad.reshape(nt, 1, n_pad), x, g.reshape(1, d))


def _expert_kernel(be_ref, nu_ref, x_ref, w1_ref, w3_ref, w2_ref, y_ref):
    del be_ref
    used = pl.program_id(0) < nu_ref[0]

    @pl.when(jnp.logical_not(used))
    def _():
        y_ref[...] = jnp.zeros(y_ref.shape, F32)

    @pl.when(used)
    def _():
        x = _from_tiles(x_ref, x_ref.shape[0] // ROW_TILE).astype(BF16)
        y = _swiglu(x, lambda c: w1_ref[0, :, c], lambda c: w3_ref[0, :, c], lambda c: w2_ref[0, c, :])
        _to_tiles(y_ref, y)


def _experts(buf, blk_e, n_used, w1, w3, w2, blk):
    d, f = w1.shape[1], w1.shape[2]
    once = pl.Buffered(1)
    grid_spec = pltpu.PrefetchScalarGridSpec(
        num_scalar_prefetch=2,
        grid=(buf.shape[0] // (blk * ROW_TILE),),
        in_specs=[pl.BlockSpec((blk * ROW_TILE, LANES), lambda b, be, nu: (b, 0)),
                  pl.BlockSpec((1, d, f), lambda b, be, nu: (be[b], 0, 0), pipeline_mode=once),
                  pl.BlockSpec((1, d, f), lambda b, be, nu: (be[b], 0, 0), pipeline_mode=once),
                  pl.BlockSpec((1, f, d), lambda b, be, nu: (be[b], 0, 0), pipeline_mode=once)],
        out_specs=pl.BlockSpec((blk * ROW_TILE, LANES), lambda b, be, nu: (b, 0)))
    return pl.pallas_call(
        _expert_kernel,
        grid_spec=grid_spec,
        out_shape=jax.ShapeDtypeStruct(buf.shape, F32),
        compiler_params=_cparams(("arbitrary",)),
        name="moe_experts",
    )(blk_e, n_used, buf, w1, w3, w2)


def _combine_kernel(nt, dcur_ref, dnxt_ref, x_ref, rt_ref, fn_ref, y_hbm, o_ref, ya, yb, sem):
    i = pl.program_id(0)
    tm = x_ref.shape[0]
    slot = i % 2

    def gather(dref, s):
        def start(r, c):
            _row_copy(y_hbm, dref[0, 0, TOP_K * r], ya.at[s], r, sem.at[s, 0]).start(priority=0)
            _row_copy(y_hbm, dref[0, 0, TOP_K * r + 1], yb.at[s], r, sem.at[s, 1]).start(
                priority=1 % DMA_QUEUES)
            return c

        lax.fori_loop(0, tm, start, 0, unroll=8)

    @pl.when(i == 0)
    def _():
        gather(dcur_ref, slot)

    @pl.when(i + 1 < nt)
    def _():
        gather(dnxt_ref, 1 - slot)

    _wait_rows(y_hbm, ya.at[slot], sem.at[slot, 0], False)
    _wait_rows(y_hbm, yb.at[slot], sem.at[slot, 1], False)
    rt = rt_ref[...]
    lane = lax.broadcasted_iota(jnp.int32, rt.shape, 1)
    g1 = jnp.sum(jnp.where(lane == 2, rt, 0.0), axis=-1, keepdims=True)
    g2 = jnp.sum(jnp.where(lane == 3, rt, 0.0), axis=-1, keepdims=True)
    moe = g1 * _from_tiles(ya.at[slot], tm) + g2 * _from_tiles(yb.at[slot], tm)
    o_ref[...] = _rms(x_ref[...] + moe, fn_ref[...])


def _combine(x, route, fnorm, y, dest, tm):
    t, d = x.shape
    nt = t // tm
    dest3 = dest.reshape(nt, 1, TOP_K * tm)
    return pl.pallas_call(
        functools.partial(_combine_kernel, nt),
        grid=(nt,),
        in_specs=[pl.BlockSpec((1, 1, TOP_K * tm), lambda i: (i, 0, 0), memory_space=pltpu.SMEM),
                  pl.BlockSpec((1, 1, TOP_K * tm), lambda i: (jnp.minimum(i + 1, nt - 1), 0, 0),
                               memory_space=pltpu.SMEM),
                  pl.BlockSpec((tm, d), lambda i: (i, 0)),
                  pl.BlockSpec((tm, LANES), lambda i: (i, 0)),
                  pl.BlockSpec((1, d), lambda i: (0, 0)),
                  pl.BlockSpec(memory_space=pl.ANY)],
        out_specs=pl.BlockSpec((tm, d), lambda i: (i, 0)),
        out_shape=jax.ShapeDtypeStruct((t, d), F32),
        scratch_shapes=[pltpu.VMEM((2, tm * ROW_TILE, LANES), F32),
                        pltpu.VMEM((2, tm * ROW_TILE, LANES), F32),
                        pltpu.SemaphoreType.DMA((2, 2))],
        compiler_params=_cparams(("arbitrary",)),
        name="moe_combine",
    )(dest3, dest3, x, route, fnorm.reshape(1, d), y)


def _moe(x, route, g, w1, w3, w2, fnorm, tm):
    blk = min(MOE_BLOCK, x.shape[0])
    dest, pad, blk_e, n_used, nblk = _route_plan(route, blk)
    buf = _dispatch(x, g, dest, pad, nblk * blk, tm)
    y = _experts(buf, blk_e, n_used, w1, w3, w2, blk)
    return _combine(x, route, fnorm, y, dest, tm)


def _block_diag(w):
    g, a, b = w.shape
    eye = jnp.eye(g, dtype=w.dtype)
    return (eye[:, None, :, None] * w[:, :, None, :]).reshape(g * a, g * b)


def kernel(x, l0_norm_mix, l0_w_in, l0_conv_w, l0_conv_b, l0_gate_a_w, l0_gate_a_b, l0_gate_x_w, l0_gate_x_b, l0_lru_lambda, l0_shift_mu, l0_w0, l0_w_up, l0_a0, l0_a_up, l0_g_up, l0_k_k, l0_k_a, l0_r_k, l0_ln_x_w, l0_ln_x_b, l0_w_out, l0_norm_ffn, l0_ffn_w1, l0_ffn_w3, l0_ffn_w2, l1_norm_mix, l1_w_in, l1_conv_w, l1_conv_b, l1_wq, l1_wk, l1_wv, l1_ig_b, l1_fg_b, l1_mh_norm_w, l1_w_out, l1_norm_ffn, l1_router, l1_moe_w1, l1_moe_w3, l1_moe_w2, final_norm):
    batch, seq, d = x.shape
    t = batch * seq
    xt = x.reshape(t, d)
    tm = min(512, t)

    w_in0 = l0_w_in.astype(BF16)
    p_lru, p_rw = _norm_mm(xt, l0_norm_mix, w_in0, (2 * LRU_WIDTH, RW_IN), tm, "l0_in")
    y_lru = _lru(p_lru, batch, seq, l0_conv_w, l0_conv_b,
                 _block_diag(l0_gate_a_w).astype(BF16), l0_gate_a_b,
                 _block_diag(l0_gate_x_w).astype(BF16), l0_gate_x_b, l0_lru_lambda,
                 min(512, seq))
    y_rw = _rwkv(p_rw, batch, seq, l0_shift_mu, l0_w0, l0_w_up, l0_a0, l0_a_up, l0_g_up,
                 l0_k_k, l0_k_a, l0_r_k.reshape(-1), l0_ln_x_w, l0_ln_x_b)
    w_out0 = l0_w_out.astype(BF16)
    ne, _, f = l1_moe_w1.shape
    x2, (mw1, mw3, mw2) = _ffn(
        xt, y_lru, y_rw, w_out0[:LRU_WIDTH], w_out0[LRU_WIDTH:], l0_norm_ffn,
        l0_ffn_w1.astype(BF16), l0_ffn_w3.astype(BF16), l0_ffn_w2.astype(BF16),
        [l1_moe_w1.reshape(ne * d, f), l1_moe_w3.reshape(ne * d, f), l1_moe_w2.reshape(ne * f, d)], tm)

    n_in1 = 2 * MLSTM_WIDTH + LANES
    w_in1 = jnp.zeros((d, n_in1), F32).at[:, :l1_w_in.shape[1]].set(l1_w_in).astype(BF16)
    (p1,) = _norm_mm(x2, l1_norm_mix, w_in1, (n_in1,), tm, "l1_in")
    h1 = _mlstm(p1, batch, seq, l1_conv_w, l1_conv_b, l1_wq, l1_wk, l1_wv, l1_ig_b, l1_fg_b,
                l1_mh_norm_w)
    router = jnp.zeros((d, LANES), F32).at[:, :N_EXPERTS].set(l1_router)
    x3, route = _out_router(x2, h1, l1_w_out.astype(BF16), l1_norm_ffn, router, tm)
    out = _moe(x3, route, l1_norm_ffn, mw1.reshape(ne, d, f), mw3.reshape(ne, d, f),
               mw2.reshape(ne, f, d), final_norm, tm)
    return out.reshape(batch, seq, d)
```

```python
import functools

import jax
import jax.numpy as jnp
from jax import lax
from jax.experimental import pallas as pl
from jax.experimental.pallas import tpu as pltpu

F32 = jnp.float32
BF16 = jnp.bfloat16

D_MODEL = 1024
LRU_WIDTH = 512
LRU_BLOCKS = 8
LRU_C = 8.0
LRU_SEGS = 8
CONV_WIDTH = 4
RWKV_HEADS = 8
RWKV_HEAD = 64
RWKV_WIDTH = 512
LN_X_EPS = 1e-5 * RWKV_HEAD
MLSTM_HEADS = 8
MLSTM_HEAD = 128
MLSTM_WIDTH = 1024
MLSTM_CHUNK = 128
D_FF = 2816
N_EXPERTS = 8
TOP_K = 2
DMA_QUEUES = 2
ROUTER_SPLIT = 4
MOE_BLOCK = 512
FF_CHUNKS = (768, 768, 768, 512)
RMS_EPS = 1e-6
RW_IN = 3 * RWKV_WIDTH + 64 + 64 + 128
RW_CHUNK = 64
RW_STEP_ROWS = 512
RW_WAVE_ROWS = 256
LANES = 128
MXU_WIDTH = 256
CARRY_ROWS = 8
ROW_TILE = 8
VMEM_LIMIT = 56 * 1024 * 1024


def _cparams(sem):
    return pltpu.CompilerParams(dimension_semantics=sem, vmem_limit_bytes=VMEM_LIMIT)


def _rms(x, w):
    return x * lax.rsqrt(jnp.mean(x * x, axis=-1, keepdims=True) + RMS_EPS) * w


def _sigmoid(x):
    return 1.0 / (1.0 + jnp.exp(-x))


def _softplus(x):
    return jnp.maximum(x, 0.0) + jnp.log(1.0 + jnp.exp(-jnp.abs(x)))


def _dot(a, b):
    return jnp.dot(a, b, preferred_element_type=F32)


def _dot_nt(a, b):
    return lax.dot_general(a, b, (((1,), (1,)), ((), ())), preferred_element_type=F32)


def _dot_tn(a, b):
    return lax.dot_general(a, b, (((0,), (0,)), ((), ())), preferred_element_type=F32)


def _split2(x):
    hi = x.astype(BF16)
    lo = (x - hi.astype(F32)).astype(BF16)
    return hi, lo


def _split3(x):
    hi = x.astype(BF16)
    r = x - hi.astype(F32)
    mid = r.astype(BF16)
    lo = (r - mid.astype(F32)).astype(BF16)
    return hi, mid, lo


def _mm3(fn, a, b):
    ah, al = _split2(a)
    bh, bl = _split2(b)
    return fn(ah, bh) + fn(al, bh) + fn(ah, bl)


def _mm_exact_lhs(fn, a_bf16, b):
    h, m, l = _split3(b)
    return fn(a_bf16, h) + fn(a_bf16, m) + fn(a_bf16, l)


def _seg_sum(x, seg):
    s = seg.shape[0]
    outs = []
    for c in range(x.shape[1] // s):
        hi, lo = _split2(x[:, c * s:(c + 1) * s])
        outs.append(_dot(hi, seg) + _dot(lo, seg))
    return jnp.concatenate(outs, axis=1)


def _shift_hist(buf_ref, x, first):
    n = x.shape[0]

    @pl.when(first)
    def _():
        buf_ref[0:CARRY_ROWS, :] = jnp.zeros((CARRY_ROWS, x.shape[1]), F32)

    @pl.when(jnp.logical_not(first))
    def _():
        buf_ref[0:CARRY_ROWS, :] = buf_ref[n:n + CARRY_ROWS, :]

    buf_ref[CARRY_ROWS:CARRY_ROWS + n, :] = x


def _causal_conv(buf_ref, n, w, b):
    acc = b
    for j in range(CONV_WIDTH):
        off = CARRY_ROWS - (CONV_WIDTH - 1) + j
        acc = acc + w[j:j + 1, :] * buf_ref[off:off + n, :]
    return acc


def _norm_mm_kernel(x_ref, g_ref, w_ref, *o_refs):
    xn = _rms(x_ref[...], g_ref[...]).astype(BF16)
    c0 = 0
    for o_ref in o_refs:
        c1 = c0 + o_ref.shape[1]
        o_ref[...] = _dot(xn, w_ref[:, c0:c1]).astype(o_ref.dtype)
        c0 = c1


def _norm_mm(x, g, w, widths, tm, name):
    t, d = x.shape
    n = w.shape[1]
    assert sum(widths) == n
    return pl.pallas_call(
        _norm_mm_kernel,
        grid=(t // tm,),
        in_specs=[pl.BlockSpec((tm, d), lambda i: (i, 0)),
                  pl.BlockSpec((1, d), lambda i: (0, 0)),
                  pl.BlockSpec((d, n), lambda i: (0, 0))],
        out_specs=[pl.BlockSpec((tm, c), lambda i: (i, 0)) for c in widths],
        out_shape=[jax.ShapeDtypeStruct((t, c), BF16) for c in widths],
        compiler_params=_cparams(("parallel",)),
        name=name,
    )(x, g.reshape(1, d), w)


def _lru_kernel(p_ref, cw_ref, cb_ref, wa_ref, ba_ref, wx_ref, bx_ref, lam_ref, o_ref,
                xbuf, abuf, bbuf, hbuf, hcar):
    i = pl.program_id(1)
    n = p_ref.shape[0]
    first = i == 0
    _shift_hist(xbuf, p_ref[:, 0:LRU_WIDTH].astype(F32), first)
    xc = _causal_conv(xbuf, n, cw_ref[...], cb_ref[...])
    xcb = xc.astype(BF16)
    r = _sigmoid(_dot(xcb, wa_ref[...]) + ba_ref[...])
    ig = _sigmoid(_dot(xcb, wx_ref[...]) + bx_ref[...])
    log_a = (-LRU_C) * r * _softplus(-lam_ref[...])
    a = jnp.exp(log_a)
    mult = jnp.sqrt(1.0 - a * a)
    row = lax.broadcasted_iota(jnp.int32, (n, 1), 0)
    mult = jnp.where(jnp.logical_and(first, row == 0), 1.0, mult)
    bvals = mult * ig * xc
    n_slab = LRU_WIDTH // LANES
    slabs = [slice(c * LANES, (c + 1) * LANES) for c in range(n_slab)]
    seg = n // LRU_SEGS
    pitch = _lru_pitch(n)
    for c, cs in enumerate(slabs):
        for s in range(LRU_SEGS):
            abuf[c, s * pitch:s * pitch + seg, :] = a[s * seg:(s + 1) * seg, cs]
            bbuf[c, s * pitch:s * pitch + seg, :] = bvals[s * seg:(s + 1) * seg, cs]

    @pl.when(first)
    def _():
        hcar[...] = jnp.zeros(hcar.shape, F32)

    def step(j, carry):
        rows = pl.ds(j, LRU_SEGS, stride=pitch)
        out = []
        for c in range(n_slab):
            h, acc = carry[c]
            a_j = abuf[c, rows, :]
            h = a_j * h + bbuf[c, rows, :]
            acc = acc * a_j
            hbuf[c, rows, :] = h
            abuf[c, rows, :] = acc
            out.append((h, acc))
        return tuple(out)

    init = tuple((jnp.zeros((LRU_SEGS, LANES), F32), jnp.ones((LRU_SEGS, LANES), F32))
                 for _ in range(n_slab))
    ends = lax.fori_loop(0, seg, step, init, unroll=8)
    cols = []
    for c, cs in enumerate(slabs):
        h_end, a_end = ends[c]
        h0 = hcar[0:1, cs]
        parts = []
        for s in range(LRU_SEGS):
            rows = slice(s * pitch, s * pitch + seg)
            parts.append(hbuf[c, rows, :] + abuf[c, rows, :] * h0)
            h0 = h_end[s:s + 1, :] + a_end[s:s + 1, :] * h0
        hcar[0:1, cs] = h0
        cols.append(jnp.concatenate(parts, axis=0))
    gate = p_ref[:, LRU_WIDTH:2 * LRU_WIDTH].astype(F32)
    gelu = 0.5 * gate * (1.0 + jnp.tanh(0.7978845608028654 * (gate + 0.044715 * gate * gate * gate)))
    o_ref[...] = (jnp.concatenate(cols, axis=1) * gelu).astype(o_ref.dtype)


def _lru_pitch(n):
    tiles = n // LRU_SEGS // CARRY_ROWS
    return (tiles + 1 - tiles % 2) * CARRY_ROWS


def _lru(p_lru, batch, seq, conv_w, conv_b, wa, ba, wx, bx, lam, tb):
    nb = seq // tb
    c = LRU_WIDTH
    vec = lambda: pl.BlockSpec((1, c), lambda b, i: (0, 0))
    return pl.pallas_call(
        _lru_kernel,
        grid=(batch, nb),
        in_specs=[pl.BlockSpec((tb, 2 * c), lambda b, i: (b * nb + i, 0)),
                  pl.BlockSpec((CONV_WIDTH, c), lambda b, i: (0, 0)), vec(),
                  pl.BlockSpec((c, c), lambda b, i: (0, 0)), vec(),
                  pl.BlockSpec((c, c), lambda b, i: (0, 0)), vec(), vec()],
        out_specs=pl.BlockSpec((tb, c), lambda b, i: (b * nb + i, 0)),
        out_shape=jax.ShapeDtypeStruct((batch * seq, c), BF16),
        scratch_shapes=[pltpu.VMEM((tb + CARRY_ROWS, c), F32)]
        + [pltpu.VMEM((c // LANES, LRU_SEGS * _lru_pitch(tb), LANES), F32)] * 3
        + [pltpu.VMEM((CARRY_ROWS, c), F32)],
        compiler_params=_cparams(("parallel", "arbitrary")),
        name="rg_lru",
    )(p_lru, conv_w, conv_b.reshape(1, c), wa, ba.reshape(1, c), wx, bx.reshape(1, c),
      lam.reshape(1, c))


def _blk(x, masks):
    xb = x.astype(BF16)
    return jnp.concatenate([jnp.where(mk, xb, jnp.zeros_like(xb)) for mk in masks], axis=0)


def _rwkv_kernel(p_ref, mu_ref, w0_ref, wup_ref, a0_ref, aup_ref, gup_ref, kk_ref, ka_ref,
                 rk_ref, lnw_ref, lnb_ref, seg_ref, tri_ref, o_ref, pbuf, state):
    L = RW_CHUNK
    W = RWKV_WIDTH
    rows = p_ref.shape[0]
    wave = tri_ref.shape[0]
    first = pl.program_id(1) == 0
    p = p_ref[...].astype(F32)
    _shift_hist(pbuf, p, first)
    prev = pbuf[CARRY_ROWS - 1:CARRY_ROWS - 1 + rows, :]
    ps = p + mu_ref[...] * (prev - p)
    seg = seg_ref[...]
    tri = tri_ref[...]

    @pl.when(first)
    def _():
        state[...] = jnp.zeros(state.shape, F32)

    lane = lax.broadcasted_iota(jnp.int32, (1, LANES), 1)
    m1 = [lane < RWKV_HEAD, lane >= RWKV_HEAD]
    m2 = [jnp.concatenate([mk, mk], axis=1) for mk in m1]
    ti = lax.broadcasted_iota(jnp.int32, (L, LANES), 0)
    si = lax.broadcasted_iota(jnp.int32, (L, LANES), 1) % RWKV_HEAD
    strict = si < ti
    incl = si <= ti
    bi = lax.broadcasted_iota(jnp.int32, (LANES, LANES), 0) // RWKV_HEAD
    bj = lax.broadcasted_iota(jnp.int32, (LANES, LANES), 1) // RWKV_HEAD
    bd = bi == bj

    def tril(mask, s):
        return jnp.where(mask, s, 0.0).astype(BF16)

    wv = [slice(w * wave, (w + 1) * wave) for w in range(rows // wave)]
    ws = range(len(wv))
    r = [ps[rw, 0:W] for rw in wv]
    k = [ps[rw, W:2 * W] for rw in wv]
    v = [ps[rw, 2 * W:3 * W] for rw in wv]
    x2 = [ps[rw, 3 * W:3 * W + LANES] for rw in wv]
    xg = [ps[rw, 3 * W + LANES:3 * W + 2 * LANES] for rw in wv]
    wl = [w0_ref[...] + _dot(jnp.tanh(x2[w]).astype(BF16), wup_ref[...]) for w in ws]
    a = [_sigmoid(a0_ref[...] + _dot(x2[w].astype(BF16), aup_ref[...])) for w in ws]
    g = [_dot(_sigmoid(xg[w]).astype(BF16), gup_ref[...]) for w in ws]
    kk = [k[w] * kk_ref[...] for w in ws]
    kk = [kk[w] / jnp.maximum(jnp.sqrt(_seg_sum(kk[w] * kk[w], seg)), 1e-12) for w in ws]
    lw = [-jnp.exp(-_softplus(-wl[w]) - 0.5) for w in ws]
    cw = [_mm_exact_lhs(_dot, tri, lw[w]) for w in ws]

    n_pairs = RWKV_HEADS // 2
    n_ch = wave // L
    s_cur = [state[pr] for pr in range(n_pairs)]
    for w in ws:
        cw_end = jnp.concatenate(
            [jnp.broadcast_to(cw[w][c * L + L - 1:c * L + L, :], (L, W)) for c in range(n_ch)], axis=0)
        w_in = jnp.exp(cw[w])
        w_inv = jnp.exp(-cw[w])
        w_end = jnp.exp(cw_end - cw[w])
        k2 = k[w] * (1.0 + (a[w] - 1.0) * ka_ref[...])
        kka = kk[w] * a[w]
        a_t = -kk[w] * jnp.exp(cw[w] - lw[w])
        b_t = kka * w_inv
        k_t = k2 * w_inv
        r_t = r[w] * w_in
        b_bar = kka * w_end
        k_bar = k2 * w_end
        w_tot = jnp.exp(cw_end)
        vw = v[w]

        chains = [(slice(c * L, (c + 1) * L), slice(pr * LANES, (pr + 1) * LANES))
                  for c in range(n_ch) for pr in range(n_pairs)]
        idx = range(len(chains))
        a_c = [a_t[rs, sl] for rs, sl in chains]
        r_c = [r_t[rs, sl] for rs, sl in chains]
        ar = [jnp.concatenate([a_c[i], r_c[i]], axis=0).astype(BF16) for i in idx]
        sb = [_dot_nt(ar[i], _blk(b_t[rs, sl], m1)) for i, (rs, sl) in enumerate(chains)]
        sk = [_dot_nt(ar[i], _blk(k_t[rs, sl], m1)) for i, (rs, sl) in enumerate(chains)]
        vblk = [_blk(vw[rs, sl], m1) for rs, sl in chains]
        rhs = [jnp.concatenate([a_c[i], _dot(tril(strict, sk[i][0:L]), vblk[i])], axis=1) for i in idx]
        pw = [tril(strict, sb[i][0:L]) for i in idx]
        tinv = [jnp.where(si == ti, 1.0, 0.0) + pw[i].astype(F32) for i in idx]
        pw = [_dot(pw[i], _blk(pw[i], m1)).astype(BF16) for i in idx]
        for _ in range(4):
            both = [_dot(pw[i], jnp.concatenate([_blk(pw[i], m1), _blk(tinv[i], m1)], axis=1)) for i in idx]
            tinv = [tinv[i] + both[i][:, LANES:2 * LANES] for i in idx]
            pw = [both[i][:, 0:LANES].astype(BF16) for i in idx]
        tinv = [tinv[i] + _dot(pw[i], _blk(tinv[i], m1)) for i in idx]
        x = [_dot(tinv[i].astype(BF16), _blk(rhs[i], m2)) for i in idx]
        corr = [_dot(tril(incl, sb[i][L:2 * L]), _blk(x[i], m2)) for i in idx]
        r_hat = [(r_c[i] + corr[i][:, 0:LANES]).astype(BF16) for i in idx]
        y0 = [corr[i][:, LANES:2 * LANES] + _dot(tril(incl, sk[i][L:2 * L]), vblk[i]) for i in idx]
        xb = [x[i].astype(BF16) for i in idx]
        bb = [b_bar[rs, sl].astype(BF16) for rs, sl in chains]
        gmat = [jnp.where(bd, _dot_tn(xb[i][:, 0:LANES], bb[i]), 0.0).astype(BF16) for i in idx]
        hmat = [jnp.where(bd, _dot_tn(xb[i][:, LANES:2 * LANES], bb[i])
                          + _dot_tn(vw[rs, sl].astype(BF16), k_bar[rs, sl].astype(BF16)), 0.0)
                for i, (rs, sl) in enumerate(chains)]

        y_rows = []
        for c in range(n_ch):
            ys = []
            for pr in range(n_pairs):
                i = c * n_pairs + pr
                s0 = s_cur[pr]
                s0b = s0.astype(BF16)
                ys.append(_dot_nt(r_hat[i], s0b) + y0[i])
                s_cur[pr] = s0 * w_tot[c * L:c * L + 1, chains[i][1]] + _dot(s0b, gmat[i]) + hmat[i]
            y_rows.append(jnp.concatenate(ys, axis=1))
        y = jnp.concatenate(y_rows, axis=0)

        inv = 1.0 / RWKV_HEAD
        mean = _seg_sum(y, seg) * inv
        yc = y - mean
        var = _seg_sum(yc * yc, seg) * inv
        yn = yc * lax.rsqrt(var + LN_X_EPS) * lnw_ref[...] + lnb_ref[...]
        bonus = _seg_sum(r[w] * k2 * rk_ref[...], seg) * vw
        o_ref[wv[w], :] = ((yn + bonus) * g[w]).astype(o_ref.dtype)
    for pr in range(n_pairs):
        state[pr] = s_cur[pr]


def _rwkv(p_rw, batch, seq, mu, w0, w_up, a0, a_up, g_up, k_k, k_a, r_k, ln_w, ln_b):
    L = min(RW_STEP_ROWS, seq)
    nc = seq // L
    W = RWKV_WIDTH
    wup = jnp.zeros((LANES, W), F32).at[0:64].set(w_up).astype(BF16)
    aup = jnp.zeros((LANES, W), F32).at[64:128].set(a_up).astype(BF16)
    hid = jnp.arange(MXU_WIDTH) // RWKV_HEAD
    seg = (hid[:, None] == hid[None, :]).astype(BF16)
    wave = min(RW_WAVE_ROWS, L)
    ri = jnp.arange(wave)
    tri = jnp.logical_and(ri[:, None] >= ri[None, :],
                          ri[:, None] // RW_CHUNK == ri[None, :] // RW_CHUNK).astype(BF16)
    vec = lambda: pl.BlockSpec((1, W), lambda b, i: (0, 0))
    mat = lambda s: pl.BlockSpec(s, lambda b, i: (0, 0))
    return pl.pallas_call(
        _rwkv_kernel,
        grid=(batch, nc),
        in_specs=[pl.BlockSpec((L, RW_IN), lambda b, i: (b * nc + i, 0)),
                  mat((1, RW_IN)), vec(), mat((LANES, W)), vec(), mat((LANES, W)),
                  mat((LANES, W)), vec(), vec(), vec(), vec(), vec(), mat((MXU_WIDTH, MXU_WIDTH)),
                  mat((wave, wave))],
        out_specs=pl.BlockSpec((L, W), lambda b, i: (b * nc + i, 0)),
        out_shape=jax.ShapeDtypeStruct((batch * seq, W), BF16),
        scratch_shapes=[pltpu.VMEM((L + CARRY_ROWS, RW_IN), F32),
                        pltpu.VMEM((RWKV_HEADS // 2, LANES, LANES), F32)],
        compiler_params=_cparams(("parallel", "arbitrary")),
        name="rwkv7",
    )(p_rw, mu.reshape(1, RW_IN), w0.reshape(1, W), wup, a0.reshape(1, W), aup,
      g_up.astype(BF16), k_k.reshape(1, W), k_a.reshape(1, W), r_k.reshape(1, W),
      ln_w.reshape(1, W), ln_b.reshape(1, W), seg, tri)


def _swiglu(xb, w1, w3, w2):
    edges = [sum(FF_CHUNKS[:i]) for i in range(len(FF_CHUNKS) + 1)]
    cols = [slice(edges[i], edges[i + 1]) for i in range(len(FF_CHUNKS))]
    hid = (_dot(xb, w1(cols[0])), _dot(xb, w3(cols[0])))
    acc = None
    for i, c in enumerate(cols):
        nxt = (_dot(xb, w1(cols[i + 1])), _dot(xb, w3(cols[i + 1]))) if i + 1 < len(cols) else None
        act = (hid[0] * _sigmoid(hid[0]) * hid[1]).astype(BF16)
        part = _dot(act, w2(c))
        acc = part if acc is None else acc + part
        hid = nxt
    return acc


def _ffn_kernel(n_cast, x_ref, ya_ref, yb_ref, wa_ref, wb_ref, g_ref, w1_ref, w3_ref, w2_ref, *rest):
    src, o_ref, dst = rest[:n_cast], rest[n_cast], rest[n_cast + 1:]
    x1 = x_ref[...] + _dot(ya_ref[...], wa_ref[...]) + _dot(yb_ref[...], wb_ref[...])
    xn = _rms(x1, g_ref[...]).astype(BF16)
    o_ref[...] = x1 + _swiglu(xn, lambda c: w1_ref[:, c], lambda c: w3_ref[:, c], lambda c: w2_ref[c, :])
    for s_ref, d_ref in zip(src, dst):
        d_ref[...] = s_ref[...].astype(BF16)


def _ffn(x, ya, yb, wa, wb, g, w1, w3, w2, to_cast, tm):
    t, d = x.shape
    nt = t // tm
    once = pl.Buffered(1)
    fixed = lambda a: pl.BlockSpec(a.shape, lambda i: (0, 0), pipeline_mode=once)
    slab = lambda a: pl.BlockSpec((a.shape[0] // nt, a.shape[1]), lambda i: (i, 0))
    assert all(a.shape[0] % (nt * 2 * ROW_TILE) == 0 for a in to_cast)
    outs = pl.pallas_call(
        functools.partial(_ffn_kernel, len(to_cast)),
        grid=(nt,),
        in_specs=[pl.BlockSpec((tm, d), lambda i: (i, 0)),
                  pl.BlockSpec((tm, ya.shape[1]), lambda i: (i, 0)),
                  pl.BlockSpec((tm, yb.shape[1]), lambda i: (i, 0)),
                  fixed(wa), fixed(wb),
                  pl.BlockSpec((1, d), lambda i: (0, 0)),
                  fixed(w1), fixed(w3), fixed(w2)] + [slab(a) for a in to_cast],
        out_specs=[pl.BlockSpec((tm, d), lambda i: (i, 0))] + [slab(a) for a in to_cast],
        out_shape=[jax.ShapeDtypeStruct((t, d), F32)]
        + [jax.ShapeDtypeStruct(a.shape, BF16) for a in to_cast],
        compiler_params=_cparams(("parallel",)),
        name="ffn_swiglu",
    )(x, ya, yb, wa, wb, g.reshape(1, d), w1, w3, w2, *to_cast)
    return outs[0], outs[1:]


def _mlstm_kernel(xm_ref, z_ref, gt_ref, cw_ref, cb_ref, wq_ref, wk_ref, wv_ref, gb_ref,
                  nw_ref, tri_ref, o_ref, xbuf, c_s, n_s, m_s):
    ci = pl.program_id(1)
    L = MLSTM_CHUNK
    dh = MLSTM_HEAD
    first = ci == 0
    xm = xm_ref[...].astype(F32)
    _shift_hist(xbuf, xm, first)
    xc = _causal_conv(xbuf, L, cw_ref[...], cb_ref[...])
    xc = xc * _sigmoid(xc)
    gl = gt_ref[...].astype(F32) + gb_ref[...]
    lf = jnp.minimum(gl, 0.0) - jnp.log(1.0 + jnp.exp(-jnp.abs(gl)))
    bcum = _mm_exact_lhs(_dot, tri_ref[...], lf)
    gl_t = gl.T
    bcum_t = bcum.T

    @pl.when(first)
    def _():
        c_s[...] = jnp.zeros(c_s.shape, F32)
        n_s[...] = jnp.zeros(n_s.shape, F32)
        m_s[...] = jnp.zeros(m_s.shape, F32)

    ti = lax.broadcasted_iota(jnp.int32, (L, L), 0)
    si = lax.broadcasted_iota(jnp.int32, (L, L), 1)
    causal = si <= ti
    scale = dh ** -0.5
    hs = range(MLSTM_HEADS)
    sls = [slice(h * dh, (h + 1) * dh) for h in hs]
    xs = [xc[:, sl].astype(BF16) for sl in sls]
    q = [_dot(xs[h], wq_ref[h]) * scale for h in hs]
    k = [_dot(xs[h], wk_ref[h]) for h in hs]
    vb = [_dot(xm[:, sls[h]].astype(BF16), wv_ref[h]).astype(BF16) for h in hs]
    qb = [q[h].astype(BF16) for h in hs]
    kb = [k[h].astype(BF16) for h in hs]
    qk = [_dot_nt(qb[h], kb[h]) for h in hs]
    qc = [_dot(qb[h], c_s[h].astype(BF16)) for h in hs]
    bcol = [bcum[:, MLSTM_HEADS + h:MLSTM_HEADS + h + 1] for h in hs]
    m_prev = [m_s[h:h + 1, 0:1] for h in hs]
    dlog = [jnp.where(causal, bcol[h] - bcum_t[MLSTM_HEADS + h:MLSTM_HEADS + h + 1, :] + gl_t[h:h + 1, :],
                      -jnp.inf) for h in hs]
    inter = [bcol[h] + m_prev[h] for h in hs]
    m_t = [jnp.maximum(inter[h], jnp.max(dlog[h], axis=-1, keepdims=True)) for h in hs]
    s = [qk[h] * jnp.exp(dlog[h] - m_t[h]) for h in hs]
    sv = [_dot(s[h].astype(BF16), vb[h]) for h in hs]
    b_last = [bcol[h][L - 1:L, :] for h in hs]
    wlog = [b_last[h] - bcol[h] + gl[:, h:h + 1] for h in hs]
    m_new = [jnp.maximum(b_last[h] + m_prev[h], jnp.max(wlog[h], axis=0, keepdims=True)) for h in hs]
    kw = [k[h] * jnp.exp(wlog[h] - m_new[h]) for h in hs]
    kv = [_dot_tn(kw[h].astype(BF16), vb[h]) for h in hs]
    sc = [jnp.exp(inter[h] - m_t[h]) for h in hs]
    qn = [jnp.sum(q[h] * n_s[h:h + 1, :], axis=-1, keepdims=True) for h in hs]
    ssum = [jnp.sum(s[h], axis=-1, keepdims=True) for h in hs]
    den = [sc[h] * qn[h] + ssum[h] for h in hs]
    hh = [(sc[h] * qc[h] + sv[h]) / jnp.maximum(jnp.abs(den[h]), jnp.exp(-m_t[h])) for h in hs]
    ms = [jnp.mean(hh[h] * hh[h], axis=-1, keepdims=True) for h in hs]
    dec = [jnp.exp(b_last[h] + m_prev[h] - m_new[h]) for h in hs]
    for h in hs:
        c_s[h] = dec[h] * c_s[h] + kv[h]
        n_s[h:h + 1, :] = dec[h] * n_s[h:h + 1, :] + jnp.sum(kw[h], axis=0, keepdims=True)
        m_s[h:h + 1, :] = jnp.broadcast_to(m_new[h], (1, LANES))
        hn = hh[h] * lax.rsqrt(ms[h] + RMS_EPS)
        gate = _sigmoid(z_ref[:, sls[h]].astype(F32))
        o_ref[:, sls[h]] = (gate * (hn * nw_ref[:, sls[h]])).astype(o_ref.dtype)


def _mlstm(p1, batch, seq, conv_w, conv_b, wq, wk, wv, ig_b, fg_b, mh_w):
    L = MLSTM_CHUNK
    nc = seq // L
    W = MLSTM_WIDTH
    gb = jnp.zeros((1, LANES), F32).at[0, 0:8].set(ig_b).at[0, 8:16].set(fg_b)
    tri = (jnp.arange(L)[:, None] >= jnp.arange(L)[None, :]).astype(BF16)
    hw = lambda: pl.BlockSpec((MLSTM_HEADS, MLSTM_HEAD, MLSTM_HEAD), lambda b, i: (0, 0, 0))
    return pl.pallas_call(
        _mlstm_kernel,
        grid=(batch, nc),
        in_specs=[pl.BlockSpec((L, W), lambda b, i: (b * nc + i, 0)),
                  pl.BlockSpec((L, W), lambda b, i: (b * nc + i, 1)),
                  pl.BlockSpec((L, LANES), lambda b, i: (b * nc + i, 2 * W // LANES)),
                  pl.BlockSpec((CONV_WIDTH, W), lambda b, i: (0, 0)),
                  pl.BlockSpec((1, W), lambda b, i: (0, 0)),
                  hw(), hw(), hw(),
                  pl.BlockSpec((1, LANES), lambda b, i: (0, 0)),
                  pl.BlockSpec((1, W), lambda b, i: (0, 0)),
                  pl.BlockSpec((L, L), lambda b, i: (0, 0))],
        out_specs=pl.BlockSpec((L, W), lambda b, i: (b * nc + i, 0)),
        out_shape=jax.ShapeDtypeStruct((batch * seq, W), BF16),
        scratch_shapes=[pltpu.VMEM((L + CARRY_ROWS, W), F32),
                        pltpu.VMEM((MLSTM_HEADS, MLSTM_HEAD, MLSTM_HEAD), F32),
                        pltpu.VMEM((MLSTM_HEADS, LANES), F32),
                        pltpu.VMEM((MLSTM_HEADS, LANES), F32)],
        compiler_params=_cparams(("parallel", "arbitrary")),
        name="mlstm",
    )(p1, p1, p1, conv_w, conv_b.reshape(1, W), wq.astype(BF16), wk.astype(BF16),
      wv.astype(BF16), gb, mh_w.reshape(1, W), tri)


def _out_router_kernel(x_ref, y_ref, w_ref, g_ref, rt_ref, x3_ref, o_ref):
    tm = x_ref.shape[0]
    sub = tm // ROUTER_SPLIT
    rs = [slice(q * sub, (q + 1) * sub) for q in range(ROUTER_SPLIT)]
    qs = range(ROUTER_SPLIT)
    w = w_ref[...]
    x3 = [x_ref[r, :] + _dot(y_ref[r, :], w) for r in rs]
    for q in qs:
        x3_ref[rs[q], :] = x3[q]
    gw = g_ref[...]
    rt = rt_ref[...]
    logits = [_mm3(_dot, _rms(x3[q], gw), rt) for q in qs]
    lane = lax.broadcasted_iota(jnp.int32, (sub, LANES), 1)
    lg = [jnp.where(lane < N_EXPERTS, logits[q], -jnp.inf) for q in qs]
    v1 = [jnp.max(lg[q], axis=-1, keepdims=True) for q in qs]
    i1 = [jnp.min(jnp.where(lg[q] == v1[q], lane, LANES), axis=-1, keepdims=True) for q in qs]
    lg2 = [jnp.where(lane == i1[q], -jnp.inf, lg[q]) for q in qs]
    v2 = [jnp.max(lg2[q], axis=-1, keepdims=True) for q in qs]
    i2 = [jnp.min(jnp.where(lg2[q] == v2[q], lane, LANES), axis=-1, keepdims=True) for q in qs]
    for q in qs:
        ex = jnp.exp(v2[q] - v1[q])
        g1 = 1.0 / (1.0 + ex)
        g2 = ex / (1.0 + ex)
        out = jnp.where(lane == 0, i1[q].astype(F32), 0.0)
        out = jnp.where(lane == 1, i2[q].astype(F32), out)
        out = jnp.where(lane == 2, g1, out)
        o_ref[rs[q], :] = jnp.where(lane == 3, g2, out)


def _out_router(x, y, w, g, router, tm):
    t, d = x.shape
    return pl.pallas_call(
        _out_router_kernel,
        grid=(t // tm,),
        in_specs=[pl.BlockSpec((tm, d), lambda i: (i, 0)),
                  pl.BlockSpec((tm, y.shape[1]), lambda i: (i, 0)),
                  pl.BlockSpec(w.shape, lambda i: (0, 0)),
                  pl.BlockSpec((1, d), lambda i: (0, 0)),
                  pl.BlockSpec((d, LANES), lambda i: (0, 0))],
        out_specs=[pl.BlockSpec((tm, d), lambda i: (i, 0)),
                   pl.BlockSpec((tm, LANES), lambda i: (i, 0))],
        out_shape=[jax.ShapeDtypeStruct((t, d), F32), jax.ShapeDtypeStruct((t, LANES), F32)],
        compiler_params=_cparams(("parallel",)),
        name="l1_out_router",
    )(x, y, w, g.reshape(1, d), router)


def _prefix_sum(x):
    m, c = x.shape
    group = min(LANES, m)
    xg = x.reshape(m // group, group, c).astype(F32)
    gi = jnp.arange(group)
    local = jnp.einsum("ij,bjc->bic", (gi[:, None] >= gi[None, :]).astype(F32), xg,
                       precision=lax.Precision.HIGHEST)
    bi = jnp.arange(m // group)
    before = jnp.dot((bi[:, None] > bi[None, :]).astype(F32), local[:, -1, :],
                     precision=lax.Precision.HIGHEST)
    return jnp.round(local + before[:, None, :]).astype(jnp.int32).reshape(m, c)


def _route_plan(route, blk):
    e = route[:, 0:TOP_K].astype(jnp.int32).reshape(-1)
    m = e.shape[0]
    onehot = (e[:, None] == jnp.arange(N_EXPERTS, dtype=jnp.int32)[None, :]).astype(jnp.int32)
    csum = _prefix_sum(onehot)
    rank = jnp.sum((csum - onehot) * onehot, axis=1)
    nb = (csum[-1] + blk - 1) // blk
    bend = _prefix_sum(nb[:, None])[:, 0]
    dest = jnp.sum(onehot * (bend - nb)[None, :], axis=1) * blk + rank
    nblk = m // blk + N_EXPERTS
    n_used = bend[-1]
    bidx = jnp.minimum(jnp.arange(nblk, dtype=jnp.int32), n_used - 1)
    blk_e = jnp.minimum(jnp.sum(bidx[:, None] >= bend[None, :], axis=1), N_EXPERTS - 1).astype(jnp.int32)
    gap_start = jnp.concatenate([(bend - nb) * blk + csum[-1], (n_used * blk)[None]])
    gap_len = jnp.concatenate([nb * blk - csum[-1], ((nblk - n_used) * blk)[None]])
    gap_end = _prefix_sum(gap_len[:, None])[:, 0]
    q = jnp.arange(nblk * blk - m, dtype=jnp.int32)
    gi = jnp.sum(q[:, None] >= gap_end[None, :], axis=1)
    pad = gap_start[gi] + q - (gap_end - gap_len)[gi]
    return dest.astype(jnp.int32), pad.astype(jnp.int32), blk_e, n_used.reshape(1).astype(jnp.int32), nblk


def _to_tiles(ref, x):
    n = x.shape[0]
    for s in range(ROW_TILE):
        ref[pl.ds(s, n, stride=ROW_TILE), :] = x[:, s * LANES:(s + 1) * LANES]


def _from_tiles(ref, n):
    return jnp.concatenate([ref[pl.ds(s, n, stride=ROW_TILE), :] for s in range(ROW_TILE)], axis=1)


def _row_copy(src, src_row, dst, dst_row, sem):
    s0 = pl.multiple_of(src_row * ROW_TILE, ROW_TILE)
    d0 = pl.multiple_of(dst_row * ROW_TILE, ROW_TILE)
    return pltpu.make_async_copy(src.at[pl.ds(s0, ROW_TILE), :], dst.at[pl.ds(d0, ROW_TILE), :], sem)


def _wait_rows(hbm, vmem, sem, to_hbm):
    rows = hbm.at[pl.ds(0, vmem.shape[0]), :]
    (pltpu.make_async_copy(vmem, rows, sem) if to_hbm else pltpu.make_async_copy(rows, vmem, sem)).wait()


def _dispatch_kernel(nt, dest_ref, pad_ref, x_ref, g_ref, buf_out, xn_s, zero_s, sem, zsem):
    i = pl.program_id(0)
    tm = x_ref.shape[0]
    slot = i % 2
    n_pad = pad_ref.shape[2]

    def drain(s):
        for _ in range(TOP_K):
            _wait_rows(buf_out, xn_s.at[s], sem.at[s], True)

    @pl.when(i >= 2)
    def _():
        drain(slot)

    zero_s[...] = jnp.zeros(zero_s.shape, F32)

    def zero(q, c):
        _row_copy(zero_s, q, buf_out, pad_ref[0, 0, q], zsem.at[0]).start()
        return c

    lax.fori_loop(0, n_pad, zero, 0, unroll=8)
    _to_tiles(xn_s.at[slot], _rms(x_ref[...], g_ref[...]))

    def start(r, c):
        for k in range(TOP_K):
            _row_copy(xn_s.at[slot], r, buf_out, dest_ref[0, 0, TOP_K * r + k],
                      sem.at[slot]).start(priority=k % DMA_QUEUES)
        return c

    lax.fori_loop(0, tm, start, 0, unroll=8)
    _wait_rows(buf_out, zero_s, zsem.at[0], True)

    @pl.when(i == nt - 1)
    def _():
        drain(slot)
        if nt > 1:
            drain(1 - slot)


def _dispatch(x, g, dest, pad, rows, tm):
    t, d = x.shape
    nt = t // tm
    n_pad = pad.shape[0] // nt
    assert n_pad * nt == pad.shape[0]
    return pl.pallas_call(
        functools.partial(_dispatch_kernel, nt),
        grid=(nt,),
        in_specs=[pl.BlockSpec((1, 1, TOP_K * tm), lambda i: (i, 0, 0), memory_space=pltpu.SMEM),
                  pl.BlockSpec((1, 1, n_pad), lambda i: (i, 0, 0), memory_space=pltpu.SMEM),
                  pl.BlockSpec((tm, d), lambda i: (i, 0)),
                  pl.BlockSpec((1, d), lambda i: (0, 0))],
        out_specs=pl.BlockSpec(memory_space=pl.ANY),
        out_shape=jax.ShapeDtypeStruct((rows * ROW_TILE, LANES), F32),
        scratch_shapes=[pltpu.VMEM((2, tm * ROW_TILE, LANES), F32),
                        pltpu.VMEM((n_pad * ROW_TILE, LANES), F32),
                        pltpu.SemaphoreType.DMA((2,)), pltpu.SemaphoreType.DMA((1,))],
        compiler_params=_cparams(("arbitrary",)),
        name="moe_dispatch",
    )(dest.reshape(nt, 1, TOP_K * tm), pad.reshape(nt, 1, n_pad), x, g.reshape(1, d))


def _expert_kernel(be_ref, nu_ref, x_ref, w1_ref, w3_ref, w2_ref, y_ref):
    del be_ref
    used = pl.program_id(0) < nu_ref[0]

    @pl.when(jnp.logical_not(used))
    def _():
        y_ref[...] = jnp.zeros(y_ref.shape, F32)

    @pl.when(used)
    def _():
        x = _from_tiles(x_ref, x_ref.shape[0] // ROW_TILE).astype(BF16)
        y = _swiglu(x, lambda c: w1_ref[0, :, c], lambda c: w3_ref[0, :, c], lambda c: w2_ref[0, c, :])
        _to_tiles(y_ref, y)


def _experts(buf, blk_e, n_used, w1, w3, w2, blk):
    d, f = w1.shape[1], w1.shape[2]
    once = pl.Buffered(2)
    grid_spec = pltpu.PrefetchScalarGridSpec(
        num_scalar_prefetch=2,
        grid=(buf.shape[0] // (blk * ROW_TILE),),
        in_specs=[pl.BlockSpec((blk * ROW_TILE, LANES), lambda b, be, nu: (b, 0)),
                  pl.BlockSpec((1, d, f), lambda b, be, nu: (be[b], 0, 0), pipeline_mode=once),
                  pl.BlockSpec((1, d, f), lambda b, be, nu: (be[b], 0, 0), pipeline_mode=once),
                  pl.BlockSpec((1, f, d), lambda b, be, nu: (be[b], 0, 0), pipeline_mode=once)],
        out_specs=pl.BlockSpec((blk * ROW_TILE, LANES), lambda b, be, nu: (b, 0)))
    return pl.pallas_call(
        _expert_kernel,
        grid_spec=grid_spec,
        out_shape=jax.ShapeDtypeStruct(buf.shape, F32),
        compiler_params=_cparams(("arbitrary",)),
        name="moe_experts",
    )(blk_e, n_used, buf, w1, w3, w2)


def _combine_kernel(nt, dcur_ref, dnxt_ref, x_ref, rt_ref, fn_ref, y_hbm, o_ref, ya, yb, sem):
    i = pl.program_id(0)
    tm = x_ref.shape[0]
    slot = i % 2

    def gather(dref, s):
        def start(r, c):
            _row_copy(y_hbm, dref[0, 0, TOP_K * r], ya.at[s], r, sem.at[s, 0]).start(priority=0)
            _row_copy(y_hbm, dref[0, 0, TOP_K * r + 1], yb.at[s], r, sem.at[s, 1]).start(
                priority=1 % DMA_QUEUES)
            return c

        lax.fori_loop(0, tm, start, 0, unroll=8)

    @pl.when(i == 0)
    def _():
        gather(dcur_ref, slot)

    @pl.when(i + 1 < nt)
    def _():
        gather(dnxt_ref, 1 - slot)

    _wait_rows(y_hbm, ya.at[slot], sem.at[slot, 0], False)
    _wait_rows(y_hbm, yb.at[slot], sem.at[slot, 1], False)
    rt = rt_ref[...]
    lane = lax.broadcasted_iota(jnp.int32, rt.shape, 1)
    g1 = jnp.sum(jnp.where(lane == 2, rt, 0.0), axis=-1, keepdims=True)
    g2 = jnp.sum(jnp.where(lane == 3, rt, 0.0), axis=-1, keepdims=True)
    moe = g1 * _from_tiles(ya.at[slot], tm) + g2 * _from_tiles(yb.at[slot], tm)
    o_ref[...] = _rms(x_ref[...] + moe, fn_ref[...])


def _combine(x, route, fnorm, y, dest, tm):
    t, d = x.shape
    nt = t // tm
    dest3 = dest.reshape(nt, 1, TOP_K * tm)
    return pl.pallas_call(
        functools.partial(_combine_kernel, nt),
        grid=(nt,),
        in_specs=[pl.BlockSpec((1, 1, TOP_K * tm), lambda i: (i, 0, 0), memory_space=pltpu.SMEM),
                  pl.BlockSpec((1, 1, TOP_K * tm), lambda i: (jnp.minimum(i + 1, nt - 1), 0, 0),
                               memory_space=pltpu.SMEM),
                  pl.BlockSpec((tm, d), lambda i: (i, 0)),
                  pl.BlockSpec((tm, LANES), lambda i: (i, 0)),
                  pl.BlockSpec((1, d), lambda i: (0, 0)),
                  pl.BlockSpec(memory_space=pl.ANY)],
        out_specs=pl.BlockSpec((tm, d), lambda i: (i, 0)),
        out_shape=jax.ShapeDtypeStruct((t, d), F32),
        scratch_shapes=[pltpu.VMEM((2, tm * ROW_TILE, LANES), F32),
                        pltpu.VMEM((2, tm * ROW_TILE, LANES), F32),
                        pltpu.SemaphoreType.DMA((2, 2))],
        compiler_params=_cparams(("arbitrary",)),
        name="moe_combine",
    )(dest3, dest3, x, route, fnorm.reshape(1, d), y)


def _moe(x, route, g, w1, w3, w2, fnorm, tm):
    blk = min(MOE_BLOCK, x.shape[0])
    dest, pad, blk_e, n_used, nblk = _route_plan(route, blk)
    buf = _dispatch(x, g, dest, pad, nblk * blk, tm)
    y = _experts(buf, blk_e, n_used, w1, w3, w2, blk)
    return _combine(x, route, fnorm, y, dest, tm)


def _block_diag(w):
    g, a, b = w.shape
    eye = jnp.eye(g, dtype=w.dtype)
    return (eye[:, None, :, None] * w[:, :, None, :]).reshape(g * a, g * b)


def kernel(x, l0_norm_mix, l0_w_in, l0_conv_w, l0_conv_b, l0_gate_a_w, l0_gate_a_b, l0_gate_x_w, l0_gate_x_b, l0_lru_lambda, l0_shift_mu, l0_w0, l0_w_up, l0_a0, l0_a_up, l0_g_up, l0_k_k, l0_k_a, l0_r_k, l0_ln_x_w, l0_ln_x_b, l0_w_out, l0_norm_ffn, l0_ffn_w1, l0_ffn_w3, l0_ffn_w2, l1_norm_mix, l1_w_in, l1_conv_w, l1_conv_b, l1_wq, l1_wk, l1_wv, l1_ig_b, l1_fg_b, l1_mh_norm_w, l1_w_out, l1_norm_ffn, l1_router, l1_moe_w1, l1_moe_w3, l1_moe_w2, final_norm):
    batch, seq, d = x.shape
    t = batch * seq
    xt = x.reshape(t, d)
    tm = min(512, t)

    w_in0 = l0_w_in.astype(BF16)
    p_lru, p_rw = _norm_mm(xt, l0_norm_mix, w_in0, (2 * LRU_WIDTH, RW_IN), tm, "l0_in")
    y_lru = _lru(p_lru, batch, seq, l0_conv_w, l0_conv_b,
                 _block_diag(l0_gate_a_w).astype(BF16), l0_gate_a_b,
                 _block_diag(l0_gate_x_w).astype(BF16), l0_gate_x_b, l0_lru_lambda,
                 min(512, seq))
    y_rw = _rwkv(p_rw, batch, seq, l0_shift_mu, l0_w0, l0_w_up, l0_a0, l0_a_up, l0_g_up,
                 l0_k_k, l0_k_a, l0_r_k.reshape(-1), l0_ln_x_w, l0_ln_x_b)
    w_out0 = l0_w_out.astype(BF16)
    ne, _, f = l1_moe_w1.shape
    x2, (mw1, mw3, mw2) = _ffn(
        xt, y_lru, y_rw, w_out0[:LRU_WIDTH], w_out0[LRU_WIDTH:], l0_norm_ffn,
        l0_ffn_w1.astype(BF16), l0_ffn_w3.astype(BF16), l0_ffn_w2.astype(BF16),
        [l1_moe_w1.reshape(ne * d, f), l1_moe_w3.reshape(ne * d, f), l1_moe_w2.reshape(ne * f, d)], tm)

    n_in1 = 2 * MLSTM_WIDTH + LANES
    w_in1 = jnp.zeros((d, n_in1), F32).at[:, :l1_w_in.shape[1]].set(l1_w_in).astype(BF16)
    (p1,) = _norm_mm(x2, l1_norm_mix, w_in1, (n_in1,), tm, "l1_in")
    h1 = _mlstm(p1, batch, seq, l1_conv_w, l1_conv_b, l1_wq, l1_wk, l1_wv, l1_ig_b, l1_fg_b,
                l1_mh_norm_w)
    router = jnp.zeros((d, LANES), F32).at[:, :N_EXPERTS].set(l1_router)
    x3, route = _out_router(x2, h1, l1_w_out.astype(BF16), l1_norm_ffn, router, tm)
    out = _moe(x3, route, l1_norm_ffn, mw1.reshape(ne, d, f), mw3.reshape(ne, d, f),
               mw2.reshape(ne, f, d), final_norm, tm)
    return out.reshape(batch, seq, d)
```

```python
import functools

import jax
import jax.numpy as jnp
from jax import lax
from jax.experimental import pallas as pl
from jax.experimental.pallas import tpu as pltpu

F32 = jnp.float32
BF16 = jnp.bfloat16

D_MODEL = 1024
LRU_WIDTH = 512
LRU_BLOCKS = 8
LRU_C = 8.0
LRU_SEGS = 8
CONV_WIDTH = 4
RWKV_HEADS = 8
RWKV_HEAD = 64
RWKV_WIDTH = 512
LN_X_EPS = 1e-5 * RWKV_HEAD
MLSTM_HEADS = 8
MLSTM_HEAD = 128
MLSTM_WIDTH = 1024
MLSTM_CHUNK = 128
MLSTM_HEAD_GROUP = 8
MLSTM_QKV_ROWS = 512
D_FF = 2816
N_EXPERTS = 8
TOP_K = 2
DMA_QUEUES = 2
ROUTER_SPLIT = 4
MOE_BLOCK = 512
FF_CHUNKS = (768, 768, 768, 512)
RMS_EPS = 1e-6
RW_IN = 3 * RWKV_WIDTH + 64 + 64 + 128
RW_CHUNK = 64
RW_STEP_ROWS = 512
RW_WAVE_ROWS = 256
LANES = 128
MXU_WIDTH = 256
CARRY_ROWS = 8
ROW_TILE = 8
VMEM_LIMIT = 56 * 1024 * 1024


def _cparams(sem):
    return pltpu.CompilerParams(dimension_semantics=sem, vmem_limit_bytes=VMEM_LIMIT)


def _rms(x, w):
    return x * lax.rsqrt(jnp.mean(x * x, axis=-1, keepdims=True) + RMS_EPS) * w


def _sigmoid(x):
    return 1.0 / (1.0 + jnp.exp(-x))


def _softplus(x):
    return jnp.maximum(x, 0.0) + jnp.log(1.0 + jnp.exp(-jnp.abs(x)))


def _dot(a, b):
    return jnp.dot(a, b, preferred_element_type=F32)


def _dot_nt(a, b):
    return lax.dot_general(a, b, (((1,), (1,)), ((), ())), preferred_element_type=F32)


def _dot_tn(a, b):
    return lax.dot_general(a, b, (((0,), (0,)), ((), ())), preferred_element_type=F32)


def _split2(x):
    hi = x.astype(BF16)
    lo = (x - hi.astype(F32)).astype(BF16)
    return hi, lo


def _split3(x):
    hi = x.astype(BF16)
    r = x - hi.astype(F32)
    mid = r.astype(BF16)
    lo = (r - mid.astype(F32)).astype(BF16)
    return hi, mid, lo


def _mm3(fn, a, b):
    ah, al = _split2(a)
    bh, bl = _split2(b)
    return fn(ah, bh) + fn(al, bh) + fn(ah, bl)


def _mm_exact_lhs(fn, a_bf16, b):
    h, m, l = _split3(b)
    return fn(a_bf16, h) + fn(a_bf16, m) + fn(a_bf16, l)


def _seg_sum(x, seg):
    s = seg.shape[0]
    outs = []
    for c in range(x.shape[1] // s):
        hi, lo = _split2(x[:, c * s:(c + 1) * s])
        outs.append(_dot(hi, seg) + _dot(lo, seg))
    return jnp.concatenate(outs, axis=1)


def _shift_hist(buf_ref, x, first):
    n = x.shape[0]

    @pl.when(first)
    def _():
        buf_ref[0:CARRY_ROWS, :] = jnp.zeros((CARRY_ROWS, x.shape[1]), F32)

    @pl.when(jnp.logical_not(first))
    def _():
        buf_ref[0:CARRY_ROWS, :] = buf_ref[n:n + CARRY_ROWS, :]

    buf_ref[CARRY_ROWS:CARRY_ROWS + n, :] = x


def _causal_conv(buf_ref, n, w, b, rotate):
    if rotate:
        full = buf_ref[0:CARRY_ROWS + n, :]
        acc = b + w[CONV_WIDTH - 1:CONV_WIDTH, :] * full[CARRY_ROWS:, :]
        for lag in range(1, CONV_WIDTH):
            tap = CONV_WIDTH - 1 - lag
            acc = acc + w[tap:tap + 1, :] * pltpu.roll(full, lag, axis=0)[CARRY_ROWS:, :]
        return acc
    acc = b
    for j in range(CONV_WIDTH):
        off = CARRY_ROWS - (CONV_WIDTH - 1) + j
        acc = acc + w[j:j + 1, :] * buf_ref[off:off + n, :]
    return acc


def _norm_mm_kernel(x_ref, g_ref, w_ref, *o_refs):
    xn = _rms(x_ref[...], g_ref[...]).astype(BF16)
    c0 = 0
    for o_ref in o_refs:
        c1 = c0 + o_ref.shape[1]
        o_ref[...] = _dot(xn, w_ref[:, c0:c1]).astype(o_ref.dtype)
        c0 = c1


def _norm_mm(x, g, w, widths, tm, name):
    t, d = x.shape
    n = w.shape[1]
    assert sum(widths) == n
    return pl.pallas_call(
        _norm_mm_kernel,
        grid=(t // tm,),
        in_specs=[pl.BlockSpec((tm, d), lambda i: (i, 0)),
                  pl.BlockSpec((1, d), lambda i: (0, 0)),
                  pl.BlockSpec((d, n), lambda i: (0, 0))],
        out_specs=[pl.BlockSpec((tm, c), lambda i: (i, 0)) for c in widths],
        out_shape=[jax.ShapeDtypeStruct((t, c), BF16) for c in widths],
        compiler_params=_cparams(("parallel",)),
        name=name,
    )(x, g.reshape(1, d), w)


def _lru_kernel(p_ref, cw_ref, cb_ref, wa_ref, ba_ref, wx_ref, bx_ref, lam_ref, o_ref,
                xbuf, abuf, bbuf, hbuf, hcar):
    i = pl.program_id(1)
    n = p_ref.shape[0]
    first = i == 0
    _shift_hist(xbuf, p_ref[:, 0:LRU_WIDTH].astype(F32), first)
    xc = _causal_conv(xbuf, n, cw_ref[...], cb_ref[...], True)
    xcb = xc.astype(BF16)
    r = _sigmoid(_dot(xcb, wa_ref[...]) + ba_ref[...])
    ig = _sigmoid(_dot(xcb, wx_ref[...]) + bx_ref[...])
    log_a = (-LRU_C) * r * _softplus(-lam_ref[...])
    a = jnp.exp(log_a)
    mult = jnp.sqrt(1.0 - a * a)
    row = lax.broadcasted_iota(jnp.int32, (n, 1), 0)
    mult = jnp.where(jnp.logical_and(first, row == 0), 1.0, mult)
    bvals = mult * ig * xc
    n_slab = LRU_WIDTH // LANES
    slabs = [slice(c * LANES, (c + 1) * LANES) for c in range(n_slab)]
    seg = n // LRU_SEGS
    pitch = _lru_pitch(n)
    for c, cs in enumerate(slabs):
        for s in range(LRU_SEGS):
            abuf[c, s * pitch:s * pitch + seg, :] = a[s * seg:(s + 1) * seg, cs]
            bbuf[c, s * pitch:s * pitch + seg, :] = bvals[s * seg:(s + 1) * seg, cs]

    @pl.when(first)
    def _():
        hcar[...] = jnp.zeros(hcar.shape, F32)

    def step(j, carry):
        rows = pl.ds(j, LRU_SEGS, stride=pitch)
        out = []
        for c in range(n_slab):
            h, acc = carry[c]
            a_j = abuf[c, rows, :]
            h = a_j * h + bbuf[c, rows, :]
            acc = acc * a_j
            hbuf[c, rows, :] = h
            abuf[c, rows, :] = acc
            out.append((h, acc))
        return tuple(out)

    init = tuple((jnp.zeros((LRU_SEGS, LANES), F32), jnp.ones((LRU_SEGS, LANES), F32))
                 for _ in range(n_slab))
    ends = lax.fori_loop(0, seg, step, init, unroll=8)
    cols = []
    for c, cs in enumerate(slabs):
        h_end, a_end = ends[c]
        h0 = hcar[0:1, cs]
        parts = []
        for s in range(LRU_SEGS):
            rows = slice(s * pitch, s * pitch + seg)
            parts.append(hbuf[c, rows, :] + abuf[c, rows, :] * h0)
            h0 = h_end[s:s + 1, :] + a_end[s:s + 1, :] * h0
        hcar[0:1, cs] = h0
        cols.append(jnp.concatenate(parts, axis=0))
    gate = p_ref[:, LRU_WIDTH:2 * LRU_WIDTH].astype(F32)
    gelu = 0.5 * gate * (1.0 + jnp.tanh(0.7978845608028654 * (gate + 0.044715 * gate * gate * gate)))
    o_ref[...] = (jnp.concatenate(cols, axis=1) * gelu).astype(o_ref.dtype)


def _lru_pitch(n):
    tiles = n // LRU_SEGS // CARRY_ROWS
    return (tiles + 1 - tiles % 2) * CARRY_ROWS


def _lru(p_lru, batch, seq, conv_w, conv_b, wa, ba, wx, bx, lam, tb):
    nb = seq // tb
    c = LRU_WIDTH
    vec = lambda: pl.BlockSpec((1, c), lambda b, i: (0, 0))
    return pl.pallas_call(
        _lru_kernel,
        grid=(batch, nb),
        in_specs=[pl.BlockSpec((tb, 2 * c), lambda b, i: (b * nb + i, 0)),
                  pl.BlockSpec((CONV_WIDTH, c), lambda b, i: (0, 0)), vec(),
                  pl.BlockSpec((c, c), lambda b, i: (0, 0)), vec(),
                  pl.BlockSpec((c, c), lambda b, i: (0, 0)), vec(), vec()],
        out_specs=pl.BlockSpec((tb, c), lambda b, i: (b * nb + i, 0)),
        out_shape=jax.ShapeDtypeStruct((batch * seq, c), BF16),
        scratch_shapes=[pltpu.VMEM((tb + CARRY_ROWS, c), F32)]
        + [pltpu.VMEM((c // LANES, LRU_SEGS * _lru_pitch(tb), LANES), F32)] * 3
        + [pltpu.VMEM((CARRY_ROWS, c), F32)],
        compiler_params=_cparams(("parallel", "arbitrary")),
        name="rg_lru",
    )(p_lru, conv_w, conv_b.reshape(1, c), wa, ba.reshape(1, c), wx, bx.reshape(1, c),
      lam.reshape(1, c))


def _blk(x, masks):
    xb = x.astype(BF16)
    return jnp.concatenate([jnp.where(mk, xb, jnp.zeros_like(xb)) for mk in masks], axis=0)


def _rwkv_kernel(p_ref, mu_ref, w0_ref, wup_ref, a0_ref, aup_ref, gup_ref, kk_ref, ka_ref,
                 rk_ref, lnw_ref, lnb_ref, seg_ref, tri_ref, o_ref, pbuf, state):
    L = RW_CHUNK
    W = RWKV_WIDTH
    rows = p_ref.shape[0]
    wave = tri_ref.shape[0]
    first = pl.program_id(1) == 0
    p = p_ref[...].astype(F32)
    _shift_hist(pbuf, p, first)
    prev = pbuf[CARRY_ROWS - 1:CARRY_ROWS - 1 + rows, :]
    ps = p + mu_ref[...] * (prev - p)
    seg = seg_ref[...]
    tri = tri_ref[...]

    @pl.when(first)
    def _():
        state[...] = jnp.zeros(state.shape, F32)

    lane = lax.broadcasted_iota(jnp.int32, (1, LANES), 1)
    m1 = [lane < RWKV_HEAD, lane >= RWKV_HEAD]
    m2 = [jnp.concatenate([mk, mk], axis=1) for mk in m1]
    ti = lax.broadcasted_iota(jnp.int32, (L, LANES), 0)
    si = lax.broadcasted_iota(jnp.int32, (L, LANES), 1) % RWKV_HEAD
    strict = si < ti
    incl = si <= ti
    bi = lax.broadcasted_iota(jnp.int32, (LANES, LANES), 0) // RWKV_HEAD
    bj = lax.broadcasted_iota(jnp.int32, (LANES, LANES), 1) // RWKV_HEAD
    bd = bi == bj

    def tril(mask, s):
        return jnp.where(mask, s, 0.0).astype(BF16)

    wv = [slice(w * wave, (w + 1) * wave) for w in range(rows // wave)]
    ws = range(len(wv))
    r = [ps[rw, 0:W] for rw in wv]
    k = [ps[rw, W:2 * W] for rw in wv]
    v = [ps[rw, 2 * W:3 * W] for rw in wv]
    x2 = [ps[rw, 3 * W:3 * W + LANES] for rw in wv]
    xg = [ps[rw, 3 * W + LANES:3 * W + 2 * LANES] for rw in wv]
    wl = [w0_ref[...] + _dot(jnp.tanh(x2[w]).astype(BF16), wup_ref[...]) for w in ws]
    a = [_sigmoid(a0_ref[...] + _dot(x2[w].astype(BF16), aup_ref[...])) for w in ws]
    g = [_dot(_sigmoid(xg[w]).astype(BF16), gup_ref[...]) for w in ws]
    kk = [k[w] * kk_ref[...] for w in ws]
    kk = [kk[w] / jnp.maximum(jnp.sqrt(_seg_sum(kk[w] * kk[w], seg)), 1e-12) for w in ws]
    lw = [-jnp.exp(-_softplus(-wl[w]) - 0.5) for w in ws]
    cw = [_mm_exact_lhs(_dot, tri, lw[w]) for w in ws]

    n_pairs = RWKV_HEADS // 2
    n_ch = wave // L
    s_cur = [state[pr] for pr in range(n_pairs)]
    for w in ws:
        cw_end = jnp.concatenate(
            [jnp.broadcast_to(cw[w][c * L + L - 1:c * L + L, :], (L, W)) for c in range(n_ch)], axis=0)
        w_in = jnp.exp(cw[w])
        w_inv = jnp.exp(-cw[w])
        w_end = jnp.exp(cw_end - cw[w])
        k2 = k[w] * (1.0 + (a[w] - 1.0) * ka_ref[...])
        kka = kk[w] * a[w]
        a_t = -kk[w] * jnp.exp(cw[w] - lw[w])
        b_t = kka * w_inv
        k_t = k2 * w_inv
        r_t = r[w] * w_in
        b_bar = kka * w_end
        k_bar = k2 * w_end
        w_tot = jnp.exp(cw_end)
        vw = v[w]

        chains = [(slice(c * L, (c + 1) * L), slice(pr * LANES, (pr + 1) * LANES))
                  for c in range(n_ch) for pr in range(n_pairs)]
        idx = range(len(chains))
        a_c = [a_t[rs, sl] for rs, sl in chains]
        r_c = [r_t[rs, sl] for rs, sl in chains]
        ar = [jnp.concatenate([a_c[i], r_c[i]], axis=0).astype(BF16) for i in idx]
        sb = [_dot_nt(ar[i], _blk(b_t[rs, sl], m1)) for i, (rs, sl) in enumerate(chains)]
        sk = [_dot_nt(ar[i], _blk(k_t[rs, sl], m1)) for i, (rs, sl) in enumerate(chains)]
        vblk = [_blk(vw[rs, sl], m1) for rs, sl in chains]
        rhs = [jnp.concatenate([a_c[i], _dot(tril(strict, sk[i][0:L]), vblk[i])], axis=1) for i in idx]
        pw = [tril(strict, sb[i][0:L]) for i in idx]
        tinv = [jnp.where(si == ti, 1.0, 0.0) + pw[i].astype(F32) for i in idx]
        pw = [_dot(pw[i], _blk(pw[i], m1)).astype(BF16) for i in idx]
        for _ in range(4):
            both = [_dot(pw[i], jnp.concatenate([_blk(pw[i], m1), _blk(tinv[i], m1)], axis=1)) for i in idx]
            tinv = [tinv[i] + both[i][:, LANES:2 * LANES] for i in idx]
            pw = [both[i][:, 0:LANES].astype(BF16) for i in idx]
        tinv = [tinv[i] + _dot(pw[i], _blk(tinv[i], m1)) for i in idx]
        x = [_dot(tinv[i].astype(BF16), _blk(rhs[i], m2)) for i in idx]
        corr = [_dot(tril(incl, sb[i][L:2 * L]), _blk(x[i], m2)) for i in idx]
        r_hat = [(r_c[i] + corr[i][:, 0:LANES]).astype(BF16) for i in idx]
        y0 = [corr[i][:, LANES:2 * LANES] + _dot(tril(incl, sk[i][L:2 * L]), vblk[i]) for i in idx]
        xb = [x[i].astype(BF16) for i in idx]
        bb = [b_bar[rs, sl].astype(BF16) for rs, sl in chains]
        gmat = [jnp.where(bd, _dot_tn(xb[i][:, 0:LANES], bb[i]), 0.0).astype(BF16) for i in idx]
        hmat = [jnp.where(bd, _dot_tn(xb[i][:, LANES:2 * LANES], bb[i])
                          + _dot_tn(vw[rs, sl].astype(BF16), k_bar[rs, sl].astype(BF16)), 0.0)
                for i, (rs, sl) in enumerate(chains)]

        y_rows = []
        for c in range(n_ch):
            ys = []
            for pr in range(n_pairs):
                i = c * n_pairs + pr
                s0 = s_cur[pr]
                s0b = s0.astype(BF16)
                ys.append(_dot_nt(r_hat[i], s0b) + y0[i])
                s_cur[pr] = s0 * w_tot[c * L:c * L + 1, chains[i][1]] + _dot(s0b, gmat[i]) + hmat[i]
            y_rows.append(jnp.concatenate(ys, axis=1))
        y = jnp.concatenate(y_rows, axis=0)

        inv = 1.0 / RWKV_HEAD
        mean = _seg_sum(y, seg) * inv
        yc = y - mean
        var = _seg_sum(yc * yc, seg) * inv
        yn = yc * lax.rsqrt(var + LN_X_EPS) * lnw_ref[...] + lnb_ref[...]
        bonus = _seg_sum(r[w] * k2 * rk_ref[...], seg) * vw
        o_ref[wv[w], :] = ((yn + bonus) * g[w]).astype(o_ref.dtype)
    for pr in range(n_pairs):
        state[pr] = s_cur[pr]


def _rwkv(p_rw, batch, seq, mu, w0, w_up, a0, a_up, g_up, k_k, k_a, r_k, ln_w, ln_b):
    L = min(RW_STEP_ROWS, seq)
    nc = seq // L
    W = RWKV_WIDTH
    wup = jnp.zeros((LANES, W), F32).at[0:64].set(w_up).astype(BF16)
    aup = jnp.zeros((LANES, W), F32).at[64:128].set(a_up).astype(BF16)
    hid = jnp.arange(MXU_WIDTH) // RWKV_HEAD
    seg = (hid[:, None] == hid[None, :]).astype(BF16)
    wave = min(RW_WAVE_ROWS, L)
    ri = jnp.arange(wave)
    tri = jnp.logical_and(ri[:, None] >= ri[None, :],
                          ri[:, None] // RW_CHUNK == ri[None, :] // RW_CHUNK).astype(BF16)
    vec = lambda: pl.BlockSpec((1, W), lambda b, i: (0, 0))
    mat = lambda s: pl.BlockSpec(s, lambda b, i: (0, 0))
    return pl.pallas_call(
        _rwkv_kernel,
        grid=(batch, nc),
        in_specs=[pl.BlockSpec((L, RW_IN), lambda b, i: (b * nc + i, 0)),
                  mat((1, RW_IN)), vec(), mat((LANES, W)), vec(), mat((LANES, W)),
                  mat((LANES, W)), vec(), vec(), vec(), vec(), vec(), mat((MXU_WIDTH, MXU_WIDTH)),
                  mat((wave, wave))],
        out_specs=pl.BlockSpec((L, W), lambda b, i: (b * nc + i, 0)),
        out_shape=jax.ShapeDtypeStruct((batch * seq, W), BF16),
        scratch_shapes=[pltpu.VMEM((L + CARRY_ROWS, RW_IN), F32),
                        pltpu.VMEM((RWKV_HEADS // 2, LANES, LANES), F32)],
        compiler_params=_cparams(("parallel", "arbitrary")),
        name="rwkv7",
    )(p_rw, mu.reshape(1, RW_IN), w0.reshape(1, W), wup, a0.reshape(1, W), aup,
      g_up.astype(BF16), k_k.reshape(1, W), k_a.reshape(1, W), r_k.reshape(1, W),
      ln_w.reshape(1, W), ln_b.reshape(1, W), seg, tri)


def _swiglu(xb, w1, w3, w2):
    edges = [sum(FF_CHUNKS[:i]) for i in range(len(FF_CHUNKS) + 1)]
    cols = [slice(edges[i], edges[i + 1]) for i in range(len(FF_CHUNKS))]
    hid = (_dot(xb, w1(cols[0])), _dot(xb, w3(cols[0])))
    acc = None
    for i, c in enumerate(cols):
        nxt = (_dot(xb, w1(cols[i + 1])), _dot(xb, w3(cols[i + 1]))) if i + 1 < len(cols) else None
        act = (hid[0] * _sigmoid(hid[0]) * hid[1]).astype(BF16)
        part = _dot(act, w2(c))
        acc = part if acc is None else acc + part
        hid = nxt
    return acc


def _ffn_kernel(n_cast, x_ref, ya_ref, yb_ref, wa_ref, wb_ref, g_ref, w1_ref, w3_ref, w2_ref, *rest):
    src, o_ref, dst = rest[:n_cast], rest[n_cast], rest[n_cast + 1:]
    x1 = x_ref[...] + _dot(ya_ref[...], wa_ref[...]) + _dot(yb_ref[...], wb_ref[...])
    xn = _rms(x1, g_ref[...]).astype(BF16)
    o_ref[...] = x1 + _swiglu(xn, lambda c: w1_ref[:, c], lambda c: w3_ref[:, c], lambda c: w2_ref[c, :])
    for s_ref, d_ref in zip(src, dst):
        d_ref[...] = s_ref[...].astype(BF16)


def _ffn(x, ya, yb, wa, wb, g, w1, w3, w2, to_cast, tm):
    t, d = x.shape
    nt = t // tm
    once = pl.Buffered(1)
    fixed = lambda a: pl.BlockSpec(a.shape, lambda i: (0, 0), pipeline_mode=once)
    slab = lambda a: pl.BlockSpec((a.shape[0] // nt, a.shape[1]), lambda i: (i, 0))
    assert all(a.shape[0] % (nt * 2 * ROW_TILE) == 0 for a in to_cast)
    outs = pl.pallas_call(
        functools.partial(_ffn_kernel, len(to_cast)),
        grid=(nt,),
        in_specs=[pl.BlockSpec((tm, d), lambda i: (i, 0)),
                  pl.BlockSpec((tm, ya.shape[1]), lambda i: (i, 0)),
                  pl.BlockSpec((tm, yb.shape[1]), lambda i: (i, 0)),
                  fixed(wa), fixed(wb),
                  pl.BlockSpec((1, d), lambda i: (0, 0)),
                  fixed(w1), fixed(w3), fixed(w2)] + [slab(a) for a in to_cast],
        out_specs=[pl.BlockSpec((tm, d), lambda i: (i, 0))] + [slab(a) for a in to_cast],
        out_shape=[jax.ShapeDtypeStruct((t, d), F32)]
        + [jax.ShapeDtypeStruct(a.shape, BF16) for a in to_cast],
        compiler_params=_cparams(("parallel",)),
        name="ffn_swiglu",
    )(x, ya, yb, wa, wb, g.reshape(1, d), w1, w3, w2, *to_cast)
    return outs[0], outs[1:]


def _mlstm_qkv_kernel(xm_ref, halo_ref, gt_ref, cw_ref, cb_ref, wq_ref, wk_ref, wv_ref, gb_ref, tri_ref,
                      q_ref, k_ref, v_ref, gs_ref, gtr_ref, xbuf):
    n = xm_ref.shape[0]
    dh = MLSTM_HEAD
    L = MLSTM_CHUNK
    nh = MLSTM_HEADS
    lane = lax.broadcasted_iota(jnp.int32, (L, LANES), 1)
    row = lax.broadcasted_iota(jnp.int32, (L, LANES), 0)
    for c in range(n // L):
        rs = slice(c * L, (c + 1) * L)
        gl = gt_ref[rs, :].astype(F32) + gb_ref[...]
        lf = jnp.minimum(gl, 0.0) - jnp.log(1.0 + jnp.exp(-jnp.abs(gl)))
        bcum = _mm_exact_lhs(_dot, tri_ref[...], lf)
        b_al = pltpu.roll(bcum, LANES - nh, axis=1)
        gmat = gl - b_al
        cmax = gmat
        d = 1
        while d < L:
            cmax = jnp.maximum(cmax, jnp.where(row >= d, pltpu.roll(cmax, d, axis=0), -jnp.inf))
            d *= 2
        gs_ref[rs, :] = jnp.where(lane < nh, gmat,
                                  jnp.where(lane < 2 * nh, pltpu.roll(cmax, nh, axis=1),
                                            pltpu.roll(b_al, 2 * nh, axis=1)))
        gtr_ref[rs, :] = gmat.T
    xm = xm_ref[...]
    hist = halo_ref[halo_ref.shape[0] - CARRY_ROWS:, :].astype(F32)
    xbuf[0:CARRY_ROWS, :] = jnp.where(pl.program_id(1) == 0, 0.0, hist)
    xbuf[CARRY_ROWS:CARRY_ROWS + n, :] = xm.astype(F32)
    xc = _causal_conv(xbuf, n, cw_ref[...], cb_ref[...], False)
    xc = (xc * _sigmoid(xc)).astype(BF16)
    scale = dh ** -0.5
    for h in range(MLSTM_HEADS):
        sl = slice(h * dh, (h + 1) * dh)
        q_ref[:, sl] = (_dot(xc[:, sl], wq_ref[h]) * scale).astype(BF16)
        k_ref[:, sl] = _dot(xc[:, sl], wk_ref[h]).astype(BF16)
        v_ref[:, sl] = _dot(xm[:, sl], wv_ref[h]).astype(BF16)


def _mlstm_kernel(q_ref, k_ref, v_ref, z_ref, gs_ref, gtr_ref, nw_ref, o_ref, cn_s, m_s):
    ci = pl.program_id(1)
    L = MLSTM_CHUNK
    dh = MLSTM_HEAD
    nh = MLSTM_HEADS
    first = ci == 0
    gs = gs_ref[...]
    g_t = gtr_ref[...]

    @pl.when(first)
    def _():
        cn_s[...] = jnp.zeros(cn_s.shape, F32)
        m_s[...] = jnp.zeros(m_s.shape, F32)

    ti = lax.broadcasted_iota(jnp.int32, (L, L), 0)
    si = lax.broadcasted_iota(jnp.int32, (L, L), 1)
    causal = si <= ti
    ones = jnp.ones((L, dh), BF16)
    sls = [slice(h * dh, (h + 1) * dh) for h in range(MLSTM_HEADS)]
    for g0 in range(0, MLSTM_HEADS, MLSTM_HEAD_GROUP):
        hs = range(g0, g0 + MLSTM_HEAD_GROUP)
        qb = {h: q_ref[:, sls[h]] for h in hs}
        kb = {h: k_ref[:, sls[h]] for h in hs}
        v1 = {h: jnp.concatenate([v_ref[:, sls[h]], ones], axis=1) for h in hs}
        qk = {h: _dot_nt(qb[h], kb[h]) for h in hs}
        qcn = {h: _dot(qb[h], cn_s[h].astype(BF16)) for h in hs}
        m_prev = {h: m_s[h:h + 1, 0:1] for h in hs}
        mx = {h: jnp.maximum(m_prev[h], gs[:, nh + h:nh + h + 1]) for h in hs}
        s = {h: qk[h] * jnp.exp(jnp.where(causal, g_t[h:h + 1, :] - mx[h], -jnp.inf)) for h in hs}
        sv = {h: _dot(s[h].astype(BF16), v1[h]) for h in hs}
        m_last = {h: mx[h][L - 1:L, :] for h in hs}
        kw = {h: kb[h].astype(F32) * jnp.exp(gs[:, h:h + 1] - m_last[h]) for h in hs}
        kv = {h: _dot_tn(kw[h].astype(BF16), v1[h]) for h in hs}
        sc = {h: jnp.exp(m_prev[h] - mx[h]) for h in hs}
        den = {h: sc[h] * qcn[h][:, dh:] + sv[h][:, dh:] for h in hs}
        floor = {h: jnp.exp(-(gs[:, 2 * nh + h:2 * nh + h + 1] + mx[h])) for h in hs}
        hh = {h: (sc[h] * qcn[h][:, :dh] + sv[h][:, :dh]) / jnp.maximum(jnp.abs(den[h]), floor[h]) for h in hs}
        ms = {h: jnp.mean(hh[h] * hh[h], axis=-1, keepdims=True) for h in hs}
        dec = {h: jnp.exp(m_prev[h] - m_last[h]) for h in hs}
        for h in hs:
            cn_s[h] = dec[h] * cn_s[h] + kv[h]
            m_s[h:h + 1, :] = jnp.broadcast_to(gs[L - 1:L, 2 * nh + h:2 * nh + h + 1] + m_last[h], (1, LANES))
            hn = hh[h] * lax.rsqrt(ms[h] + RMS_EPS)
            gate = _sigmoid(z_ref[:, sls[h]].astype(F32))
            o_ref[:, sls[h]] = (gate * (hn * nw_ref[:, sls[h]])).astype(o_ref.dtype)


def _mlstm(p1, batch, seq, conv_w, conv_b, wq, wk, wv, ig_b, fg_b, mh_w):
    L = MLSTM_CHUNK
    nc = seq // L
    W = MLSTM_WIDTH
    t = batch * seq
    tb = min(MLSTM_QKV_ROWS, seq)
    nb = seq // tb
    halo = 2 * CARRY_ROWS
    hw = lambda: pl.BlockSpec((MLSTM_HEADS, MLSTM_HEAD, MLSTM_HEAD), lambda b, i: (0, 0, 0))
    rows = lambda: pl.BlockSpec((tb, W), lambda b, i: (b * nb + i, 0))
    gate_rows = lambda: pl.BlockSpec((tb, LANES), lambda b, i: (b * nb + i, 0))
    gb = jnp.zeros((1, LANES), F32).at[0, 0:8].set(ig_b).at[0, 8:16].set(fg_b)
    tri = (jnp.arange(L)[:, None] >= jnp.arange(L)[None, :]).astype(BF16)
    q, k, v, gs, gtr = pl.pallas_call(
        _mlstm_qkv_kernel,
        grid=(batch, nb),
        in_specs=[rows(),
                  pl.BlockSpec((halo, W), lambda b, i: (jnp.maximum((b * nb + i) * (tb // halo) - 1, 0), 0)),
                  pl.BlockSpec((tb, LANES), lambda b, i: (b * nb + i, 2 * W // LANES)),
                  pl.BlockSpec((CONV_WIDTH, W), lambda b, i: (0, 0)),
                  pl.BlockSpec((1, W), lambda b, i: (0, 0)),
                  hw(), hw(), hw(),
                  pl.BlockSpec((1, LANES), lambda b, i: (0, 0)),
                  pl.BlockSpec((L, L), lambda b, i: (0, 0))],
        out_specs=[rows(), rows(), rows(), gate_rows(), gate_rows()],
        out_shape=[jax.ShapeDtypeStruct((t, W), BF16)] * 3 + [jax.ShapeDtypeStruct((t, LANES), F32)] * 2,
        scratch_shapes=[pltpu.VMEM((tb + CARRY_ROWS, W), F32)],
        compiler_params=_cparams(("parallel", "parallel")),
        name="mlstm_qkv",
    )(p1, p1, p1, conv_w, conv_b.reshape(1, W), wq.astype(BF16), wk.astype(BF16), wv.astype(BF16), gb, tri)

    chunk = lambda: pl.BlockSpec((L, W), lambda b, i: (b * nc + i, 0))
    return pl.pallas_call(
        _mlstm_kernel,
        grid=(batch, nc),
        in_specs=[chunk(), chunk(), chunk(),
                  pl.BlockSpec((L, W), lambda b, i: (b * nc + i, 1)),
                  pl.BlockSpec((L, LANES), lambda b, i: (b * nc + i, 0)),
                  pl.BlockSpec((L, LANES), lambda b, i: (b * nc + i, 0)),
                  pl.BlockSpec((1, W), lambda b, i: (0, 0))],
        out_specs=pl.BlockSpec((L, W), lambda b, i: (b * nc + i, 0)),
        out_shape=jax.ShapeDtypeStruct((t, W), BF16),
        scratch_shapes=[pltpu.VMEM((MLSTM_HEADS, MLSTM_HEAD, 2 * MLSTM_HEAD), F32),
                        pltpu.VMEM((MLSTM_HEADS, LANES), F32)],
        compiler_params=_cparams(("parallel", "arbitrary")),
        name="mlstm",
    )(q, k, v, p1, gs, gtr, mh_w.reshape(1, W))


def _out_router_kernel(x_ref, y_ref, w_ref, g_ref, rt_ref, x3_ref, o_ref):
    tm = x_ref.shape[0]
    sub = tm // ROUTER_SPLIT
    rs = [slice(q * sub, (q + 1) * sub) for q in range(ROUTER_SPLIT)]
    qs = range(ROUTER_SPLIT)
    w = w_ref[...]
    x3 = [x_ref[r, :] + _dot(y_ref[r, :], w) for r in rs]
    for q in qs:
        x3_ref[rs[q], :] = x3[q]
    gw = g_ref[...]
    rt = rt_ref[...]
    logits = [_mm3(_dot, _rms(x3[q], gw), rt) for q in qs]
    lane = lax.broadcasted_iota(jnp.int32, (sub, LANES), 1)
    lg = [jnp.where(lane < N_EXPERTS, logits[q], -jnp.inf) for q in qs]
    v1 = [jnp.max(lg[q], axis=-1, keepdims=True) for q in qs]
    i1 = [jnp.min(jnp.where(lg[q] == v1[q], lane, LANES), axis=-1, keepdims=True) for q in qs]
    lg2 = [jnp.where(lane == i1[q], -jnp.inf, lg[q]) for q in qs]
    v2 = [jnp.max(lg2[q], axis=-1, keepdims=True) for q in qs]
    i2 = [jnp.min(jnp.where(lg2[q] == v2[q], lane, LANES), axis=-1, keepdims=True) for q in qs]
    for q in qs:
        ex = jnp.exp(v2[q] - v1[q])
        g1 = 1.0 / (1.0 + ex)
        g2 = ex / (1.0 + ex)
        out = jnp.where(lane == 0, i1[q].astype(F32), 0.0)
        out = jnp.where(lane == 1, i2[q].astype(F32), out)
        out = jnp.where(lane == 2, g1, out)
        o_ref[rs[q], :] = jnp.where(lane == 3, g2, out)


def _out_router(x, y, w, g, router, tm):
    t, d = x.shape
    return pl.pallas_call(
        _out_router_kernel,
        grid=(t // tm,),
        in_specs=[pl.BlockSpec((tm, d), lambda i: (i, 0)),
                  pl.BlockSpec((tm, y.shape[1]), lambda i: (i, 0)),
                  pl.BlockSpec(w.shape, lambda i: (0, 0)),
                  pl.BlockSpec((1, d), lambda i: (0, 0)),
                  pl.BlockSpec((d, LANES), lambda i: (0, 0))],
        out_specs=[pl.BlockSpec((tm, d), lambda i: (i, 0)),
                   pl.BlockSpec((tm, LANES), lambda i: (i, 0))],
        out_shape=[jax.ShapeDtypeStruct((t, d), F32), jax.ShapeDtypeStruct((t, LANES), F32)],
        compiler_params=_cparams(("parallel",)),
        name="l1_out_router",
    )(x, y, w, g.reshape(1, d), router)


def _prefix_sum(x):
    m, c = x.shape
    group = min(LANES, m)
    xg = x.reshape(m // group, group, c).astype(F32)
    gi = jnp.arange(group)
    local = jnp.einsum("ij,bjc->bic", (gi[:, None] >= gi[None, :]).astype(F32), xg,
                       precision=lax.Precision.HIGHEST)
    bi = jnp.arange(m // group)
    before = jnp.dot((bi[:, None] > bi[None, :]).astype(F32), local[:, -1, :],
                     precision=lax.Precision.HIGHEST)
    return jnp.round(local + before[:, None, :]).astype(jnp.int32).reshape(m, c)


def _route_plan(route, blk):
    e = route[:, 0:TOP_K].astype(jnp.int32).reshape(-1)
    m = e.shape[0]
    onehot = (e[:, None] == jnp.arange(N_EXPERTS, dtype=jnp.int32)[None, :]).astype(jnp.int32)
    csum = _prefix_sum(onehot)
    rank = jnp.sum((csum - onehot) * onehot, axis=1)
    nb = (csum[-1] + blk - 1) // blk
    bend = _prefix_sum(nb[:, None])[:, 0]
    dest = jnp.sum(onehot * (bend - nb)[None, :], axis=1) * blk + rank
    nblk = m // blk + N_EXPERTS
    n_used = bend[-1]
    bidx = jnp.minimum(jnp.arange(nblk, dtype=jnp.int32), n_used - 1)
    blk_e = jnp.minimum(jnp.sum(bidx[:, None] >= bend[None, :], axis=1), N_EXPERTS - 1).astype(jnp.int32)
    gap_start = jnp.concatenate([(bend - nb) * blk + csum[-1], (n_used * blk)[None]])
    gap_len = jnp.concatenate([nb * blk - csum[-1], ((nblk - n_used) * blk)[None]])
    gap_end = _prefix_sum(gap_len[:, None])[:, 0]
    q = jnp.arange(nblk * blk - m, dtype=jnp.int32)
    gi = jnp.sum(q[:, None] >= gap_end[None, :], axis=1)
    pad = gap_start[gi] + q - (gap_end - gap_len)[gi]
    return dest.astype(jnp.int32), pad.astype(jnp.int32), blk_e, n_used.reshape(1).astype(jnp.int32), nblk


def _to_tiles(ref, x):
    n = x.shape[0]
    for s in range(ROW_TILE):
        ref[pl.ds(s, n, stride=ROW_TILE), :] = x[:, s * LANES:(s + 1) * LANES]


def _from_tiles(ref, n):
    return jnp.concatenate([ref[pl.ds(s, n, stride=ROW_TILE), :] for s in range(ROW_TILE)], axis=1)


def _row_copy(src, src_row, dst, dst_row, sem):
    s0 = pl.multiple_of(src_row * ROW_TILE, ROW_TILE)
    d0 = pl.multiple_of(dst_row * ROW_TILE, ROW_TILE)
    return pltpu.make_async_copy(src.at[pl.ds(s0, ROW_TILE), :], dst.at[pl.ds(d0, ROW_TILE), :], sem)


def _wait_rows(hbm, vmem, sem, to_hbm):
    rows = hbm.at[pl.ds(0, vmem.shape[0]), :]
    (pltpu.make_async_copy(vmem, rows, sem) if to_hbm else pltpu.make_async_copy(rows, vmem, sem)).wait()


def _dispatch_kernel(nt, dest_ref, pad_ref, x_ref, g_ref, buf_out, xn_s, zero_s, sem, zsem):
    i = pl.program_id(0)
    tm = x_ref.shape[0]
    slot = i % 2
    n_pad = pad_ref.shape[2]

    def drain(s):
        for _ in range(TOP_K):
            _wait_rows(buf_out, xn_s.at[s], sem.at[s], True)

    @pl.when(i >= 2)
    def _():
        drain(slot)

    zero_s[...] = jnp.zeros(zero_s.shape, F32)

    def zero(q, c):
        _row_copy(zero_s, q, buf_out, pad_ref[0, 0, q], zsem.at[0]).start()
        return c

    lax.fori_loop(0, n_pad, zero, 0, unroll=8)
    _to_tiles(xn_s.at[slot], _rms(x_ref[...], g_ref[...]))

    def start(r, c):
        for k in range(TOP_K):
            _row_copy(xn_s.at[slot], r, buf_out, dest_ref[0, 0, TOP_K * r + k],
                      sem.at[slot]).start(priority=k % DMA_QUEUES)
        return c

    lax.fori_loop(0, tm, start, 0, unroll=8)
    _wait_rows(buf_out, zero_s, zsem.at[0], True)

    @pl.when(i == nt - 1)
    def _():
        drain(slot)
        if nt > 1:
            drain(1 - slot)


def _dispatch(x, g, dest, pad, rows, tm):
    t, d = x.shape
    nt = t // tm
    n_pad = pad.shape[0] // nt
    assert n_pad * nt == pad.shape[0]
    return pl.pallas_call(
        functools.partial(_dispatch_kernel, nt),
        grid=(nt,),
        in_specs=[pl.BlockSpec((1, 1, TOP_K * tm), lambda i: (i, 0, 0), memory_space=pltpu.SMEM),
                  pl.BlockSpec((1, 1, n_pad), lambda i: (i, 0, 0), memory_space=pltpu.SMEM),
                  pl.BlockSpec((tm, d), lambda i: (i, 0)),
                  pl.BlockSpec((1, d), lambda i: (0, 0))],
        out_specs=pl.BlockSpec(memory_space=pl.ANY),
        out_shape=jax.ShapeDtypeStruct((rows * ROW_TILE, LANES), F32),
        scratch_shapes=[pltpu.VMEM((2, tm * ROW_TILE, LANES), F32),
                        pltpu.VMEM((n_pad * ROW_TILE, LANES), F32),
                        pltpu.SemaphoreType.DMA((2,)), pltpu.SemaphoreType.DMA((1,))],
        compiler_params=_cparams(("arbitrary",)),
        name="moe_dispatch",
    )(dest.reshape(nt, 1, TOP_K * tm), pad.reshape(nt, 1, n_pad), x, g.reshape(1, d))


def _expert_kernel(be_ref, nu_ref, x_ref, w1_ref, w3_ref, w2_ref, y_ref):
    del be_ref
    used = pl.program_id(0) < nu_ref[0]

    @pl.when(jnp.logical_not(used))
    def _():
        y_ref[...] = jnp.zeros(y_ref.shape, F32)

    @pl.when(used)
    def _():
        x = _from_tiles(x_ref, x_ref.shape[0] // ROW_TILE).astype(BF16)
        y = _swiglu(x, lambda c: w1_ref[0, :, c], lambda c: w3_ref[0, :, c], lambda c: w2_ref[0, c, :])
        _to_tiles(y_ref, y)


def _experts(buf, blk_e, n_used, w1, w3, w2, blk):
    d, f = w1.shape[1], w1.shape[2]
    once = pl.Buffered(2)
    grid_spec = pltpu.PrefetchScalarGridSpec(
        num_scalar_prefetch=2,
        grid=(buf.shape[0] // (blk * ROW_TILE),),
        in_specs=[pl.BlockSpec((blk * ROW_TILE, LANES), lambda b, be, nu: (b, 0)),
                  pl.BlockSpec((1, d, f), lambda b, be, nu: (be[b], 0, 0), pipeline_mode=once),
                  pl.BlockSpec((1, d, f), lambda b, be, nu: (be[b], 0, 0), pipeline_mode=once),
                  pl.BlockSpec((1, f, d), lambda b, be, nu: (be[b], 0, 0), pipeline_mode=once)],
        out_specs=pl.BlockSpec((blk * ROW_TILE, LANES), lambda b, be, nu: (b, 0)))
    return pl.pallas_call(
        _expert_kernel,
        grid_spec=grid_spec,
        out_shape=jax.ShapeDtypeStruct(buf.shape, F32),
        compiler_params=_cparams(("arbitrary",)),
        name="moe_experts",
    )(blk_e, n_used, buf, w1, w3, w2)


def _combine_kernel(nt, dcur_ref, dnxt_ref, x_ref, rt_ref, fn_ref, y_hbm, o_ref, ya, yb, sem):
    i = pl.program_id(0)
    tm = x_ref.shape[0]
    slot = i % 2

    def gather(dref, s):
        def start(r, c):
            _row_copy(y_hbm, dref[0, 0, TOP_K * r], ya.at[s], r, sem.at[s, 0]).start(priority=0)
            _row_copy(y_hbm, dref[0, 0, TOP_K * r + 1], yb.at[s], r, sem.at[s, 1]).start(
                priority=1 % DMA_QUEUES)
            return c

        lax.fori_loop(0, tm, start, 0, unroll=8)

    @pl.when(i == 0)
    def _():
        gather(dcur_ref, slot)

    @pl.when(i + 1 < nt)
    def _():
        gather(dnxt_ref, 1 - slot)

    _wait_rows(y_hbm, ya.at[slot], sem.at[slot, 0], False)
    _wait_rows(y_hbm, yb.at[slot], sem.at[slot, 1], False)
    rt = rt_ref[...]
    lane = lax.broadcasted_iota(jnp.int32, rt.shape, 1)
    g1 = jnp.sum(jnp.where(lane == 2, rt, 0.0), axis=-1, keepdims=True)
    g2 = jnp.sum(jnp.where(lane == 3, rt, 0.0), axis=-1, keepdims=True)
    moe = g1 * _from_tiles(ya.at[slot], tm) + g2 * _from_tiles(yb.at[slot], tm)
    o_ref[...] = _rms(x_ref[...] + moe, fn_ref[...])


def _combine(x, route, fnorm, y, dest, tm):
    t, d = x.shape
    nt = t // tm
    dest3 = dest.reshape(nt, 1, TOP_K * tm)
    return pl.pallas_call(
        functools.partial(_combine_kernel, nt),
        grid=(nt,),
        in_specs=[pl.BlockSpec((1, 1, TOP_K * tm), lambda i: (i, 0, 0), memory_space=pltpu.SMEM),
                  pl.BlockSpec((1, 1, TOP_K * tm), lambda i: (jnp.minimum(i + 1, nt - 1), 0, 0),
                               memory_space=pltpu.SMEM),
                  pl.BlockSpec((tm, d), lambda i: (i, 0)),
                  pl.BlockSpec((tm, LANES), lambda i: (i, 0)),
                  pl.BlockSpec((1, d), lambda i: (0, 0)),
                  pl.BlockSpec(memory_space=pl.ANY)],
        out_specs=pl.BlockSpec((tm, d), lambda i: (i, 0)),
        out_shape=jax.ShapeDtypeStruct((t, d), F32),
        scratch_shapes=[pltpu.VMEM((2, tm * ROW_TILE, LANES), F32),
                        pltpu.VMEM((2, tm * ROW_TILE, LANES), F32),
                        pltpu.SemaphoreType.DMA((2, 2))],
        compiler_params=_cparams(("arbitrary",)),
        name="moe_combine",
    )(dest3, dest3, x, route, fnorm.reshape(1, d), y)


def _moe(x, route, g, w1, w3, w2, fnorm, tm):
    blk = min(MOE_BLOCK, x.shape[0])
    dest, pad, blk_e, n_used, nblk = _route_plan(route, blk)
    buf = _dispatch(x, g, dest, pad, nblk * blk, tm)
    y = _experts(buf, blk_e, n_used, w1, w3, w2, blk)
    return _combine(x, route, fnorm, y, dest, tm)


def _block_diag(w):
    g, a, b = w.shape
    eye = jnp.eye(g, dtype=w.dtype)
    return (eye[:, None, :, None] * w[:, :, None, :]).reshape(g * a, g * b)


def kernel(x, l0_norm_mix, l0_w_in, l0_conv_w, l0_conv_b, l0_gate_a_w, l0_gate_a_b, l0_gate_x_w, l0_gate_x_b, l0_lru_lambda, l0_shift_mu, l0_w0, l0_w_up, l0_a0, l0_a_up, l0_g_up, l0_k_k, l0_k_a, l0_r_k, l0_ln_x_w, l0_ln_x_b, l0_w_out, l0_norm_ffn, l0_ffn_w1, l0_ffn_w3, l0_ffn_w2, l1_norm_mix, l1_w_in, l1_conv_w, l1_conv_b, l1_wq, l1_wk, l1_wv, l1_ig_b, l1_fg_b, l1_mh_norm_w, l1_w_out, l1_norm_ffn, l1_router, l1_moe_w1, l1_moe_w3, l1_moe_w2, final_norm):
    batch, seq, d = x.shape
    t = batch * seq
    xt = x.reshape(t, d)
    tm = min(512, t)

    w_in0 = l0_w_in.astype(BF16)
    p_lru, p_rw = _norm_mm(xt, l0_norm_mix, w_in0, (2 * LRU_WIDTH, RW_IN), tm, "l0_in")
    y_lru = _lru(p_lru, batch, seq, l0_conv_w, l0_conv_b,
                 _block_diag(l0_gate_a_w).astype(BF16), l0_gate_a_b,
                 _block_diag(l0_gate_x_w).astype(BF16), l0_gate_x_b, l0_lru_lambda,
                 min(512, seq))
    y_rw = _rwkv(p_rw, batch, seq, l0_shift_mu, l0_w0, l0_w_up, l0_a0, l0_a_up, l0_g_up,
                 l0_k_k, l0_k_a, l0_r_k.reshape(-1), l0_ln_x_w, l0_ln_x_b)
    w_out0 = l0_w_out.astype(BF16)
    ne, _, f = l1_moe_w1.shape
    x2, (mw1, mw3, mw2) = _ffn(
        xt, y_lru, y_rw, w_out0[:LRU_WIDTH], w_out0[LRU_WIDTH:], l0_norm_ffn,
        l0_ffn_w1.astype(BF16), l0_ffn_w3.astype(BF16), l0_ffn_w2.astype(BF16),
        [l1_moe_w1.reshape(ne * d, f), l1_moe_w3.reshape(ne * d, f), l1_moe_w2.reshape(ne * f, d)], tm)

    n_in1 = 2 * MLSTM_WIDTH + LANES
    w_in1 = jnp.zeros((d, n_in1), F32).at[:, :l1_w_in.shape[1]].set(l1_w_in).astype(BF16)
    (p1,) = _norm_mm(x2, l1_norm_mix, w_in1, (n_in1,), tm, "l1_in")
    h1 = _mlstm(p1, batch, seq, l1_conv_w, l1_conv_b, l1_wq, l1_wk, l1_wv, l1_ig_b, l1_fg_b,
                l1_mh_norm_w)
    router = jnp.zeros((d, LANES), F32).at[:, :N_EXPERTS].set(l1_router)
    x3, route = _out_router(x2, h1, l1_w_out.astype(BF16), l1_norm_ffn, router, tm)
    out = _moe(x3, route, l1_norm_ffn, mw1.reshape(ne, d, f), mw3.reshape(ne, d, f),
               mw2.reshape(ne, f, d), final_norm, tm)
    return out.reshape(batch, seq, d)
```

```python
import functools

import jax
import jax.numpy as jnp
from jax import lax
from jax.experimental import pallas as pl
from jax.experimental.pallas import tpu as pltpu

F32 = jnp.float32
BF16 = jnp.bfloat16

D_MODEL = 1024
LRU_WIDTH = 512
LRU_BLOCKS = 8
LRU_C = 8.0
LRU_SEGS = 8
CONV_WIDTH = 4
RWKV_HEADS = 8
RWKV_HEAD = 64
RWKV_WIDTH = 512
LN_X_EPS = 1e-5 * RWKV_HEAD
MLSTM_HEADS = 8
MLSTM_HEAD = 128
MLSTM_WIDTH = 1024
MLSTM_CHUNK = 128
MLSTM_HEAD_GROUP = 8
D_FF = 2816
N_EXPERTS = 8
TOP_K = 2
DMA_QUEUES = 2
ROUTER_SPLIT = 4
MOE_BLOCK = 512
FF_CHUNKS = (768, 768, 768, 512)
RMS_EPS = 1e-6
RW_IN = 3 * RWKV_WIDTH + 64 + 64 + 128
RW_CHUNK = 64
RW_STEP_ROWS = 512
RW_WAVE_ROWS = 256
LANES = 128
MXU_WIDTH = 256
CARRY_ROWS = 8
ROW_TILE = 8
VMEM_LIMIT = 56 * 1024 * 1024


def _cparams(sem):
    return pltpu.CompilerParams(dimension_semantics=sem, vmem_limit_bytes=VMEM_LIMIT)


def _rms(x, w):
    return x * lax.rsqrt(jnp.mean(x * x, axis=-1, keepdims=True) + RMS_EPS) * w


def _sigmoid(x):
    return 1.0 / (1.0 + jnp.exp(-x))


def _softplus(x):
    return jnp.maximum(x, 0.0) + jnp.log(1.0 + jnp.exp(-jnp.abs(x)))


def _dot(a, b):
    return jnp.dot(a, b, preferred_element_type=F32)


def _dot_nt(a, b):
    return lax.dot_general(a, b, (((1,), (1,)), ((), ())), preferred_element_type=F32)


def _dot_tn(a, b):
    return lax.dot_general(a, b, (((0,), (0,)), ((), ())), preferred_element_type=F32)


def _split2(x):
    hi = x.astype(BF16)
    lo = (x - hi.astype(F32)).astype(BF16)
    return hi, lo


def _split3(x):
    hi = x.astype(BF16)
    r = x - hi.astype(F32)
    mid = r.astype(BF16)
    lo = (r - mid.astype(F32)).astype(BF16)
    return hi, mid, lo


def _mm3(fn, a, b):
    ah, al = _split2(a)
    bh, bl = _split2(b)
    return fn(ah, bh) + fn(al, bh) + fn(ah, bl)


def _mm_exact_lhs(fn, a_bf16, b):
    h, m, l = _split3(b)
    return fn(a_bf16, h) + fn(a_bf16, m) + fn(a_bf16, l)


def _seg_sum(x, seg):
    s = seg.shape[0]
    outs = []
    for c in range(x.shape[1] // s):
        hi, lo = _split2(x[:, c * s:(c + 1) * s])
        outs.append(_dot(hi, seg) + _dot(lo, seg))
    return jnp.concatenate(outs, axis=1)


def _shift_hist(buf_ref, x, first):
    n = x.shape[0]

    @pl.when(first)
    def _():
        buf_ref[0:CARRY_ROWS, :] = jnp.zeros((CARRY_ROWS, x.shape[1]), F32)

    @pl.when(jnp.logical_not(first))
    def _():
        buf_ref[0:CARRY_ROWS, :] = buf_ref[n:n + CARRY_ROWS, :]

    buf_ref[CARRY_ROWS:CARRY_ROWS + n, :] = x


def _causal_conv(buf_ref, n, w, b, rotate):
    if rotate:
        full = buf_ref[0:CARRY_ROWS + n, :]
        acc = b + w[CONV_WIDTH - 1:CONV_WIDTH, :] * full[CARRY_ROWS:, :]
        for lag in range(1, CONV_WIDTH):
            tap = CONV_WIDTH - 1 - lag
            acc = acc + w[tap:tap + 1, :] * pltpu.roll(full, lag, axis=0)[CARRY_ROWS:, :]
        return acc
    acc = b
    for j in range(CONV_WIDTH):
        off = CARRY_ROWS - (CONV_WIDTH - 1) + j
        acc = acc + w[j:j + 1, :] * buf_ref[off:off + n, :]
    return acc


def _norm_mm_kernel(x_ref, g_ref, w_ref, *o_refs):
    xn = _rms(x_ref[...], g_ref[...]).astype(BF16)
    c0 = 0
    for o_ref in o_refs:
        c1 = c0 + o_ref.shape[1]
        o_ref[...] = _dot(xn, w_ref[:, c0:c1]).astype(o_ref.dtype)
        c0 = c1


def _norm_mm(x, g, w, widths, tm, name):
    t, d = x.shape
    n = w.shape[1]
    assert sum(widths) == n
    return pl.pallas_call(
        _norm_mm_kernel,
        grid=(t // tm,),
        in_specs=[pl.BlockSpec((tm, d), lambda i: (i, 0)),
                  pl.BlockSpec((1, d), lambda i: (0, 0)),
                  pl.BlockSpec((d, n), lambda i: (0, 0))],
        out_specs=[pl.BlockSpec((tm, c), lambda i: (i, 0)) for c in widths],
        out_shape=[jax.ShapeDtypeStruct((t, c), BF16) for c in widths],
        compiler_params=_cparams(("parallel",)),
        name=name,
    )(x, g.reshape(1, d), w)


def _lru_kernel(p_ref, cw_ref, cb_ref, wa_ref, ba_ref, wx_ref, bx_ref, lam_ref, o_ref,
                xbuf, abuf, bbuf, hbuf, hcar):
    i = pl.program_id(1)
    n = p_ref.shape[0]
    first = i == 0
    _shift_hist(xbuf, p_ref[:, 0:LRU_WIDTH].astype(F32), first)
    xc = _causal_conv(xbuf, n, cw_ref[...], cb_ref[...], True)
    xcb = xc.astype(BF16)
    r = _sigmoid(_dot(xcb, wa_ref[...]) + ba_ref[...])
    ig = _sigmoid(_dot(xcb, wx_ref[...]) + bx_ref[...])
    log_a = (-LRU_C) * r * _softplus(-lam_ref[...])
    a = jnp.exp(log_a)
    mult = jnp.sqrt(1.0 - a * a)
    row = lax.broadcasted_iota(jnp.int32, (n, 1), 0)
    mult = jnp.where(jnp.logical_and(first, row == 0), 1.0, mult)
    bvals = mult * ig * xc
    n_slab = LRU_WIDTH // LANES
    slabs = [slice(c * LANES, (c + 1) * LANES) for c in range(n_slab)]
    seg = n // LRU_SEGS
    pitch = _lru_pitch(n)
    for c, cs in enumerate(slabs):
        for s in range(LRU_SEGS):
            abuf[c, s * pitch:s * pitch + seg, :] = a[s * seg:(s + 1) * seg, cs]
            bbuf[c, s * pitch:s * pitch + seg, :] = bvals[s * seg:(s + 1) * seg, cs]

    @pl.when(first)
    def _():
        hcar[...] = jnp.zeros(hcar.shape, F32)

    def step(j, carry):
        rows = pl.ds(j, LRU_SEGS, stride=pitch)
        out = []
        for c in range(n_slab):
            h, acc = carry[c]
            a_j = abuf[c, rows, :]
            h = a_j * h + bbuf[c, rows, :]
            acc = acc * a_j
            hbuf[c, rows, :] = h
            abuf[c, rows, :] = acc
            out.append((h, acc))
        return tuple(out)

    init = tuple((jnp.zeros((LRU_SEGS, LANES), F32), jnp.ones((LRU_SEGS, LANES), F32))
                 for _ in range(n_slab))
    ends = lax.fori_loop(0, seg, step, init, unroll=8)
    cols = []
    for c, cs in enumerate(slabs):
        h_end, a_end = ends[c]
        h0 = hcar[0:1, cs]
        parts = []
        for s in range(LRU_SEGS):
            rows = slice(s * pitch, s * pitch + seg)
            parts.append(hbuf[c, rows, :] + abuf[c, rows, :] * h0)
            h0 = h_end[s:s + 1, :] + a_end[s:s + 1, :] * h0
        hcar[0:1, cs] = h0
        cols.append(jnp.concatenate(parts, axis=0))
    gate = p_ref[:, LRU_WIDTH:2 * LRU_WIDTH].astype(F32)
    gelu = 0.5 * gate * (1.0 + jnp.tanh(0.7978845608028654 * (gate + 0.044715 * gate * gate * gate)))
    o_ref[...] = (jnp.concatenate(cols, axis=1) * gelu).astype(o_ref.dtype)


def _lru_pitch(n):
    tiles = n // LRU_SEGS // CARRY_ROWS
    return (tiles + 1 - tiles % 2) * CARRY_ROWS


def _lru(p_lru, batch, seq, conv_w, conv_b, wa, ba, wx, bx, lam, tb):
    nb = seq // tb
    c = LRU_WIDTH
    vec = lambda: pl.BlockSpec((1, c), lambda b, i: (0, 0))
    return pl.pallas_call(
        _lru_kernel,
        grid=(batch, nb),
        in_specs=[pl.BlockSpec((tb, 2 * c), lambda b, i: (b * nb + i, 0)),
                  pl.BlockSpec((CONV_WIDTH, c), lambda b, i: (0, 0)), vec(),
                  pl.BlockSpec((c, c), lambda b, i: (0, 0)), vec(),
                  pl.BlockSpec((c, c), lambda b, i: (0, 0)), vec(), vec()],
        out_specs=pl.BlockSpec((tb, c), lambda b, i: (b * nb + i, 0)),
        out_shape=jax.ShapeDtypeStruct((batch * seq, c), BF16),
        scratch_shapes=[pltpu.VMEM((tb + CARRY_ROWS, c), F32)]
        + [pltpu.VMEM((c // LANES, LRU_SEGS * _lru_pitch(tb), LANES), F32)] * 3
        + [pltpu.VMEM((CARRY_ROWS, c), F32)],
        compiler_params=_cparams(("parallel", "arbitrary")),
        name="rg_lru",
    )(p_lru, conv_w, conv_b.reshape(1, c), wa, ba.reshape(1, c), wx, bx.reshape(1, c),
      lam.reshape(1, c))


def _blk(x, masks):
    xb = x.astype(BF16)
    return jnp.concatenate([jnp.where(mk, xb, jnp.zeros_like(xb)) for mk in masks], axis=0)


def _rwkv_kernel(p_ref, mu_ref, w0_ref, wup_ref, a0_ref, aup_ref, gup_ref, kk_ref, ka_ref,
                 rk_ref, lnw_ref, lnb_ref, seg_ref, tri_ref, o_ref, pbuf, state):
    L = RW_CHUNK
    W = RWKV_WIDTH
    rows = p_ref.shape[0]
    wave = tri_ref.shape[0]
    first = pl.program_id(1) == 0
    p = p_ref[...].astype(F32)
    _shift_hist(pbuf, p, first)
    prev = pbuf[CARRY_ROWS - 1:CARRY_ROWS - 1 + rows, :]
    ps = p + mu_ref[...] * (prev - p)
    seg = seg_ref[...]
    tri = tri_ref[...]

    @pl.when(first)
    def _():
        state[...] = jnp.zeros(state.shape, F32)

    lane = lax.broadcasted_iota(jnp.int32, (1, LANES), 1)
    m1 = [lane < RWKV_HEAD, lane >= RWKV_HEAD]
    m2 = [jnp.concatenate([mk, mk], axis=1) for mk in m1]
    ti = lax.broadcasted_iota(jnp.int32, (L, LANES), 0)
    si = lax.broadcasted_iota(jnp.int32, (L, LANES), 1) % RWKV_HEAD
    strict = si < ti
    incl = si <= ti
    bi = lax.broadcasted_iota(jnp.int32, (LANES, LANES), 0) // RWKV_HEAD
    bj = lax.broadcasted_iota(jnp.int32, (LANES, LANES), 1) // RWKV_HEAD
    bd = bi == bj

    def tril(mask, s):
        return jnp.where(mask, s, 0.0).astype(BF16)

    wv = [slice(w * wave, (w + 1) * wave) for w in range(rows // wave)]
    ws = range(len(wv))
    r = [ps[rw, 0:W] for rw in wv]
    k = [ps[rw, W:2 * W] for rw in wv]
    v = [ps[rw, 2 * W:3 * W] for rw in wv]
    x2 = [ps[rw, 3 * W:3 * W + LANES] for rw in wv]
    xg = [ps[rw, 3 * W + LANES:3 * W + 2 * LANES] for rw in wv]
    wl = [w0_ref[...] + _dot(jnp.tanh(x2[w]).astype(BF16), wup_ref[...]) for w in ws]
    a = [_sigmoid(a0_ref[...] + _dot(x2[w].astype(BF16), aup_ref[...])) for w in ws]
    g = [_dot(_sigmoid(xg[w]).astype(BF16), gup_ref[...]) for w in ws]
    kk = [k[w] * kk_ref[...] for w in ws]
    kk = [kk[w] / jnp.maximum(jnp.sqrt(_seg_sum(kk[w] * kk[w], seg)), 1e-12) for w in ws]
    lw = [-jnp.exp(-_softplus(-wl[w]) - 0.5) for w in ws]
    cw = [_mm_exact_lhs(_dot, tri, lw[w]) for w in ws]

    n_pairs = RWKV_HEADS // 2
    n_ch = wave // L
    s_cur = [state[pr] for pr in range(n_pairs)]
    for w in ws:
        cw_end = jnp.concatenate(
            [jnp.broadcast_to(cw[w][c * L + L - 1:c * L + L, :], (L, W)) for c in range(n_ch)], axis=0)
        w_in = jnp.exp(cw[w])
        w_inv = jnp.exp(-cw[w])
        w_end = jnp.exp(cw_end - cw[w])
        k2 = k[w] * (1.0 + (a[w] - 1.0) * ka_ref[...])
        kka = kk[w] * a[w]
        a_t = -kk[w] * jnp.exp(cw[w] - lw[w])
        b_t = kka * w_inv
        k_t = k2 * w_inv
        r_t = r[w] * w_in
        b_bar = kka * w_end
        k_bar = k2 * w_end
        w_tot = jnp.exp(cw_end)
        vw = v[w]

        chains = [(slice(c * L, (c + 1) * L), slice(pr * LANES, (pr + 1) * LANES))
                  for c in range(n_ch) for pr in range(n_pairs)]
        idx = range(len(chains))
        a_c = [a_t[rs, sl] for rs, sl in chains]
        r_c = [r_t[rs, sl] for rs, sl in chains]
        ar = [jnp.concatenate([a_c[i], r_c[i]], axis=0).astype(BF16) for i in idx]
        sb = [_dot_nt(ar[i], _blk(b_t[rs, sl], m1)) for i, (rs, sl) in enumerate(chains)]
        sk = [_dot_nt(ar[i], _blk(k_t[rs, sl], m1)) for i, (rs, sl) in enumerate(chains)]
        vblk = [_blk(vw[rs, sl], m1) for rs, sl in chains]
        rhs = [jnp.concatenate([a_c[i], _dot(tril(strict, sk[i][0:L]), vblk[i])], axis=1) for i in idx]
        pw = [tril(strict, sb[i][0:L]) for i in idx]
        tinv = [jnp.where(si == ti, 1.0, 0.0) + pw[i].astype(F32) for i in idx]
        pw = [_dot(pw[i], _blk(pw[i], m1)).astype(BF16) for i in idx]
        for _ in range(4):
            both = [_dot(pw[i], jnp.concatenate([_blk(pw[i], m1), _blk(tinv[i], m1)], axis=1)) for i in idx]
            tinv = [tinv[i] + both[i][:, LANES:2 * LANES] for i in idx]
            pw = [both[i][:, 0:LANES].astype(BF16) for i in idx]
        tinv = [tinv[i] + _dot(pw[i], _blk(tinv[i], m1)) for i in idx]
        x = [_dot(tinv[i].astype(BF16), _blk(rhs[i], m2)) for i in idx]
        corr = [_dot(tril(incl, sb[i][L:2 * L]), _blk(x[i], m2)) for i in idx]
        r_hat = [(r_c[i] + corr[i][:, 0:LANES]).astype(BF16) for i in idx]
        y0 = [corr[i][:, LANES:2 * LANES] + _dot(tril(incl, sk[i][L:2 * L]), vblk[i]) for i in idx]
        xb = [x[i].astype(BF16) for i in idx]
        bb = [b_bar[rs, sl].astype(BF16) for rs, sl in chains]
        gmat = [jnp.where(bd, _dot_tn(xb[i][:, 0:LANES], bb[i]), 0.0).astype(BF16) for i in idx]
        hmat = [jnp.where(bd, _dot_tn(xb[i][:, LANES:2 * LANES], bb[i])
                          + _dot_tn(vw[rs, sl].astype(BF16), k_bar[rs, sl].astype(BF16)), 0.0)
                for i, (rs, sl) in enumerate(chains)]

        y_rows = []
        for c in range(n_ch):
            ys = []
            for pr in range(n_pairs):
                i = c * n_pairs + pr
                s0 = s_cur[pr]
                s0b = s0.astype(BF16)
                ys.append(_dot_nt(r_hat[i], s0b) + y0[i])
                s_cur[pr] = s0 * w_tot[c * L:c * L + 1, chains[i][1]] + _dot(s0b, gmat[i]) + hmat[i]
            y_rows.append(jnp.concatenate(ys, axis=1))
        y = jnp.concatenate(y_rows, axis=0)

        inv = 1.0 / RWKV_HEAD
        mean = _seg_sum(y, seg) * inv
        yc = y - mean
        var = _seg_sum(yc * yc, seg) * inv
        yn = yc * lax.rsqrt(var + LN_X_EPS) * lnw_ref[...] + lnb_ref[...]
        bonus = _seg_sum(r[w] * k2 * rk_ref[...], seg) * vw
        o_ref[wv[w], :] = ((yn + bonus) * g[w]).astype(o_ref.dtype)
    for pr in range(n_pairs):
        state[pr] = s_cur[pr]


def _rwkv(p_rw, batch, seq, mu, w0, w_up, a0, a_up, g_up, k_k, k_a, r_k, ln_w, ln_b):
    L = min(RW_STEP_ROWS, seq)
    nc = seq // L
    W = RWKV_WIDTH
    wup = jnp.zeros((LANES, W), F32).at[0:64].set(w_up).astype(BF16)
    aup = jnp.zeros((LANES, W), F32).at[64:128].set(a_up).astype(BF16)
    hid = jnp.arange(MXU_WIDTH) // RWKV_HEAD
    seg = (hid[:, None] == hid[None, :]).astype(BF16)
    wave = min(RW_WAVE_ROWS, L)
    ri = jnp.arange(wave)
    tri = jnp.logical_and(ri[:, None] >= ri[None, :],
                          ri[:, None] // RW_CHUNK == ri[None, :] // RW_CHUNK).astype(BF16)
    vec = lambda: pl.BlockSpec((1, W), lambda b, i: (0, 0))
    mat = lambda s: pl.BlockSpec(s, lambda b, i: (0, 0))
    return pl.pallas_call(
        _rwkv_kernel,
        grid=(batch, nc),
        in_specs=[pl.BlockSpec((L, RW_IN), lambda b, i: (b * nc + i, 0)),
                  mat((1, RW_IN)), vec(), mat((LANES, W)), vec(), mat((LANES, W)),
                  mat((LANES, W)), vec(), vec(), vec(), vec(), vec(), mat((MXU_WIDTH, MXU_WIDTH)),
                  mat((wave, wave))],
        out_specs=pl.BlockSpec((L, W), lambda b, i: (b * nc + i, 0)),
        out_shape=jax.ShapeDtypeStruct((batch * seq, W), BF16),
        scratch_shapes=[pltpu.VMEM((L + CARRY_ROWS, RW_IN), F32),
                        pltpu.VMEM((RWKV_HEADS // 2, LANES, LANES), F32)],
        compiler_params=_cparams(("parallel", "arbitrary")),
        name="rwkv7",
    )(p_rw, mu.reshape(1, RW_IN), w0.reshape(1, W), wup, a0.reshape(1, W), aup,
      g_up.astype(BF16), k_k.reshape(1, W), k_a.reshape(1, W), r_k.reshape(1, W),
      ln_w.reshape(1, W), ln_b.reshape(1, W), seg, tri)


def _swiglu(xb, w1, w3, w2):
    edges = [sum(FF_CHUNKS[:i]) for i in range(len(FF_CHUNKS) + 1)]
    cols = [slice(edges[i], edges[i + 1]) for i in range(len(FF_CHUNKS))]
    hid = (_dot(xb, w1(cols[0])), _dot(xb, w3(cols[0])))
    acc = None
    for i, c in enumerate(cols):
        nxt = (_dot(xb, w1(cols[i + 1])), _dot(xb, w3(cols[i + 1]))) if i + 1 < len(cols) else None
        act = (hid[0] * _sigmoid(hid[0]) * hid[1]).astype(BF16)
        part = _dot(act, w2(c))
        acc = part if acc is None else acc + part
        hid = nxt
    return acc


def _ffn_kernel(n_cast, x_ref, ya_ref, yb_ref, wa_ref, wb_ref, g_ref, w1_ref, w3_ref, w2_ref, *rest):
    src, o_ref, dst = rest[:n_cast], rest[n_cast], rest[n_cast + 1:]
    x1 = x_ref[...] + _dot(ya_ref[...], wa_ref[...]) + _dot(yb_ref[...], wb_ref[...])
    xn = _rms(x1, g_ref[...]).astype(BF16)
    o_ref[...] = x1 + _swiglu(xn, lambda c: w1_ref[:, c], lambda c: w3_ref[:, c], lambda c: w2_ref[c, :])
    for s_ref, d_ref in zip(src, dst):
        d_ref[...] = s_ref[...].astype(BF16)


def _ffn(x, ya, yb, wa, wb, g, w1, w3, w2, to_cast, tm):
    t, d = x.shape
    nt = t // tm
    once = pl.Buffered(1)
    fixed = lambda a: pl.BlockSpec(a.shape, lambda i: (0, 0), pipeline_mode=once)
    slab = lambda a: pl.BlockSpec((a.shape[0] // nt, a.shape[1]), lambda i: (i, 0))
    assert all(a.shape[0] % (nt * 2 * ROW_TILE) == 0 for a in to_cast)
    outs = pl.pallas_call(
        functools.partial(_ffn_kernel, len(to_cast)),
        grid=(nt,),
        in_specs=[pl.BlockSpec((tm, d), lambda i: (i, 0)),
                  pl.BlockSpec((tm, ya.shape[1]), lambda i: (i, 0)),
                  pl.BlockSpec((tm, yb.shape[1]), lambda i: (i, 0)),
                  fixed(wa), fixed(wb),
                  pl.BlockSpec((1, d), lambda i: (0, 0)),
                  fixed(w1), fixed(w3), fixed(w2)] + [slab(a) for a in to_cast],
        out_specs=[pl.BlockSpec((tm, d), lambda i: (i, 0))] + [slab(a) for a in to_cast],
        out_shape=[jax.ShapeDtypeStruct((t, d), F32)]
        + [jax.ShapeDtypeStruct(a.shape, BF16) for a in to_cast],
        compiler_params=_cparams(("parallel",)),
        name="ffn_swiglu",
    )(x, ya, yb, wa, wb, g.reshape(1, d), w1, w3, w2, *to_cast)
    return outs[0], outs[1:]


def _mlstm_in_kernel(nb, x_ref, g_ref, w_ref, cw_ref, cb_ref, wq_ref, wk_ref, wv_ref, gb_ref, tri_ref,
                     q_ref, k_ref, v_ref, z_ref, gs_ref, gtr_ref, xbuf, xm_s, z_s, gt_s):
    i = pl.program_id(0)
    n = x_ref.shape[0]
    W = MLSTM_WIDTH
    dh = MLSTM_HEAD
    L = MLSTM_CHUNK
    nh = MLSTM_HEADS
    slot = i % 2

    @pl.when(i == 0)
    def _():
        xm_s[...] = jnp.zeros(xm_s.shape, BF16)
        z_s[...] = jnp.zeros(z_s.shape, BF16)
        gt_s[...] = jnp.zeros(gt_s.shape, F32)

    xm = xm_s[1 - slot]
    z_ref[...] = z_s[1 - slot]
    gates = gt_s[1 - slot]
    _shift_hist(xbuf, xm.astype(F32), jnp.logical_or(i == 0, (i - 1) % nb == 0))

    xn = _rms(x_ref[...], g_ref[...]).astype(BF16)
    cw = cw_ref[...]
    cb = cb_ref[...]
    scale = dh ** -0.5
    heads_per = MXU_WIDTH // dh
    groups = [slice(c * MXU_WIDTH, (c + 1) * MXU_WIDTH) for c in range(W // MXU_WIDTH)]

    def qkv(c, xc):
        for u in range(heads_per):
            h = c * heads_per + u
            sl = slice(h * dh, (h + 1) * dh)
            xh = xc[:, u * dh:(u + 1) * dh]
            q_ref[:, sl] = (_dot(xh, wq_ref[h]) * scale).astype(BF16)
            k_ref[:, sl] = _dot(xh, wk_ref[h]).astype(BF16)
            v_ref[:, sl] = _dot(xm[:, sl], wv_ref[h]).astype(BF16)

    xm_new, z_new, xc_prev = [], [], None
    for c, cols in enumerate(groups):
        xm_new.append(_dot(xn, w_ref[:, cols]).astype(BF16))
        z_new.append(_dot(xn, w_ref[:, W + c * MXU_WIDTH:W + (c + 1) * MXU_WIDTH]).astype(BF16))
        acc = cb[:, cols]
        for j in range(CONV_WIDTH):
            off = CARRY_ROWS - (CONV_WIDTH - 1) + j
            acc = acc + cw[j:j + 1, cols] * xbuf[off:off + n, cols]
        if c > 0:
            qkv(c - 1, xc_prev)
        xc_prev = (acc * _sigmoid(acc)).astype(BF16)
    qkv(len(groups) - 1, xc_prev)
    gt_new = _dot(xn, w_ref[:, 2 * W:2 * W + LANES])

    lane = lax.broadcasted_iota(jnp.int32, (L, LANES), 1)
    row = lax.broadcasted_iota(jnp.int32, (L, LANES), 0)
    for c in range(n // L):
        rs = slice(c * L, (c + 1) * L)
        gl = gates[rs, :] + gb_ref[...]
        lf = jnp.minimum(gl, 0.0) - jnp.log(1.0 + jnp.exp(-jnp.abs(gl)))
        bcum = _mm_exact_lhs(_dot, tri_ref[...], lf)
        b_al = pltpu.roll(bcum, LANES - nh, axis=1)
        gmat = gl - b_al
        cmax = gmat
        d = 1
        while d < L:
            cmax = jnp.maximum(cmax, jnp.where(row >= d, pltpu.roll(cmax, d, axis=0), -jnp.inf))
            d *= 2
        gs_ref[rs, :] = jnp.where(lane < nh, gmat,
                                  jnp.where(lane < 2 * nh, pltpu.roll(cmax, nh, axis=1),
                                            pltpu.roll(b_al, 2 * nh, axis=1)))
        gtr_ref[rs, :] = gmat.T
    xm_s[slot] = jnp.concatenate(xm_new, axis=1)
    z_s[slot] = jnp.concatenate(z_new, axis=1)
    gt_s[slot] = gt_new


def _mlstm_kernel(q_ref, k_ref, v_ref, z_ref, gs_ref, gtr_ref, nw_ref, o_ref, cn_s, m_s):
    ci = pl.program_id(1)
    L = MLSTM_CHUNK
    dh = MLSTM_HEAD
    nh = MLSTM_HEADS
    first = ci == 0
    gs = gs_ref[...]
    g_t = gtr_ref[...]

    @pl.when(first)
    def _():
        cn_s[...] = jnp.zeros(cn_s.shape, F32)
        m_s[...] = jnp.zeros(m_s.shape, F32)

    ti = lax.broadcasted_iota(jnp.int32, (L, L), 0)
    si = lax.broadcasted_iota(jnp.int32, (L, L), 1)
    causal = si <= ti
    ones = jnp.ones((L, dh), BF16)
    sls = [slice(h * dh, (h + 1) * dh) for h in range(MLSTM_HEADS)]
    for g0 in range(0, MLSTM_HEADS, MLSTM_HEAD_GROUP):
        hs = range(g0, g0 + MLSTM_HEAD_GROUP)
        qb = {h: q_ref[:, sls[h]] for h in hs}
        kb = {h: k_ref[:, sls[h]] for h in hs}
        v1 = {h: jnp.concatenate([v_ref[:, sls[h]], ones], axis=1) for h in hs}
        qk = {h: _dot_nt(qb[h], kb[h]) for h in hs}
        qcn = {h: _dot(qb[h], cn_s[h].astype(BF16)) for h in hs}
        m_prev = {h: m_s[h:h + 1, 0:1] for h in hs}
        mx = {h: jnp.maximum(m_prev[h], gs[:, nh + h:nh + h + 1]) for h in hs}
        s = {h: qk[h] * jnp.exp(jnp.where(causal, g_t[h:h + 1, :] - mx[h], -jnp.inf)) for h in hs}
        sv = {h: _dot(s[h].astype(BF16), v1[h]) for h in hs}
        m_last = {h: mx[h][L - 1:L, :] for h in hs}
        kw = {h: kb[h].astype(F32) * jnp.exp(gs[:, h:h + 1] - m_last[h]) for h in hs}
        kv = {h: _dot_tn(kw[h].astype(BF16), v1[h]) for h in hs}
        sc = {h: jnp.exp(m_prev[h] - mx[h]) for h in hs}
        den = {h: sc[h] * qcn[h][:, dh:] + sv[h][:, dh:] for h in hs}
        floor = {h: jnp.exp(-(gs[:, 2 * nh + h:2 * nh + h + 1] + mx[h])) for h in hs}
        hh = {h: (sc[h] * qcn[h][:, :dh] + sv[h][:, :dh]) / jnp.maximum(jnp.abs(den[h]), floor[h]) for h in hs}
        ms = {h: jnp.mean(hh[h] * hh[h], axis=-1, keepdims=True) for h in hs}
        dec = {h: jnp.exp(m_prev[h] - m_last[h]) for h in hs}
        for h in hs:
            cn_s[h] = dec[h] * cn_s[h] + kv[h]
            m_s[h:h + 1, :] = jnp.broadcast_to(gs[L - 1:L, 2 * nh + h:2 * nh + h + 1] + m_last[h], (1, LANES))
            hn = hh[h] * lax.rsqrt(ms[h] + RMS_EPS)
            gate = _sigmoid(z_ref[:, sls[h]].astype(F32))
            o_ref[:, sls[h]] = (gate * (hn * nw_ref[:, sls[h]])).astype(o_ref.dtype)


def _mlstm(x, norm_w, w_in, batch, seq, conv_w, conv_b, wq, wk, wv, ig_b, fg_b, mh_w, tm):
    L = MLSTM_CHUNK
    nc = seq // L
    W = MLSTM_WIDTH
    t, d = x.shape
    tb = min(tm, seq)
    nb = seq // tb
    n_blk = batch * nb
    fixed = lambda shape: pl.BlockSpec(shape, lambda i: (0,) * len(shape))
    hw = lambda: fixed((MLSTM_HEADS, MLSTM_HEAD, MLSTM_HEAD))
    done = lambda width: pl.BlockSpec((tb, width), lambda i: (jnp.maximum(i - 1, 0), 0))
    gb = jnp.zeros((1, LANES), F32).at[0, 0:8].set(ig_b).at[0, 8:16].set(fg_b)
    tri = (jnp.arange(L)[:, None] >= jnp.arange(L)[None, :]).astype(BF16)
    q, k, v, z, gs, gtr = pl.pallas_call(
        functools.partial(_mlstm_in_kernel, nb),
        grid=(n_blk + 1,),
        in_specs=[pl.BlockSpec((tb, d), lambda i: (jnp.minimum(i, n_blk - 1), 0)),
                  fixed((1, d)), fixed(w_in.shape), fixed((CONV_WIDTH, W)), fixed((1, W)),
                  hw(), hw(), hw(), fixed((1, LANES)), fixed((L, L))],
        out_specs=[done(W), done(W), done(W), done(W), done(LANES), done(LANES)],
        out_shape=[jax.ShapeDtypeStruct((t, W), BF16)] * 4 + [jax.ShapeDtypeStruct((t, LANES), F32)] * 2,
        scratch_shapes=[pltpu.VMEM((tb + CARRY_ROWS, W), F32), pltpu.VMEM((2, tb, W), BF16),
                        pltpu.VMEM((2, tb, W), BF16), pltpu.VMEM((2, tb, LANES), F32)],
        compiler_params=_cparams(("arbitrary",)),
        name="l1_in_mlstm",
    )(x, norm_w.reshape(1, d), w_in, conv_w, conv_b.reshape(1, W), wq.astype(BF16), wk.astype(BF16),
      wv.astype(BF16), gb, tri)

    chunk = lambda: pl.BlockSpec((L, W), lambda b, i: (b * nc + i, 0))
    return pl.pallas_call(
        _mlstm_kernel,
        grid=(batch, nc),
        in_specs=[chunk(), chunk(), chunk(), chunk(),
                  pl.BlockSpec((L, LANES), lambda b, i: (b * nc + i, 0)),
                  pl.BlockSpec((L, LANES), lambda b, i: (b * nc + i, 0)),
                  pl.BlockSpec((1, W), lambda b, i: (0, 0))],
        out_specs=pl.BlockSpec((L, W), lambda b, i: (b * nc + i, 0)),
        out_shape=jax.ShapeDtypeStruct((t, W), BF16),
        scratch_shapes=[pltpu.VMEM((MLSTM_HEADS, MLSTM_HEAD, 2 * MLSTM_HEAD), F32),
                        pltpu.VMEM((MLSTM_HEADS, LANES), F32)],
        compiler_params=_cparams(("parallel", "arbitrary")),
        name="mlstm",
    )(q, k, v, z, gs, gtr, mh_w.reshape(1, W))


def _out_router_kernel(x_ref, y_ref, w_ref, g_ref, rt_ref, x3_ref, o_ref):
    tm = x_ref.shape[0]
    sub = tm // ROUTER_SPLIT
    rs = [slice(q * sub, (q + 1) * sub) for q in range(ROUTER_SPLIT)]
    qs = range(ROUTER_SPLIT)
    w = w_ref[...]
    x3 = [x_ref[r, :] + _dot(y_ref[r, :], w) for r in rs]
    for q in qs:
        x3_ref[rs[q], :] = x3[q]
    gw = g_ref[...]
    rt = rt_ref[...]
    logits = [_mm3(_dot, _rms(x3[q], gw), rt) for q in qs]
    lane = lax.broadcasted_iota(jnp.int32, (sub, LANES), 1)
    lg = [jnp.where(lane < N_EXPERTS, logits[q], -jnp.inf) for q in qs]
    v1 = [jnp.max(lg[q], axis=-1, keepdims=True) for q in qs]
    i1 = [jnp.min(jnp.where(lg[q] == v1[q], lane, LANES), axis=-1, keepdims=True) for q in qs]
    lg2 = [jnp.where(lane == i1[q], -jnp.inf, lg[q]) for q in qs]
    v2 = [jnp.max(lg2[q], axis=-1, keepdims=True) for q in qs]
    i2 = [jnp.min(jnp.where(lg2[q] == v2[q], lane, LANES), axis=-1, keepdims=True) for q in qs]
    for q in qs:
        ex = jnp.exp(v2[q] - v1[q])
        g1 = 1.0 / (1.0 + ex)
        g2 = ex / (1.0 + ex)
        out = jnp.where(lane == 0, i1[q].astype(F32), 0.0)
        out = jnp.where(lane == 1, i2[q].astype(F32), out)
        out = jnp.where(lane == 2, g1, out)
        o_ref[rs[q], :] = jnp.where(lane == 3, g2, out)


def _out_router(x, y, w, g, router, tm):
    t, d = x.shape
    return pl.pallas_call(
        _out_router_kernel,
        grid=(t // tm,),
        in_specs=[pl.BlockSpec((tm, d), lambda i: (i, 0)),
                  pl.BlockSpec((tm, y.shape[1]), lambda i: (i, 0)),
                  pl.BlockSpec(w.shape, lambda i: (0, 0)),
                  pl.BlockSpec((1, d), lambda i: (0, 0)),
                  pl.BlockSpec((d, LANES), lambda i: (0, 0))],
        out_specs=[pl.BlockSpec((tm, d), lambda i: (i, 0)),
                   pl.BlockSpec((tm, LANES), lambda i: (i, 0))],
        out_shape=[jax.ShapeDtypeStruct((t, d), F32), jax.ShapeDtypeStruct((t, LANES), F32)],
        compiler_params=_cparams(("parallel",)),
        name="l1_out_router",
    )(x, y, w, g.reshape(1, d), router)


def _prefix_sum(x):
    m, c = x.shape
    group = min(LANES, m)
    xg = x.reshape(m // group, group, c).astype(F32)
    gi = jnp.arange(group)
    local = jnp.einsum("ij,bjc->bic", (gi[:, None] >= gi[None, :]).astype(F32), xg,
                       precision=lax.Precision.HIGHEST)
    bi = jnp.arange(m // group)
    before = jnp.dot((bi[:, None] > bi[None, :]).astype(F32), local[:, -1, :],
                     precision=lax.Precision.HIGHEST)
    return jnp.round(local + before[:, None, :]).astype(jnp.int32).reshape(m, c)


def _route_plan(route, blk):
    e = route[:, 0:TOP_K].astype(jnp.int32).reshape(-1)
    m = e.shape[0]
    onehot = (e[:, None] == jnp.arange(N_EXPERTS, dtype=jnp.int32)[None, :]).astype(jnp.int32)
    csum = _prefix_sum(onehot)
    rank = jnp.sum((csum - onehot) * onehot, axis=1)
    nb = (csum[-1] + blk - 1) // blk
    bend = _prefix_sum(nb[:, None])[:, 0]
    dest = jnp.sum(onehot * (bend - nb)[None, :], axis=1) * blk + rank
    nblk = m // blk + N_EXPERTS
    n_used = bend[-1]
    bidx = jnp.minimum(jnp.arange(nblk, dtype=jnp.int32), n_used - 1)
    blk_e = jnp.minimum(jnp.sum(bidx[:, None] >= bend[None, :], axis=1), N_EXPERTS - 1).astype(jnp.int32)
    gap_start = jnp.concatenate([(bend - nb) * blk + csum[-1], (n_used * blk)[None]])
    gap_len = jnp.concatenate([nb * blk - csum[-1], ((nblk - n_used) * blk)[None]])
    gap_end = _prefix_sum(gap_len[:, None])[:, 0]
    q = jnp.arange(nblk * blk - m, dtype=jnp.int32)
    gi = jnp.sum(q[:, None] >= gap_end[None, :], axis=1)
    pad = gap_start[gi] + q - (gap_end - gap_len)[gi]
    return dest.astype(jnp.int32), pad.astype(jnp.int32), blk_e, n_used.reshape(1).astype(jnp.int32), nblk


def _to_tiles(ref, x):
    n = x.shape[0]
    for s in range(ROW_TILE):
        ref[pl.ds(s, n, stride=ROW_TILE), :] = x[:, s * LANES:(s + 1) * LANES]


def _from_tiles(ref, n):
    return jnp.concatenate([ref[pl.ds(s, n, stride=ROW_TILE), :] for s in range(ROW_TILE)], axis=1)


def _row_copy(src, src_row, dst, dst_row, sem):
    s0 = pl.multiple_of(src_row * ROW_TILE, ROW_TILE)
    d0 = pl.multiple_of(dst_row * ROW_TILE, ROW_TILE)
    return pltpu.make_async_copy(src.at[pl.ds(s0, ROW_TILE), :], dst.at[pl.ds(d0, ROW_TILE), :], sem)


def _wait_rows(hbm, vmem, sem, to_hbm):
    rows = hbm.at[pl.ds(0, vmem.shape[0]), :]
    (pltpu.make_async_copy(vmem, rows, sem) if to_hbm else pltpu.make_async_copy(rows, vmem, sem)).wait()


def _dispatch_kernel(nt, dest_ref, pad_ref, x_ref, g_ref, buf_out, xn_s, zero_s, sem, zsem):
    i = pl.program_id(0)
    tm = x_ref.shape[0]
    slot = i % 2
    n_pad = pad_ref.shape[2]

    def drain(s):
        for _ in range(TOP_K):
            _wait_rows(buf_out, xn_s.at[s], sem.at[s], True)

    @pl.when(i >= 2)
    def _():
        drain(slot)

    zero_s[...] = jnp.zeros(zero_s.shape, F32)

    def zero(q, c):
        _row_copy(zero_s, q, buf_out, pad_ref[0, 0, q], zsem.at[0]).start()
        return c

    lax.fori_loop(0, n_pad, zero, 0, unroll=8)
    _to_tiles(xn_s.at[slot], _rms(x_ref[...], g_ref[...]))

    def start(r, c):
        for k in range(TOP_K):
            _row_copy(xn_s.at[slot], r, buf_out, dest_ref[0, 0, TOP_K * r + k],
                      sem.at[slot]).start(priority=k % DMA_QUEUES)
        return c

    lax.fori_loop(0, tm, start, 0, unroll=8)
    _wait_rows(buf_out, zero_s, zsem.at[0], True)

    @pl.when(i == nt - 1)
    def _():
        drain(slot)
        if nt > 1:
            drain(1 - slot)


def _dispatch(x, g, dest, pad, rows, tm):
    t, d = x.shape
    nt = t // tm
    n_pad = pad.shape[0] // nt
    assert n_pad * nt == pad.shape[0]
    return pl.pallas_call(
        functools.partial(_dispatch_kernel, nt),
        grid=(nt,),
        in_specs=[pl.BlockSpec((1, 1, TOP_K * tm), lambda i: (i, 0, 0), memory_space=pltpu.SMEM),
                  pl.BlockSpec((1, 1, n_pad), lambda i: (i, 0, 0), memory_space=pltpu.SMEM),
                  pl.BlockSpec((tm, d), lambda i: (i, 0)),
                  pl.BlockSpec((1, d), lambda i: (0, 0))],
        out_specs=pl.BlockSpec(memory_space=pl.ANY),
        out_shape=jax.ShapeDtypeStruct((rows * ROW_TILE, LANES), F32),
        scratch_shapes=[pltpu.VMEM((2, tm * ROW_TILE, LANES), F32),
                        pltpu.VMEM((n_pad * ROW_TILE, LANES), F32),
                        pltpu.SemaphoreType.DMA((2,)), pltpu.SemaphoreType.DMA((1,))],
        compiler_params=_cparams(("arbitrary",)),
        name="moe_dispatch",
    )(dest.reshape(nt, 1, TOP_K * tm), pad.reshape(nt, 1, n_pad), x, g.reshape(1, d))


def _expert_kernel(be_ref, nu_ref, x_ref, w1_ref, w3_ref, w2_ref, y_ref):
    del be_ref
    used = pl.program_id(0) < nu_ref[0]

    @pl.when(jnp.logical_not(used))
    def _():
        y_ref[...] = jnp.zeros(y_ref.shape, F32)

    @pl.when(used)
    def _():
        x = _from_tiles(x_ref, x_ref.shape[0] // ROW_TILE).astype(BF16)
        y = _swiglu(x, lambda c: w1_ref[0, :, c], lambda c: w3_ref[0, :, c], lambda c: w2_ref[0, c, :])
        _to_tiles(y_ref, y)


def _experts(buf, blk_e, n_used, w1, w3, w2, blk):
    d, f = w1.shape[1], w1.shape[2]
    once = pl.Buffered(2)
    grid_spec = pltpu.PrefetchScalarGridSpec(
        num_scalar_prefetch=2,
        grid=(buf.shape[0] // (blk * ROW_TILE),),
        in_specs=[pl.BlockSpec((blk * ROW_TILE, LANES), lambda b, be, nu: (b, 0)),
                  pl.BlockSpec((1, d, f), lambda b, be, nu: (be[b], 0, 0), pipeline_mode=once),
                  pl.BlockSpec((1, d, f), lambda b, be, nu: (be[b], 0, 0), pipeline_mode=once),
                  pl.BlockSpec((1, f, d), lambda b, be, nu: (be[b], 0, 0), pipeline_mode=once)],
        out_specs=pl.BlockSpec((blk * ROW_TILE, LANES), lambda b, be, nu: (b, 0)))
    return pl.pallas_call(
        _expert_kernel,
        grid_spec=grid_spec,
        out_shape=jax.ShapeDtypeStruct(buf.shape, F32),
        compiler_params=_cparams(("arbitrary",)),
        name="moe_experts",
    )(blk_e, n_used, buf, w1, w3, w2)


def _combine_kernel(nt, dcur_ref, dnxt_ref, x_ref, rt_ref, fn_ref, y_hbm, o_ref, ya, yb, sem):
    i = pl.program_id(0)
    tm = x_ref.shape[0]
    slot = i % 2

    def gather(dref, s):
        def start(r, c):
            _row_copy(y_hbm, dref[0, 0, TOP_K * r], ya.at[s], r, sem.at[s, 0]).start(priority=0)
            _row_copy(y_hbm, dref[0, 0, TOP_K * r + 1], yb.at[s], r, sem.at[s, 1]).start(
                priority=1 % DMA_QUEUES)
            return c

        lax.fori_loop(0, tm, start, 0, unroll=8)

    @pl.when(i == 0)
    def _():
        gather(dcur_ref, slot)

    @pl.when(i + 1 < nt)
    def _():
        gather(dnxt_ref, 1 - slot)

    _wait_rows(y_hbm, ya.at[slot], sem.at[slot, 0], False)
    _wait_rows(y_hbm, yb.at[slot], sem.at[slot, 1], False)
    rt = rt_ref[...]
    lane = lax.broadcasted_iota(jnp.int32, rt.shape, 1)
    g1 = jnp.sum(jnp.where(lane == 2, rt, 0.0), axis=-1, keepdims=True)
    g2 = jnp.sum(jnp.where(lane == 3, rt, 0.0), axis=-1, keepdims=True)
    moe = g1 * _from_tiles(ya.at[slot], tm) + g2 * _from_tiles(yb.at[slot], tm)
    o_ref[...] = _rms(x_ref[...] + moe, fn_ref[...])


def _combine(x, route, fnorm, y, dest, tm):
    t, d = x.shape
    nt = t // tm
    dest3 = dest.reshape(nt, 1, TOP_K * tm)
    return pl.pallas_call(
        functools.partial(_combine_kernel, nt),
        grid=(nt,),
        in_specs=[pl.BlockSpec((1, 1, TOP_K * tm), lambda i: (i, 0, 0), memory_space=pltpu.SMEM),
                  pl.BlockSpec((1, 1, TOP_K * tm), lambda i: (jnp.minimum(i + 1, nt - 1), 0, 0),
                               memory_space=pltpu.SMEM),
                  pl.BlockSpec((tm, d), lambda i: (i, 0)),
                  pl.BlockSpec((tm, LANES), lambda i: (i, 0)),
                  pl.BlockSpec((1, d), lambda i: (0, 0)),
                  pl.BlockSpec(memory_space=pl.ANY)],
        out_specs=pl.BlockSpec((tm, d), lambda i: (i, 0)),
        out_shape=jax.ShapeDtypeStruct((t, d), F32),
        scratch_shapes=[pltpu.VMEM((2, tm * ROW_TILE, LANES), F32),
                        pltpu.VMEM((2, tm * ROW_TILE, LANES), F32),
                        pltpu.SemaphoreType.DMA((2, 2))],
        compiler_params=_cparams(("arbitrary",)),
        name="moe_combine",
    )(dest3, dest3, x, route, fnorm.reshape(1, d), y)


def _moe(x, route, g, w1, w3, w2, fnorm, tm):
    blk = min(MOE_BLOCK, x.shape[0])
    dest, pad, blk_e, n_used, nblk = _route_plan(route, blk)
    buf = _dispatch(x, g, dest, pad, nblk * blk, tm)
    y = _experts(buf, blk_e, n_used, w1, w3, w2, blk)
    return _combine(x, route, fnorm, y, dest, tm)


def _block_diag(w):
    g, a, b = w.shape
    eye = jnp.eye(g, dtype=w.dtype)
    return (eye[:, None, :, None] * w[:, :, None, :]).reshape(g * a, g * b)


def kernel(x, l0_norm_mix, l0_w_in, l0_conv_w, l0_conv_b, l0_gate_a_w, l0_gate_a_b, l0_gate_x_w, l0_gate_x_b, l0_lru_lambda, l0_shift_mu, l0_w0, l0_w_up, l0_a0, l0_a_up, l0_g_up, l0_k_k, l0_k_a, l0_r_k, l0_ln_x_w, l0_ln_x_b, l0_w_out, l0_norm_ffn, l0_ffn_w1, l0_ffn_w3, l0_ffn_w2, l1_norm_mix, l1_w_in, l1_conv_w, l1_conv_b, l1_wq, l1_wk, l1_wv, l1_ig_b, l1_fg_b, l1_mh_norm_w, l1_w_out, l1_norm_ffn, l1_router, l1_moe_w1, l1_moe_w3, l1_moe_w2, final_norm):
    batch, seq, d = x.shape
    t = batch * seq
    xt = x.reshape(t, d)
    tm = min(512, t)

    w_in0 = l0_w_in.astype(BF16)
    p_lru, p_rw = _norm_mm(xt, l0_norm_mix, w_in0, (2 * LRU_WIDTH, RW_IN), tm, "l0_in")
    y_lru = _lru(p_lru, batch, seq, l0_conv_w, l0_conv_b,
                 _block_diag(l0_gate_a_w).astype(BF16), l0_gate_a_b,
                 _block_diag(l0_gate_x_w).astype(BF16), l0_gate_x_b, l0_lru_lambda,
                 min(512, seq))
    y_rw = _rwkv(p_rw, batch, seq, l0_shift_mu, l0_w0, l0_w_up, l0_a0, l0_a_up, l0_g_up,
                 l0_k_k, l0_k_a, l0_r_k.reshape(-1), l0_ln_x_w, l0_ln_x_b)
    w_out0 = l0_w_out.astype(BF16)
    ne, _, f = l1_moe_w1.shape
    x2, (mw1, mw3, mw2) = _ffn(
        xt, y_lru, y_rw, w_out0[:LRU_WIDTH], w_out0[LRU_WIDTH:], l0_norm_ffn,
        l0_ffn_w1.astype(BF16), l0_ffn_w3.astype(BF16), l0_ffn_w2.astype(BF16),
        [l1_moe_w1.reshape(ne * d, f), l1_moe_w3.reshape(ne * d, f), l1_moe_w2.reshape(ne * f, d)], tm)

    n_in1 = 2 * MLSTM_WIDTH + LANES
    w_in1 = jnp.zeros((d, n_in1), F32).at[:, :l1_w_in.shape[1]].set(l1_w_in).astype(BF16)
    h1 = _mlstm(x2, l1_norm_mix, w_in1, batch, seq, l1_conv_w, l1_conv_b, l1_wq, l1_wk, l1_wv,
                l1_ig_b, l1_fg_b, l1_mh_norm_w, tm)
    router = jnp.zeros((d, LANES), F32).at[:, :N_EXPERTS].set(l1_router)
    x3, route = _out_router(x2, h1, l1_w_out.astype(BF16), l1_norm_ffn, router, tm)
    out = _moe(x3, route, l1_norm_ffn, mw1.reshape(ne, d, f), mw3.reshape(ne, d, f),
               mw2.reshape(ne, f, d), final_norm, tm)
    return out.reshape(batch, seq, d)
```

```python
import functools

import jax
import jax.numpy as jnp
from jax import lax
from jax.experimental import pallas as pl
from jax.experimental.pallas import tpu as pltpu

F32 = jnp.float32
BF16 = jnp.bfloat16

D_MODEL = 1024
TOKEN_TILE = 512
LRU_WIDTH = 512
LRU_C = 8.0
LRU_SEGS = 8
CONV_WIDTH = 4
RWKV_HEADS = 8
RWKV_HEAD = 64
RWKV_WIDTH = 512
LN_X_EPS = 1e-5 * RWKV_HEAD
MLSTM_HEADS = 8
MLSTM_HEAD = 128
MLSTM_WIDTH = 1024
MLSTM_CHUNK = 128
MLSTM_STEP_ROWS = 256
D_FF = 2816
N_EXPERTS = 8
TOP_K = 2
DMA_QUEUES = 2
ROUTER_SPLIT = 4
MOE_BLOCK = 512
FF_CHUNKS = (768, 768, 768, 512)
RMS_EPS = 1e-6
RW_IN = 3 * RWKV_WIDTH + 64 + 64 + 128
RW_CHUNK = 64
RW_STEP_ROWS = 512
RW_WAVE_ROWS = 256
LANES = 128
MXU_WIDTH = 256
CARRY_ROWS = 8
ROW_TILE = 8
VMEM_LIMIT = 56 * 1024 * 1024


def _cparams(sem):
    return pltpu.CompilerParams(dimension_semantics=sem, vmem_limit_bytes=VMEM_LIMIT)


def _rms(x, w):
    return x * lax.rsqrt(jnp.mean(x * x, axis=-1, keepdims=True) + RMS_EPS) * w


def _sigmoid(x):
    return 1.0 / (1.0 + jnp.exp(-x))


def _softplus(x):
    return jnp.maximum(x, 0.0) + jnp.log(1.0 + jnp.exp(-jnp.abs(x)))


def _dot(a, b):
    return jnp.dot(a, b, preferred_element_type=F32)


def _dot_nt(a, b):
    return lax.dot_general(a, b, (((1,), (1,)), ((), ())), preferred_element_type=F32)


def _dot_tn(a, b):
    return lax.dot_general(a, b, (((0,), (0,)), ((), ())), preferred_element_type=F32)


def _split2(x):
    hi = x.astype(BF16)
    lo = (x - hi.astype(F32)).astype(BF16)
    return hi, lo


def _split3(x):
    hi = x.astype(BF16)
    r = x - hi.astype(F32)
    mid = r.astype(BF16)
    lo = (r - mid.astype(F32)).astype(BF16)
    return hi, mid, lo


def _mm3(fn, a, b):
    ah, al = _split2(a)
    bh, bl = _split2(b)
    return fn(ah, bh) + fn(al, bh) + fn(ah, bl)


def _mm_exact_lhs(fn, a_bf16, b):
    h, m, l = _split3(b)
    return fn(a_bf16, h) + fn(a_bf16, m) + fn(a_bf16, l)


def _seg_sum(x, seg):
    s = seg.shape[0]
    outs = []
    for c in range(x.shape[1] // s):
        hi, lo = _split2(x[:, c * s:(c + 1) * s])
        outs.append(_dot(hi, seg) + _dot(lo, seg))
    return jnp.concatenate(outs, axis=1)


def _shift_hist(buf_ref, x, first):
    n = x.shape[0]

    @pl.when(first)
    def _():
        buf_ref[0:CARRY_ROWS, :] = jnp.zeros((CARRY_ROWS, x.shape[1]), F32)

    @pl.when(jnp.logical_not(first))
    def _():
        buf_ref[0:CARRY_ROWS, :] = buf_ref[n:n + CARRY_ROWS, :]

    buf_ref[CARRY_ROWS:CARRY_ROWS + n, :] = x


def _causal_conv(buf_ref, n, w, b, rotate):
    if rotate:
        full = buf_ref[0:CARRY_ROWS + n, :]
        acc = b + w[CONV_WIDTH - 1:CONV_WIDTH, :] * full[CARRY_ROWS:, :]
        for lag in range(1, CONV_WIDTH):
            tap = CONV_WIDTH - 1 - lag
            acc = acc + w[tap:tap + 1, :] * pltpu.roll(full, lag, axis=0)[CARRY_ROWS:, :]
        return acc
    acc = b
    for j in range(CONV_WIDTH):
        off = CARRY_ROWS - (CONV_WIDTH - 1) + j
        acc = acc + w[j:j + 1, :] * buf_ref[off:off + n, :]
    return acc


def _norm_mm_kernel(x_ref, g_ref, w_ref, *o_refs):
    xn = _rms(x_ref[...], g_ref[...]).astype(BF16)
    c0 = 0
    for o_ref in o_refs:
        c1 = c0 + o_ref.shape[1]
        o_ref[...] = _dot(xn, w_ref[:, c0:c1]).astype(o_ref.dtype)
        c0 = c1


def _norm_mm(x, g, w, widths, tm, name):
    t, d = x.shape
    n = w.shape[1]
    assert sum(widths) == n
    return pl.pallas_call(
        _norm_mm_kernel,
        grid=(t // tm,),
        in_specs=[pl.BlockSpec((tm, d), lambda i: (i, 0)),
                  pl.BlockSpec((1, d), lambda i: (0, 0)),
                  pl.BlockSpec((d, n), lambda i: (0, 0))],
        out_specs=[pl.BlockSpec((tm, c), lambda i: (i, 0)) for c in widths],
        out_shape=[jax.ShapeDtypeStruct((t, c), BF16) for c in widths],
        compiler_params=_cparams(("parallel",)),
        name=name,
    )(x, g.reshape(1, d), w)


def _lru_kernel(p_ref, cw_ref, cb_ref, wa_ref, ba_ref, wx_ref, bx_ref, lam_ref, o_ref,
                xbuf, abuf, bbuf, hbuf, hcar):
    i = pl.program_id(1)
    n = p_ref.shape[0]
    first = i == 0
    _shift_hist(xbuf, p_ref[:, 0:LRU_WIDTH].astype(F32), first)
    xc = _causal_conv(xbuf, n, cw_ref[...], cb_ref[...], True)
    xcb = xc.astype(BF16)
    r = _sigmoid(_dot(xcb, wa_ref[...]) + ba_ref[...])
    ig = _sigmoid(_dot(xcb, wx_ref[...]) + bx_ref[...])
    log_a = (-LRU_C) * r * _softplus(-lam_ref[...])
    a = jnp.exp(log_a)
    mult = jnp.sqrt(1.0 - a * a)
    row = lax.broadcasted_iota(jnp.int32, (n, 1), 0)
    mult = jnp.where(jnp.logical_and(first, row == 0), 1.0, mult)
    bvals = mult * ig * xc
    n_slab = LRU_WIDTH // LANES
    slabs = [slice(c * LANES, (c + 1) * LANES) for c in range(n_slab)]
    seg = n // LRU_SEGS
    pitch = _lru_pitch(n)
    for c, cs in enumerate(slabs):
        for s in range(LRU_SEGS):
            abuf[c, s * pitch:s * pitch + seg, :] = a[s * seg:(s + 1) * seg, cs]
            bbuf[c, s * pitch:s * pitch + seg, :] = bvals[s * seg:(s + 1) * seg, cs]

    @pl.when(first)
    def _():
        hcar[...] = jnp.zeros(hcar.shape, F32)

    def step(j, carry):
        rows = pl.ds(j, LRU_SEGS, stride=pitch)
        out = []
        for c in range(n_slab):
            h, acc = carry[c]
            a_j = abuf[c, rows, :]
            h = a_j * h + bbuf[c, rows, :]
            acc = acc * a_j
            hbuf[c, rows, :] = h
            abuf[c, rows, :] = acc
            out.append((h, acc))
        return tuple(out)

    init = tuple((jnp.zeros((LRU_SEGS, LANES), F32), jnp.ones((LRU_SEGS, LANES), F32))
                 for _ in range(n_slab))
    ends = lax.fori_loop(0, seg, step, init, unroll=8)
    cols = []
    for c, cs in enumerate(slabs):
        h_end, a_end = ends[c]
        h0 = hcar[0:1, cs]
        parts = []
        for s in range(LRU_SEGS):
            rows = slice(s * pitch, s * pitch + seg)
            parts.append(hbuf[c, rows, :] + abuf[c, rows, :] * h0)
            h0 = h_end[s:s + 1, :] + a_end[s:s + 1, :] * h0
        hcar[0:1, cs] = h0
        cols.append(jnp.concatenate(parts, axis=0))
    gate = p_ref[:, LRU_WIDTH:2 * LRU_WIDTH].astype(F32)
    gelu = 0.5 * gate * (1.0 + jnp.tanh(0.7978845608028654 * (gate + 0.044715 * gate * gate * gate)))
    o_ref[...] = (jnp.concatenate(cols, axis=1) * gelu).astype(o_ref.dtype)


def _lru_pitch(n):
    tiles = n // LRU_SEGS // CARRY_ROWS
    return (tiles + 1 - tiles % 2) * CARRY_ROWS


def _lru(p_lru, batch, seq, conv_w, conv_b, wa, ba, wx, bx, lam, tb):
    nb = seq // tb
    c = LRU_WIDTH
    vec = lambda: pl.BlockSpec((1, c), lambda b, i: (0, 0))
    return pl.pallas_call(
        _lru_kernel,
        grid=(batch, nb),
        in_specs=[pl.BlockSpec((tb, 2 * c), lambda b, i: (b * nb + i, 0)),
                  pl.BlockSpec((CONV_WIDTH, c), lambda b, i: (0, 0)), vec(),
                  pl.BlockSpec((c, c), lambda b, i: (0, 0)), vec(),
                  pl.BlockSpec((c, c), lambda b, i: (0, 0)), vec(), vec()],
        out_specs=pl.BlockSpec((tb, c), lambda b, i: (b * nb + i, 0)),
        out_shape=jax.ShapeDtypeStruct((batch * seq, c), BF16),
        scratch_shapes=[pltpu.VMEM((tb + CARRY_ROWS, c), F32)]
        + [pltpu.VMEM((c // LANES, LRU_SEGS * _lru_pitch(tb), LANES), F32)] * 3
        + [pltpu.VMEM((CARRY_ROWS, c), F32)],
        compiler_params=_cparams(("parallel", "arbitrary")),
        name="rg_lru",
    )(p_lru, conv_w, conv_b.reshape(1, c), wa, ba.reshape(1, c), wx, bx.reshape(1, c),
      lam.reshape(1, c))


def _blk(x, masks):
    xb = x.astype(BF16)
    return jnp.concatenate([jnp.where(mk, xb, jnp.zeros_like(xb)) for mk in masks], axis=0)


def _rwkv_kernel(p_ref, mu_ref, w0_ref, wup_ref, a0_ref, aup_ref, gup_ref, kk_ref, ka_ref,
                 rk_ref, lnw_ref, lnb_ref, seg_ref, tri_ref, o_ref, pbuf, state):
    L = RW_CHUNK
    W = RWKV_WIDTH
    rows = p_ref.shape[0]
    wave = tri_ref.shape[0]
    first = pl.program_id(1) == 0
    p = p_ref[...].astype(F32)
    _shift_hist(pbuf, p, first)
    prev = pbuf[CARRY_ROWS - 1:CARRY_ROWS - 1 + rows, :]
    ps = p + mu_ref[...] * (prev - p)
    seg = seg_ref[...]
    tri = tri_ref[...]

    @pl.when(first)
    def _():
        state[...] = jnp.zeros(state.shape, F32)

    lane = lax.broadcasted_iota(jnp.int32, (1, LANES), 1)
    m1 = [lane < RWKV_HEAD, lane >= RWKV_HEAD]
    m2 = [jnp.concatenate([mk, mk], axis=1) for mk in m1]
    ti = lax.broadcasted_iota(jnp.int32, (L, LANES), 0)
    si = lax.broadcasted_iota(jnp.int32, (L, LANES), 1) % RWKV_HEAD
    strict = si < ti
    incl = si <= ti
    bi = lax.broadcasted_iota(jnp.int32, (LANES, LANES), 0) // RWKV_HEAD
    bj = lax.broadcasted_iota(jnp.int32, (LANES, LANES), 1) // RWKV_HEAD
    bd = bi == bj

    def tril(mask, s):
        return jnp.where(mask, s, 0.0).astype(BF16)

    wv = [slice(w * wave, (w + 1) * wave) for w in range(rows // wave)]
    ws = range(len(wv))
    r = [ps[rw, 0:W] for rw in wv]
    k = [ps[rw, W:2 * W] for rw in wv]
    v = [ps[rw, 2 * W:3 * W] for rw in wv]
    x2 = [ps[rw, 3 * W:3 * W + LANES] for rw in wv]
    xg = [ps[rw, 3 * W + LANES:3 * W + 2 * LANES] for rw in wv]
    wl = [w0_ref[...] + _dot(jnp.tanh(x2[w]).astype(BF16), wup_ref[...]) for w in ws]
    a = [_sigmoid(a0_ref[...] + _dot(x2[w].astype(BF16), aup_ref[...])) for w in ws]
    g = [_dot(_sigmoid(xg[w]).astype(BF16), gup_ref[...]) for w in ws]
    kk = [k[w] * kk_ref[...] for w in ws]
    kk = [kk[w] / jnp.maximum(jnp.sqrt(_seg_sum(kk[w] * kk[w], seg)), 1e-12) for w in ws]
    lw = [-jnp.exp(-_softplus(-wl[w]) - 0.5) for w in ws]
    cw = [_mm_exact_lhs(_dot, tri, lw[w]) for w in ws]

    n_pairs = RWKV_HEADS // 2
    n_ch = wave // L
    pair_cols = [slice(pr * LANES, (pr + 1) * LANES) for pr in range(n_pairs)]
    ops = {}

    def scale_pair(w, pr):
        sl = pair_cols[pr]
        cwp = cw[w][:, sl]
        cw_end = jnp.concatenate(
            [jnp.broadcast_to(cwp[c * L + L - 1:c * L + L, :], (L, LANES)) for c in range(n_ch)], axis=0)
        w_inv = jnp.exp(-cwp)
        w_end = jnp.exp(cw_end - cwp)
        k2 = k[w][:, sl] * (1.0 + (a[w][:, sl] - 1.0) * ka_ref[:, sl])
        kka = kk[w][:, sl] * a[w][:, sl]
        ops[w, pr] = dict(a_t=-kk[w][:, sl] * jnp.exp(cwp - lw[w][:, sl]), b_t=kka * w_inv, k_t=k2 * w_inv,
                          r_t=r[w][:, sl] * jnp.exp(cwp), b_bar=kka * w_end, k_bar=k2 * w_end,
                          w_tot=jnp.exp(cw_end), k2=k2, v=v[w][:, sl])

    for pr in range(n_pairs):
        scale_pair(0, pr)
    s_cur = [state[pr] for pr in range(n_pairs)]
    for w in ws:
        todo = [pr for pr in range(n_pairs)] if w + 1 < len(wv) else []

        def next_piece():
            if todo:
                scale_pair(w + 1, todo.pop(0))

        chains = [(slice(c * L, (c + 1) * L), pr) for c in range(n_ch) for pr in range(n_pairs)]
        idx = range(len(chains))
        a_c = [ops[w, pr]["a_t"][rs, :] for rs, pr in chains]
        r_c = [ops[w, pr]["r_t"][rs, :] for rs, pr in chains]
        ar = [jnp.concatenate([a_c[i], r_c[i]], axis=0).astype(BF16) for i in idx]
        sb = [_dot_nt(ar[i], _blk(ops[w, pr]["b_t"][rs, :], m1)) for i, (rs, pr) in enumerate(chains)]
        sk = [_dot_nt(ar[i], _blk(ops[w, pr]["k_t"][rs, :], m1)) for i, (rs, pr) in enumerate(chains)]
        vblk = [_blk(ops[w, pr]["v"][rs, :], m1) for rs, pr in chains]
        next_piece()
        rhs = [jnp.concatenate([a_c[i], _dot(tril(strict, sk[i][0:L]), vblk[i])], axis=1) for i in idx]
        pw = [tril(strict, sb[i][0:L]) for i in idx]
        tinv = [jnp.where(si == ti, 1.0, 0.0) + pw[i].astype(F32) for i in idx]
        pw = [_dot(pw[i], _blk(pw[i], m1)).astype(BF16) for i in idx]
        next_piece()
        for _ in range(4):
            both = [_dot(pw[i], jnp.concatenate([_blk(pw[i], m1), _blk(tinv[i], m1)], axis=1)) for i in idx]
            tinv = [tinv[i] + both[i][:, LANES:2 * LANES] for i in idx]
            pw = [both[i][:, 0:LANES].astype(BF16) for i in idx]
            next_piece()
        tinv = [tinv[i] + _dot(pw[i], _blk(tinv[i], m1)) for i in idx]
        x = [_dot(tinv[i].astype(BF16), _blk(rhs[i], m2)) for i in idx]
        corr = [_dot(tril(incl, sb[i][L:2 * L]), _blk(x[i], m2)) for i in idx]
        r_hat = [(r_c[i] + corr[i][:, 0:LANES]).astype(BF16) for i in idx]
        y0 = [corr[i][:, LANES:2 * LANES] + _dot(tril(incl, sk[i][L:2 * L]), vblk[i]) for i in idx]
        xb = [x[i].astype(BF16) for i in idx]
        bb = [ops[w, pr]["b_bar"][rs, :].astype(BF16) for rs, pr in chains]
        gmat = [jnp.where(bd, _dot_tn(xb[i][:, 0:LANES], bb[i]), 0.0).astype(BF16) for i in idx]
        hmat = [jnp.where(bd, _dot_tn(xb[i][:, LANES:2 * LANES], bb[i])
                          + _dot_tn(ops[w, pr]["v"][rs, :].astype(BF16),
                                    ops[w, pr]["k_bar"][rs, :].astype(BF16)), 0.0)
                for i, (rs, pr) in enumerate(chains)]

        y_rows = []
        for c in range(n_ch):
            ys = []
            for pr in range(n_pairs):
                i = c * n_pairs + pr
                s0 = s_cur[pr]
                s0b = s0.astype(BF16)
                ys.append(_dot_nt(r_hat[i], s0b) + y0[i])
                s_cur[pr] = s0 * ops[w, pr]["w_tot"][c * L:c * L + 1, :] + _dot(s0b, gmat[i]) + hmat[i]
            y_rows.append(jnp.concatenate(ys, axis=1))
        y = jnp.concatenate(y_rows, axis=0)
        k2 = jnp.concatenate([ops[w, pr]["k2"] for pr in range(n_pairs)], axis=1)

        inv = 1.0 / RWKV_HEAD
        mean = _seg_sum(y, seg) * inv
        yc = y - mean
        var = _seg_sum(yc * yc, seg) * inv
        yn = yc * lax.rsqrt(var + LN_X_EPS) * lnw_ref[...] + lnb_ref[...]
        bonus = _seg_sum(r[w] * k2 * rk_ref[...], seg) * v[w]
        o_ref[wv[w], :] = ((yn + bonus) * g[w]).astype(o_ref.dtype)
    for pr in range(n_pairs):
        state[pr] = s_cur[pr]


def _rwkv(p_rw, batch, seq, mu, w0, w_up, a0, a_up, g_up, k_k, k_a, r_k, ln_w, ln_b):
    L = min(RW_STEP_ROWS, seq)
    nc = seq // L
    W = RWKV_WIDTH
    wup = jnp.zeros((LANES, W), F32).at[0:64].set(w_up).astype(BF16)
    aup = jnp.zeros((LANES, W), F32).at[64:128].set(a_up).astype(BF16)
    hid = jnp.arange(MXU_WIDTH) // RWKV_HEAD
    seg = (hid[:, None] == hid[None, :]).astype(BF16)
    wave = min(RW_WAVE_ROWS, L)
    ri = jnp.arange(wave)
    tri = jnp.logical_and(ri[:, None] >= ri[None, :],
                          ri[:, None] // RW_CHUNK == ri[None, :] // RW_CHUNK).astype(BF16)
    vec = lambda: pl.BlockSpec((1, W), lambda b, i: (0, 0))
    mat = lambda s: pl.BlockSpec(s, lambda b, i: (0, 0))
    return pl.pallas_call(
        _rwkv_kernel,
        grid=(batch, nc),
        in_specs=[pl.BlockSpec((L, RW_IN), lambda b, i: (b * nc + i, 0)),
                  mat((1, RW_IN)), vec(), mat((LANES, W)), vec(), mat((LANES, W)),
                  mat((LANES, W)), vec(), vec(), vec(), vec(), vec(), mat((MXU_WIDTH, MXU_WIDTH)),
                  mat((wave, wave))],
        out_specs=pl.BlockSpec((L, W), lambda b, i: (b * nc + i, 0)),
        out_shape=jax.ShapeDtypeStruct((batch * seq, W), BF16),
        scratch_shapes=[pltpu.VMEM((L + CARRY_ROWS, RW_IN), F32),
                        pltpu.VMEM((RWKV_HEADS // 2, LANES, LANES), F32)],
        compiler_params=_cparams(("parallel", "arbitrary")),
        name="rwkv7",
    )(p_rw, mu.reshape(1, RW_IN), w0.reshape(1, W), wup, a0.reshape(1, W), aup,
      g_up.astype(BF16), k_k.reshape(1, W), k_a.reshape(1, W), r_k.reshape(1, W),
      ln_w.reshape(1, W), ln_b.reshape(1, W), seg, tri)


def _swiglu(xb, w1, w3, w2):
    edges = [sum(FF_CHUNKS[:i]) for i in range(len(FF_CHUNKS) + 1)]
    cols = [slice(edges[i], edges[i + 1]) for i in range(len(FF_CHUNKS))]
    hid = (_dot(xb, w1(cols[0])), _dot(xb, w3(cols[0])))
    acc = None
    for i, c in enumerate(cols):
        nxt = (_dot(xb, w1(cols[i + 1])), _dot(xb, w3(cols[i + 1]))) if i + 1 < len(cols) else None
        act = (hid[0] * _sigmoid(hid[0]) * hid[1]).astype(BF16)
        part = _dot(act, w2(c))
        acc = part if acc is None else acc + part
        hid = nxt
    return acc


def _ffn_kernel(n_cast, x_ref, ya_ref, yb_ref, wa_ref, wb_ref, g_ref, w1_ref, w3_ref, w2_ref, *rest):
    src, o_ref, dst = rest[:n_cast], rest[n_cast], rest[n_cast + 1:]
    x1 = x_ref[...] + _dot(ya_ref[...], wa_ref[...]) + _dot(yb_ref[...], wb_ref[...])
    xn = _rms(x1, g_ref[...]).astype(BF16)
    o_ref[...] = x1 + _swiglu(xn, lambda c: w1_ref[:, c], lambda c: w3_ref[:, c], lambda c: w2_ref[c, :])
    for s_ref, d_ref in zip(src, dst):
        d_ref[...] = s_ref[...].astype(BF16)


def _ffn(x, ya, yb, wa, wb, g, w1, w3, w2, to_cast, tm):
    t, d = x.shape
    nt = t // tm
    once = pl.Buffered(1)
    fixed = lambda a: pl.BlockSpec(a.shape, lambda i: (0, 0), pipeline_mode=once)
    slab = lambda a: pl.BlockSpec((a.shape[0] // nt, a.shape[1]), lambda i: (i, 0))
    assert all(a.shape[0] % (nt * 2 * ROW_TILE) == 0 for a in to_cast)
    outs = pl.pallas_call(
        functools.partial(_ffn_kernel, len(to_cast)),
        grid=(nt,),
        in_specs=[pl.BlockSpec((tm, d), lambda i: (i, 0)),
                  pl.BlockSpec((tm, ya.shape[1]), lambda i: (i, 0)),
                  pl.BlockSpec((tm, yb.shape[1]), lambda i: (i, 0)),
                  fixed(wa), fixed(wb),
                  pl.BlockSpec((1, d), lambda i: (0, 0)),
                  fixed(w1), fixed(w3), fixed(w2)] + [slab(a) for a in to_cast],
        out_specs=[pl.BlockSpec((tm, d), lambda i: (i, 0))] + [slab(a) for a in to_cast],
        out_shape=[jax.ShapeDtypeStruct((t, d), F32)]
        + [jax.ShapeDtypeStruct(a.shape, BF16) for a in to_cast],
        compiler_params=_cparams(("parallel",)),
        name="ffn_swiglu",
    )(x, ya, yb, wa, wb, g.reshape(1, d), w1, w3, w2, *to_cast)
    return outs[0], outs[1:]


def _mlstm_in_kernel(nb, x_ref, g_ref, w_ref, cw_ref, cb_ref, wq_ref, wk_ref, wv_ref, gb_ref, tri_ref,
                     q_ref, k_ref, v_ref, z_ref, gs_ref, gtr_ref, xbuf, xm_s, z_s, gt_s):
    i = pl.program_id(0)
    n = x_ref.shape[0]
    W = MLSTM_WIDTH
    dh = MLSTM_HEAD
    L = MLSTM_CHUNK
    nh = MLSTM_HEADS
    slot = i % 2

    @pl.when(i == 0)
    def _():
        xm_s[...] = jnp.zeros(xm_s.shape, BF16)
        z_s[...] = jnp.zeros(z_s.shape, BF16)
        gt_s[...] = jnp.zeros(gt_s.shape, F32)

    xm = xm_s[1 - slot]
    z_ref[...] = z_s[1 - slot]
    gates = gt_s[1 - slot]
    _shift_hist(xbuf, xm.astype(F32), jnp.logical_or(i == 0, (i - 1) % nb == 0))

    xn = _rms(x_ref[...], g_ref[...]).astype(BF16)
    cw = cw_ref[...]
    cb = cb_ref[...]
    scale = dh ** -0.5
    heads_per = MXU_WIDTH // dh
    groups = [slice(c * MXU_WIDTH, (c + 1) * MXU_WIDTH) for c in range(W // MXU_WIDTH)]

    def qkv(c, xc):
        for u in range(heads_per):
            h = c * heads_per + u
            sl = slice(h * dh, (h + 1) * dh)
            xh = xc[:, u * dh:(u + 1) * dh]
            q_ref[:, sl] = (_dot(xh, wq_ref[h]) * scale).astype(BF16)
            k_ref[:, sl] = _dot(xh, wk_ref[h]).astype(BF16)
            v_ref[:, sl] = _dot(xm[:, sl], wv_ref[h]).astype(BF16)

    xm_new, z_new, xc_prev = [], [], None
    for c, cols in enumerate(groups):
        xm_new.append(_dot(xn, w_ref[:, cols]).astype(BF16))
        z_new.append(_dot(xn, w_ref[:, W + c * MXU_WIDTH:W + (c + 1) * MXU_WIDTH]).astype(BF16))
        acc = cb[:, cols]
        for j in range(CONV_WIDTH):
            off = CARRY_ROWS - (CONV_WIDTH - 1) + j
            acc = acc + cw[j:j + 1, cols] * xbuf[off:off + n, cols]
        if c > 0:
            qkv(c - 1, xc_prev)
        xc_prev = (acc * _sigmoid(acc)).astype(BF16)
    qkv(len(groups) - 1, xc_prev)
    gt_new = _dot(xn, w_ref[:, 2 * W:2 * W + LANES])

    lane = lax.broadcasted_iota(jnp.int32, (L, LANES), 1)
    row = lax.broadcasted_iota(jnp.int32, (L, LANES), 0)
    for c in range(n // L):
        rs = slice(c * L, (c + 1) * L)
        gl = gates[rs, :] + gb_ref[...]
        lf = jnp.minimum(gl, 0.0) - jnp.log(1.0 + jnp.exp(-jnp.abs(gl)))
        bcum = _mm_exact_lhs(_dot, tri_ref[...], lf)
        b_al = pltpu.roll(bcum, LANES - nh, axis=1)
        gmat = gl - b_al
        cmax = gmat
        d = 1
        while d < L:
            cmax = jnp.maximum(cmax, jnp.where(row >= d, pltpu.roll(cmax, d, axis=0), -jnp.inf))
            d *= 2
        gs_ref[rs, :] = jnp.where(lane < nh, gmat,
                                  jnp.where(lane < 2 * nh, pltpu.roll(cmax, nh, axis=1),
                                            pltpu.roll(b_al, 2 * nh, axis=1)))
        gtr_ref[rs, :] = gmat.T
    xm_s[slot] = jnp.concatenate(xm_new, axis=1)
    z_s[slot] = jnp.concatenate(z_new, axis=1)
    gt_s[slot] = gt_new


def _mlstm_kernel(q_ref, k_ref, v_ref, z_ref, gs_ref, gtr_ref, nw_ref, o_ref, cn_s, m_s):
    L = MLSTM_CHUNK
    dh = MLSTM_HEAD
    nh = MLSTM_HEADS
    n_ch = q_ref.shape[0] // L

    @pl.when(pl.program_id(1) == 0)
    def _():
        cn_s[...] = jnp.zeros(cn_s.shape, F32)
        m_s[...] = jnp.zeros(m_s.shape, F32)

    ti = lax.broadcasted_iota(jnp.int32, (L, L), 0)
    si = lax.broadcasted_iota(jnp.int32, (L, L), 1)
    causal = si <= ti
    ones = jnp.ones((L, dh), BF16)
    rws = [slice(c * L, (c + 1) * L) for c in range(n_ch)]
    sls = [slice(h * dh, (h + 1) * dh) for h in range(nh)]
    gs = [gs_ref[rw, :] for rw in rws]
    g_t = [gtr_ref[rw, :] for rw in rws]
    units = [(c, h) for c in range(n_ch) for h in range(nh)]

    head_lanes = jnp.logical_and(lax.broadcasted_iota(jnp.int32, (1, LANES), 1) >= nh,
                                 lax.broadcasted_iota(jnp.int32, (1, LANES), 1) < 2 * nh)
    mp = m_s[0:1, :]
    mx_all, sc_all, fl_all, kwe_all, dec_all = [], [], [], [], []
    for c in range(n_ch):
        b8 = pltpu.roll(gs[c], LANES - nh, axis=1)
        g8 = pltpu.roll(gs[c], nh, axis=1)
        mx = jnp.maximum(mp, gs[c])
        m_last = mx[L - 1:L, :]
        mx_all.append(mx)
        sc_all.append(jnp.exp(mp - mx))
        fl_all.append(jnp.exp(-(b8 + mx)))
        kwe_all.append(jnp.exp(g8 - m_last))
        dec_all.append(jnp.exp(mp - m_last))
        mp = jnp.where(head_lanes, b8[L - 1:L, :] + m_last, 0.0)
    m_s[0:1, :] = mp
    col = lambda arr, h: arr[:, nh + h:nh + h + 1]

    qb = {u: q_ref[rws[u[0]], sls[u[1]]] for u in units}
    kb = {u: k_ref[rws[u[0]], sls[u[1]]] for u in units}
    v1 = {u: jnp.concatenate([v_ref[rws[u[0]], sls[u[1]]], ones], axis=1) for u in units}
    qk = {u: _dot_nt(qb[u], kb[u]) for u in units}
    s = {(c, h): qk[c, h] * jnp.exp(jnp.where(causal, g_t[c][h:h + 1, :] - col(mx_all[c], h), -jnp.inf))
         for c, h in units}
    sv = {u: _dot(s[u].astype(BF16), v1[u]) for u in units}
    kw = {(c, h): kb[c, h].astype(F32) * col(kwe_all[c], h) for c, h in units}
    kv = {u: _dot_tn(kw[u].astype(BF16), v1[u]) for u in units}
    sc = {(c, h): col(sc_all[c], h) for c, h in units}
    floor = {(c, h): col(fl_all[c], h) for c, h in units}
    dec = {(c, h): col(dec_all[c], h) for c, h in units}

    cn = {h: cn_s[h] for h in range(nh)}
    for c in range(n_ch):
        hs = range(nh)
        qcn = {h: _dot(qb[c, h], cn[h].astype(BF16)) for h in hs}
        den = {h: sc[c, h] * qcn[h][:, dh:] + sv[c, h][:, dh:] for h in hs}
        hh = {h: (sc[c, h] * qcn[h][:, :dh] + sv[c, h][:, :dh]) / jnp.maximum(jnp.abs(den[h]), floor[c, h])
              for h in hs}
        ms = {h: jnp.mean(hh[h] * hh[h], axis=-1, keepdims=True) for h in hs}
        for h in hs:
            cn[h] = dec[c, h] * cn[h] + kv[c, h]
            hn = hh[h] * lax.rsqrt(ms[h] + RMS_EPS)
            gate = _sigmoid(z_ref[rws[c], sls[h]].astype(F32))
            o_ref[rws[c], sls[h]] = (gate * (hn * nw_ref[:, sls[h]])).astype(o_ref.dtype)
    for h in range(nh):
        cn_s[h] = cn[h]


def _mlstm(x, norm_w, w_in, batch, seq, conv_w, conv_b, wq, wk, wv, ig_b, fg_b, mh_w, tm):
    L = MLSTM_CHUNK
    nc = seq // L
    W = MLSTM_WIDTH
    t, d = x.shape
    tb = min(tm, seq)
    nb = seq // tb
    n_blk = batch * nb
    fixed = lambda shape: pl.BlockSpec(shape, lambda i: (0,) * len(shape))
    hw = lambda: fixed((MLSTM_HEADS, MLSTM_HEAD, MLSTM_HEAD))
    done = lambda width: pl.BlockSpec((tb, width), lambda i: (jnp.maximum(i - 1, 0), 0))
    gb = jnp.zeros((1, LANES), F32).at[0, 0:8].set(ig_b).at[0, 8:16].set(fg_b)
    tri = (jnp.arange(L)[:, None] >= jnp.arange(L)[None, :]).astype(BF16)
    q, k, v, z, gs, gtr = pl.pallas_call(
        functools.partial(_mlstm_in_kernel, nb),
        grid=(n_blk + 1,),
        in_specs=[pl.BlockSpec((tb, d), lambda i: (jnp.minimum(i, n_blk - 1), 0)),
                  fixed((1, d)), fixed(w_in.shape), fixed((CONV_WIDTH, W)), fixed((1, W)),
                  hw(), hw(), hw(), fixed((1, LANES)), fixed((L, L))],
        out_specs=[done(W), done(W), done(W), done(W), done(LANES), done(LANES)],
        out_shape=[jax.ShapeDtypeStruct((t, W), BF16)] * 4 + [jax.ShapeDtypeStruct((t, LANES), F32)] * 2,
        scratch_shapes=[pltpu.VMEM((tb + CARRY_ROWS, W), F32), pltpu.VMEM((2, tb, W), BF16),
                        pltpu.VMEM((2, tb, W), BF16), pltpu.VMEM((2, tb, LANES), F32)],
        compiler_params=_cparams(("arbitrary",)),
        name="l1_in_mlstm",
    )(x, norm_w.reshape(1, d), w_in, conv_w, conv_b.reshape(1, W), wq.astype(BF16), wk.astype(BF16),
      wv.astype(BF16), gb, tri)

    rb = min(MLSTM_STEP_ROWS, seq)
    nc = seq // rb
    chunk = lambda: pl.BlockSpec((rb, W), lambda b, i: (b * nc + i, 0))
    return pl.pallas_call(
        _mlstm_kernel,
        grid=(batch, nc),
        in_specs=[chunk(), chunk(), chunk(), chunk(),
                  pl.BlockSpec((rb, LANES), lambda b, i: (b * nc + i, 0)),
                  pl.BlockSpec((rb, LANES), lambda b, i: (b * nc + i, 0)),
                  pl.BlockSpec((1, W), lambda b, i: (0, 0))],
        out_specs=pl.BlockSpec((rb, W), lambda b, i: (b * nc + i, 0)),
        out_shape=jax.ShapeDtypeStruct((t, W), BF16),
        scratch_shapes=[pltpu.VMEM((MLSTM_HEADS, MLSTM_HEAD, 2 * MLSTM_HEAD), F32),
                        pltpu.VMEM((MLSTM_HEADS, LANES), F32)],
        compiler_params=_cparams(("parallel", "arbitrary")),
        name="mlstm",
    )(q, k, v, z, gs, gtr, mh_w.reshape(1, W))


def _out_router_kernel(x_ref, y_ref, w_ref, g_ref, rt_ref, x3_ref, o_ref):
    tm = x_ref.shape[0]
    sub = tm // ROUTER_SPLIT
    rs = [slice(q * sub, (q + 1) * sub) for q in range(ROUTER_SPLIT)]
    qs = range(ROUTER_SPLIT)
    w = w_ref[...]
    x3 = [x_ref[r, :] + _dot(y_ref[r, :], w) for r in rs]
    for q in qs:
        x3_ref[rs[q], :] = x3[q]
    gw = g_ref[...]
    rt = rt_ref[...]
    logits = [_mm3(_dot, _rms(x3[q], gw), rt) for q in qs]
    lane = lax.broadcasted_iota(jnp.int32, (sub, LANES), 1)
    lg = [jnp.where(lane < N_EXPERTS, logits[q], -jnp.inf) for q in qs]
    v1 = [jnp.max(lg[q], axis=-1, keepdims=True) for q in qs]
    i1 = [jnp.min(jnp.where(lg[q] == v1[q], lane, LANES), axis=-1, keepdims=True) for q in qs]
    lg2 = [jnp.where(lane == i1[q], -jnp.inf, lg[q]) for q in qs]
    v2 = [jnp.max(lg2[q], axis=-1, keepdims=True) for q in qs]
    i2 = [jnp.min(jnp.where(lg2[q] == v2[q], lane, LANES), axis=-1, keepdims=True) for q in qs]
    for q in qs:
        ex = jnp.exp(v2[q] - v1[q])
        g1 = 1.0 / (1.0 + ex)
        g2 = ex / (1.0 + ex)
        out = jnp.where(lane == 0, i1[q].astype(F32), 0.0)
        out = jnp.where(lane == 1, i2[q].astype(F32), out)
        out = jnp.where(lane == 2, g1, out)
        o_ref[rs[q], :] = jnp.where(lane == 3, g2, out)


def _out_router(x, y, w, g, router, tm):
    t, d = x.shape
    return pl.pallas_call(
        _out_router_kernel,
        grid=(t // tm,),
        in_specs=[pl.BlockSpec((tm, d), lambda i: (i, 0)),
                  pl.BlockSpec((tm, y.shape[1]), lambda i: (i, 0)),
                  pl.BlockSpec(w.shape, lambda i: (0, 0)),
                  pl.BlockSpec((1, d), lambda i: (0, 0)),
                  pl.BlockSpec((d, LANES), lambda i: (0, 0))],
        out_specs=[pl.BlockSpec((tm, d), lambda i: (i, 0)),
                   pl.BlockSpec((tm, LANES), lambda i: (i, 0))],
        out_shape=[jax.ShapeDtypeStruct((t, d), F32), jax.ShapeDtypeStruct((t, LANES), F32)],
        compiler_params=_cparams(("parallel",)),
        name="l1_out_router",
    )(x, y, w, g.reshape(1, d), router)


def _prefix_sum(x):
    m, c = x.shape
    group = min(LANES, m)
    xg = x.reshape(m // group, group, c).astype(F32)
    gi = jnp.arange(group)
    local = jnp.einsum("ij,bjc->bic", (gi[:, None] >= gi[None, :]).astype(F32), xg,
                       precision=lax.Precision.HIGHEST)
    bi = jnp.arange(m // group)
    before = jnp.dot((bi[:, None] > bi[None, :]).astype(F32), local[:, -1, :],
                     precision=lax.Precision.HIGHEST)
    return jnp.round(local + before[:, None, :]).astype(jnp.int32).reshape(m, c)


def _route_plan(route, blk):
    e = route[:, 0:TOP_K].astype(jnp.int32).reshape(-1)
    m = e.shape[0]
    onehot = (e[:, None] == jnp.arange(N_EXPERTS, dtype=jnp.int32)[None, :]).astype(jnp.int32)
    csum = _prefix_sum(onehot)
    rank = jnp.sum((csum - onehot) * onehot, axis=1)
    nb = (csum[-1] + blk - 1) // blk
    bend = _prefix_sum(nb[:, None])[:, 0]
    dest = jnp.sum(onehot * (bend - nb)[None, :], axis=1) * blk + rank
    nblk = m // blk + N_EXPERTS
    n_used = bend[-1]
    bidx = jnp.minimum(jnp.arange(nblk, dtype=jnp.int32), n_used - 1)
    blk_e = jnp.minimum(jnp.sum(bidx[:, None] >= bend[None, :], axis=1), N_EXPERTS - 1).astype(jnp.int32)
    gap_start = jnp.concatenate([(bend - nb) * blk + csum[-1], (n_used * blk)[None]])
    gap_len = jnp.concatenate([nb * blk - csum[-1], ((nblk - n_used) * blk)[None]])
    gap_end = _prefix_sum(gap_len[:, None])[:, 0]
    q = jnp.arange(nblk * blk - m, dtype=jnp.int32)
    gi = jnp.sum(q[:, None] >= gap_end[None, :], axis=1)
    pad = gap_start[gi] + q - (gap_end - gap_len)[gi]
    return dest.astype(jnp.int32), pad.astype(jnp.int32), blk_e, n_used.reshape(1).astype(jnp.int32), nblk


def _to_tiles(ref, x):
    n = x.shape[0]
    for s in range(ROW_TILE):
        ref[pl.ds(s, n, stride=ROW_TILE), :] = x[:, s * LANES:(s + 1) * LANES]


def _from_tiles(ref, n):
    return jnp.concatenate([ref[pl.ds(s, n, stride=ROW_TILE), :] for s in range(ROW_TILE)], axis=1)


def _row_copy(src, src_row, dst, dst_row, sem):
    s0 = pl.multiple_of(src_row * ROW_TILE, ROW_TILE)
    d0 = pl.multiple_of(dst_row * ROW_TILE, ROW_TILE)
    return pltpu.make_async_copy(src.at[pl.ds(s0, ROW_TILE), :], dst.at[pl.ds(d0, ROW_TILE), :], sem)


def _wait_rows(hbm, vmem, sem, to_hbm):
    rows = hbm.at[pl.ds(0, vmem.shape[0]), :]
    (pltpu.make_async_copy(vmem, rows, sem) if to_hbm else pltpu.make_async_copy(rows, vmem, sem)).wait()


def _dispatch_kernel(nt, dest_ref, pad_ref, x_ref, g_ref, buf_out, xn_s, zero_s, sem, zsem):
    i = pl.program_id(0)
    tm = x_ref.shape[0]
    slot = i % 2
    n_pad = pad_ref.shape[2]

    def drain(s):
        for _ in range(TOP_K):
            _wait_rows(buf_out, xn_s.at[s], sem.at[s], True)

    @pl.when(i >= 2)
    def _():
        drain(slot)

    zero_s[...] = jnp.zeros(zero_s.shape, F32)

    def zero(q, c):
        _row_copy(zero_s, q, buf_out, pad_ref[0, 0, q], zsem.at[0]).start()
        return c

    lax.fori_loop(0, n_pad, zero, 0, unroll=8)
    _to_tiles(xn_s.at[slot], _rms(x_ref[...], g_ref[...]))

    def start(r, c):
        for k in range(TOP_K):
            _row_copy(xn_s.at[slot], r, buf_out, dest_ref[0, 0, TOP_K * r + k],
                      sem.at[slot]).start(priority=k % DMA_QUEUES)
        return c

    lax.fori_loop(0, tm, start, 0, unroll=8)
    _wait_rows(buf_out, zero_s, zsem.at[0], True)

    @pl.when(i == nt - 1)
    def _():
        drain(slot)
        if nt > 1:
            drain(1 - slot)


def _dispatch(x, g, dest, pad, rows, tm):
    t, d = x.shape
    nt = t // tm
    n_pad = pad.shape[0] // nt
    assert n_pad * nt == pad.shape[0]
    return pl.pallas_call(
        functools.partial(_dispatch_kernel, nt),
        grid=(nt,),
        in_specs=[pl.BlockSpec((1, 1, TOP_K * tm), lambda i: (i, 0, 0), memory_space=pltpu.SMEM),
                  pl.BlockSpec((1, 1, n_pad), lambda i: (i, 0, 0), memory_space=pltpu.SMEM),
                  pl.BlockSpec((tm, d), lambda i: (i, 0)),
                  pl.BlockSpec((1, d), lambda i: (0, 0))],
        out_specs=pl.BlockSpec(memory_space=pl.ANY),
        out_shape=jax.ShapeDtypeStruct((rows * ROW_TILE, LANES), F32),
        scratch_shapes=[pltpu.VMEM((2, tm * ROW_TILE, LANES), F32),
                        pltpu.VMEM((n_pad * ROW_TILE, LANES), F32),
                        pltpu.SemaphoreType.DMA((2,)), pltpu.SemaphoreType.DMA((1,))],
        compiler_params=_cparams(("arbitrary",)),
        name="moe_dispatch",
    )(dest.reshape(nt, 1, TOP_K * tm), pad.reshape(nt, 1, n_pad), x, g.reshape(1, d))


def _expert_kernel(be_ref, nu_ref, x_ref, w1_ref, w3_ref, w2_ref, y_ref):
    del be_ref
    used = pl.program_id(0) < nu_ref[0]

    @pl.when(jnp.logical_not(used))
    def _():
        y_ref[...] = jnp.zeros(y_ref.shape, F32)

    @pl.when(used)
    def _():
        x = _from_tiles(x_ref, x_ref.shape[0] // ROW_TILE).astype(BF16)
        y = _swiglu(x, lambda c: w1_ref[0, :, c], lambda c: w3_ref[0, :, c], lambda c: w2_ref[0, c, :])
        _to_tiles(y_ref, y)


def _experts(buf, blk_e, n_used, w1, w3, w2, blk):
    d, f = w1.shape[1], w1.shape[2]
    once = pl.Buffered(2)
    grid_spec = pltpu.PrefetchScalarGridSpec(
        num_scalar_prefetch=2,
        grid=(buf.shape[0] // (blk * ROW_TILE),),
        in_specs=[pl.BlockSpec((blk * ROW_TILE, LANES), lambda b, be, nu: (b, 0)),
                  pl.BlockSpec((1, d, f), lambda b, be, nu: (be[b], 0, 0), pipeline_mode=once),
                  pl.BlockSpec((1, d, f), lambda b, be, nu: (be[b], 0, 0), pipeline_mode=once),
                  pl.BlockSpec((1, f, d), lambda b, be, nu: (be[b], 0, 0), pipeline_mode=once)],
        out_specs=pl.BlockSpec((blk * ROW_TILE, LANES), lambda b, be, nu: (b, 0)))
    return pl.pallas_call(
        _expert_kernel,
        grid_spec=grid_spec,
        out_shape=jax.ShapeDtypeStruct(buf.shape, F32),
        compiler_params=_cparams(("arbitrary",)),
        name="moe_experts",
    )(blk_e, n_used, buf, w1, w3, w2)


def _combine_kernel(nt, dcur_ref, dnxt_ref, x_ref, rt_ref, fn_ref, y_hbm, o_ref, ya, yb, sem):
    i = pl.program_id(0)
    tm = x_ref.shape[0]
    slot = i % 2

    def gather(dref, s):
        def start(r, c):
            _row_copy(y_hbm, dref[0, 0, TOP_K * r], ya.at[s], r, sem.at[s, 0]).start(priority=0)
            _row_copy(y_hbm, dref[0, 0, TOP_K * r + 1], yb.at[s], r, sem.at[s, 1]).start(
                priority=1 % DMA_QUEUES)
            return c

        lax.fori_loop(0, tm, start, 0, unroll=8)

    @pl.when(i == 0)
    def _():
        gather(dcur_ref, slot)

    @pl.when(i + 1 < nt)
    def _():
        gather(dnxt_ref, 1 - slot)

    _wait_rows(y_hbm, ya.at[slot], sem.at[slot, 0], False)
    _wait_rows(y_hbm, yb.at[slot], sem.at[slot, 1], False)
    rt = rt_ref[...]
    lane = lax.broadcasted_iota(jnp.int32, rt.shape, 1)
    g1 = jnp.sum(jnp.where(lane == 2, rt, 0.0), axis=-1, keepdims=True)
    g2 = jnp.sum(jnp.where(lane == 3, rt, 0.0), axis=-1, keepdims=True)
    moe = g1 * _from_tiles(ya.at[slot], tm) + g2 * _from_tiles(yb.at[slot], tm)
    o_ref[...] = _rms(x_ref[...] + moe, fn_ref[...])


def _combine(x, route, fnorm, y, dest, tm):
    t, d = x.shape
    nt = t // tm
    dest3 = dest.reshape(nt, 1, TOP_K * tm)
    return pl.pallas_call(
        functools.partial(_combine_kernel, nt),
        grid=(nt,),
        in_specs=[pl.BlockSpec((1, 1, TOP_K * tm), lambda i: (i, 0, 0), memory_space=pltpu.SMEM),
                  pl.BlockSpec((1, 1, TOP_K * tm), lambda i: (jnp.minimum(i + 1, nt - 1), 0, 0),
                               memory_space=pltpu.SMEM),
                  pl.BlockSpec((tm, d), lambda i: (i, 0)),
                  pl.BlockSpec((tm, LANES), lambda i: (i, 0)),
                  pl.BlockSpec((1, d), lambda i: (0, 0)),
                  pl.BlockSpec(memory_space=pl.ANY)],
        out_specs=pl.BlockSpec((tm, d), lambda i: (i, 0)),
        out_shape=jax.ShapeDtypeStruct((t, d), F32),
        scratch_shapes=[pltpu.VMEM((2, tm * ROW_TILE, LANES), F32),
                        pltpu.VMEM((2, tm * ROW_TILE, LANES), F32),
                        pltpu.SemaphoreType.DMA((2, 2))],
        compiler_params=_cparams(("arbitrary",)),
        name="moe_combine",
    )(dest3, dest3, x, route, fnorm.reshape(1, d), y)


def _moe(x, route, g, w1, w3, w2, fnorm, tm):
    blk = min(MOE_BLOCK, x.shape[0])
    dest, pad, blk_e, n_used, nblk = _route_plan(route, blk)
    buf = _dispatch(x, g, dest, pad, nblk * blk, tm)
    y = _experts(buf, blk_e, n_used, w1, w3, w2, blk)
    return _combine(x, route, fnorm, y, dest, tm)


def _block_diag(w):
    g, a, b = w.shape
    eye = jnp.eye(g, dtype=w.dtype)
    return (eye[:, None, :, None] * w[:, :, None, :]).reshape(g * a, g * b)


def kernel(x, l0_norm_mix, l0_w_in, l0_conv_w, l0_conv_b, l0_gate_a_w, l0_gate_a_b, l0_gate_x_w, l0_gate_x_b, l0_lru_lambda, l0_shift_mu, l0_w0, l0_w_up, l0_a0, l0_a_up, l0_g_up, l0_k_k, l0_k_a, l0_r_k, l0_ln_x_w, l0_ln_x_b, l0_w_out, l0_norm_ffn, l0_ffn_w1, l0_ffn_w3, l0_ffn_w2, l1_norm_mix, l1_w_in, l1_conv_w, l1_conv_b, l1_wq, l1_wk, l1_wv, l1_ig_b, l1_fg_b, l1_mh_norm_w, l1_w_out, l1_norm_ffn, l1_router, l1_moe_w1, l1_moe_w3, l1_moe_w2, final_norm):
    batch, seq, d = x.shape
    assert d == D_MODEL == ROW_TILE * LANES and seq % RW_CHUNK == 0 and seq % MLSTM_CHUNK == 0
    t = batch * seq
    xt = x.reshape(t, d)
    tm = min(TOKEN_TILE, t)

    w_in0 = l0_w_in.astype(BF16)
    p_lru, p_rw = _norm_mm(xt, l0_norm_mix, w_in0, (2 * LRU_WIDTH, RW_IN), tm, "l0_in")
    y_lru = _lru(p_lru, batch, seq, l0_conv_w, l0_conv_b,
                 _block_diag(l0_gate_a_w).astype(BF16), l0_gate_a_b,
                 _block_diag(l0_gate_x_w).astype(BF16), l0_gate_x_b, l0_lru_lambda,
                 min(TOKEN_TILE, seq))
    y_rw = _rwkv(p_rw, batch, seq, l0_shift_mu, l0_w0, l0_w_up, l0_a0, l0_a_up, l0_g_up,
                 l0_k_k, l0_k_a, l0_r_k.reshape(-1), l0_ln_x_w, l0_ln_x_b)
    w_out0 = l0_w_out.astype(BF16)
    ne, _, f = l1_moe_w1.shape
    x2, (mw1, mw3, mw2) = _ffn(
        xt, y_lru, y_rw, w_out0[:LRU_WIDTH], w_out0[LRU_WIDTH:], l0_norm_ffn,
        l0_ffn_w1.astype(BF16), l0_ffn_w3.astype(BF16), l0_ffn_w2.astype(BF16),
        [l1_moe_w1.reshape(ne * d, f), l1_moe_w3.reshape(ne * d, f), l1_moe_w2.reshape(ne * f, d)], tm)

    n_in1 = 2 * MLSTM_WIDTH + LANES
    w_in1 = jnp.zeros((d, n_in1), F32).at[:, :l1_w_in.shape[1]].set(l1_w_in).astype(BF16)
    h1 = _mlstm(x2, l1_norm_mix, w_in1, batch, seq, l1_conv_w, l1_conv_b, l1_wq, l1_wk, l1_wv,
                l1_ig_b, l1_fg_b, l1_mh_norm_w, tm)
    router = jnp.zeros((d, LANES), F32).at[:, :N_EXPERTS].set(l1_router)
    x3, route = _out_router(x2, h1, l1_w_out.astype(BF16), l1_norm_ffn, router, tm)
    out = _moe(x3, route, l1_norm_ffn, mw1.reshape(ne, d, f), mw3.reshape(ne, d, f),
               mw2.reshape(ne, f, d), final_norm, tm)
    return out.reshape(batch, seq, d)
```

```python
import functools

import jax
import jax.numpy as jnp
from jax import lax
from jax.experimental import pallas as pl
from jax.experimental.pallas import tpu as pltpu

F32 = jnp.float32
BF16 = jnp.bfloat16

D_MODEL = 1024
TOKEN_TILE = 512
LRU_WIDTH = 512
LRU_C = 8.0
LRU_SEGS = 8
CONV_WIDTH = 4
RWKV_HEADS = 8
RWKV_HEAD = 64
RWKV_WIDTH = 512
LN_X_EPS = 1e-5 * RWKV_HEAD
MLSTM_HEADS = 8
MLSTM_HEAD = 128
MLSTM_WIDTH = 1024
MLSTM_CHUNK = 128
MLSTM_STEP_ROWS = 256
D_FF = 2816
N_EXPERTS = 8
TOP_K = 2
DMA_QUEUES = 2
ROUTER_SPLIT = 4
MOE_BLOCK = 512
FF_CHUNKS = (768, 768, 768, 512)
RMS_EPS = 1e-6
RW_IN = 3 * RWKV_WIDTH + 64 + 64 + 128
RW_CHUNK = 64
RW_STEP_ROWS = 512
RW_WAVE_ROWS = 256
LANES = 128
MXU_WIDTH = 256
CARRY_ROWS = 8
ROW_TILE = 8
VMEM_LIMIT = 56 * 1024 * 1024


def _cparams(sem):
    return pltpu.CompilerParams(dimension_semantics=sem, vmem_limit_bytes=VMEM_LIMIT)


def _rms(x, w):
    return x * lax.rsqrt(jnp.mean(x * x, axis=-1, keepdims=True) + RMS_EPS) * w


def _sigmoid(x):
    return 1.0 / (1.0 + jnp.exp(-x))


def _softplus(x):
    return jnp.maximum(x, 0.0) + jnp.log(1.0 + jnp.exp(-jnp.abs(x)))


def _dot(a, b):
    return jnp.dot(a, b, preferred_element_type=F32)


def _dot_nt(a, b):
    return lax.dot_general(a, b, (((1,), (1,)), ((), ())), preferred_element_type=F32)


def _dot_tn(a, b):
    return lax.dot_general(a, b, (((0,), (0,)), ((), ())), preferred_element_type=F32)


def _split2(x):
    hi = x.astype(BF16)
    lo = (x - hi.astype(F32)).astype(BF16)
    return hi, lo


def _split3(x):
    hi = x.astype(BF16)
    r = x - hi.astype(F32)
    mid = r.astype(BF16)
    lo = (r - mid.astype(F32)).astype(BF16)
    return hi, mid, lo


def _mm3(fn, a, b):
    ah, al = _split2(a)
    bh, bl = _split2(b)
    return fn(ah, bh) + fn(al, bh) + fn(ah, bl)


def _mm_exact_lhs(fn, a_bf16, b):
    h, m, l = _split3(b)
    return fn(a_bf16, h) + fn(a_bf16, m) + fn(a_bf16, l)


def _seg_sum(x, seg):
    s = seg.shape[0]
    outs = []
    for c in range(x.shape[1] // s):
        hi, lo = _split2(x[:, c * s:(c + 1) * s])
        outs.append(_dot(hi, seg) + _dot(lo, seg))
    return jnp.concatenate(outs, axis=1)


def _shift_hist(buf_ref, x, first):
    n = x.shape[0]

    @pl.when(first)
    def _():
        buf_ref[0:CARRY_ROWS, :] = jnp.zeros((CARRY_ROWS, x.shape[1]), F32)

    @pl.when(jnp.logical_not(first))
    def _():
        buf_ref[0:CARRY_ROWS, :] = buf_ref[n:n + CARRY_ROWS, :]

    buf_ref[CARRY_ROWS:CARRY_ROWS + n, :] = x


def _causal_conv(buf_ref, n, w, b, rotate):
    if rotate:
        full = buf_ref[0:CARRY_ROWS + n, :]
        acc = b + w[CONV_WIDTH - 1:CONV_WIDTH, :] * full[CARRY_ROWS:, :]
        for lag in range(1, CONV_WIDTH):
            tap = CONV_WIDTH - 1 - lag
            acc = acc + w[tap:tap + 1, :] * pltpu.roll(full, lag, axis=0)[CARRY_ROWS:, :]
        return acc
    acc = b
    for j in range(CONV_WIDTH):
        off = CARRY_ROWS - (CONV_WIDTH - 1) + j
        acc = acc + w[j:j + 1, :] * buf_ref[off:off + n, :]
    return acc


def _norm_mm_kernel(x_ref, g_ref, w_ref, *o_refs):
    xn = _rms(x_ref[...], g_ref[...]).astype(BF16)
    c0 = 0
    for o_ref in o_refs:
        c1 = c0 + o_ref.shape[1]
        o_ref[...] = _dot(xn, w_ref[:, c0:c1]).astype(o_ref.dtype)
        c0 = c1


def _norm_mm(x, g, w, widths, tm, name):
    t, d = x.shape
    n = w.shape[1]
    assert sum(widths) == n
    return pl.pallas_call(
        _norm_mm_kernel,
        grid=(t // tm,),
        in_specs=[pl.BlockSpec((tm, d), lambda i: (i, 0)),
                  pl.BlockSpec((1, d), lambda i: (0, 0)),
                  pl.BlockSpec((d, n), lambda i: (0, 0))],
        out_specs=[pl.BlockSpec((tm, c), lambda i: (i, 0)) for c in widths],
        out_shape=[jax.ShapeDtypeStruct((t, c), BF16) for c in widths],
        compiler_params=_cparams(("parallel",)),
        name=name,
    )(x, g.reshape(1, d), w)


def _lru_kernel(p_ref, cw_ref, cb_ref, wa_ref, ba_ref, wx_ref, bx_ref, lam_ref, o_ref,
                xbuf, abuf, bbuf, hbuf, hcar):
    i = pl.program_id(1)
    n = p_ref.shape[0]
    first = i == 0
    _shift_hist(xbuf, p_ref[:, 0:LRU_WIDTH].astype(F32), first)
    xc = _causal_conv(xbuf, n, cw_ref[...], cb_ref[...], True)
    xcb = xc.astype(BF16)
    r = _sigmoid(_dot(xcb, wa_ref[...]) + ba_ref[...])
    ig = _sigmoid(_dot(xcb, wx_ref[...]) + bx_ref[...])
    log_a = (-LRU_C) * r * _softplus(-lam_ref[...])
    a = jnp.exp(log_a)
    mult = jnp.sqrt(1.0 - a * a)
    row = lax.broadcasted_iota(jnp.int32, (n, 1), 0)
    mult = jnp.where(jnp.logical_and(first, row == 0), 1.0, mult)
    bvals = mult * ig * xc
    n_slab = LRU_WIDTH // LANES
    slabs = [slice(c * LANES, (c + 1) * LANES) for c in range(n_slab)]
    seg = n // LRU_SEGS
    pitch = _lru_pitch(n)
    for c, cs in enumerate(slabs):
        for s in range(LRU_SEGS):
            abuf[c, s * pitch:s * pitch + seg, :] = a[s * seg:(s + 1) * seg, cs]
            bbuf[c, s * pitch:s * pitch + seg, :] = bvals[s * seg:(s + 1) * seg, cs]

    @pl.when(first)
    def _():
        hcar[...] = jnp.zeros(hcar.shape, F32)

    def step(j, carry):
        rows = pl.ds(j, LRU_SEGS, stride=pitch)
        out = []
        for c in range(n_slab):
            h, acc = carry[c]
            a_j = abuf[c, rows, :]
            h = a_j * h + bbuf[c, rows, :]
            acc = acc * a_j
            hbuf[c, rows, :] = h
            abuf[c, rows, :] = acc
            out.append((h, acc))
        return tuple(out)

    init = tuple((jnp.zeros((LRU_SEGS, LANES), F32), jnp.ones((LRU_SEGS, LANES), F32))
                 for _ in range(n_slab))
    ends = lax.fori_loop(0, seg, step, init, unroll=8)
    cols = []
    for c, cs in enumerate(slabs):
        h_end, a_end = ends[c]
        h0 = hcar[0:1, cs]
        parts = []
        for s in range(LRU_SEGS):
            rows = slice(s * pitch, s * pitch + seg)
            parts.append(hbuf[c, rows, :] + abuf[c, rows, :] * h0)
            h0 = h_end[s:s + 1, :] + a_end[s:s + 1, :] * h0
        hcar[0:1, cs] = h0
        cols.append(jnp.concatenate(parts, axis=0))
    gate = p_ref[:, LRU_WIDTH:2 * LRU_WIDTH].astype(F32)
    gelu = 0.5 * gate * (1.0 + jnp.tanh(0.7978845608028654 * (gate + 0.044715 * gate * gate * gate)))
    o_ref[...] = (jnp.concatenate(cols, axis=1) * gelu).astype(o_ref.dtype)


def _lru_pitch(n):
    tiles = n // LRU_SEGS // CARRY_ROWS
    return (tiles + 1 - tiles % 2) * CARRY_ROWS


def _lru(p_lru, batch, seq, conv_w, conv_b, wa, ba, wx, bx, lam, tb):
    nb = seq // tb
    c = LRU_WIDTH
    vec = lambda: pl.BlockSpec((1, c), lambda b, i: (0, 0))
    return pl.pallas_call(
        _lru_kernel,
        grid=(batch, nb),
        in_specs=[pl.BlockSpec((tb, 2 * c), lambda b, i: (b * nb + i, 0)),
                  pl.BlockSpec((CONV_WIDTH, c), lambda b, i: (0, 0)), vec(),
                  pl.BlockSpec((c, c), lambda b, i: (0, 0)), vec(),
                  pl.BlockSpec((c, c), lambda b, i: (0, 0)), vec(), vec()],
        out_specs=pl.BlockSpec((tb, c), lambda b, i: (b * nb + i, 0)),
        out_shape=jax.ShapeDtypeStruct((batch * seq, c), BF16),
        scratch_shapes=[pltpu.VMEM((tb + CARRY_ROWS, c), F32)]
        + [pltpu.VMEM((c // LANES, LRU_SEGS * _lru_pitch(tb), LANES), F32)] * 3
        + [pltpu.VMEM((CARRY_ROWS, c), F32)],
        compiler_params=_cparams(("parallel", "arbitrary")),
        name="rg_lru",
    )(p_lru, conv_w, conv_b.reshape(1, c), wa, ba.reshape(1, c), wx, bx.reshape(1, c),
      lam.reshape(1, c))


def _blk(x, masks):
    xb = x.astype(BF16)
    return jnp.concatenate([jnp.where(mk, xb, jnp.zeros_like(xb)) for mk in masks], axis=0)


def _rwkv_kernel(p_ref, mu_ref, w0_ref, wup_ref, a0_ref, aup_ref, gup_ref, kk_ref, ka_ref,
                 rk_ref, lnw_ref, lnb_ref, seg_ref, tri_ref, o_ref, pbuf, state):
    L = RW_CHUNK
    W = RWKV_WIDTH
    rows = p_ref.shape[0]
    wave = tri_ref.shape[0]
    first = pl.program_id(1) == 0
    p = p_ref[...].astype(F32)
    _shift_hist(pbuf, p, first)
    prev = pltpu.roll(pbuf[0:CARRY_ROWS + rows, :], 1, axis=0)[CARRY_ROWS:, :]
    ps = p + mu_ref[...] * (prev - p)
    seg = seg_ref[...]
    tri = tri_ref[...]

    @pl.when(first)
    def _():
        state[...] = jnp.zeros(state.shape, F32)

    lane = lax.broadcasted_iota(jnp.int32, (1, LANES), 1)
    m1 = [lane < RWKV_HEAD, lane >= RWKV_HEAD]
    m2 = [jnp.concatenate([mk, mk], axis=1) for mk in m1]
    ti = lax.broadcasted_iota(jnp.int32, (L, LANES), 0)
    si = lax.broadcasted_iota(jnp.int32, (L, LANES), 1) % RWKV_HEAD
    strict = si < ti
    incl = si <= ti
    bi = lax.broadcasted_iota(jnp.int32, (LANES, LANES), 0) // RWKV_HEAD
    bj = lax.broadcasted_iota(jnp.int32, (LANES, LANES), 1) // RWKV_HEAD
    bd = bi == bj

    def tril(mask, s):
        return jnp.where(mask, s, 0.0).astype(BF16)

    wv = [slice(w * wave, (w + 1) * wave) for w in range(rows // wave)]
    ws = range(len(wv))
    r = [ps[rw, 0:W] for rw in wv]
    k = [ps[rw, W:2 * W] for rw in wv]
    v = [ps[rw, 2 * W:3 * W] for rw in wv]
    x2 = [ps[rw, 3 * W:3 * W + LANES] for rw in wv]
    xg = [ps[rw, 3 * W + LANES:3 * W + 2 * LANES] for rw in wv]
    wl = [w0_ref[...] + _dot(jnp.tanh(x2[w]).astype(BF16), wup_ref[...]) for w in ws]
    a = [_sigmoid(a0_ref[...] + _dot(x2[w].astype(BF16), aup_ref[...])) for w in ws]
    g = [_dot(_sigmoid(xg[w]).astype(BF16), gup_ref[...]) for w in ws]
    kk = [k[w] * kk_ref[...] for w in ws]
    kk = [kk[w] / jnp.maximum(jnp.sqrt(_seg_sum(kk[w] * kk[w], seg)), 1e-12) for w in ws]
    lw = [-jnp.exp(-_softplus(-wl[w]) - 0.5) for w in ws]
    cw = [_mm_exact_lhs(_dot, tri, lw[w]) for w in ws]

    n_pairs = RWKV_HEADS // 2
    n_ch = wave // L
    pair_cols = [slice(pr * LANES, (pr + 1) * LANES) for pr in range(n_pairs)]
    ops = {}

    def scale_pair(w, pr):
        sl = pair_cols[pr]
        cwp = cw[w][:, sl]
        cw_end = jnp.concatenate(
            [jnp.broadcast_to(cwp[c * L + L - 1:c * L + L, :], (L, LANES)) for c in range(n_ch)], axis=0)
        w_inv = jnp.exp(-cwp)
        w_end = jnp.exp(cw_end - cwp)
        k2 = k[w][:, sl] * (1.0 + (a[w][:, sl] - 1.0) * ka_ref[:, sl])
        kka = kk[w][:, sl] * a[w][:, sl]
        ops[w, pr] = dict(a_t=-kk[w][:, sl] * jnp.exp(cwp - lw[w][:, sl]), b_t=kka * w_inv, k_t=k2 * w_inv,
                          r_t=r[w][:, sl] * jnp.exp(cwp), b_bar=kka * w_end, k_bar=k2 * w_end,
                          w_tot=jnp.exp(cw_end), k2=k2, v=v[w][:, sl])

    for pr in range(n_pairs):
        scale_pair(0, pr)
    s_cur = [state[pr] for pr in range(n_pairs)]
    for w in ws:
        todo = [pr for pr in range(n_pairs)] if w + 1 < len(wv) else []

        def next_piece():
            if todo:
                scale_pair(w + 1, todo.pop(0))

        chains = [(slice(c * L, (c + 1) * L), pr) for c in range(n_ch) for pr in range(n_pairs)]
        idx = range(len(chains))
        a_c = [ops[w, pr]["a_t"][rs, :] for rs, pr in chains]
        r_c = [ops[w, pr]["r_t"][rs, :] for rs, pr in chains]
        ar = [jnp.concatenate([a_c[i], r_c[i]], axis=0).astype(BF16) for i in idx]
        sb = [_dot_nt(ar[i], _blk(ops[w, pr]["b_t"][rs, :], m1)) for i, (rs, pr) in enumerate(chains)]
        sk = [_dot_nt(ar[i], _blk(ops[w, pr]["k_t"][rs, :], m1)) for i, (rs, pr) in enumerate(chains)]
        vblk = [_blk(ops[w, pr]["v"][rs, :], m1) for rs, pr in chains]
        next_piece()
        rhs = [jnp.concatenate([a_c[i], _dot(tril(strict, sk[i][0:L]), vblk[i])], axis=1) for i in idx]
        pw = [tril(strict, sb[i][0:L]) for i in idx]
        tinv = [jnp.where(si == ti, 1.0, 0.0) + pw[i].astype(F32) for i in idx]
        pw = [_dot(pw[i], _blk(pw[i], m1)).astype(BF16) for i in idx]
        next_piece()
        for _ in range(4):
            both = [_dot(pw[i], jnp.concatenate([_blk(pw[i], m1), _blk(tinv[i], m1)], axis=1)) for i in idx]
            tinv = [tinv[i] + both[i][:, LANES:2 * LANES] for i in idx]
            pw = [both[i][:, 0:LANES].astype(BF16) for i in idx]
            next_piece()
        tinv = [tinv[i] + _dot(pw[i], _blk(tinv[i], m1)) for i in idx]
        x = [_dot(tinv[i].astype(BF16), _blk(rhs[i], m2)) for i in idx]
        corr = [_dot(tril(incl, sb[i][L:2 * L]), _blk(x[i], m2)) for i in idx]
        r_hat = [(r_c[i] + corr[i][:, 0:LANES]).astype(BF16) for i in idx]
        y0 = [corr[i][:, LANES:2 * LANES] + _dot(tril(incl, sk[i][L:2 * L]), vblk[i]) for i in idx]
        xb = [x[i].astype(BF16) for i in idx]
        bb = [ops[w, pr]["b_bar"][rs, :].astype(BF16) for rs, pr in chains]
        gmat = [jnp.where(bd, _dot_tn(xb[i][:, 0:LANES], bb[i]), 0.0).astype(BF16) for i in idx]
        hmat = [jnp.where(bd, _dot_tn(xb[i][:, LANES:2 * LANES], bb[i])
                          + _dot_tn(ops[w, pr]["v"][rs, :].astype(BF16),
                                    ops[w, pr]["k_bar"][rs, :].astype(BF16)), 0.0)
                for i, (rs, pr) in enumerate(chains)]

        y_rows = []
        for c in range(n_ch):
            ys = []
            for pr in range(n_pairs):
                i = c * n_pairs + pr
                s0 = s_cur[pr]
                s0b = s0.astype(BF16)
                ys.append(_dot_nt(r_hat[i], s0b) + y0[i])
                s_cur[pr] = s0 * ops[w, pr]["w_tot"][c * L:c * L + 1, :] + _dot(s0b, gmat[i]) + hmat[i]
            y_rows.append(jnp.concatenate(ys, axis=1))
        y = jnp.concatenate(y_rows, axis=0)
        k2 = jnp.concatenate([ops[w, pr]["k2"] for pr in range(n_pairs)], axis=1)

        inv = 1.0 / RWKV_HEAD
        mean = _seg_sum(y, seg) * inv
        yc = y - mean
        var = _seg_sum(yc * yc, seg) * inv
        yn = yc * lax.rsqrt(var + LN_X_EPS) * lnw_ref[...] + lnb_ref[...]
        bonus = _seg_sum(r[w] * k2 * rk_ref[...], seg) * v[w]
        o_ref[wv[w], :] = ((yn + bonus) * g[w]).astype(o_ref.dtype)
    for pr in range(n_pairs):
        state[pr] = s_cur[pr]


def _rwkv(p_rw, batch, seq, mu, w0, w_up, a0, a_up, g_up, k_k, k_a, r_k, ln_w, ln_b):
    L = min(RW_STEP_ROWS, seq)
    nc = seq // L
    W = RWKV_WIDTH
    wup = jnp.zeros((LANES, W), F32).at[0:64].set(w_up).astype(BF16)
    aup = jnp.zeros((LANES, W), F32).at[64:128].set(a_up).astype(BF16)
    hid = jnp.arange(MXU_WIDTH) // RWKV_HEAD
    seg = (hid[:, None] == hid[None, :]).astype(BF16)
    wave = min(RW_WAVE_ROWS, L)
    ri = jnp.arange(wave)
    tri = jnp.logical_and(ri[:, None] >= ri[None, :],
                          ri[:, None] // RW_CHUNK == ri[None, :] // RW_CHUNK).astype(BF16)
    vec = lambda: pl.BlockSpec((1, W), lambda b, i: (0, 0))
    mat = lambda s: pl.BlockSpec(s, lambda b, i: (0, 0))
    return pl.pallas_call(
        _rwkv_kernel,
        grid=(batch, nc),
        in_specs=[pl.BlockSpec((L, RW_IN), lambda b, i: (b * nc + i, 0)),
                  mat((1, RW_IN)), vec(), mat((LANES, W)), vec(), mat((LANES, W)),
                  mat((LANES, W)), vec(), vec(), vec(), vec(), vec(), mat((MXU_WIDTH, MXU_WIDTH)),
                  mat((wave, wave))],
        out_specs=pl.BlockSpec((L, W), lambda b, i: (b * nc + i, 0)),
        out_shape=jax.ShapeDtypeStruct((batch * seq, W), BF16),
        scratch_shapes=[pltpu.VMEM((L + CARRY_ROWS, RW_IN), F32),
                        pltpu.VMEM((RWKV_HEADS // 2, LANES, LANES), F32)],
        compiler_params=_cparams(("parallel", "arbitrary")),
        name="rwkv7",
    )(p_rw, mu.reshape(1, RW_IN), w0.reshape(1, W), wup, a0.reshape(1, W), aup,
      g_up.astype(BF16), k_k.reshape(1, W), k_a.reshape(1, W), r_k.reshape(1, W),
      ln_w.reshape(1, W), ln_b.reshape(1, W), seg, tri)


def _swiglu(xb, w1, w3, w2):
    edges = [sum(FF_CHUNKS[:i]) for i in range(len(FF_CHUNKS) + 1)]
    cols = [slice(edges[i], edges[i + 1]) for i in range(len(FF_CHUNKS))]
    hid = (_dot(xb, w1(cols[0])), _dot(xb, w3(cols[0])))
    acc = None
    for i, c in enumerate(cols):
        nxt = (_dot(xb, w1(cols[i + 1])), _dot(xb, w3(cols[i + 1]))) if i + 1 < len(cols) else None
        act = (hid[0] * _sigmoid(hid[0]) * hid[1]).astype(BF16)
        part = _dot(act, w2(c))
        acc = part if acc is None else acc + part
        hid = nxt
    return acc


def _ffn_kernel(n_cast, x_ref, ya_ref, yb_ref, wa_ref, wb_ref, g_ref, w1_ref, w3_ref, w2_ref, *rest):
    src, o_ref, dst = rest[:n_cast], rest[n_cast], rest[n_cast + 1:]
    x1 = x_ref[...] + _dot(ya_ref[...], wa_ref[...]) + _dot(yb_ref[...], wb_ref[...])
    xn = _rms(x1, g_ref[...]).astype(BF16)
    o_ref[...] = x1 + _swiglu(xn, lambda c: w1_ref[:, c], lambda c: w3_ref[:, c], lambda c: w2_ref[c, :])
    for s_ref, d_ref in zip(src, dst):
        d_ref[...] = s_ref[...].astype(BF16)


def _ffn(x, ya, yb, wa, wb, g, w1, w3, w2, to_cast, tm):
    t, d = x.shape
    nt = t // tm
    once = pl.Buffered(1)
    fixed = lambda a: pl.BlockSpec(a.shape, lambda i: (0, 0), pipeline_mode=once)
    slab = lambda a: pl.BlockSpec((a.shape[0] // nt, a.shape[1]), lambda i: (i, 0))
    assert all(a.shape[0] % (nt * 2 * ROW_TILE) == 0 for a in to_cast)
    outs = pl.pallas_call(
        functools.partial(_ffn_kernel, len(to_cast)),
        grid=(nt,),
        in_specs=[pl.BlockSpec((tm, d), lambda i: (i, 0)),
                  pl.BlockSpec((tm, ya.shape[1]), lambda i: (i, 0)),
                  pl.BlockSpec((tm, yb.shape[1]), lambda i: (i, 0)),
                  fixed(wa), fixed(wb),
                  pl.BlockSpec((1, d), lambda i: (0, 0)),
                  fixed(w1), fixed(w3), fixed(w2)] + [slab(a) for a in to_cast],
        out_specs=[pl.BlockSpec((tm, d), lambda i: (i, 0))] + [slab(a) for a in to_cast],
        out_shape=[jax.ShapeDtypeStruct((t, d), F32)]
        + [jax.ShapeDtypeStruct(a.shape, BF16) for a in to_cast],
        compiler_params=_cparams(("parallel",)),
        name="ffn_swiglu",
    )(x, ya, yb, wa, wb, g.reshape(1, d), w1, w3, w2, *to_cast)
    return outs[0], outs[1:]


def _mlstm_in_kernel(nb, x_ref, g_ref, w_ref, cw_ref, cb_ref, wq_ref, wk_ref, wv_ref, gb_ref, tri_ref,
                     q_ref, k_ref, v_ref, z_ref, gs_ref, gtr_ref, xbuf, xm_s, z_s, gt_s):
    i = pl.program_id(0)
    n = x_ref.shape[0]
    W = MLSTM_WIDTH
    dh = MLSTM_HEAD
    L = MLSTM_CHUNK
    nh = MLSTM_HEADS
    slot = i % 2

    @pl.when(i == 0)
    def _():
        xm_s[...] = jnp.zeros(xm_s.shape, BF16)
        z_s[...] = jnp.zeros(z_s.shape, BF16)
        gt_s[...] = jnp.zeros(gt_s.shape, F32)

    xm = xm_s[1 - slot]
    z_ref[...] = z_s[1 - slot]
    gates = gt_s[1 - slot]
    _shift_hist(xbuf, xm.astype(F32), jnp.logical_or(i == 0, (i - 1) % nb == 0))

    xn = _rms(x_ref[...], g_ref[...]).astype(BF16)
    cw = cw_ref[...]
    cb = cb_ref[...]
    scale = dh ** -0.5
    heads_per = MXU_WIDTH // dh
    groups = [slice(c * MXU_WIDTH, (c + 1) * MXU_WIDTH) for c in range(W // MXU_WIDTH)]

    def qkv(c, xc):
        for u in range(heads_per):
            h = c * heads_per + u
            sl = slice(h * dh, (h + 1) * dh)
            xh = xc[:, u * dh:(u + 1) * dh]
            q_ref[:, sl] = (_dot(xh, wq_ref[h]) * scale).astype(BF16)
            k_ref[:, sl] = _dot(xh, wk_ref[h]).astype(BF16)
            v_ref[:, sl] = _dot(xm[:, sl], wv_ref[h]).astype(BF16)

    xm_new, z_new, xc_prev = [], [], None
    for c, cols in enumerate(groups):
        xm_new.append(_dot(xn, w_ref[:, cols]).astype(BF16))
        z_new.append(_dot(xn, w_ref[:, W + c * MXU_WIDTH:W + (c + 1) * MXU_WIDTH]).astype(BF16))
        full = xbuf[0:CARRY_ROWS + n, cols]
        acc = cb[:, cols] + cw[CONV_WIDTH - 1:CONV_WIDTH, cols] * full[CARRY_ROWS:, :]
        for lag in range(1, CONV_WIDTH):
            tap = CONV_WIDTH - 1 - lag
            acc = acc + cw[tap:tap + 1, cols] * pltpu.roll(full, lag, axis=0)[CARRY_ROWS:, :]
        if c > 0:
            qkv(c - 1, xc_prev)
        xc_prev = (acc * _sigmoid(acc)).astype(BF16)
    qkv(len(groups) - 1, xc_prev)
    gt_new = _dot(xn, w_ref[:, 2 * W:2 * W + LANES])

    lane = lax.broadcasted_iota(jnp.int32, (L, LANES), 1)
    row = lax.broadcasted_iota(jnp.int32, (L, LANES), 0)
    for c in range(n // L):
        rs = slice(c * L, (c + 1) * L)
        gl = gates[rs, :] + gb_ref[...]
        lf = jnp.minimum(gl, 0.0) - jnp.log(1.0 + jnp.exp(-jnp.abs(gl)))
        bcum = _mm_exact_lhs(_dot, tri_ref[...], lf)
        b_al = pltpu.roll(bcum, LANES - nh, axis=1)
        gmat = gl - b_al
        cmax = gmat
        d = 1
        while d < L:
            cmax = jnp.maximum(cmax, jnp.where(row >= d, pltpu.roll(cmax, d, axis=0), -jnp.inf))
            d *= 2
        gs_ref[rs, :] = jnp.where(lane < nh, gmat,
                                  jnp.where(lane < 2 * nh, pltpu.roll(cmax, nh, axis=1),
                                            pltpu.roll(b_al, 2 * nh, axis=1)))
        gtr_ref[rs, :] = gmat.T
    xm_s[slot] = jnp.concatenate(xm_new, axis=1)
    z_s[slot] = jnp.concatenate(z_new, axis=1)
    gt_s[slot] = gt_new


def _mlstm_kernel(q_ref, k_ref, v_ref, z_ref, gs_ref, gtr_ref, nw_ref, o_ref, cn_s, m_s):
    L = MLSTM_CHUNK
    dh = MLSTM_HEAD
    nh = MLSTM_HEADS
    n_ch = q_ref.shape[0] // L

    @pl.when(pl.program_id(1) == 0)
    def _():
        cn_s[...] = jnp.zeros(cn_s.shape, F32)
        m_s[...] = jnp.zeros(m_s.shape, F32)

    ti = lax.broadcasted_iota(jnp.int32, (L, L), 0)
    si = lax.broadcasted_iota(jnp.int32, (L, L), 1)
    causal = si <= ti
    ones = jnp.ones((L, dh), BF16)
    rws = [slice(c * L, (c + 1) * L) for c in range(n_ch)]
    sls = [slice(h * dh, (h + 1) * dh) for h in range(nh)]
    gs = [gs_ref[rw, :] for rw in rws]
    g_t = [gtr_ref[rw, :] for rw in rws]
    units = [(c, h) for c in range(n_ch) for h in range(nh)]

    head_lanes = jnp.logical_and(lax.broadcasted_iota(jnp.int32, (1, LANES), 1) >= nh,
                                 lax.broadcasted_iota(jnp.int32, (1, LANES), 1) < 2 * nh)
    mp = m_s[0:1, :]
    mx_all, sc_all, fl_all, kwe_all, dec_all = [], [], [], [], []
    for c in range(n_ch):
        b8 = pltpu.roll(gs[c], LANES - nh, axis=1)
        g8 = pltpu.roll(gs[c], nh, axis=1)
        mx = jnp.maximum(mp, gs[c])
        m_last = mx[L - 1:L, :]
        mx_all.append(mx)
        sc_all.append(jnp.exp(mp - mx))
        fl_all.append(jnp.exp(-(b8 + mx)))
        kwe_all.append(jnp.exp(g8 - m_last))
        dec_all.append(jnp.exp(mp - m_last))
        mp = jnp.where(head_lanes, b8[L - 1:L, :] + m_last, 0.0)
    m_s[0:1, :] = mp
    col = lambda arr, h: arr[:, nh + h:nh + h + 1]

    qb = {u: q_ref[rws[u[0]], sls[u[1]]] for u in units}
    kb = {u: k_ref[rws[u[0]], sls[u[1]]] for u in units}
    v1 = {u: jnp.concatenate([v_ref[rws[u[0]], sls[u[1]]], ones], axis=1) for u in units}
    qk = {u: _dot_nt(qb[u], kb[u]) for u in units}
    s = {(c, h): qk[c, h] * jnp.exp(jnp.where(causal, g_t[c][h:h + 1, :] - col(mx_all[c], h), -jnp.inf))
         for c, h in units}
    sv = {u: _dot(s[u].astype(BF16), v1[u]) for u in units}
    kw = {(c, h): kb[c, h].astype(F32) * col(kwe_all[c], h) for c, h in units}
    kv = {u: _dot_tn(kw[u].astype(BF16), v1[u]) for u in units}
    sc = {(c, h): col(sc_all[c], h) for c, h in units}
    floor = {(c, h): col(fl_all[c], h) for c, h in units}
    dec = {(c, h): col(dec_all[c], h) for c, h in units}

    cn = {h: cn_s[h] for h in range(nh)}
    for c in range(n_ch):
        hs = range(nh)
        qcn = {h: _dot(qb[c, h], cn[h].astype(BF16)) for h in hs}
        den = {h: sc[c, h] * qcn[h][:, dh:] + sv[c, h][:, dh:] for h in hs}
        hh = {h: (sc[c, h] * qcn[h][:, :dh] + sv[c, h][:, :dh]) / jnp.maximum(jnp.abs(den[h]), floor[c, h])
              for h in hs}
        ms = {h: jnp.mean(hh[h] * hh[h], axis=-1, keepdims=True) for h in hs}
        for h in hs:
            cn[h] = dec[c, h] * cn[h] + kv[c, h]
            hn = hh[h] * lax.rsqrt(ms[h] + RMS_EPS)
            gate = _sigmoid(z_ref[rws[c], sls[h]].astype(F32))
            o_ref[rws[c], sls[h]] = (gate * (hn * nw_ref[:, sls[h]])).astype(o_ref.dtype)
    for h in range(nh):
        cn_s[h] = cn[h]


def _mlstm(x, norm_w, w_in, batch, seq, conv_w, conv_b, wq, wk, wv, ig_b, fg_b, mh_w, tm):
    L = MLSTM_CHUNK
    nc = seq // L
    W = MLSTM_WIDTH
    t, d = x.shape
    tb = min(tm, seq)
    nb = seq // tb
    n_blk = batch * nb
    fixed = lambda shape: pl.BlockSpec(shape, lambda i: (0,) * len(shape))
    hw = lambda: fixed((MLSTM_HEADS, MLSTM_HEAD, MLSTM_HEAD))
    done = lambda width: pl.BlockSpec((tb, width), lambda i: (jnp.maximum(i - 1, 0), 0))
    gb = jnp.zeros((1, LANES), F32).at[0, 0:8].set(ig_b).at[0, 8:16].set(fg_b)
    tri = (jnp.arange(L)[:, None] >= jnp.arange(L)[None, :]).astype(BF16)
    q, k, v, z, gs, gtr = pl.pallas_call(
        functools.partial(_mlstm_in_kernel, nb),
        grid=(n_blk + 1,),
        in_specs=[pl.BlockSpec((tb, d), lambda i: (jnp.minimum(i, n_blk - 1), 0)),
                  fixed((1, d)), fixed(w_in.shape), fixed((CONV_WIDTH, W)), fixed((1, W)),
                  hw(), hw(), hw(), fixed((1, LANES)), fixed((L, L))],
        out_specs=[done(W), done(W), done(W), done(W), done(LANES), done(LANES)],
        out_shape=[jax.ShapeDtypeStruct((t, W), BF16)] * 4 + [jax.ShapeDtypeStruct((t, LANES), F32)] * 2,
        scratch_shapes=[pltpu.VMEM((tb + CARRY_ROWS, W), F32), pltpu.VMEM((2, tb, W), BF16),
                        pltpu.VMEM((2, tb, W), BF16), pltpu.VMEM((2, tb, LANES), F32)],
        compiler_params=_cparams(("arbitrary",)),
        name="l1_in_mlstm",
    )(x, norm_w.reshape(1, d), w_in, conv_w, conv_b.reshape(1, W), wq.astype(BF16), wk.astype(BF16),
      wv.astype(BF16), gb, tri)

    rb = min(MLSTM_STEP_ROWS, seq)
    nc = seq // rb
    chunk = lambda: pl.BlockSpec((rb, W), lambda b, i: (b * nc + i, 0))
    return pl.pallas_call(
        _mlstm_kernel,
        grid=(batch, nc),
        in_specs=[chunk(), chunk(), chunk(), chunk(),
                  pl.BlockSpec((rb, LANES), lambda b, i: (b * nc + i, 0)),
                  pl.BlockSpec((rb, LANES), lambda b, i: (b * nc + i, 0)),
                  pl.BlockSpec((1, W), lambda b, i: (0, 0))],
        out_specs=pl.BlockSpec((rb, W), lambda b, i: (b * nc + i, 0)),
        out_shape=jax.ShapeDtypeStruct((t, W), BF16),
        scratch_shapes=[pltpu.VMEM((MLSTM_HEADS, MLSTM_HEAD, 2 * MLSTM_HEAD), F32),
                        pltpu.VMEM((MLSTM_HEADS, LANES), F32)],
        compiler_params=_cparams(("parallel", "arbitrary")),
        name="mlstm",
    )(q, k, v, z, gs, gtr, mh_w.reshape(1, W))


def _out_router_kernel(x_ref, y_ref, w_ref, g_ref, rt_ref, x3_ref, o_ref):
    tm = x_ref.shape[0]
    sub = tm // ROUTER_SPLIT
    rs = [slice(q * sub, (q + 1) * sub) for q in range(ROUTER_SPLIT)]
    qs = range(ROUTER_SPLIT)
    w = w_ref[...]
    x3 = [x_ref[r, :] + _dot(y_ref[r, :], w) for r in rs]
    for q in qs:
        x3_ref[rs[q], :] = x3[q]
    gw = g_ref[...]
    rt = rt_ref[...]
    logits = [_mm3(_dot, _rms(x3[q], gw), rt) for q in qs]
    lane = lax.broadcasted_iota(jnp.int32, (sub, LANES), 1)
    lg = [jnp.where(lane < N_EXPERTS, logits[q], -jnp.inf) for q in qs]
    v1 = [jnp.max(lg[q], axis=-1, keepdims=True) for q in qs]
    i1 = [jnp.min(jnp.where(lg[q] == v1[q], lane, LANES), axis=-1, keepdims=True) for q in qs]
    lg2 = [jnp.where(lane == i1[q], -jnp.inf, lg[q]) for q in qs]
    v2 = [jnp.max(lg2[q], axis=-1, keepdims=True) for q in qs]
    i2 = [jnp.min(jnp.where(lg2[q] == v2[q], lane, LANES), axis=-1, keepdims=True) for q in qs]
    for q in qs:
        ex = jnp.exp(v2[q] - v1[q])
        g1 = 1.0 / (1.0 + ex)
        g2 = ex / (1.0 + ex)
        out = jnp.where(lane == 0, i1[q].astype(F32), 0.0)
        out = jnp.where(lane == 1, i2[q].astype(F32), out)
        out = jnp.where(lane == 2, g1, out)
        o_ref[rs[q], :] = jnp.where(lane == 3, g2, out)


def _out_router(x, y, w, g, router, tm):
    t, d = x.shape
    return pl.pallas_call(
        _out_router_kernel,
        grid=(t // tm,),
        in_specs=[pl.BlockSpec((tm, d), lambda i: (i, 0)),
                  pl.BlockSpec((tm, y.shape[1]), lambda i: (i, 0)),
                  pl.BlockSpec(w.shape, lambda i: (0, 0)),
                  pl.BlockSpec((1, d), lambda i: (0, 0)),
                  pl.BlockSpec((d, LANES), lambda i: (0, 0))],
        out_specs=[pl.BlockSpec((tm, d), lambda i: (i, 0)),
                   pl.BlockSpec((tm, LANES), lambda i: (i, 0))],
        out_shape=[jax.ShapeDtypeStruct((t, d), F32), jax.ShapeDtypeStruct((t, LANES), F32)],
        compiler_params=_cparams(("parallel",)),
        name="l1_out_router",
    )(x, y, w, g.reshape(1, d), router)


def _prefix_sum(x):
    m, c = x.shape
    group = min(LANES, m)
    xg = x.reshape(m // group, group, c).astype(F32)
    gi = jnp.arange(group)
    local = jnp.einsum("ij,bjc->bic", (gi[:, None] >= gi[None, :]).astype(F32), xg,
                       precision=lax.Precision.HIGHEST)
    bi = jnp.arange(m // group)
    before = jnp.dot((bi[:, None] > bi[None, :]).astype(F32), local[:, -1, :],
                     precision=lax.Precision.HIGHEST)
    return jnp.round(local + before[:, None, :]).astype(jnp.int32).reshape(m, c)


def _route_plan(route, blk):
    e = route[:, 0:TOP_K].astype(jnp.int32).reshape(-1)
    m = e.shape[0]
    onehot = (e[:, None] == jnp.arange(N_EXPERTS, dtype=jnp.int32)[None, :]).astype(jnp.int32)
    csum = _prefix_sum(onehot)
    rank = jnp.sum((csum - onehot) * onehot, axis=1)
    nb = (csum[-1] + blk - 1) // blk
    bend = _prefix_sum(nb[:, None])[:, 0]
    dest = jnp.sum(onehot * (bend - nb)[None, :], axis=1) * blk + rank
    nblk = m // blk + N_EXPERTS
    n_used = bend[-1]
    bidx = jnp.minimum(jnp.arange(nblk, dtype=jnp.int32), n_used - 1)
    blk_e = jnp.minimum(jnp.sum(bidx[:, None] >= bend[None, :], axis=1), N_EXPERTS - 1).astype(jnp.int32)
    gap_start = jnp.concatenate([(bend - nb) * blk + csum[-1], (n_used * blk)[None]])
    gap_len = jnp.concatenate([nb * blk - csum[-1], ((nblk - n_used) * blk)[None]])
    gap_end = _prefix_sum(gap_len[:, None])[:, 0]
    q = jnp.arange(nblk * blk - m, dtype=jnp.int32)
    gi = jnp.sum(q[:, None] >= gap_end[None, :], axis=1)
    pad = gap_start[gi] + q - (gap_end - gap_len)[gi]
    return dest.astype(jnp.int32), pad.astype(jnp.int32), blk_e, n_used.reshape(1).astype(jnp.int32), nblk


def _to_tiles(ref, x):
    n = x.shape[0]
    for s in range(ROW_TILE):
        ref[pl.ds(s, n, stride=ROW_TILE), :] = x[:, s * LANES:(s + 1) * LANES]


def _from_tiles(ref, n):
    return jnp.concatenate([ref[pl.ds(s, n, stride=ROW_TILE), :] for s in range(ROW_TILE)], axis=1)


def _row_copy(src, src_row, dst, dst_row, sem):
    s0 = pl.multiple_of(src_row * ROW_TILE, ROW_TILE)
    d0 = pl.multiple_of(dst_row * ROW_TILE, ROW_TILE)
    return pltpu.make_async_copy(src.at[pl.ds(s0, ROW_TILE), :], dst.at[pl.ds(d0, ROW_TILE), :], sem)


def _wait_rows(hbm, vmem, sem, to_hbm):
    rows = hbm.at[pl.ds(0, vmem.shape[0]), :]
    (pltpu.make_async_copy(vmem, rows, sem) if to_hbm else pltpu.make_async_copy(rows, vmem, sem)).wait()


def _dispatch_kernel(nt, dest_ref, pad_ref, x_ref, g_ref, buf_out, xn_s, zero_s, sem, zsem):
    i = pl.program_id(0)
    tm = x_ref.shape[0]
    slot = i % 2
    n_pad = pad_ref.shape[2]

    def drain(s):
        for _ in range(TOP_K):
            _wait_rows(buf_out, xn_s.at[s], sem.at[s], True)

    @pl.when(i >= 2)
    def _():
        drain(slot)

    zero_s[...] = jnp.zeros(zero_s.shape, F32)

    def zero(q, c):
        _row_copy(zero_s, q, buf_out, pad_ref[0, 0, q], zsem.at[0]).start()
        return c

    lax.fori_loop(0, n_pad, zero, 0, unroll=8)
    _to_tiles(xn_s.at[slot], _rms(x_ref[...], g_ref[...]))

    def start(r, c):
        for k in range(TOP_K):
            _row_copy(xn_s.at[slot], r, buf_out, dest_ref[0, 0, TOP_K * r + k],
                      sem.at[slot]).start(priority=k % DMA_QUEUES)
        return c

    lax.fori_loop(0, tm, start, 0, unroll=8)
    _wait_rows(buf_out, zero_s, zsem.at[0], True)

    @pl.when(i == nt - 1)
    def _():
        drain(slot)
        if nt > 1:
            drain(1 - slot)


def _dispatch(x, g, dest, pad, rows, tm):
    t, d = x.shape
    nt = t // tm
    n_pad = pad.shape[0] // nt
    assert n_pad * nt == pad.shape[0]
    return pl.pallas_call(
        functools.partial(_dispatch_kernel, nt),
        grid=(nt,),
        in_specs=[pl.BlockSpec((1, 1, TOP_K * tm), lambda i: (i, 0, 0), memory_space=pltpu.SMEM),
                  pl.BlockSpec((1, 1, n_pad), lambda i: (i, 0, 0), memory_space=pltpu.SMEM),
                  pl.BlockSpec((tm, d), lambda i: (i, 0)),
                  pl.BlockSpec((1, d), lambda i: (0, 0))],
        out_specs=pl.BlockSpec(memory_space=pl.ANY),
        out_shape=jax.ShapeDtypeStruct((rows * ROW_TILE, LANES), F32),
        scratch_shapes=[pltpu.VMEM((2, tm * ROW_TILE, LANES), F32),
                        pltpu.VMEM((n_pad * ROW_TILE, LANES), F32),
                        pltpu.SemaphoreType.DMA((2,)), pltpu.SemaphoreType.DMA((1,))],
        compiler_params=_cparams(("arbitrary",)),
        name="moe_dispatch",
    )(dest.reshape(nt, 1, TOP_K * tm), pad.reshape(nt, 1, n_pad), x, g.reshape(1, d))


def _expert_kernel(be_ref, nu_ref, x_ref, w1_ref, w3_ref, w2_ref, y_ref):
    del be_ref
    used = pl.program_id(0) < nu_ref[0]

    @pl.when(jnp.logical_not(used))
    def _():
        y_ref[...] = jnp.zeros(y_ref.shape, F32)

    @pl.when(used)
    def _():
        x = _from_tiles(x_ref, x_ref.shape[0] // ROW_TILE).astype(BF16)
        y = _swiglu(x, lambda c: w1_ref[0, :, c], lambda c: w3_ref[0, :, c], lambda c: w2_ref[0, c, :])
        _to_tiles(y_ref, y)


def _experts(buf, blk_e, n_used, w1, w3, w2, blk):
    d, f = w1.shape[1], w1.shape[2]
    once = pl.Buffered(2)
    grid_spec = pltpu.PrefetchScalarGridSpec(
        num_scalar_prefetch=2,
        grid=(buf.shape[0] // (blk * ROW_TILE),),
        in_specs=[pl.BlockSpec((blk * ROW_TILE, LANES), lambda b, be, nu: (b, 0)),
                  pl.BlockSpec((1, d, f), lambda b, be, nu: (be[b], 0, 0), pipeline_mode=once),
                  pl.BlockSpec((1, d, f), lambda b, be, nu: (be[b], 0, 0), pipeline_mode=once),
                  pl.BlockSpec((1, f, d), lambda b, be, nu: (be[b], 0, 0), pipeline_mode=once)],
        out_specs=pl.BlockSpec((blk * ROW_TILE, LANES), lambda b, be, nu: (b, 0)))
    return pl.pallas_call(
        _expert_kernel,
        grid_spec=grid_spec,
        out_shape=jax.ShapeDtypeStruct(buf.shape, F32),
        compiler_params=_cparams(("arbitrary",)),
        name="moe_experts",
    )(blk_e, n_used, buf, w1, w3, w2)


def _combine_kernel(nt, dcur_ref, dnxt_ref, x_ref, rt_ref, fn_ref, y_hbm, o_ref, ya, yb, sem):
    i = pl.program_id(0)
    tm = x_ref.shape[0]
    slot = i % 2

    def gather(dref, s):
        def start(r, c):
            _row_copy(y_hbm, dref[0, 0, TOP_K * r], ya.at[s], r, sem.at[s, 0]).start(priority=0)
            _row_copy(y_hbm, dref[0, 0, TOP_K * r + 1], yb.at[s], r, sem.at[s, 1]).start(
                priority=1 % DMA_QUEUES)
            return c

        lax.fori_loop(0, tm, start, 0, unroll=8)

    @pl.when(i == 0)
    def _():
        gather(dcur_ref, slot)

    @pl.when(i + 1 < nt)
    def _():
        gather(dnxt_ref, 1 - slot)

    _wait_rows(y_hbm, ya.at[slot], sem.at[slot, 0], False)
    _wait_rows(y_hbm, yb.at[slot], sem.at[slot, 1], False)
    rt = rt_ref[...]
    lane = lax.broadcasted_iota(jnp.int32, rt.shape, 1)
    g1 = jnp.sum(jnp.where(lane == 2, rt, 0.0), axis=-1, keepdims=True)
    g2 = jnp.sum(jnp.where(lane == 3, rt, 0.0), axis=-1, keepdims=True)
    moe = g1 * _from_tiles(ya.at[slot], tm) + g2 * _from_tiles(yb.at[slot], tm)
    o_ref[...] = _rms(x_ref[...] + moe, fn_ref[...])


def _combine(x, route, fnorm, y, dest, tm):
    t, d = x.shape
    nt = t // tm
    dest3 = dest.reshape(nt, 1, TOP_K * tm)
    return pl.pallas_call(
        functools.partial(_combine_kernel, nt),
        grid=(nt,),
        in_specs=[pl.BlockSpec((1, 1, TOP_K * tm), lambda i: (i, 0, 0), memory_space=pltpu.SMEM),
                  pl.BlockSpec((1, 1, TOP_K * tm), lambda i: (jnp.minimum(i + 1, nt - 1), 0, 0),
                               memory_space=pltpu.SMEM),
                  pl.BlockSpec((tm, d), lambda i: (i, 0)),
                  pl.BlockSpec((tm, LANES), lambda i: (i, 0)),
                  pl.BlockSpec((1, d), lambda i: (0, 0)),
                  pl.BlockSpec(memory_space=pl.ANY)],
        out_specs=pl.BlockSpec((tm, d), lambda i: (i, 0)),
        out_shape=jax.ShapeDtypeStruct((t, d), F32),
        scratch_shapes=[pltpu.VMEM((2, tm * ROW_TILE, LANES), F32),
                        pltpu.VMEM((2, tm * ROW_TILE, LANES), F32),
                        pltpu.SemaphoreType.DMA((2, 2))],
        compiler_params=_cparams(("arbitrary",)),
        name="moe_combine",
    )(dest3, dest3, x, route, fnorm.reshape(1, d), y)


def _moe(x, route, g, w1, w3, w2, fnorm, tm):
    blk = min(MOE_BLOCK, x.shape[0])
    dest, pad, blk_e, n_used, nblk = _route_plan(route, blk)
    buf = _dispatch(x, g, dest, pad, nblk * blk, tm)
    y = _experts(buf, blk_e, n_used, w1, w3, w2, blk)
    return _combine(x, route, fnorm, y, dest, tm)


def _block_diag(w):
    g, a, b = w.shape
    eye = jnp.eye(g, dtype=w.dtype)
    return (eye[:, None, :, None] * w[:, :, None, :]).reshape(g * a, g * b)


def kernel(x, l0_norm_mix, l0_w_in, l0_conv_w, l0_conv_b, l0_gate_a_w, l0_gate_a_b, l0_gate_x_w, l0_gate_x_b, l0_lru_lambda, l0_shift_mu, l0_w0, l0_w_up, l0_a0, l0_a_up, l0_g_up, l0_k_k, l0_k_a, l0_r_k, l0_ln_x_w, l0_ln_x_b, l0_w_out, l0_norm_ffn, l0_ffn_w1, l0_ffn_w3, l0_ffn_w2, l1_norm_mix, l1_w_in, l1_conv_w, l1_conv_b, l1_wq, l1_wk, l1_wv, l1_ig_b, l1_fg_b, l1_mh_norm_w, l1_w_out, l1_norm_ffn, l1_router, l1_moe_w1, l1_moe_w3, l1_moe_w2, final_norm):
    batch, seq, d = x.shape
    assert d == D_MODEL == ROW_TILE * LANES and seq % RW_CHUNK == 0 and seq % MLSTM_CHUNK == 0
    t = batch * seq
    xt = x.reshape(t, d)
    tm = min(TOKEN_TILE, t)

    w_in0 = l0_w_in.astype(BF16)
    p_lru, p_rw = _norm_mm(xt, l0_norm_mix, w_in0, (2 * LRU_WIDTH, RW_IN), tm, "l0_in")
    y_lru = _lru(p_lru, batch, seq, l0_conv_w, l0_conv_b,
                 _block_diag(l0_gate_a_w).astype(BF16), l0_gate_a_b,
                 _block_diag(l0_gate_x_w).astype(BF16), l0_gate_x_b, l0_lru_lambda,
                 min(TOKEN_TILE, seq))
    y_rw = _rwkv(p_rw, batch, seq, l0_shift_mu, l0_w0, l0_w_up, l0_a0, l0_a_up, l0_g_up,
                 l0_k_k, l0_k_a, l0_r_k.reshape(-1), l0_ln_x_w, l0_ln_x_b)
    w_out0 = l0_w_out.astype(BF16)
    ne, _, f = l1_moe_w1.shape
    x2, (mw1, mw3, mw2) = _ffn(
        xt, y_lru, y_rw, w_out0[:LRU_WIDTH], w_out0[LRU_WIDTH:], l0_norm_ffn,
        l0_ffn_w1.astype(BF16), l0_ffn_w3.astype(BF16), l0_ffn_w2.astype(BF16),
        [l1_moe_w1.reshape(ne * d, f), l1_moe_w3.reshape(ne * d, f), l1_moe_w2.reshape(ne * f, d)], tm)

    n_in1 = 2 * MLSTM_WIDTH + LANES
    w_in1 = jnp.zeros((d, n_in1), F32).at[:, :l1_w_in.shape[1]].set(l1_w_in).astype(BF16)
    h1 = _mlstm(x2, l1_norm_mix, w_in1, batch, seq, l1_conv_w, l1_conv_b, l1_wq, l1_wk, l1_wv,
                l1_ig_b, l1_fg_b, l1_mh_norm_w, tm)
    router = jnp.zeros((d, LANES), F32).at[:, :N_EXPERTS].set(l1_router)
    x3, route = _out_router(x2, h1, l1_w_out.astype(BF16), l1_norm_ffn, router, tm)
    out = _moe(x3, route, l1_norm_ffn, mw1.reshape(ne, d, f), mw3.reshape(ne, d, f),
               mw2.reshape(ne, f, d), final_norm, tm)
    return out.reshape(batch, seq, d)
```

```python
import functools

import jax
import jax.numpy as jnp
from jax import lax
from jax.experimental import pallas as pl
from jax.experimental.pallas import tpu as pltpu

F32 = jnp.float32
BF16 = jnp.bfloat16

D_MODEL = 1024
TOKEN_TILE = 512
LRU_WIDTH = 512
LRU_C = 8.0
LRU_SEGS = 8
CONV_WIDTH = 4
RWKV_HEADS = 8
RWKV_HEAD = 64
RWKV_WIDTH = 512
LN_X_EPS = 1e-5 * RWKV_HEAD
MLSTM_HEADS = 8
MLSTM_HEAD = 128
MLSTM_WIDTH = 1024
MLSTM_CHUNK = 128
MLSTM_STEP_ROWS = 256
D_FF = 2816
N_EXPERTS = 8
TOP_K = 2
DMA_QUEUES = 2
ROUTER_SPLIT = 4
MOE_BLOCK = 512
FF_CHUNKS = (768, 768, 768, 512)
RMS_EPS = 1e-6
RW_IN = 3 * RWKV_WIDTH + 64 + 64 + 128
RW_CHUNK = 64
RW_STEP_ROWS = 512
RW_WAVE_ROWS = 256
LANES = 128
MXU_WIDTH = 256
CARRY_ROWS = 8
ROW_TILE = 8
VMEM_LIMIT = 56 * 1024 * 1024


def _cparams(sem):
    return pltpu.CompilerParams(dimension_semantics=sem, vmem_limit_bytes=VMEM_LIMIT)


def _rms(x, w):
    return x * lax.rsqrt(jnp.mean(x * x, axis=-1, keepdims=True) + RMS_EPS) * w


def _sigmoid(x):
    return 1.0 / (1.0 + jnp.exp(-x))


def _softplus(x):
    return jnp.maximum(x, 0.0) + jnp.log(1.0 + jnp.exp(-jnp.abs(x)))


def _dot(a, b):
    return jnp.dot(a, b, preferred_element_type=F32)


def _dot_nt(a, b):
    return lax.dot_general(a, b, (((1,), (1,)), ((), ())), preferred_element_type=F32)


def _dot_tn(a, b):
    return lax.dot_general(a, b, (((0,), (0,)), ((), ())), preferred_element_type=F32)


def _split2(x):
    hi = x.astype(BF16)
    lo = (x - hi.astype(F32)).astype(BF16)
    return hi, lo


def _split3(x):
    hi = x.astype(BF16)
    r = x - hi.astype(F32)
    mid = r.astype(BF16)
    lo = (r - mid.astype(F32)).astype(BF16)
    return hi, mid, lo


def _mm3(fn, a, b):
    ah, al = _split2(a)
    bh, bl = _split2(b)
    return fn(ah, bh) + fn(al, bh) + fn(ah, bl)


def _mm_exact_lhs(fn, a_bf16, b):
    h, m, l = _split3(b)
    return fn(a_bf16, h) + fn(a_bf16, m) + fn(a_bf16, l)


def _seg_sum(x, seg):
    s = seg.shape[0]
    outs = []
    for c in range(x.shape[1] // s):
        hi, lo = _split2(x[:, c * s:(c + 1) * s])
        outs.append(_dot(hi, seg) + _dot(lo, seg))
    return jnp.concatenate(outs, axis=1)


def _shift_hist(buf_ref, x, first):
    n = x.shape[0]

    @pl.when(first)
    def _():
        buf_ref[0:CARRY_ROWS, :] = jnp.zeros((CARRY_ROWS, x.shape[1]), F32)

    @pl.when(jnp.logical_not(first))
    def _():
        buf_ref[0:CARRY_ROWS, :] = buf_ref[n:n + CARRY_ROWS, :]

    buf_ref[CARRY_ROWS:CARRY_ROWS + n, :] = x


def _causal_conv(buf_ref, n, w, b, rotate):
    if rotate:
        full = buf_ref[0:CARRY_ROWS + n, :]
        acc = b + w[CONV_WIDTH - 1:CONV_WIDTH, :] * full[CARRY_ROWS:, :]
        for lag in range(1, CONV_WIDTH):
            tap = CONV_WIDTH - 1 - lag
            acc = acc + w[tap:tap + 1, :] * pltpu.roll(full, lag, axis=0)[CARRY_ROWS:, :]
        return acc
    acc = b
    for j in range(CONV_WIDTH):
        off = CARRY_ROWS - (CONV_WIDTH - 1) + j
        acc = acc + w[j:j + 1, :] * buf_ref[off:off + n, :]
    return acc


def _norm_mm_kernel(x_ref, g_ref, w_ref, *o_refs):
    xn = _rms(x_ref[...], g_ref[...]).astype(BF16)
    c0 = 0
    for o_ref in o_refs:
        c1 = c0 + o_ref.shape[1]
        o_ref[...] = _dot(xn, w_ref[:, c0:c1]).astype(o_ref.dtype)
        c0 = c1


def _norm_mm(x, g, w, widths, tm, name):
    t, d = x.shape
    n = w.shape[1]
    assert sum(widths) == n
    return pl.pallas_call(
        _norm_mm_kernel,
        grid=(t // tm,),
        in_specs=[pl.BlockSpec((tm, d), lambda i: (i, 0)),
                  pl.BlockSpec((1, d), lambda i: (0, 0)),
                  pl.BlockSpec((d, n), lambda i: (0, 0))],
        out_specs=[pl.BlockSpec((tm, c), lambda i: (i, 0)) for c in widths],
        out_shape=[jax.ShapeDtypeStruct((t, c), BF16) for c in widths],
        compiler_params=_cparams(("parallel",)),
        name=name,
    )(x, g.reshape(1, d), w)


def _lru_kernel(p_ref, cw_ref, cb_ref, wa_ref, ba_ref, wx_ref, bx_ref, lam_ref, o_ref,
                xbuf, abuf, bbuf, hbuf, hcar):
    i = pl.program_id(1)
    n = p_ref.shape[0]
    first = i == 0
    _shift_hist(xbuf, p_ref[:, 0:LRU_WIDTH].astype(F32), first)
    xc = _causal_conv(xbuf, n, cw_ref[...], cb_ref[...], True)
    xcb = xc.astype(BF16)
    r = _sigmoid(_dot(xcb, wa_ref[...]) + ba_ref[...])
    ig = _sigmoid(_dot(xcb, wx_ref[...]) + bx_ref[...])
    log_a = (-LRU_C) * r * _softplus(-lam_ref[...])
    a = jnp.exp(log_a)
    mult = jnp.sqrt(1.0 - a * a)
    row = lax.broadcasted_iota(jnp.int32, (n, 1), 0)
    mult = jnp.where(jnp.logical_and(first, row == 0), 1.0, mult)
    bvals = mult * ig * xc
    n_slab = LRU_WIDTH // LANES
    slabs = [slice(c * LANES, (c + 1) * LANES) for c in range(n_slab)]
    seg = n // LRU_SEGS
    pitch = _lru_pitch(n)
    for c, cs in enumerate(slabs):
        for s in range(LRU_SEGS):
            abuf[c, s * pitch:s * pitch + seg, :] = a[s * seg:(s + 1) * seg, cs]
            bbuf[c, s * pitch:s * pitch + seg, :] = bvals[s * seg:(s + 1) * seg, cs]

    @pl.when(first)
    def _():
        hcar[...] = jnp.zeros(hcar.shape, F32)

    def step(j, carry):
        rows = pl.ds(j, LRU_SEGS, stride=pitch)
        out = []
        for c in range(n_slab):
            h, acc = carry[c]
            a_j = abuf[c, rows, :]
            h = a_j * h + bbuf[c, rows, :]
            acc = acc * a_j
            hbuf[c, rows, :] = h
            abuf[c, rows, :] = acc
            out.append((h, acc))
        return tuple(out)

    init = tuple((jnp.zeros((LRU_SEGS, LANES), F32), jnp.ones((LRU_SEGS, LANES), F32))
                 for _ in range(n_slab))
    ends = lax.fori_loop(0, seg, step, init, unroll=8)
    cols = []
    for c, cs in enumerate(slabs):
        h_end, a_end = ends[c]
        h0 = hcar[0:1, cs]
        parts = []
        for s in range(LRU_SEGS):
            rows = slice(s * pitch, s * pitch + seg)
            parts.append(hbuf[c, rows, :] + abuf[c, rows, :] * h0)
            h0 = h_end[s:s + 1, :] + a_end[s:s + 1, :] * h0
        hcar[0:1, cs] = h0
        cols.append(jnp.concatenate(parts, axis=0))
    gate = p_ref[:, LRU_WIDTH:2 * LRU_WIDTH].astype(F32)
    gelu = 0.5 * gate * (1.0 + jnp.tanh(0.7978845608028654 * (gate + 0.044715 * gate * gate * gate)))
    o_ref[...] = (jnp.concatenate(cols, axis=1) * gelu).astype(o_ref.dtype)


def _lru_pitch(n):
    tiles = n // LRU_SEGS // CARRY_ROWS
    return (tiles + 1 - tiles % 2) * CARRY_ROWS


def _lru(p_lru, batch, seq, conv_w, conv_b, wa, ba, wx, bx, lam, tb):
    nb = seq // tb
    c = LRU_WIDTH
    vec = lambda: pl.BlockSpec((1, c), lambda b, i: (0, 0))
    return pl.pallas_call(
        _lru_kernel,
        grid=(batch, nb),
        in_specs=[pl.BlockSpec((tb, 2 * c), lambda b, i: (b * nb + i, 0)),
                  pl.BlockSpec((CONV_WIDTH, c), lambda b, i: (0, 0)), vec(),
                  pl.BlockSpec((c, c), lambda b, i: (0, 0)), vec(),
                  pl.BlockSpec((c, c), lambda b, i: (0, 0)), vec(), vec()],
        out_specs=pl.BlockSpec((tb, c), lambda b, i: (b * nb + i, 0)),
        out_shape=jax.ShapeDtypeStruct((batch * seq, c), BF16),
        scratch_shapes=[pltpu.VMEM((tb + CARRY_ROWS, c), F32)]
        + [pltpu.VMEM((c // LANES, LRU_SEGS * _lru_pitch(tb), LANES), F32)] * 3
        + [pltpu.VMEM((CARRY_ROWS, c), F32)],
        compiler_params=_cparams(("parallel", "arbitrary")),
        name="rg_lru",
    )(p_lru, conv_w, conv_b.reshape(1, c), wa, ba.reshape(1, c), wx, bx.reshape(1, c),
      lam.reshape(1, c))


def _blk(x, masks):
    xb = x.astype(BF16)
    return jnp.concatenate([jnp.where(mk, xb, jnp.zeros_like(xb)) for mk in masks], axis=0)


def _rwkv_kernel(p_ref, mu_ref, w0_ref, wup_ref, a0_ref, aup_ref, gup_ref, kk_ref, ka_ref,
                 rk_ref, lnw_ref, lnb_ref, seg_ref, tri_ref, o_ref, pbuf, state):
    L = RW_CHUNK
    W = RWKV_WIDTH
    rows = p_ref.shape[0]
    wave = tri_ref.shape[0]
    first = pl.program_id(1) == 0
    p = p_ref[...].astype(F32)
    _shift_hist(pbuf, p, first)
    prev = pltpu.roll(pbuf[0:CARRY_ROWS + rows, :], 1, axis=0)[CARRY_ROWS:, :]
    ps = p + mu_ref[...] * (prev - p)
    seg = seg_ref[...]
    tri = tri_ref[...]

    @pl.when(first)
    def _():
        state[...] = jnp.zeros(state.shape, F32)

    lane = lax.broadcasted_iota(jnp.int32, (1, LANES), 1)
    m1 = [lane < RWKV_HEAD, lane >= RWKV_HEAD]
    m2 = [jnp.concatenate([mk, mk], axis=1) for mk in m1]
    ti = lax.broadcasted_iota(jnp.int32, (L, LANES), 0)
    si = lax.broadcasted_iota(jnp.int32, (L, LANES), 1) % RWKV_HEAD
    strict = si < ti
    incl = si <= ti
    bi = lax.broadcasted_iota(jnp.int32, (LANES, LANES), 0) // RWKV_HEAD
    bj = lax.broadcasted_iota(jnp.int32, (LANES, LANES), 1) // RWKV_HEAD
    bd = bi == bj

    def tril(mask, s):
        return jnp.where(mask, s, 0.0).astype(BF16)

    wv = [slice(w * wave, (w + 1) * wave) for w in range(rows // wave)]
    ws = range(len(wv))
    r = [ps[rw, 0:W] for rw in wv]
    k = [ps[rw, W:2 * W] for rw in wv]
    v = [ps[rw, 2 * W:3 * W] for rw in wv]
    x2 = [ps[rw, 3 * W:3 * W + LANES] for rw in wv]
    xg = [ps[rw, 3 * W + LANES:3 * W + 2 * LANES] for rw in wv]
    wl = [w0_ref[...] + _dot(jnp.tanh(x2[w]).astype(BF16), wup_ref[...]) for w in ws]
    a = [_sigmoid(a0_ref[...] + _dot(x2[w].astype(BF16), aup_ref[...])) for w in ws]
    g = [_dot(_sigmoid(xg[w]).astype(BF16), gup_ref[...]) for w in ws]
    kk = [k[w] * kk_ref[...] for w in ws]
    kk = [kk[w] / jnp.maximum(jnp.sqrt(_seg_sum(kk[w] * kk[w], seg)), 1e-12) for w in ws]
    lw = [-jnp.exp(-_softplus(-wl[w]) - 0.5) for w in ws]
    cw = [_mm_exact_lhs(_dot, tri, lw[w]) for w in ws]

    n_pairs = RWKV_HEADS // 2
    n_ch = wave // L
    pair_cols = [slice(pr * LANES, (pr + 1) * LANES) for pr in range(n_pairs)]
    ops = {}

    def scale_pair(w, pr):
        sl = pair_cols[pr]
        cwp = cw[w][:, sl]
        cw_end = jnp.concatenate(
            [jnp.broadcast_to(cwp[c * L + L - 1:c * L + L, :], (L, LANES)) for c in range(n_ch)], axis=0)
        w_inv = jnp.exp(-cwp)
        w_end = jnp.exp(cw_end - cwp)
        k2 = k[w][:, sl] * (1.0 + (a[w][:, sl] - 1.0) * ka_ref[:, sl])
        kka = kk[w][:, sl] * a[w][:, sl]
        ops[w, pr] = dict(a_t=-kk[w][:, sl] * jnp.exp(cwp - lw[w][:, sl]), b_t=kka * w_inv, k_t=k2 * w_inv,
                          r_t=r[w][:, sl] * jnp.exp(cwp), b_bar=kka * w_end, k_bar=k2 * w_end,
                          w_tot=jnp.exp(cw_end), k2=k2, v=v[w][:, sl])

    for pr in range(n_pairs):
        scale_pair(0, pr)
    s_cur = [state[pr] for pr in range(n_pairs)]
    for w in ws:
        todo = [pr for pr in range(n_pairs)] if w + 1 < len(wv) else []

        def next_piece():
            if todo:
                scale_pair(w + 1, todo.pop(0))

        chains = [(slice(c * L, (c + 1) * L), pr) for c in range(n_ch) for pr in range(n_pairs)]
        idx = range(len(chains))
        a_c = [ops[w, pr]["a_t"][rs, :] for rs, pr in chains]
        r_c = [ops[w, pr]["r_t"][rs, :] for rs, pr in chains]
        ar = [jnp.concatenate([a_c[i], r_c[i]], axis=0).astype(BF16) for i in idx]
        sb = [_dot_nt(ar[i], _blk(ops[w, pr]["b_t"][rs, :], m1)) for i, (rs, pr) in enumerate(chains)]
        sk = [_dot_nt(ar[i], _blk(ops[w, pr]["k_t"][rs, :], m1)) for i, (rs, pr) in enumerate(chains)]
        vblk = [_blk(ops[w, pr]["v"][rs, :], m1) for rs, pr in chains]
        next_piece()
        rhs = [jnp.concatenate([a_c[i], _dot(tril(strict, sk[i][0:L]), vblk[i])], axis=1) for i in idx]
        pw = [tril(strict, sb[i][0:L]) for i in idx]
        tinv = [jnp.where(si == ti, 1.0, 0.0) + pw[i].astype(F32) for i in idx]
        pw = [_dot(pw[i], _blk(pw[i], m1)).astype(BF16) for i in idx]
        next_piece()
        for _ in range(4):
            both = [_dot(pw[i], jnp.concatenate([_blk(pw[i], m1), _blk(tinv[i], m1)], axis=1)) for i in idx]
            tinv = [tinv[i] + both[i][:, LANES:2 * LANES] for i in idx]
            pw = [both[i][:, 0:LANES].astype(BF16) for i in idx]
            next_piece()
        tinv = [tinv[i] + _dot(pw[i], _blk(tinv[i], m1)) for i in idx]
        x = [_dot(tinv[i].astype(BF16), _blk(rhs[i], m2)) for i in idx]
        corr = [_dot(tril(incl, sb[i][L:2 * L]), _blk(x[i], m2)) for i in idx]
        r_hat = [(r_c[i] + corr[i][:, 0:LANES]).astype(BF16) for i in idx]
        y0 = [corr[i][:, LANES:2 * LANES] + _dot(tril(incl, sk[i][L:2 * L]), vblk[i]) for i in idx]
        xb = [x[i].astype(BF16) for i in idx]
        bb = [ops[w, pr]["b_bar"][rs, :].astype(BF16) for rs, pr in chains]
        gmat = [jnp.where(bd, _dot_tn(xb[i][:, 0:LANES], bb[i]), 0.0).astype(BF16) for i in idx]
        hmat = [jnp.where(bd, _dot_tn(xb[i][:, LANES:2 * LANES], bb[i])
                          + _dot_tn(ops[w, pr]["v"][rs, :].astype(BF16),
                                    ops[w, pr]["k_bar"][rs, :].astype(BF16)), 0.0)
                for i, (rs, pr) in enumerate(chains)]

        y_rows = []
        for c in range(n_ch):
            ys = []
            for pr in range(n_pairs):
                i = c * n_pairs + pr
                s0 = s_cur[pr]
                s0b = s0.astype(BF16)
                ys.append(_dot_nt(r_hat[i], s0b) + y0[i])
                s_cur[pr] = s0 * ops[w, pr]["w_tot"][c * L:c * L + 1, :] + _dot(s0b, gmat[i]) + hmat[i]
            y_rows.append(jnp.concatenate(ys, axis=1))
        y = jnp.concatenate(y_rows, axis=0)
        k2 = jnp.concatenate([ops[w, pr]["k2"] for pr in range(n_pairs)], axis=1)

        inv = 1.0 / RWKV_HEAD
        mean = _seg_sum(y, seg) * inv
        yc = y - mean
        var = _seg_sum(yc * yc, seg) * inv
        yn = yc * lax.rsqrt(var + LN_X_EPS) * lnw_ref[...] + lnb_ref[...]
        bonus = _seg_sum(r[w] * k2 * rk_ref[...], seg) * v[w]
        o_ref[wv[w], :] = ((yn + bonus) * g[w]).astype(o_ref.dtype)
    for pr in range(n_pairs):
        state[pr] = s_cur[pr]


def _rwkv(p_rw, batch, seq, mu, w0, w_up, a0, a_up, g_up, k_k, k_a, r_k, ln_w, ln_b):
    L = min(RW_STEP_ROWS, seq)
    nc = seq // L
    W = RWKV_WIDTH
    wup = jnp.zeros((LANES, W), F32).at[0:64].set(w_up).astype(BF16)
    aup = jnp.zeros((LANES, W), F32).at[64:128].set(a_up).astype(BF16)
    hid = jnp.arange(MXU_WIDTH) // RWKV_HEAD
    seg = (hid[:, None] == hid[None, :]).astype(BF16)
    wave = min(RW_WAVE_ROWS, L)
    ri = jnp.arange(wave)
    tri = jnp.logical_and(ri[:, None] >= ri[None, :],
                          ri[:, None] // RW_CHUNK == ri[None, :] // RW_CHUNK).astype(BF16)
    vec = lambda: pl.BlockSpec((1, W), lambda b, i: (0, 0))
    mat = lambda s: pl.BlockSpec(s, lambda b, i: (0, 0))
    return pl.pallas_call(
        _rwkv_kernel,
        grid=(batch, nc),
        in_specs=[pl.BlockSpec((L, RW_IN), lambda b, i: (b * nc + i, 0)),
                  mat((1, RW_IN)), vec(), mat((LANES, W)), vec(), mat((LANES, W)),
                  mat((LANES, W)), vec(), vec(), vec(), vec(), vec(), mat((MXU_WIDTH, MXU_WIDTH)),
                  mat((wave, wave))],
        out_specs=pl.BlockSpec((L, W), lambda b, i: (b * nc + i, 0)),
        out_shape=jax.ShapeDtypeStruct((batch * seq, W), BF16),
        scratch_shapes=[pltpu.VMEM((L + CARRY_ROWS, RW_IN), F32),
                        pltpu.VMEM((RWKV_HEADS // 2, LANES, LANES), F32)],
        compiler_params=_cparams(("parallel", "arbitrary")),
        name="rwkv7",
    )(p_rw, mu.reshape(1, RW_IN), w0.reshape(1, W), wup, a0.reshape(1, W), aup,
      g_up.astype(BF16), k_k.reshape(1, W), k_a.reshape(1, W), r_k.reshape(1, W),
      ln_w.reshape(1, W), ln_b.reshape(1, W), seg, tri)


def _swiglu(xb, w1, w3, w2):
    edges = [sum(FF_CHUNKS[:i]) for i in range(len(FF_CHUNKS) + 1)]
    cols = [slice(edges[i], edges[i + 1]) for i in range(len(FF_CHUNKS))]
    hid = (_dot(xb, w1(cols[0])), _dot(xb, w3(cols[0])))
    acc = None
    for i, c in enumerate(cols):
        nxt = (_dot(xb, w1(cols[i + 1])), _dot(xb, w3(cols[i + 1]))) if i + 1 < len(cols) else None
        act = (hid[0] * _sigmoid(hid[0]) * hid[1]).astype(BF16)
        part = _dot(act, w2(c))
        acc = part if acc is None else acc + part
        hid = nxt
    return acc


def _ffn_kernel(n_cast, x_ref, ya_ref, yb_ref, wa_ref, wb_ref, g_ref, w1_ref, w3_ref, w2_ref, *rest):
    src, o_ref, dst = rest[:n_cast], rest[n_cast], rest[n_cast + 1:]
    x1 = x_ref[...] + _dot(ya_ref[...], wa_ref[...]) + _dot(yb_ref[...], wb_ref[...])
    xn = _rms(x1, g_ref[...]).astype(BF16)
    o_ref[...] = x1 + _swiglu(xn, lambda c: w1_ref[:, c], lambda c: w3_ref[:, c], lambda c: w2_ref[c, :])
    for s_ref, d_ref in zip(src, dst):
        d_ref[...] = s_ref[...].astype(BF16)


def _ffn(x, ya, yb, wa, wb, g, w1, w3, w2, to_cast, tm):
    t, d = x.shape
    nt = t // tm
    once = pl.Buffered(1)
    fixed = lambda a: pl.BlockSpec(a.shape, lambda i: (0, 0), pipeline_mode=once)
    slab = lambda a: pl.BlockSpec((a.shape[0] // nt, a.shape[1]), lambda i: (i, 0))
    assert all(a.shape[0] % (nt * 2 * ROW_TILE) == 0 for a in to_cast)
    outs = pl.pallas_call(
        functools.partial(_ffn_kernel, len(to_cast)),
        grid=(nt,),
        in_specs=[pl.BlockSpec((tm, d), lambda i: (i, 0)),
                  pl.BlockSpec((tm, ya.shape[1]), lambda i: (i, 0)),
                  pl.BlockSpec((tm, yb.shape[1]), lambda i: (i, 0)),
                  fixed(wa), fixed(wb),
                  pl.BlockSpec((1, d), lambda i: (0, 0)),
                  fixed(w1), fixed(w3), fixed(w2)] + [slab(a) for a in to_cast],
        out_specs=[pl.BlockSpec((tm, d), lambda i: (i, 0))] + [slab(a) for a in to_cast],
        out_shape=[jax.ShapeDtypeStruct((t, d), F32)]
        + [jax.ShapeDtypeStruct(a.shape, BF16) for a in to_cast],
        compiler_params=_cparams(("parallel",)),
        name="ffn_swiglu",
    )(x, ya, yb, wa, wb, g.reshape(1, d), w1, w3, w2, *to_cast)
    return outs[0], outs[1:]


def _mlstm_in_kernel(nb, x_ref, g_ref, w_ref, cw_ref, cb_ref, wq_ref, wk_ref, wv_ref, gb_ref, tri_ref,
                     q_ref, k_ref, kt_ref, v_ref, z_ref, gs_ref, gtr_ref, xbuf, xm_s, z_s, gt_s):
    i = pl.program_id(0)
    n = x_ref.shape[0]
    W = MLSTM_WIDTH
    dh = MLSTM_HEAD
    L = MLSTM_CHUNK
    nh = MLSTM_HEADS
    slot = i % 2

    @pl.when(i == 0)
    def _():
        xm_s[...] = jnp.zeros(xm_s.shape, BF16)
        z_s[...] = jnp.zeros(z_s.shape, BF16)
        gt_s[...] = jnp.zeros(gt_s.shape, F32)

    xm = xm_s[1 - slot]
    z_ref[...] = z_s[1 - slot]
    gates = gt_s[1 - slot]
    _shift_hist(xbuf, xm.astype(F32), jnp.logical_or(i == 0, (i - 1) % nb == 0))

    xn = _rms(x_ref[...], g_ref[...]).astype(BF16)
    cw = cw_ref[...]
    cb = cb_ref[...]
    scale = dh ** -0.5
    heads_per = MXU_WIDTH // dh
    groups = [slice(c * MXU_WIDTH, (c + 1) * MXU_WIDTH) for c in range(W // MXU_WIDTH)]

    def qkv(c, xc):
        for u in range(heads_per):
            h = c * heads_per + u
            sl = slice(h * dh, (h + 1) * dh)
            xh = xc[:, u * dh:(u + 1) * dh]
            q_ref[:, sl] = (_dot(xh, wq_ref[h]) * scale).astype(BF16)
            kh = _dot(xh, wk_ref[h])
            k_ref[:, sl] = kh.astype(BF16)
            for cc in range(n // L):
                kt_ref[cc * L:(cc + 1) * L, sl] = kh[cc * L:(cc + 1) * L, :].T.astype(BF16)
            v_ref[:, sl] = _dot(xm[:, sl], wv_ref[h]).astype(BF16)

    xm_new, z_new, xc_prev = [], [], None
    for c, cols in enumerate(groups):
        xm_new.append(_dot(xn, w_ref[:, cols]).astype(BF16))
        z_new.append(_dot(xn, w_ref[:, W + c * MXU_WIDTH:W + (c + 1) * MXU_WIDTH]).astype(BF16))
        full = xbuf[0:CARRY_ROWS + n, cols]
        acc = cb[:, cols] + cw[CONV_WIDTH - 1:CONV_WIDTH, cols] * full[CARRY_ROWS:, :]
        for lag in range(1, CONV_WIDTH):
            tap = CONV_WIDTH - 1 - lag
            acc = acc + cw[tap:tap + 1, cols] * pltpu.roll(full, lag, axis=0)[CARRY_ROWS:, :]
        if c > 0:
            qkv(c - 1, xc_prev)
        xc_prev = (acc * _sigmoid(acc)).astype(BF16)
    qkv(len(groups) - 1, xc_prev)
    gt_new = _dot(xn, w_ref[:, 2 * W:2 * W + LANES])

    lane = lax.broadcasted_iota(jnp.int32, (L, LANES), 1)
    row = lax.broadcasted_iota(jnp.int32, (L, LANES), 0)
    for c in range(n // L):
        rs = slice(c * L, (c + 1) * L)
        gl = gates[rs, :] + gb_ref[...]
        lf = jnp.minimum(gl, 0.0) - jnp.log(1.0 + jnp.exp(-jnp.abs(gl)))
        bcum = _mm_exact_lhs(_dot, tri_ref[...], lf)
        b_al = pltpu.roll(bcum, LANES - nh, axis=1)
        gmat = gl - b_al
        cmax = gmat
        d = 1
        while d < L:
            cmax = jnp.maximum(cmax, jnp.where(row >= d, pltpu.roll(cmax, d, axis=0), -jnp.inf))
            d *= 2
        gs_ref[rs, :] = jnp.where(lane < nh, gmat,
                                  jnp.where(lane < 2 * nh, pltpu.roll(cmax, nh, axis=1),
                                            pltpu.roll(b_al, 2 * nh, axis=1)))
        gtr_ref[rs, :] = gmat.T
    xm_s[slot] = jnp.concatenate(xm_new, axis=1)
    z_s[slot] = jnp.concatenate(z_new, axis=1)
    gt_s[slot] = gt_new


def _mlstm_kernel(q_ref, k_ref, kt_ref, v_ref, z_ref, gs_ref, gtr_ref, nw_ref, o_ref, cn_s, m_s):
    L = MLSTM_CHUNK
    dh = MLSTM_HEAD
    nh = MLSTM_HEADS
    n_ch = q_ref.shape[0] // L

    @pl.when(pl.program_id(1) == 0)
    def _():
        cn_s[...] = jnp.zeros(cn_s.shape, F32)
        m_s[...] = jnp.zeros(m_s.shape, F32)

    ti = lax.broadcasted_iota(jnp.int32, (L, L), 0)
    si = lax.broadcasted_iota(jnp.int32, (L, L), 1)
    causal = si <= ti
    ones = jnp.ones((L, dh), BF16)
    rws = [slice(c * L, (c + 1) * L) for c in range(n_ch)]
    sls = [slice(h * dh, (h + 1) * dh) for h in range(nh)]
    gs = [gs_ref[rw, :] for rw in rws]
    g_t = [gtr_ref[rw, :] for rw in rws]
    units = [(c, h) for c in range(n_ch) for h in range(nh)]

    head_lanes = jnp.logical_and(lax.broadcasted_iota(jnp.int32, (1, LANES), 1) >= nh,
                                 lax.broadcasted_iota(jnp.int32, (1, LANES), 1) < 2 * nh)
    mp = m_s[0:1, :]
    mx_all, sc_all, fl_all, dec_all = [], [], [], []
    for c in range(n_ch):
        b8 = pltpu.roll(gs[c], LANES - nh, axis=1)
        mx = jnp.maximum(mp, gs[c])
        m_last = mx[L - 1:L, :]
        mx_all.append(mx)
        sc_all.append(jnp.exp(mp - mx))
        fl_all.append(jnp.exp(-(b8 + mx)))
        dec_all.append(jnp.exp(mp - m_last))
        mp = jnp.where(head_lanes, b8[L - 1:L, :] + m_last, 0.0)
    m_s[0:1, :] = mp
    col = lambda arr, h: arr[:, nh + h:nh + h + 1]

    qb = {u: q_ref[rws[u[0]], sls[u[1]]] for u in units}
    kb = {u: k_ref[rws[u[0]], sls[u[1]]] for u in units}
    v1 = {u: jnp.concatenate([v_ref[rws[u[0]], sls[u[1]]], ones], axis=1) for u in units}
    qk = {u: _dot_nt(qb[u], kb[u]) for u in units}
    s = {(c, h): qk[c, h] * jnp.exp(jnp.where(causal, g_t[c][h:h + 1, :] - col(mx_all[c], h), -jnp.inf))
         for c, h in units}
    sv = {u: _dot(s[u].astype(BF16), v1[u]) for u in units}
    kw = {(c, h): kt_ref[rws[c], sls[h]].astype(F32)
          * jnp.exp(g_t[c][h:h + 1, :] - col(mx_all[c], h)[L - 1:L, :]) for c, h in units}
    kv = {u: _dot(kw[u].astype(BF16), v1[u]) for u in units}
    sc = {(c, h): col(sc_all[c], h) for c, h in units}
    floor = {(c, h): col(fl_all[c], h) for c, h in units}
    dec = {(c, h): col(dec_all[c], h) for c, h in units}

    cn = {h: cn_s[h] for h in range(nh)}
    for c in range(n_ch):
        hs = range(nh)
        qcn = {h: _dot(qb[c, h], cn[h].astype(BF16)) for h in hs}
        den = {h: sc[c, h] * qcn[h][:, dh:] + sv[c, h][:, dh:] for h in hs}
        hh = {h: (sc[c, h] * qcn[h][:, :dh] + sv[c, h][:, :dh]) / jnp.maximum(jnp.abs(den[h]), floor[c, h])
              for h in hs}
        ms = {h: jnp.mean(hh[h] * hh[h], axis=-1, keepdims=True) for h in hs}
        for h in hs:
            cn[h] = dec[c, h] * cn[h] + kv[c, h]
            hn = hh[h] * lax.rsqrt(ms[h] + RMS_EPS)
            gate = _sigmoid(z_ref[rws[c], sls[h]].astype(F32))
            o_ref[rws[c], sls[h]] = (gate * (hn * nw_ref[:, sls[h]])).astype(o_ref.dtype)
    for h in range(nh):
        cn_s[h] = cn[h]


def _mlstm(x, norm_w, w_in, batch, seq, conv_w, conv_b, wq, wk, wv, ig_b, fg_b, mh_w, tm):
    L = MLSTM_CHUNK
    nc = seq // L
    W = MLSTM_WIDTH
    t, d = x.shape
    tb = min(tm, seq)
    nb = seq // tb
    n_blk = batch * nb
    fixed = lambda shape: pl.BlockSpec(shape, lambda i: (0,) * len(shape))
    hw = lambda: fixed((MLSTM_HEADS, MLSTM_HEAD, MLSTM_HEAD))
    done = lambda width: pl.BlockSpec((tb, width), lambda i: (jnp.maximum(i - 1, 0), 0))
    gb = jnp.zeros((1, LANES), F32).at[0, 0:8].set(ig_b).at[0, 8:16].set(fg_b)
    tri = (jnp.arange(L)[:, None] >= jnp.arange(L)[None, :]).astype(BF16)
    q, k, kt, v, z, gs, gtr = pl.pallas_call(
        functools.partial(_mlstm_in_kernel, nb),
        grid=(n_blk + 1,),
        in_specs=[pl.BlockSpec((tb, d), lambda i: (jnp.minimum(i, n_blk - 1), 0)),
                  fixed((1, d)), fixed(w_in.shape), fixed((CONV_WIDTH, W)), fixed((1, W)),
                  hw(), hw(), hw(), fixed((1, LANES)), fixed((L, L))],
        out_specs=[done(W), done(W), done(W), done(W), done(W), done(LANES), done(LANES)],
        out_shape=[jax.ShapeDtypeStruct((t, W), BF16)] * 5 + [jax.ShapeDtypeStruct((t, LANES), F32)] * 2,
        scratch_shapes=[pltpu.VMEM((tb + CARRY_ROWS, W), F32), pltpu.VMEM((2, tb, W), BF16),
                        pltpu.VMEM((2, tb, W), BF16), pltpu.VMEM((2, tb, LANES), F32)],
        compiler_params=_cparams(("arbitrary",)),
        name="l1_in_mlstm",
    )(x, norm_w.reshape(1, d), w_in, conv_w, conv_b.reshape(1, W), wq.astype(BF16), wk.astype(BF16),
      wv.astype(BF16), gb, tri)

    rb = min(MLSTM_STEP_ROWS, seq)
    nc = seq // rb
    chunk = lambda: pl.BlockSpec((rb, W), lambda b, i: (b * nc + i, 0))
    return pl.pallas_call(
        _mlstm_kernel,
        grid=(batch, nc),
        in_specs=[chunk(), chunk(), chunk(), chunk(), chunk(),
                  pl.BlockSpec((rb, LANES), lambda b, i: (b * nc + i, 0)),
                  pl.BlockSpec((rb, LANES), lambda b, i: (b * nc + i, 0)),
                  pl.BlockSpec((1, W), lambda b, i: (0, 0))],
        out_specs=pl.BlockSpec((rb, W), lambda b, i: (b * nc + i, 0)),
        out_shape=jax.ShapeDtypeStruct((t, W), BF16),
        scratch_shapes=[pltpu.VMEM((MLSTM_HEADS, MLSTM_HEAD, 2 * MLSTM_HEAD), F32),
                        pltpu.VMEM((MLSTM_HEADS, LANES), F32)],
        compiler_params=_cparams(("parallel", "arbitrary")),
        name="mlstm",
    )(q, k, kt, v, z, gs, gtr, mh_w.reshape(1, W))


def _out_router_kernel(x_ref, y_ref, w_ref, g_ref, rt_ref, x3_ref, o_ref):
    tm = x_ref.shape[0]
    sub = tm // ROUTER_SPLIT
    rs = [slice(q * sub, (q + 1) * sub) for q in range(ROUTER_SPLIT)]
    qs = range(ROUTER_SPLIT)
    w = w_ref[...]
    x3 = [x_ref[r, :] + _dot(y_ref[r, :], w) for r in rs]
    for q in qs:
        x3_ref[rs[q], :] = x3[q]
    gw = g_ref[...]
    rt = rt_ref[...]
    logits = [_mm3(_dot, _rms(x3[q], gw), rt) for q in qs]
    lane = lax.broadcasted_iota(jnp.int32, (sub, LANES), 1)
    lg = [jnp.where(lane < N_EXPERTS, logits[q], -jnp.inf) for q in qs]
    v1 = [jnp.max(lg[q], axis=-1, keepdims=True) for q in qs]
    i1 = [jnp.min(jnp.where(lg[q] == v1[q], lane, LANES), axis=-1, keepdims=True) for q in qs]
    lg2 = [jnp.where(lane == i1[q], -jnp.inf, lg[q]) for q in qs]
    v2 = [jnp.max(lg2[q], axis=-1, keepdims=True) for q in qs]
    i2 = [jnp.min(jnp.where(lg2[q] == v2[q], lane, LANES), axis=-1, keepdims=True) for q in qs]
    for q in qs:
        ex = jnp.exp(v2[q] - v1[q])
        g1 = 1.0 / (1.0 + ex)
        g2 = ex / (1.0 + ex)
        out = jnp.where(lane == 0, i1[q].astype(F32), 0.0)
        out = jnp.where(lane == 1, i2[q].astype(F32), out)
        out = jnp.where(lane == 2, g1, out)
        o_ref[rs[q], :] = jnp.where(lane == 3, g2, out)


def _out_router(x, y, w, g, router, tm):
    t, d = x.shape
    return pl.pallas_call(
        _out_router_kernel,
        grid=(t // tm,),
        in_specs=[pl.BlockSpec((tm, d), lambda i: (i, 0)),
                  pl.BlockSpec((tm, y.shape[1]), lambda i: (i, 0)),
                  pl.BlockSpec(w.shape, lambda i: (0, 0)),
                  pl.BlockSpec((1, d), lambda i: (0, 0)),
                  pl.BlockSpec((d, LANES), lambda i: (0, 0))],
        out_specs=[pl.BlockSpec((tm, d), lambda i: (i, 0)),
                   pl.BlockSpec((tm, LANES), lambda i: (i, 0))],
        out_shape=[jax.ShapeDtypeStruct((t, d), F32), jax.ShapeDtypeStruct((t, LANES), F32)],
        compiler_params=_cparams(("parallel",)),
        name="l1_out_router",
    )(x, y, w, g.reshape(1, d), router)


def _prefix_sum(x):
    m, c = x.shape
    group = min(LANES, m)
    xg = x.reshape(m // group, group, c).astype(F32)
    gi = jnp.arange(group)
    local = jnp.einsum("ij,bjc->bic", (gi[:, None] >= gi[None, :]).astype(F32), xg,
                       precision=lax.Precision.HIGHEST)
    bi = jnp.arange(m // group)
    before = jnp.dot((bi[:, None] > bi[None, :]).astype(F32), local[:, -1, :],
                     precision=lax.Precision.HIGHEST)
    return jnp.round(local + before[:, None, :]).astype(jnp.int32).reshape(m, c)


def _route_plan(route, blk):
    e = route[:, 0:TOP_K].astype(jnp.int32).reshape(-1)
    m = e.shape[0]
    onehot = (e[:, None] == jnp.arange(N_EXPERTS, dtype=jnp.int32)[None, :]).astype(jnp.int32)
    csum = _prefix_sum(onehot)
    rank = jnp.sum((csum - onehot) * onehot, axis=1)
    nb = (csum[-1] + blk - 1) // blk
    bend = _prefix_sum(nb[:, None])[:, 0]
    dest = jnp.sum(onehot * (bend - nb)[None, :], axis=1) * blk + rank
    nblk = m // blk + N_EXPERTS
    n_used = bend[-1]
    bidx = jnp.minimum(jnp.arange(nblk, dtype=jnp.int32), n_used - 1)
    blk_e = jnp.minimum(jnp.sum(bidx[:, None] >= bend[None, :], axis=1), N_EXPERTS - 1).astype(jnp.int32)
    gap_start = jnp.concatenate([(bend - nb) * blk + csum[-1], (n_used * blk)[None]])
    gap_len = jnp.concatenate([nb * blk - csum[-1], ((nblk - n_used) * blk)[None]])
    gap_end = _prefix_sum(gap_len[:, None])[:, 0]
    q = jnp.arange(nblk * blk - m, dtype=jnp.int32)
    gi = jnp.sum(q[:, None] >= gap_end[None, :], axis=1)
    pad = gap_start[gi] + q - (gap_end - gap_len)[gi]
    return dest.astype(jnp.int32), pad.astype(jnp.int32), blk_e, n_used.reshape(1).astype(jnp.int32), nblk


def _to_tiles(ref, x):
    n = x.shape[0]
    for s in range(ROW_TILE):
        ref[pl.ds(s, n, stride=ROW_TILE), :] = x[:, s * LANES:(s + 1) * LANES]


def _from_tiles(ref, n):
    return jnp.concatenate([ref[pl.ds(s, n, stride=ROW_TILE), :] for s in range(ROW_TILE)], axis=1)


def _row_copy(src, src_row, dst, dst_row, sem):
    s0 = pl.multiple_of(src_row * ROW_TILE, ROW_TILE)
    d0 = pl.multiple_of(dst_row * ROW_TILE, ROW_TILE)
    return pltpu.make_async_copy(src.at[pl.ds(s0, ROW_TILE), :], dst.at[pl.ds(d0, ROW_TILE), :], sem)


def _wait_rows(hbm, vmem, sem, to_hbm):
    rows = hbm.at[pl.ds(0, vmem.shape[0]), :]
    (pltpu.make_async_copy(vmem, rows, sem) if to_hbm else pltpu.make_async_copy(rows, vmem, sem)).wait()


def _dispatch_kernel(nt, dest_ref, pad_ref, x_ref, g_ref, buf_out, xn_s, zero_s, sem, zsem):
    i = pl.program_id(0)
    tm = x_ref.shape[0]
    slot = i % 2
    n_pad = pad_ref.shape[2]

    def drain(s):
        for _ in range(TOP_K):
            _wait_rows(buf_out, xn_s.at[s], sem.at[s], True)

    @pl.when(i >= 2)
    def _():
        drain(slot)

    zero_s[...] = jnp.zeros(zero_s.shape, F32)

    def zero(q, c):
        _row_copy(zero_s, q, buf_out, pad_ref[0, 0, q], zsem.at[0]).start()
        return c

    lax.fori_loop(0, n_pad, zero, 0, unroll=8)
    _to_tiles(xn_s.at[slot], _rms(x_ref[...], g_ref[...]))

    def start(r, c):
        for k in range(TOP_K):
            _row_copy(xn_s.at[slot], r, buf_out, dest_ref[0, 0, TOP_K * r + k],
                      sem.at[slot]).start(priority=k % DMA_QUEUES)
        return c

    lax.fori_loop(0, tm, start, 0, unroll=8)
    _wait_rows(buf_out, zero_s, zsem.at[0], True)

    @pl.when(i == nt - 1)
    def _():
        drain(slot)
        if nt > 1:
            drain(1 - slot)


def _dispatch(x, g, dest, pad, rows, tm):
    t, d = x.shape
    nt = t // tm
    n_pad = pad.shape[0] // nt
    assert n_pad * nt == pad.shape[0]
    return pl.pallas_call(
        functools.partial(_dispatch_kernel, nt),
        grid=(nt,),
        in_specs=[pl.BlockSpec((1, 1, TOP_K * tm), lambda i: (i, 0, 0), memory_space=pltpu.SMEM),
                  pl.BlockSpec((1, 1, n_pad), lambda i: (i, 0, 0), memory_space=pltpu.SMEM),
                  pl.BlockSpec((tm, d), lambda i: (i, 0)),
                  pl.BlockSpec((1, d), lambda i: (0, 0))],
        out_specs=pl.BlockSpec(memory_space=pl.ANY),
        out_shape=jax.ShapeDtypeStruct((rows * ROW_TILE, LANES), F32),
        scratch_shapes=[pltpu.VMEM((2, tm * ROW_TILE, LANES), F32),
                        pltpu.VMEM((n_pad * ROW_TILE, LANES), F32),
                        pltpu.SemaphoreType.DMA((2,)), pltpu.SemaphoreType.DMA((1,))],
        compiler_params=_cparams(("arbitrary",)),
        name="moe_dispatch",
    )(dest.reshape(nt, 1, TOP_K * tm), pad.reshape(nt, 1, n_pad), x, g.reshape(1, d))


def _expert_kernel(be_ref, nu_ref, x_ref, w1_ref, w3_ref, w2_ref, y_ref):
    del be_ref
    used = pl.program_id(0) < nu_ref[0]

    @pl.when(jnp.logical_not(used))
    def _():
        y_ref[...] = jnp.zeros(y_ref.shape, F32)

    @pl.when(used)
    def _():
        x = _from_tiles(x_ref, x_ref.shape[0] // ROW_TILE).astype(BF16)
        y = _swiglu(x, lambda c: w1_ref[0, :, c], lambda c: w3_ref[0, :, c], lambda c: w2_ref[0, c, :])
        _to_tiles(y_ref, y)


def _experts(buf, blk_e, n_used, w1, w3, w2, blk):
    d, f = w1.shape[1], w1.shape[2]
    once = pl.Buffered(2)
    grid_spec = pltpu.PrefetchScalarGridSpec(
        num_scalar_prefetch=2,
        grid=(buf.shape[0] // (blk * ROW_TILE),),
        in_specs=[pl.BlockSpec((blk * ROW_TILE, LANES), lambda b, be, nu: (b, 0)),
                  pl.BlockSpec((1, d, f), lambda b, be, nu: (be[b], 0, 0), pipeline_mode=once),
                  pl.BlockSpec((1, d, f), lambda b, be, nu: (be[b], 0, 0), pipeline_mode=once),
                  pl.BlockSpec((1, f, d), lambda b, be, nu: (be[b], 0, 0), pipeline_mode=once)],
        out_specs=pl.BlockSpec((blk * ROW_TILE, LANES), lambda b, be, nu: (b, 0)))
    return pl.pallas_call(
        _expert_kernel,
        grid_spec=grid_spec,
        out_shape=jax.ShapeDtypeStruct(buf.shape, F32),
        compiler_params=_cparams(("arbitrary",)),
        name="moe_experts",
    )(blk_e, n_used, buf, w1, w3, w2)


def _combine_kernel(nt, dcur_ref, dnxt_ref, x_ref, rt_ref, fn_ref, y_hbm, o_ref, ya, yb, sem):
    i = pl.program_id(0)
    tm = x_ref.shape[0]
    slot = i % 2

    def gather(dref, s):
        def start(r, c):
            _row_copy(y_hbm, dref[0, 0, TOP_K * r], ya.at[s], r, sem.at[s, 0]).start(priority=0)
            _row_copy(y_hbm, dref[0, 0, TOP_K * r + 1], yb.at[s], r, sem.at[s, 1]).start(
                priority=1 % DMA_QUEUES)
            return c

        lax.fori_loop(0, tm, start, 0, unroll=8)

    @pl.when(i == 0)
    def _():
        gather(dcur_ref, slot)

    @pl.when(i + 1 < nt)
    def _():
        gather(dnxt_ref, 1 - slot)

    _wait_rows(y_hbm, ya.at[slot], sem.at[slot, 0], False)
    _wait_rows(y_hbm, yb.at[slot], sem.at[slot, 1], False)
    rt = rt_ref[...]
    lane = lax.broadcasted_iota(jnp.int32, rt.shape, 1)
    g1 = jnp.sum(jnp.where(lane == 2, rt, 0.0), axis=-1, keepdims=True)
    g2 = jnp.sum(jnp.where(lane == 3, rt, 0.0), axis=-1, keepdims=True)
    moe = g1 * _from_tiles(ya.at[slot], tm) + g2 * _from_tiles(yb.at[slot], tm)
    o_ref[...] = _rms(x_ref[...] + moe, fn_ref[...])


def _combine(x, route, fnorm, y, dest, tm):
    t, d = x.shape
    nt = t // tm
    dest3 = dest.reshape(nt, 1, TOP_K * tm)
    return pl.pallas_call(
        functools.partial(_combine_kernel, nt),
        grid=(nt,),
        in_specs=[pl.BlockSpec((1, 1, TOP_K * tm), lambda i: (i, 0, 0), memory_space=pltpu.SMEM),
                  pl.BlockSpec((1, 1, TOP_K * tm), lambda i: (jnp.minimum(i + 1, nt - 1), 0, 0),
                               memory_space=pltpu.SMEM),
                  pl.BlockSpec((tm, d), lambda i: (i, 0)),
                  pl.BlockSpec((tm, LANES), lambda i: (i, 0)),
                  pl.BlockSpec((1, d), lambda i: (0, 0)),
                  pl.BlockSpec(memory_space=pl.ANY)],
        out_specs=pl.BlockSpec((tm, d), lambda i: (i, 0)),
        out_shape=jax.ShapeDtypeStruct((t, d), F32),
        scratch_shapes=[pltpu.VMEM((2, tm * ROW_TILE, LANES), F32),
                        pltpu.VMEM((2, tm * ROW_TILE, LANES), F32),
                        pltpu.SemaphoreType.DMA((2, 2))],
        compiler_params=_cparams(("arbitrary",)),
        name="moe_combine",
    )(dest3, dest3, x, route, fnorm.reshape(1, d), y)


def _moe(x, route, g, w1, w3, w2, fnorm, tm):
    blk = min(MOE_BLOCK, x.shape[0])
    dest, pad, blk_e, n_used, nblk = _route_plan(route, blk)
    buf = _dispatch(x, g, dest, pad, nblk * blk, tm)
    y = _experts(buf, blk_e, n_used, w1, w3, w2, blk)
    return _combine(x, route, fnorm, y, dest, tm)


def _block_diag(w):
    g, a, b = w.shape
    eye = jnp.eye(g, dtype=w.dtype)
    return (eye[:, None, :, None] * w[:, :, None, :]).reshape(g * a, g * b)


def kernel(x, l0_norm_mix, l0_w_in, l0_conv_w, l0_conv_b, l0_gate_a_w, l0_gate_a_b, l0_gate_x_w, l0_gate_x_b, l0_lru_lambda, l0_shift_mu, l0_w0, l0_w_up, l0_a0, l0_a_up, l0_g_up, l0_k_k, l0_k_a, l0_r_k, l0_ln_x_w, l0_ln_x_b, l0_w_out, l0_norm_ffn, l0_ffn_w1, l0_ffn_w3, l0_ffn_w2, l1_norm_mix, l1_w_in, l1_conv_w, l1_conv_b, l1_wq, l1_wk, l1_wv, l1_ig_b, l1_fg_b, l1_mh_norm_w, l1_w_out, l1_norm_ffn, l1_router, l1_moe_w1, l1_moe_w3, l1_moe_w2, final_norm):
    batch, seq, d = x.shape
    assert d == D_MODEL == ROW_TILE * LANES and seq % RW_CHUNK == 0 and seq % MLSTM_CHUNK == 0
    t = batch * seq
    xt = x.reshape(t, d)
    tm = min(TOKEN_TILE, t)

    w_in0 = l0_w_in.astype(BF16)
    p_lru, p_rw = _norm_mm(xt, l0_norm_mix, w_in0, (2 * LRU_WIDTH, RW_IN), tm, "l0_in")
    y_lru = _lru(p_lru, batch, seq, l0_conv_w, l0_conv_b,
                 _block_diag(l0_gate_a_w).astype(BF16), l0_gate_a_b,
                 _block_diag(l0_gate_x_w).astype(BF16), l0_gate_x_b, l0_lru_lambda,
                 min(TOKEN_TILE, seq))
    y_rw = _rwkv(p_rw, batch, seq, l0_shift_mu, l0_w0, l0_w_up, l0_a0, l0_a_up, l0_g_up,
                 l0_k_k, l0_k_a, l0_r_k.reshape(-1), l0_ln_x_w, l0_ln_x_b)
    w_out0 = l0_w_out.astype(BF16)
    ne, _, f = l1_moe_w1.shape
    x2, (mw1, mw3, mw2) = _ffn(
        xt, y_lru, y_rw, w_out0[:LRU_WIDTH], w_out0[LRU_WIDTH:], l0_norm_ffn,
        l0_ffn_w1.astype(BF16), l0_ffn_w3.astype(BF16), l0_ffn_w2.astype(BF16),
        [l1_moe_w1.reshape(ne * d, f), l1_moe_w3.reshape(ne * d, f), l1_moe_w2.reshape(ne * f, d)], tm)

    n_in1 = 2 * MLSTM_WIDTH + LANES
    w_in1 = jnp.zeros((d, n_in1), F32).at[:, :l1_w_in.shape[1]].set(l1_w_in).astype(BF16)
    h1 = _mlstm(x2, l1_norm_mix, w_in1, batch, seq, l1_conv_w, l1_conv_b, l1_wq, l1_wk, l1_wv,
                l1_ig_b, l1_fg_b, l1_mh_norm_w, tm)
    router = jnp.zeros((d, LANES), F32).at[:, :N_EXPERTS].set(l1_router)
    x3, route = _out_router(x2, h1, l1_w_out.astype(BF16), l1_norm_ffn, router, tm)
    out = _moe(x3, route, l1_norm_ffn, mw1.reshape(ne, d, f), mw3.reshape(ne, d, f),
               mw2.reshape(ne, f, d), final_norm, tm)
    return out.reshape(batch, seq, d)
```
